```python
import jax
import jax.numpy as jnp
from jax import lax
import numpy as np

D_MODEL = 2048
BATCH = 4
SEQ = 2048
DEPTH = 1

N_META = 16
POOL_GROUPS = 4
POOL_WINDOWS = (2, 4, 8, 16)
POOL_WIDTH = D_MODEL // 2
POOL_GROUP_DIM = POOL_WIDTH // POOL_GROUPS
DN_HEADS = 16
DN_HEAD_DIM = 128
DN_WIDTH = DN_HEADS * DN_HEAD_DIM
CONV_WIDTH = 4
CHUNK = 64
NORM_EPS = 1e-6
IN_SPLIT_SIZES = (POOL_WIDTH, POOL_WIDTH, DN_WIDTH, DN_WIDTH, DN_WIDTH, DN_WIDTH, DN_HEADS, DN_HEADS, D_MODEL, D_MODEL)
IN_COLS = 2 * POOL_WIDTH + 4 * DN_WIDTH + 2 * DN_HEADS + 2 * D_MODEL

kernel_name = "hybrid_pool_gated_deltanet_block"


def rmsnorm(x, w):
    xf = x.astype(jnp.float32)
    y = xf * lax.rsqrt(jnp.mean(xf * xf, axis=-1, keepdims=True) + NORM_EPS)
    return (y * w.astype(jnp.float32)).astype(x.dtype)


def l2norm(x):
    return x * lax.rsqrt(jnp.sum(x * x, axis=-1, keepdims=True) + NORM_EPS)


def causal_multiscale_pool(u, mix_w, scale):
    Bsz, L, _ = u.shape
    uf = u.astype(jnp.float32)
    csum = jnp.concatenate([jnp.zeros((Bsz, 1, POOL_WIDTH), jnp.float32), jnp.cumsum(uf, axis=1)], axis=1)
    t = jnp.arange(1, L + 1)
    pooled = []
    for gi, w in enumerate(POOL_WINDOWS):
        c = csum[..., gi * POOL_GROUP_DIM:(gi + 1) * POOL_GROUP_DIM]
        lag = jnp.pad(c, ((0, 0), (w, 0), (0, 0)))[:, :L + 1]
        cnt = jnp.minimum(t, w).astype(jnp.float32)[None, :, None]
        pooled.append((c[:, 1:] - lag[:, 1:]) / cnt)
    pooled = jnp.concatenate(pooled, axis=-1) - uf
    pooled = pooled.reshape(Bsz, L, POOL_GROUPS, POOL_GROUP_DIM)
    mixed = jnp.einsum('blgc,gcd->blgd', pooled, mix_w.astype(jnp.float32)).reshape(Bsz, L, POOL_WIDTH)
    return mixed * scale.astype(jnp.float32)


def causal_depthwise_conv_silu(x, w):
    K = w.shape[0]
    L = x.shape[1]
    xp = jnp.pad(x, ((0, 0), (K - 1, 0), (0, 0)))
    y = xp[:, K - 1:K - 1 + L] * w[K - 1]
    for kk in range(K - 1):
        y = y + xp[:, kk:kk + L] * w[kk]
    return jax.nn.silu(y)


def chunk_gated_delta_rule(q, k, v, beta, g):
    Bsz, Lp, H, Dk = q.shape
    Dv = v.shape[-1]
    N = Lp // CHUNK

    def to_chunks(t):
        return jnp.moveaxis(t.reshape((Bsz, N, CHUNK) + t.shape[2:]), 3, 1)

    q, k, v, beta, g = to_chunks(q), to_chunks(k), to_chunks(v), to_chunks(beta), to_chunks(g)
    gcum = jnp.cumsum(g, axis=-1)
    causal = jnp.tril(jnp.ones((CHUNK, CHUNK), bool))
    strict = jnp.tril(jnp.ones((CHUNK, CHUNK), bool), -1)
    diff = gcum[..., :, None] - gcum[..., None, :]
    decay = jnp.where(causal, jnp.exp(jnp.where(causal, diff, 0.0)), 0.0)
    k_beta = k * beta[..., None]
    v_beta = v * beta[..., None]
    lmat = jnp.where(strict, jnp.einsum('bhncd,bhnsd->bhncs', k_beta, k) * decay, 0.0)
    eye = jnp.eye(CHUNK, dtype=jnp.float32)
    tmat = lax.linalg.triangular_solve(eye + lmat, jnp.broadcast_to(eye, lmat.shape), left_side=True, lower=True)
    u_c = jnp.einsum('bhncs,bhnsd->bhncd', tmat, v_beta)
    w_c = jnp.einsum('bhncs,bhnsd->bhncd', tmat, k_beta * jnp.exp(gcum)[..., None])
    qk = jnp.where(causal, jnp.einsum('bhncd,bhnsd->bhncs', q, k) * decay, 0.0)
    q_dec = q * jnp.exp(gcum)[..., None]
    k_dec = k * jnp.exp(gcum[..., -1:] - gcum)[..., None]
    g_last = jnp.exp(gcum[..., -1])

    def step(S, xs):
        u_i, w_i, q_i, k_i, qk_i, gl_i = xs
        v_new = u_i - jnp.einsum('bhcd,bhde->bhce', w_i, S)
        o_i = jnp.einsum('bhcd,bhde->bhce', q_i, S) + jnp.einsum('bhcs,bhse->bhce', qk_i, v_new)
        S = S * gl_i[..., None, None] + jnp.einsum('bhcd,bhce->bhde', k_i, v_new)
        return S, o_i

    xs = tuple(jnp.moveaxis(t, 2, 0) for t in (u_c, w_c, q_dec, k_dec, qk, g_last))
    S0 = jnp.zeros((Bsz, H, Dk, Dv), jnp.float32)
    _, o = lax.scan(step, S0, xs)
    o = jnp.transpose(o, (1, 0, 3, 2, 4))
    return o.reshape(Bsz, Lp, H, Dv)


def gated_deltanet_branch(q, k, v, z, b, a, conv_w, A_log, dt_bias, norm_w):
    Bsz, L, _ = q.shape
    qkv = causal_depthwise_conv_silu(jnp.concatenate([q, k, v], axis=-1), conv_w)
    q, k, v = jnp.split(qkv.astype(jnp.float32), 3, axis=-1)
    heads = lambda t: t.reshape(Bsz, L, DN_HEADS, DN_HEAD_DIM)
    q = l2norm(heads(q)) * (DN_HEAD_DIM ** -0.5)
    k = l2norm(heads(k))
    v = heads(v)
    beta = jax.nn.sigmoid(b.astype(jnp.float32))
    g = -jnp.exp(A_log.astype(jnp.float32)) * jax.nn.softplus(a.astype(jnp.float32) + dt_bias.astype(jnp.float32))
    pad = (-N_META) % CHUNK
    front = lambda t: jnp.pad(t, ((0, 0), (pad, 0)) + ((0, 0),) * (t.ndim - 2))
    o = chunk_gated_delta_rule(front(q), front(k), front(v), front(beta), front(g))[:, pad:]
    o = rmsnorm(o, norm_w) * jax.nn.silu(heads(z.astype(jnp.float32)))
    return o.reshape(Bsz, L, DN_WIDTH)


def hybrid_layer(h, norm_w, w_in, conv_w, A_log, dt_bias, pool_mix, pool_scale, dn_norm_w, w_pool_out, w_dn_out, w_o):
    xn = rmsnorm(h, norm_w)
    proj = xn @ w_in
    splits = np.cumsum(IN_SPLIT_SIZES[:-1]).tolist()
    u_pool, z_pool, q, k, v, z_dn, b, a, gate_pool, gate_dn = jnp.split(proj, splits, axis=-1)
    y_pool = (causal_multiscale_pool(u_pool, pool_mix, pool_scale) * jax.nn.silu(z_pool.astype(jnp.float32))).astype(h.dtype)
    y_dn = gated_deltanet_branch(q, k, v, z_dn, b, a, conv_w, A_log, dt_bias, dn_norm_w).astype(h.dtype)
    merged = jax.nn.sigmoid(gate_pool) * (y_pool @ w_pool_out) + jax.nn.sigmoid(gate_dn) * (y_dn @ w_dn_out)
    return h + merged @ w_o


def setup_inputs(seed: int = 0) -> dict:
    key = jax.random.key(seed)
    ks = jax.random.split(key, 14)
    nrm = jax.random.normal
    x = nrm(ks[0], (BATCH, SEQ, D_MODEL), jnp.float32)
    meta_tokens = nrm(ks[1], (N_META, D_MODEL), jnp.float32)
    norm_w = 1.0 + 0.02 * nrm(ks[2], (DEPTH, D_MODEL), jnp.float32)
    w_in = nrm(ks[3], (DEPTH, D_MODEL, IN_COLS), jnp.float32) * D_MODEL ** -0.5
    conv_w = nrm(ks[4], (DEPTH, CONV_WIDTH, 3 * DN_WIDTH), jnp.float32) * CONV_WIDTH ** -0.5
    A_log = jnp.log(jax.random.uniform(ks[5], (DEPTH, DN_HEADS), jnp.float32, minval=1.0, maxval=16.0))
    dt = jnp.exp(jax.random.uniform(ks[6], (DEPTH, DN_HEADS), jnp.float32, minval=float(np.log(1e-3)), maxval=float(np.log(1e-1))))
    dt_bias = dt + jnp.log(-jnp.expm1(-dt))
    pool_mix = nrm(ks[7], (DEPTH, POOL_GROUPS, POOL_GROUP_DIM, POOL_GROUP_DIM), jnp.float32) * POOL_GROUP_DIM ** -0.5
    pool_scale = 1.0 + 0.02 * nrm(ks[8], (DEPTH, POOL_WIDTH), jnp.float32)
    dn_norm_w = 1.0 + 0.02 * nrm(ks[9], (DEPTH, DN_HEAD_DIM), jnp.float32)
    w_pool_out = nrm(ks[10], (DEPTH, POOL_WIDTH, D_MODEL), jnp.float32) * POOL_WIDTH ** -0.5
    w_dn_out = nrm(ks[11], (DEPTH, DN_WIDTH, D_MODEL), jnp.float32) * DN_WIDTH ** -0.5
    w_o = nrm(ks[12], (DEPTH, D_MODEL, D_MODEL), jnp.float32) * D_MODEL ** -0.5
    final_norm_w = 1.0 + 0.02 * nrm(ks[13], (D_MODEL,), jnp.float32)
    return {"x": x, "meta_tokens": meta_tokens, "norm_w": norm_w, "w_in": w_in, "conv_w": conv_w,
            "A_log": A_log, "dt_bias": dt_bias, "pool_mix": pool_mix, "pool_scale": pool_scale,
            "dn_norm_w": dn_norm_w, "w_pool_out": w_pool_out, "w_dn_out": w_dn_out, "w_o": w_o,
            "final_norm_w": final_norm_w}


def reference(x, meta_tokens, norm_w, w_in, conv_w, A_log, dt_bias, pool_mix, pool_scale, dn_norm_w, w_pool_out, w_dn_out, w_o, final_norm_w):
    Bsz = x.shape[0]
    meta = jnp.broadcast_to(meta_tokens.astype(x.dtype)[None], (Bsz, N_META, D_MODEL))
    h = jnp.concatenate([meta, x], axis=1)
    for i in range(DEPTH):
        h = hybrid_layer(h, norm_w[i], w_in[i], conv_w[i], A_log[i], dt_bias[i], pool_mix[i], pool_scale[i],
                         dn_norm_w[i], w_pool_out[i], w_dn_out[i], w_o[i])
    return rmsnorm(h[:, N_META:], final_norm_w)
```

```python
import functools

import jax
import jax.numpy as jnp
from jax import lax
from jax.experimental import pallas as pl
from jax.experimental.pallas import tpu as pltpu

D_MODEL = 2048
N_META = 16
POOL_GROUPS = 4
POOL_WINDOWS = (2, 4, 8, 16)
POOL_WIDTH = D_MODEL // 2
POOL_GROUP_DIM = POOL_WIDTH // POOL_GROUPS
DN_HEADS = 16
DN_HEAD_DIM = 128
DN_WIDTH = DN_HEADS * DN_HEAD_DIM
CONV_WIDTH = 4
NORM_EPS = 1e-6

COL_U = 0
COL_ZP = COL_U + POOL_WIDTH
COL_Q = COL_ZP + POOL_WIDTH
COL_K = COL_Q + DN_WIDTH
COL_V = COL_K + DN_WIDTH
COL_ZD = COL_V + DN_WIDTH
COL_GP = COL_ZD + DN_WIDTH
COL_GD = COL_GP + D_MODEL
PROJ_COLS = COL_GD + D_MODEL
BA_OFFSET = 2 * POOL_WIDTH + 4 * DN_WIDTH
LANES = 128
SUBLANES = 8

PROJ_DTYPE = jnp.float32
CHUNK = 64
LEAF = 16
HEADS_PER_STEP = 2
TM_PROJ = 1024
TN_PROJ = 512
TM_OUT = 256
VMEM_LIMIT = 56 * 1024 * 1024

BF16 = jnp.bfloat16
F32 = jnp.float32


def _mm(a, b):
    return jnp.dot(a.astype(BF16), b.astype(BF16), preferred_element_type=F32)


def _mm_nt(a, b):
    return lax.dot_general(a.astype(BF16), b.astype(BF16), (((1,), (1,)), ((), ())),
                           preferred_element_type=F32)


def _mm_tn(a, b):
    return lax.dot_general(a.astype(BF16), b.astype(BF16), (((0,), (0,)), ((), ())),
                           preferred_element_type=F32)


def _sigmoid(x):
    return 1.0 / (1.0 + jnp.exp(-x))


def _silu(x):
    return x * _sigmoid(x)


def _softplus(x):
    return jnp.maximum(x, 0.0) + jnp.log1p(jnp.exp(-jnp.abs(x)))


def _in_proj_kernel(x_ref, nw_ref, w_ref, wba_ref, gpar_ref, proj_ref, gates_ref, xn_ref, *, tm, sub):
    j = pl.program_id(1)

    @pl.when(j == 0)
    def _():
        nw = nw_ref[...]

        def body(r, carry):
            rows = pl.ds(pl.multiple_of(r * sub, sub), sub)
            x = x_ref[rows, :]
            ms = jnp.mean(x * x, axis=-1, keepdims=True)
            xn_ref[rows, :] = (x * lax.rsqrt(ms + NORM_EPS) * nw).astype(BF16)
            return carry

        lax.fori_loop(0, tm // sub, body, 0)
        ba = jnp.dot(xn_ref[...], wba_ref[...], preferred_element_type=F32)
        lane = lax.broadcasted_iota(jnp.int32, ba.shape, 1)
        a_rate = jnp.exp(gpar_ref[0:1, :])
        dt_bias = gpar_ref[1:2, :]
        beta = _sigmoid(ba)
        g = -a_rate * _softplus(ba + dt_bias)
        gates_ref[...] = jnp.where(lane < DN_HEADS, beta, jnp.where(lane < 2 * DN_HEADS, g, 0.0))

    proj_ref[...] = jnp.dot(xn_ref[...], w_ref[...], preferred_element_type=F32).astype(proj_ref.dtype)


def _in_proj(x2d, norm_w, w_main, w_ba, gpar, *, tm, tn):
    rows = x2d.shape[0]
    sub = min(tm, 128)
    kern = functools.partial(_in_proj_kernel, tm=tm, sub=sub)
    return pl.pallas_call(
        kern,
        out_shape=(jax.ShapeDtypeStruct((rows, PROJ_COLS), PROJ_DTYPE),
                   jax.ShapeDtypeStruct((rows, LANES), F32)),
        grid=(rows // tm, PROJ_COLS // tn),
        in_specs=[
            pl.BlockSpec((tm, D_MODEL), lambda i, j: (i, 0)),
            pl.BlockSpec((1, D_MODEL), lambda i, j: (0, 0)),
            pl.BlockSpec((D_MODEL, tn), lambda i, j: (0, j)),
            pl.BlockSpec((D_MODEL, LANES), lambda i, j: (0, 0)),
            pl.BlockSpec((2, LANES), lambda i, j: (0, 0)),
        ],
        out_specs=(pl.BlockSpec((tm, tn), lambda i, j: (i, j)),
                   pl.BlockSpec((tm, LANES), lambda i, j: (i, 0))),
        scratch_shapes=[pltpu.VMEM((tm, D_MODEL), BF16)],
        compiler_params=pltpu.CompilerParams(
            dimension_semantics=("arbitrary", "arbitrary"), vmem_limit_bytes=VMEM_LIMIT),
        name="in_proj_rows%d" % rows,
    )(x2d, norm_w, w_main, w_ba, gpar)


def _inv_unit_lower(lmat, c):
    row = lax.broadcasted_iota(jnp.int32, (c, c), 0)
    col = lax.broadcasted_iota(jnp.int32, (c, c), 1)
    eye = (row == col).astype(F32)
    same_leaf = (row // LEAF) == (col // LEAF)
    n1 = jnp.where(same_leaf, -lmat, 0.0)
    t = eye + n1
    p = n1
    span = 2
    while span < LEAF:
        p = _mm(p, p)
        t = t + _mm(t, p)
        span *= 2
    m = 2 * LEAF
    while m <= c:
        off = ((row // m) == (col // m)) & ((row // (m // 2)) != (col // (m // 2)))
        coff = jnp.where(off, lmat, 0.0)
        t = t - _mm(t, _mm(coff, t))
        m *= 2
    return t


def _deltanet_kernel(q_ref, k_ref, v_ref, z_ref, mq_ref, mk_ref, mv_ref, gates_ref, mgates_ref,
                     cwq_ref, cwk_ref, cwv_ref, dnw_ref, y_ref,
                     seq_ref, gseq_ref, wq_s, u_s, p_s, kd_s, egl_s, *, heads, seq_len):
    c = CHUNK
    n_chunks = seq_len // c + 1
    hd = DN_HEAD_DIM
    pad = SUBLANES
    head0 = pl.program_id(1) * heads

    for idx, (m_ref, x_ref) in enumerate(((mq_ref, q_ref), (mk_ref, k_ref), (mv_ref, v_ref))):
        seq_ref[idx, 0:pad, :] = jnp.zeros((pad, heads * hd), F32)
        seq_ref[idx, pad:pad + c, :] = m_ref[...].astype(F32)

        def copy_body(r, carry, idx=idx, x_ref=x_ref):
            src = pl.ds(pl.multiple_of(r * 256, 256), 256)
            dst = pl.ds(pl.multiple_of(pad + c + r * 256, SUBLANES), 256)
            seq_ref[idx, dst, :] = x_ref[src, :].astype(F32)
            return carry

        lax.fori_loop(0, seq_len // 256, copy_body, 0)
    gseq_ref[0:c, :] = mgates_ref[...]
    gseq_ref[c:c + seq_len, :] = gates_ref[...]

    row = lax.broadcasted_iota(jnp.int32, (c, c), 0)
    col = lax.broadcasted_iota(jnp.int32, (c, c), 1)
    causal = row >= col
    strict = row > col
    tril = causal.astype(F32)
    cw = (cwq_ref[...], cwk_ref[...], cwv_ref[...])

    def conv_silu(idx, base):
        blk = seq_ref[idx, pl.ds(base, c + pad), :]
        w = cw[idx]
        y = blk[pad:pad + c] * w[CONV_WIDTH - 1:CONV_WIDTH]
        for kk in range(CONV_WIDTH - 1):
            off = pad - (CONV_WIDTH - 1) + kk
            y = y + blk[off:off + c] * w[kk:kk + 1]
        return _silu(y)

    def l2n(x):
        return x * lax.rsqrt(jnp.sum(x * x, axis=-1, keepdims=True) + NORM_EPS)

    def prep_body(ci, carry):
        base = pl.multiple_of(ci * c, c)
        qa = conv_silu(0, base)
        ka = conv_silu(1, base)
        va = conv_silu(2, base)
        gt = gseq_ref[pl.ds(base, c), :]
        gcum = jnp.dot(tril, gt, preferred_element_type=F32, precision=lax.Precision.HIGHEST)
        gcum_t = gcum.T
        for h in range(heads):
            cs = slice(h * hd, (h + 1) * hd)
            qh = l2n(qa[:, cs]) * (hd ** -0.5)
            kh = l2n(ka[:, cs])
            vh = va[:, cs]
            lane_h = head0 + h
            beta_c = _pick_col(gt, lane_h)
            g_c = _pick_col(gcum, DN_HEADS + lane_h)
            g_r = _pick_row(gcum_t, DN_HEADS + lane_h)
            g_last = g_c[c - 1:c, :]
            kb = kh * beta_c
            aq = _mm_nt(jnp.concatenate([kb, qh], axis=0), kh)
            dec = jnp.where(causal, jnp.exp(jnp.where(causal, g_c - g_r, 0.0)), 0.0)
            lmat = jnp.where(strict, aq[:c] * dec, 0.0)
            pmat = aq[c:] * dec
            tmat = _inv_unit_lower(lmat, c)
            eg = jnp.exp(g_c)
            uw = _mm(tmat, jnp.concatenate([vh * beta_c, kb * eg], axis=1))
            u_s[h, ci] = uw[:, :hd]
            wq_s[h, ci] = jnp.concatenate([uw[:, hd:], qh * eg], axis=0).astype(BF16)
            p_s[h, ci] = pmat.astype(BF16)
            kd_s[h, ci] = (kh * jnp.exp(g_last - g_c)).astype(BF16)
            egl_s[h, ci] = jnp.broadcast_to(jnp.exp(g_last), (1, hd))
        return carry

    def _pick_col(mat, lane_idx):
        lane = lax.broadcasted_iota(jnp.int32, mat.shape, 1)
        return jnp.sum(jnp.where(lane == lane_idx, mat, 0.0), axis=1, keepdims=True)

    def _pick_row(mat, row_idx):
        r = lax.broadcasted_iota(jnp.int32, mat.shape, 0)
        return jnp.sum(jnp.where(r == row_idx, mat, 0.0), axis=0, keepdims=True)

    lax.fori_loop(0, n_chunks, prep_body, 0)

    dnw = dnw_ref[...]

    def step(ci, states):
        new_states = []
        outs = []
        for h in range(heads):
            s = states[h]
            ws = _mm(wq_s[h, ci], s)
            v_new = u_s[h, ci] - ws[:c]
            o = ws[c:] + _mm(p_s[h, ci], v_new)
            s = s * egl_s[h, ci] + _mm_tn(kd_s[h, ci], v_new)
            new_states.append(s)
            outs.append(o)
        return tuple(new_states), outs

    zero_state = tuple(jnp.zeros((hd, hd), F32) for _ in range(heads))
    states, _ = step(0, zero_state)

    def rec_body(ci, states):
        states, outs = step(ci, states)
        rows = pl.ds(pl.multiple_of((ci - 1) * c, c), c)
        for h in range(heads):
            cs = slice(h * hd, (h + 1) * hd)
            o = outs[h]
            o = o * lax.rsqrt(jnp.mean(o * o, axis=-1, keepdims=True) + NORM_EPS) * dnw
            zh = z_ref[rows, cs].astype(F32)
            y_ref[rows, cs] = (o * _silu(zh)).astype(y_ref.dtype)
        return states

    lax.fori_loop(1, n_chunks, rec_body, states)


def _deltanet(proj, meta_qkv, gates, meta_gates, conv_w, dn_norm_w, *, batch, seq_len):
    heads = HEADS_PER_STEP
    gw = heads * DN_HEAD_DIM
    c = CHUNK
    n_chunks = seq_len // c + 1
    kern = functools.partial(_deltanet_kernel, heads=heads, seq_len=seq_len)
    qb, kb, vb, zb = COL_Q // gw, COL_K // gw, COL_V // gw, COL_ZD // gw
    mkb, mvb = DN_WIDTH // gw, 2 * DN_WIDTH // gw
    return pl.pallas_call(
        kern,
        out_shape=jax.ShapeDtypeStruct((batch * seq_len, DN_WIDTH), BF16),
        grid=(batch, DN_HEADS // heads),
        in_specs=[
            pl.BlockSpec((seq_len, gw), lambda b, g: (b, qb + g)),
            pl.BlockSpec((seq_len, gw), lambda b, g: (b, kb + g)),
            pl.BlockSpec((seq_len, gw), lambda b, g: (b, vb + g)),
            pl.BlockSpec((seq_len, gw), lambda b, g: (b, zb + g)),
            pl.BlockSpec((c, gw), lambda b, g: (0, g)),
            pl.BlockSpec((c, gw), lambda b, g: (0, mkb + g)),
            pl.BlockSpec((c, gw), lambda b, g: (0, mvb + g)),
            pl.BlockSpec((seq_len, LANES), lambda b, g: (b, 0)),
            pl.BlockSpec((c, LANES), lambda b, g: (0, 0)),
            pl.BlockSpec((CONV_WIDTH, gw), lambda b, g: (0, g)),
            pl.BlockSpec((CONV_WIDTH, gw), lambda b, g: (0, mkb + g)),
            pl.BlockSpec((CONV_WIDTH, gw), lambda b, g: (0, mvb + g)),
            pl.BlockSpec((1, DN_HEAD_DIM), lambda b, g: (0, 0)),
        ],
        out_specs=pl.BlockSpec((seq_len, gw), lambda b, g: (b, g)),
        scratch_shapes=[
            pltpu.VMEM((3, SUBLANES + c + seq_len, gw), F32),
            pltpu.VMEM((c + seq_len, LANES), F32),
            pltpu.VMEM((heads, n_chunks, 2 * c, DN_HEAD_DIM), BF16),
            pltpu.VMEM((heads, n_chunks, c, DN_HEAD_DIM), F32),
            pltpu.VMEM((heads, n_chunks, c, c), BF16),
            pltpu.VMEM((heads, n_chunks, c, DN_HEAD_DIM), BF16),
            pltpu.VMEM((heads, n_chunks, 1, DN_HEAD_DIM), F32),
        ],
        compiler_params=pltpu.CompilerParams(
            dimension_semantics=("arbitrary", "arbitrary"), vmem_limit_bytes=VMEM_LIMIT),
        name="deltanet",
    )(proj, proj, proj, proj, meta_qkv, meta_qkv, meta_qkv, gates, meta_gates,
      conv_w, conv_w, conv_w, dn_norm_w)


def _out_merge_kernel(u_ref, zp_ref, gp_ref, gd_ref, ydn_ref, x_ref, mu_ref, mix_ref, scale_ref,
                      wpo_ref, wdo_ref, wo_ref, fnw_ref, out_ref, ubuf_ref, *, tm):
    t = pl.program_id(1)
    hist = N_META

    @pl.when(t == 0)
    def _():
        ubuf_ref[0:hist, :] = mu_ref[...].astype(F32)

    @pl.when(t > 0)
    def _():
        ubuf_ref[0:hist, :] = ubuf_ref[tm:tm + hist, :]

    ubuf_ref[hist:hist + tm, :] = u_ref[...].astype(F32)

    pooled = []
    for gi, w in enumerate(POOL_WINDOWS):
        cs = slice(gi * POOL_GROUP_DIM, (gi + 1) * POOL_GROUP_DIM)
        acc = ubuf_ref[hist:hist + tm, cs]
        cur = acc
        for s in range(1, w):
            acc = acc + ubuf_ref[hist - s:hist - s + tm, cs]
        d = acc * (1.0 / w) - cur
        mixed = _mm(d, mix_ref[gi])
        zp = zp_ref[:, cs].astype(F32)
        pooled.append((mixed * scale_ref[:, cs] * _silu(zp)).astype(BF16))
    y_pool = jnp.concatenate(pooled, axis=1)

    t_pool = jnp.dot(y_pool, wpo_ref[...], preferred_element_type=F32)
    t_dn = jnp.dot(ydn_ref[...], wdo_ref[...], preferred_element_type=F32)
    merged = _sigmoid(gp_ref[...].astype(F32)) * t_pool + _sigmoid(gd_ref[...].astype(F32)) * t_dn
    h = x_ref[...] + jnp.dot(merged.astype(BF16), wo_ref[...], preferred_element_type=F32)
    ms = jnp.mean(h * h, axis=-1, keepdims=True)
    out_ref[...] = h * lax.rsqrt(ms + NORM_EPS) * fnw_ref[...]


def _resident(shape, index_map):
    return pl.BlockSpec(shape, index_map, pipeline_mode=pl.Buffered(1))


def _out_merge(proj, y_dn, x2d, meta_u, mix, scale, wpo, wdo, wo, fnw, *, batch, seq_len):
    tm = TM_OUT
    tiles = seq_len // tm
    kern = functools.partial(_out_merge_kernel, tm=tm)
    row = lambda b, t: b * tiles + t
    return pl.pallas_call(
        kern,
        out_shape=jax.ShapeDtypeStruct((batch * seq_len, D_MODEL), F32),
        grid=(batch, tiles),
        in_specs=[
            pl.BlockSpec((tm, POOL_WIDTH), lambda b, t: (row(b, t), COL_U // POOL_WIDTH)),
            pl.BlockSpec((tm, POOL_WIDTH), lambda b, t: (row(b, t), COL_ZP // POOL_WIDTH)),
            pl.BlockSpec((tm, D_MODEL), lambda b, t: (row(b, t), COL_GP // D_MODEL)),
            pl.BlockSpec((tm, D_MODEL), lambda b, t: (row(b, t), COL_GD // D_MODEL)),
            pl.BlockSpec((tm, DN_WIDTH), lambda b, t: (row(b, t), 0)),
            pl.BlockSpec((tm, D_MODEL), lambda b, t: (row(b, t), 0)),
            _resident((N_META, POOL_WIDTH), lambda b, t: (0, 0)),
            _resident((POOL_GROUPS, POOL_GROUP_DIM, POOL_GROUP_DIM), lambda b, t: (0, 0, 0)),
            _resident((1, POOL_WIDTH), lambda b, t: (0, 0)),
            _resident((POOL_WIDTH, D_MODEL), lambda b, t: (0, 0)),
            _resident((DN_WIDTH, D_MODEL), lambda b, t: (0, 0)),
            _resident((D_MODEL, D_MODEL), lambda b, t: (0, 0)),
            _resident((1, D_MODEL), lambda b, t: (0, 0)),
        ],
        out_specs=pl.BlockSpec((tm, D_MODEL), lambda b, t: (row(b, t), 0)),
        scratch_shapes=[pltpu.VMEM((N_META + tm, POOL_WIDTH), F32)],
        compiler_params=pltpu.CompilerParams(
            dimension_semantics=("arbitrary", "arbitrary"), vmem_limit_bytes=VMEM_LIMIT),
        name="out_merge",
    )(proj, proj, proj, proj, y_dn, x2d, meta_u, mix, scale, wpo, wdo, wo, fnw)


def kernel(x, meta_tokens, norm_w, w_in, conv_w, A_log, dt_bias, pool_mix, pool_scale, dn_norm_w,
           w_pool_out, w_dn_out, w_o, final_norm_w):
    batch, seq_len, _ = x.shape
    assert norm_w.shape[0] == 1, "single layer block"
    x2d = x.reshape(batch * seq_len, D_MODEL)

    w = w_in[0]
    w_main = jnp.concatenate([w[:, :BA_OFFSET], w[:, BA_OFFSET + 2 * DN_HEADS:]], axis=1).astype(BF16)
    w_ba = jnp.pad(w[:, BA_OFFSET:BA_OFFSET + 2 * DN_HEADS], ((0, 0), (0, LANES - 2 * DN_HEADS))).astype(BF16)
    gpar = jnp.zeros((2, LANES), F32)
    gpar = gpar.at[0, DN_HEADS:2 * DN_HEADS].set(A_log[0]).at[1, DN_HEADS:2 * DN_HEADS].set(dt_bias[0])
    nw = norm_w[0].reshape(1, D_MODEL)

    proj, gates = _in_proj(x2d, nw, w_main, w_ba, gpar, tm=TM_PROJ, tn=TN_PROJ)
    meta_proj, meta_gates = _in_proj(meta_tokens.astype(F32), nw, w_main, w_ba, gpar, tm=N_META, tn=TN_PROJ)

    front = CHUNK - N_META
    meta_qkv = jnp.pad(meta_proj[:, COL_Q:COL_ZD], ((front, 0), (0, 0)))
    meta_gates = jnp.pad(meta_gates, ((front, 0), (0, 0)))
    meta_u = meta_proj[:, COL_U:COL_U + POOL_WIDTH]

    y_dn = _deltanet(proj, meta_qkv, gates, meta_gates, conv_w[0], dn_norm_w[0].reshape(1, DN_HEAD_DIM),
                     batch=batch, seq_len=seq_len)

    out = _out_merge(proj, y_dn, x2d, meta_u, pool_mix[0].astype(BF16), pool_scale[0].reshape(1, POOL_WIDTH),
                     w_pool_out[0].astype(BF16), w_dn_out[0].astype(BF16), w_o[0].astype(BF16),
                     final_norm_w.reshape(1, D_MODEL), batch=batch, seq_len=seq_len)
    return out.reshape(batch, seq_len, D_MODEL)
```

```python
import functools

import jax
import jax.numpy as jnp
from jax import lax
from jax.experimental import pallas as pl
from jax.experimental.pallas import tpu as pltpu

D_MODEL = 2048
N_META = 16
POOL_GROUPS = 4
POOL_WINDOWS = (2, 4, 8, 16)
POOL_WIDTH = D_MODEL // 2
POOL_GROUP_DIM = POOL_WIDTH // POOL_GROUPS
DN_HEADS = 16
DN_HEAD_DIM = 128
DN_WIDTH = DN_HEADS * DN_HEAD_DIM
CONV_WIDTH = 4
NORM_EPS = 1e-6

COL_U = 0
COL_ZP = COL_U + POOL_WIDTH
COL_Q = COL_ZP + POOL_WIDTH
COL_K = COL_Q + DN_WIDTH
COL_V = COL_K + DN_WIDTH
COL_ZD = COL_V + DN_WIDTH
COL_GP = COL_ZD + DN_WIDTH
COL_GD = COL_GP + D_MODEL
PROJ_COLS = COL_GD + D_MODEL
BA_OFFSET = 2 * POOL_WIDTH + 4 * DN_WIDTH
LANES = 128
SUBLANES = 8

PROJ_DTYPE = jnp.bfloat16
CHUNK = 64
LEAF = 16
HEADS_PER_STEP = 4
TM_PROJ = 1024
TN_PROJ = 512
TM_OUT = 256
VMEM_LIMIT = 56 * 1024 * 1024

BF16 = jnp.bfloat16
F32 = jnp.float32


def _mm(a, b):
    return jnp.dot(a.astype(BF16), b.astype(BF16), preferred_element_type=F32)


def _mm_nt(a, b):
    return lax.dot_general(a.astype(BF16), b.astype(BF16), (((1,), (1,)), ((), ())),
                           preferred_element_type=F32)


def _mm_tn(a, b):
    return lax.dot_general(a.astype(BF16), b.astype(BF16), (((0,), (0,)), ((), ())),
                           preferred_element_type=F32)


def _sigmoid(x):
    return 1.0 / (1.0 + jnp.exp(-x))


def _silu(x):
    return x * _sigmoid(x)


def _softplus(x):
    return jnp.maximum(x, 0.0) + jnp.log1p(jnp.exp(-jnp.abs(x)))


def _in_proj_kernel(x_ref, nw_ref, w_ref, wba_ref, gpar_ref, proj_ref, gates_ref, xn_ref, *, tm, sub):
    j = pl.program_id(1)

    @pl.when(j == 0)
    def _():
        nw = nw_ref[...]

        def body(r, carry):
            rows = pl.ds(pl.multiple_of(r * sub, sub), sub)
            x = x_ref[rows, :]
            ms = jnp.mean(x * x, axis=-1, keepdims=True)
            xn_ref[rows, :] = (x * lax.rsqrt(ms + NORM_EPS) * nw).astype(BF16)
            return carry

        lax.fori_loop(0, tm // sub, body, 0)
        ba = jnp.dot(xn_ref[...], wba_ref[...], preferred_element_type=F32)
        lane = lax.broadcasted_iota(jnp.int32, ba.shape, 1)
        a_rate = jnp.exp(gpar_ref[0:1, :])
        dt_bias = gpar_ref[1:2, :]
        beta = _sigmoid(ba)
        g = -a_rate * _softplus(ba + dt_bias)
        gates_ref[...] = jnp.where(lane < DN_HEADS, beta, jnp.where(lane < 2 * DN_HEADS, g, 0.0))

    proj_ref[...] = jnp.dot(xn_ref[...], w_ref[...], preferred_element_type=F32).astype(proj_ref.dtype)


def _in_proj(x2d, norm_w, w_main, w_ba, gpar, *, tm, tn):
    rows = x2d.shape[0]
    sub = min(tm, 128)
    kern = functools.partial(_in_proj_kernel, tm=tm, sub=sub)
    return pl.pallas_call(
        kern,
        out_shape=(jax.ShapeDtypeStruct((rows, PROJ_COLS), PROJ_DTYPE),
                   jax.ShapeDtypeStruct((rows, LANES), F32)),
        grid=(rows // tm, PROJ_COLS // tn),
        in_specs=[
            pl.BlockSpec((tm, D_MODEL), lambda i, j: (i, 0)),
            pl.BlockSpec((1, D_MODEL), lambda i, j: (0, 0)),
            pl.BlockSpec((D_MODEL, tn), lambda i, j: (0, j)),
            pl.BlockSpec((D_MODEL, LANES), lambda i, j: (0, 0)),
            pl.BlockSpec((2, LANES), lambda i, j: (0, 0)),
        ],
        out_specs=(pl.BlockSpec((tm, tn), lambda i, j: (i, j)),
                   pl.BlockSpec((tm, LANES), lambda i, j: (i, 0))),
        scratch_shapes=[pltpu.VMEM((tm, D_MODEL), BF16)],
        compiler_params=pltpu.CompilerParams(
            dimension_semantics=("arbitrary", "arbitrary"), vmem_limit_bytes=VMEM_LIMIT),
        name="in_proj_rows%d" % rows,
    )(x2d, norm_w, w_main, w_ba, gpar)


def _inv_unit_lower(lmats, c):
    row = lax.broadcasted_iota(jnp.int32, (c, c), 0)
    col = lax.broadcasted_iota(jnp.int32, (c, c), 1)
    eye = (row == col).astype(F32)
    same_leaf = (row // LEAF) == (col // LEAF)
    ps = [jnp.where(same_leaf, -l, 0.0) for l in lmats]
    ts = [eye + p for p in ps]
    span = 2
    while span < LEAF:
        ps = [_mm(p, p) for p in ps]
        ts = [t + _mm(t, p) for t, p in zip(ts, ps)]
        span *= 2
    m = 2 * LEAF
    while m <= c:
        off = ((row // m) == (col // m)) & ((row // (m // 2)) != (col // (m // 2)))
        inner = [_mm(jnp.where(off, l, 0.0), t) for l, t in zip(lmats, ts)]
        ts = [t - _mm(t, x) for t, x in zip(ts, inner)]
        m *= 2
    return ts


def _deltanet_kernel(q_ref, k_ref, v_ref, z_ref, mq_ref, mk_ref, mv_ref, gates_ref, mgates_ref,
                     cwq_ref, cwk_ref, cwv_ref, dnw_ref, y_ref,
                     seq_ref, gseq_ref, wq_s, u_s, p_s, kd_s, egl_s, *, heads, seq_len):
    c = CHUNK
    n_chunks = seq_len // c + 1
    hd = DN_HEAD_DIM
    pad = SUBLANES
    head0 = pl.program_id(1) * heads

    for idx, (m_ref, x_ref) in enumerate(((mq_ref, q_ref), (mk_ref, k_ref), (mv_ref, v_ref))):
        seq_ref[idx, 0:pad, :] = jnp.zeros((pad, heads * hd), F32)
        seq_ref[idx, pad:pad + c, :] = m_ref[...].astype(F32)

        def copy_body(r, carry, idx=idx, x_ref=x_ref):
            src = pl.ds(pl.multiple_of(r * 256, 256), 256)
            dst = pl.ds(pl.multiple_of(pad + c + r * 256, SUBLANES), 256)
            seq_ref[idx, dst, :] = x_ref[src, :].astype(F32)
            return carry

        lax.fori_loop(0, seq_len // 256, copy_body, 0)
    gseq_ref[0:c, :] = mgates_ref[...]
    gseq_ref[c:c + seq_len, :] = gates_ref[...]

    row = lax.broadcasted_iota(jnp.int32, (c, c), 0)
    col = lax.broadcasted_iota(jnp.int32, (c, c), 1)
    causal = row >= col
    strict = row > col
    tril = causal.astype(F32)
    cw = (cwq_ref[...], cwk_ref[...], cwv_ref[...])

    def conv_silu(idx, base):
        blk = seq_ref[idx, pl.ds(base, c + pad), :]
        w = cw[idx]
        y = blk[pad:pad + c] * w[CONV_WIDTH - 1:CONV_WIDTH]
        for kk in range(CONV_WIDTH - 1):
            off = pad - (CONV_WIDTH - 1) + kk
            y = y + blk[off:off + c] * w[kk:kk + 1]
        return _silu(y)

    def l2n(x):
        return x * lax.rsqrt(jnp.sum(x * x, axis=-1, keepdims=True) + NORM_EPS)

    def pick_col(mat, lane_idx):
        lane = lax.broadcasted_iota(jnp.int32, mat.shape, 1)
        return jnp.sum(jnp.where(lane == lane_idx, mat, 0.0), axis=1, keepdims=True)

    def pick_row(mat, row_idx):
        r = lax.broadcasted_iota(jnp.int32, mat.shape, 0)
        return jnp.sum(jnp.where(r == row_idx, mat, 0.0), axis=0, keepdims=True)

    hs = range(heads)

    def prep_body(ci, carry):
        base = pl.multiple_of(ci * c, c)
        qa = conv_silu(0, base)
        ka = conv_silu(1, base)
        va = conv_silu(2, base)
        gt = gseq_ref[pl.ds(base, c), :]
        gcum = jnp.dot(tril, gt, preferred_element_type=F32, precision=lax.Precision.HIGHEST)
        gcum_t = gcum.T
        cols = [slice(h * hd, (h + 1) * hd) for h in hs]
        qh = [l2n(qa[:, cs]) * (hd ** -0.5) for cs in cols]
        kh = [l2n(ka[:, cs]) for cs in cols]
        vh = [va[:, cs] for cs in cols]
        beta_c = [pick_col(gt, head0 + h) for h in hs]
        g_c = [pick_col(gcum, DN_HEADS + head0 + h) for h in hs]
        g_r = [pick_row(gcum_t, DN_HEADS + head0 + h) for h in hs]
        g_last = [g[c - 1:c, :] for g in g_c]
        kb = [k * b for k, b in zip(kh, beta_c)]
        aq = [_mm_nt(jnp.concatenate([kb[h], qh[h]], axis=0), kh[h]) for h in hs]
        dec = [jnp.where(causal, jnp.exp(jnp.where(causal, g_c[h] - g_r[h], 0.0)), 0.0) for h in hs]
        lmat = [jnp.where(strict, aq[h][:c] * dec[h], 0.0) for h in hs]
        tmat = _inv_unit_lower(lmat, c)
        eg = [jnp.exp(g) for g in g_c]
        uw = [_mm(tmat[h], jnp.concatenate([vh[h] * beta_c[h], kb[h] * eg[h]], axis=1)) for h in hs]
        for h in hs:
            u_s[h, ci] = uw[h][:, :hd]
            wq_s[h, ci] = jnp.concatenate([uw[h][:, hd:], qh[h] * eg[h]], axis=0).astype(BF16)
            p_s[h, ci] = (aq[h][c:] * dec[h]).astype(BF16)
            kd_s[h, ci] = (kh[h] * jnp.exp(g_last[h] - g_c[h])).astype(BF16)
            egl_s[h, ci] = jnp.broadcast_to(jnp.exp(g_last[h]), (1, hd))
        return carry

    lax.fori_loop(0, n_chunks, prep_body, 0)

    dnw = dnw_ref[...]

    def step(ci, states):
        ws = [_mm(wq_s[h, ci], states[h]) for h in hs]
        v_new = [u_s[h, ci] - ws[h][:c] for h in hs]
        upd = [_mm_tn(kd_s[h, ci], v_new[h]) for h in hs]
        outs = [ws[h][c:] + _mm(p_s[h, ci], v_new[h]) for h in hs]
        new_states = tuple(states[h] * egl_s[h, ci] + upd[h] for h in hs)
        return new_states, outs

    zero_state = tuple(jnp.zeros((hd, hd), F32) for _ in hs)
    states, _ = step(0, zero_state)

    def rec_body(ci, states):
        states, outs = step(ci, states)
        rows = pl.ds(pl.multiple_of((ci - 1) * c, c), c)
        for h in hs:
            cs = slice(h * hd, (h + 1) * hd)
            o = outs[h]
            o = o * lax.rsqrt(jnp.mean(o * o, axis=-1, keepdims=True) + NORM_EPS) * dnw
            zh = z_ref[rows, cs].astype(F32)
            y_ref[rows, cs] = (o * _silu(zh)).astype(y_ref.dtype)
        return states

    lax.fori_loop(1, n_chunks, rec_body, states)


def _deltanet(proj, meta_qkv, gates, meta_gates, conv_w, dn_norm_w, *, batch, seq_len):
    heads = HEADS_PER_STEP
    gw = heads * DN_HEAD_DIM
    c = CHUNK
    n_chunks = seq_len // c + 1
    kern = functools.partial(_deltanet_kernel, heads=heads, seq_len=seq_len)
    qb, kb, vb, zb = COL_Q // gw, COL_K // gw, COL_V // gw, COL_ZD // gw
    mkb, mvb = DN_WIDTH // gw, 2 * DN_WIDTH // gw
    return pl.pallas_call(
        kern,
        out_shape=jax.ShapeDtypeStruct((batch * seq_len, DN_WIDTH), BF16),
        grid=(batch, DN_HEADS // heads),
        in_specs=[
            pl.BlockSpec((seq_len, gw), lambda b, g: (b, qb + g)),
            pl.BlockSpec((seq_len, gw), lambda b, g: (b, kb + g)),
            pl.BlockSpec((seq_len, gw), lambda b, g: (b, vb + g)),
            pl.BlockSpec((seq_len, gw), lambda b, g: (b, zb + g)),
            pl.BlockSpec((c, gw), lambda b, g: (0, g)),
            pl.BlockSpec((c, gw), lambda b, g: (0, mkb + g)),
            pl.BlockSpec((c, gw), lambda b, g: (0, mvb + g)),
            pl.BlockSpec((seq_len, LANES), lambda b, g: (b, 0)),
            pl.BlockSpec((c, LANES), lambda b, g: (0, 0)),
            pl.BlockSpec((CONV_WIDTH, gw), lambda b, g: (0, g)),
            pl.BlockSpec((CONV_WIDTH, gw), lambda b, g: (0, mkb + g)),
            pl.BlockSpec((CONV_WIDTH, gw), lambda b, g: (0, mvb + g)),
            pl.BlockSpec((1, DN_HEAD_DIM), lambda b, g: (0, 0)),
        ],
        out_specs=pl.BlockSpec((seq_len, gw), lambda b, g: (b, g)),
        scratch_shapes=[
            pltpu.VMEM((3, SUBLANES + c + seq_len, gw), F32),
            pltpu.VMEM((c + seq_len, LANES), F32),
            pltpu.VMEM((heads, n_chunks, 2 * c, DN_HEAD_DIM), BF16),
            pltpu.VMEM((heads, n_chunks, c, DN_HEAD_DIM), F32),
            pltpu.VMEM((heads, n_chunks, c, c), BF16),
            pltpu.VMEM((heads, n_chunks, c, DN_HEAD_DIM), BF16),
            pltpu.VMEM((heads, n_chunks, 1, DN_HEAD_DIM), F32),
        ],
        compiler_params=pltpu.CompilerParams(
            dimension_semantics=("arbitrary", "arbitrary"), vmem_limit_bytes=VMEM_LIMIT),
        name="deltanet",
    )(proj, proj, proj, proj, meta_qkv, meta_qkv, meta_qkv, gates, meta_gates,
      conv_w, conv_w, conv_w, dn_norm_w)


def _out_merge_kernel(u_ref, zp_ref, gp_ref, gd_ref, ydn_ref, x_ref, mu_ref, mix_ref, scale_ref,
                      wpo_ref, wdo_ref, wo_ref, fnw_ref, out_ref, ubuf_ref, *, tm):
    t = pl.program_id(1)
    hist = N_META

    @pl.when(t == 0)
    def _():
        ubuf_ref[0:hist, :] = mu_ref[...].astype(F32)

    @pl.when(t > 0)
    def _():
        ubuf_ref[0:hist, :] = ubuf_ref[tm:tm + hist, :]

    ubuf_ref[hist:hist + tm, :] = u_ref[...].astype(F32)

    pooled = []
    for gi, w in enumerate(POOL_WINDOWS):
        cs = slice(gi * POOL_GROUP_DIM, (gi + 1) * POOL_GROUP_DIM)
        acc = ubuf_ref[hist:hist + tm, cs]
        cur = acc
        for s in range(1, w):
            acc = acc + ubuf_ref[hist - s:hist - s + tm, cs]
        d = acc * (1.0 / w) - cur
        mixed = _mm(d, mix_ref[gi])
        zp = zp_ref[:, cs].astype(F32)
        pooled.append((mixed * scale_ref[:, cs] * _silu(zp)).astype(BF16))
    y_pool = jnp.concatenate(pooled, axis=1)

    t_pool = jnp.dot(y_pool, wpo_ref[...], preferred_element_type=F32)
    t_dn = jnp.dot(ydn_ref[...], wdo_ref[...], preferred_element_type=F32)
    merged = _sigmoid(gp_ref[...].astype(F32)) * t_pool + _sigmoid(gd_ref[...].astype(F32)) * t_dn
    h = x_ref[...] + jnp.dot(merged.astype(BF16), wo_ref[...], preferred_element_type=F32)
    ms = jnp.mean(h * h, axis=-1, keepdims=True)
    out_ref[...] = h * lax.rsqrt(ms + NORM_EPS) * fnw_ref[...]


def _resident(shape, index_map):
    return pl.BlockSpec(shape, index_map, pipeline_mode=pl.Buffered(1))


def _out_merge(proj, y_dn, x2d, meta_u, mix, scale, wpo, wdo, wo, fnw, *, batch, seq_len):
    tm = TM_OUT
    tiles = seq_len // tm
    kern = functools.partial(_out_merge_kernel, tm=tm)
    row = lambda b, t: b * tiles + t
    return pl.pallas_call(
        kern,
        out_shape=jax.ShapeDtypeStruct((batch * seq_len, D_MODEL), F32),
        grid=(batch, tiles),
        in_specs=[
            pl.BlockSpec((tm, POOL_WIDTH), lambda b, t: (row(b, t), COL_U // POOL_WIDTH)),
            pl.BlockSpec((tm, POOL_WIDTH), lambda b, t: (row(b, t), COL_ZP // POOL_WIDTH)),
            pl.BlockSpec((tm, D_MODEL), lambda b, t: (row(b, t), COL_GP // D_MODEL)),
            pl.BlockSpec((tm, D_MODEL), lambda b, t: (row(b, t), COL_GD // D_MODEL)),
            pl.BlockSpec((tm, DN_WIDTH), lambda b, t: (row(b, t), 0)),
            pl.BlockSpec((tm, D_MODEL), lambda b, t: (row(b, t), 0)),
            _resident((N_META, POOL_WIDTH), lambda b, t: (0, 0)),
            _resident((POOL_GROUPS, POOL_GROUP_DIM, POOL_GROUP_DIM), lambda b, t: (0, 0, 0)),
            _resident((1, POOL_WIDTH), lambda b, t: (0, 0)),
            _resident((POOL_WIDTH, D_MODEL), lambda b, t: (0, 0)),
            _resident((DN_WIDTH, D_MODEL), lambda b, t: (0, 0)),
            _resident((D_MODEL, D_MODEL), lambda b, t: (0, 0)),
            _resident((1, D_MODEL), lambda b, t: (0, 0)),
        ],
        out_specs=pl.BlockSpec((tm, D_MODEL), lambda b, t: (row(b, t), 0)),
        scratch_shapes=[pltpu.VMEM((N_META + tm, POOL_WIDTH), F32)],
        compiler_params=pltpu.CompilerParams(
            dimension_semantics=("arbitrary", "arbitrary"), vmem_limit_bytes=VMEM_LIMIT),
        name="out_merge",
    )(proj, proj, proj, proj, y_dn, x2d, meta_u, mix, scale, wpo, wdo, wo, fnw)


def kernel(x, meta_tokens, norm_w, w_in, conv_w, A_log, dt_bias, pool_mix, pool_scale, dn_norm_w,
           w_pool_out, w_dn_out, w_o, final_norm_w):
    batch, seq_len, _ = x.shape
    assert norm_w.shape[0] == 1, "single layer block"
    x2d = x.reshape(batch * seq_len, D_MODEL)

    w = w_in[0]
    w_main = jnp.concatenate([w[:, :BA_OFFSET], w[:, BA_OFFSET + 2 * DN_HEADS:]], axis=1).astype(BF16)
    w_ba = jnp.pad(w[:, BA_OFFSET:BA_OFFSET + 2 * DN_HEADS], ((0, 0), (0, LANES - 2 * DN_HEADS))).astype(BF16)
    gpar = jnp.zeros((2, LANES), F32)
    gpar = gpar.at[0, DN_HEADS:2 * DN_HEADS].set(A_log[0]).at[1, DN_HEADS:2 * DN_HEADS].set(dt_bias[0])
    nw = norm_w[0].reshape(1, D_MODEL)

    proj, gates = _in_proj(x2d, nw, w_main, w_ba, gpar, tm=TM_PROJ, tn=TN_PROJ)
    meta_proj, meta_gates = _in_proj(meta_tokens.astype(F32), nw, w_main, w_ba, gpar, tm=N_META, tn=TN_PROJ)

    front = CHUNK - N_META
    meta_qkv = jnp.pad(meta_proj[:, COL_Q:COL_ZD], ((front, 0), (0, 0)))
    meta_gates = jnp.pad(meta_gates, ((front, 0), (0, 0)))
    meta_u = meta_proj[:, COL_U:COL_U + POOL_WIDTH]

    y_dn = _deltanet(proj, meta_qkv, gates, meta_gates, conv_w[0], dn_norm_w[0].reshape(1, DN_HEAD_DIM),
                     batch=batch, seq_len=seq_len)

    out = _out_merge(proj, y_dn, x2d, meta_u, pool_mix[0].astype(BF16), pool_scale[0].reshape(1, POOL_WIDTH),
                     w_pool_out[0].astype(BF16), w_dn_out[0].astype(BF16), w_o[0].astype(BF16),
                     final_norm_w.reshape(1, D_MODEL), batch=batch, seq_len=seq_len)
    return out.reshape(batch, seq_len, D_MODEL)
```

```python
import functools

import jax
import jax.numpy as jnp
from jax import lax
from jax.experimental import pallas as pl
from jax.experimental.pallas import tpu as pltpu

D_MODEL = 2048
N_META = 16
POOL_GROUPS = 4
POOL_WINDOWS = (2, 4, 8, 16)
POOL_WIDTH = D_MODEL // 2
POOL_GROUP_DIM = POOL_WIDTH // POOL_GROUPS
DN_HEADS = 16
DN_HEAD_DIM = 128
DN_WIDTH = DN_HEADS * DN_HEAD_DIM
CONV_WIDTH = 4
NORM_EPS = 1e-6

COL_U = 0
COL_ZP = COL_U + POOL_WIDTH
COL_Q = COL_ZP + POOL_WIDTH
COL_K = COL_Q + DN_WIDTH
COL_V = COL_K + DN_WIDTH
COL_ZD = COL_V + DN_WIDTH
COL_GP = COL_ZD + DN_WIDTH
COL_GD = COL_GP + D_MODEL
PROJ_COLS = COL_GD + D_MODEL
BA_OFFSET = 2 * POOL_WIDTH + 4 * DN_WIDTH
LANES = 128
SUBLANES = 8

PROJ_DTYPE = jnp.bfloat16
CHUNK = 64
LEAF = 16
TT_DN = 512
TM_PROJ = 1024
TN_PROJ = 512
TM_OUT = 256
VMEM_LIMIT = 56 * 1024 * 1024

BF16 = jnp.bfloat16
F32 = jnp.float32


def _mm(a, b):
    return jnp.dot(a.astype(BF16), b.astype(BF16), preferred_element_type=F32)


def _mm_nt(a, b):
    return lax.dot_general(a.astype(BF16), b.astype(BF16), (((1,), (1,)), ((), ())),
                           preferred_element_type=F32)


def _mm_tn(a, b):
    return lax.dot_general(a.astype(BF16), b.astype(BF16), (((0,), (0,)), ((), ())),
                           preferred_element_type=F32)


def _sigmoid(x):
    return 1.0 / (1.0 + jnp.exp(-x))


def _silu(x):
    return x * _sigmoid(x)


def _softplus(x):
    return jnp.maximum(x, 0.0) + jnp.log1p(jnp.exp(-jnp.abs(x)))


def _in_proj_kernel(x_ref, nw_ref, w_ref, wba_ref, gpar_ref, proj_ref, gates_ref, xn_ref, *, tm, sub):
    j = pl.program_id(1)

    @pl.when(j == 0)
    def _():
        nw = nw_ref[...]

        def body(r, carry):
            rows = pl.ds(pl.multiple_of(r * sub, sub), sub)
            x = x_ref[rows, :]
            ms = jnp.mean(x * x, axis=-1, keepdims=True)
            xn_ref[rows, :] = (x * lax.rsqrt(ms + NORM_EPS) * nw).astype(BF16)
            return carry

        lax.fori_loop(0, tm // sub, body, 0)
        ba = jnp.dot(xn_ref[...], wba_ref[...], preferred_element_type=F32)
        lane = lax.broadcasted_iota(jnp.int32, ba.shape, 1)
        a_rate = jnp.exp(gpar_ref[0:1, :])
        dt_bias = gpar_ref[1:2, :]
        beta = _sigmoid(ba)
        g = -a_rate * _softplus(ba + dt_bias)
        gates_ref[...] = jnp.where(lane < DN_HEADS, beta, jnp.where(lane < 2 * DN_HEADS, g, 0.0))

    proj_ref[...] = jnp.dot(xn_ref[...], w_ref[...], preferred_element_type=F32).astype(proj_ref.dtype)


def _in_proj(x2d, norm_w, w_main, w_ba, gpar, *, tm, tn):
    rows = x2d.shape[0]
    sub = min(tm, 128)
    kern = functools.partial(_in_proj_kernel, tm=tm, sub=sub)
    return pl.pallas_call(
        kern,
        out_shape=(jax.ShapeDtypeStruct((rows, PROJ_COLS), PROJ_DTYPE),
                   jax.ShapeDtypeStruct((rows, LANES), F32)),
        grid=(rows // tm, PROJ_COLS // tn),
        in_specs=[
            pl.BlockSpec((tm, D_MODEL), lambda i, j: (i, 0)),
            pl.BlockSpec((1, D_MODEL), lambda i, j: (0, 0)),
            pl.BlockSpec((D_MODEL, tn), lambda i, j: (0, j)),
            pl.BlockSpec((D_MODEL, LANES), lambda i, j: (0, 0)),
            pl.BlockSpec((2, LANES), lambda i, j: (0, 0)),
        ],
        out_specs=(pl.BlockSpec((tm, tn), lambda i, j: (i, j)),
                   pl.BlockSpec((tm, LANES), lambda i, j: (i, 0))),
        scratch_shapes=[pltpu.VMEM((tm, D_MODEL), BF16)],
        compiler_params=pltpu.CompilerParams(
            dimension_semantics=("arbitrary", "arbitrary"), vmem_limit_bytes=VMEM_LIMIT),
        name="in_proj_rows%d" % rows,
    )(x2d, norm_w, w_main, w_ba, gpar)


PAIR = 2 * DN_HEAD_DIM


def _split3(x):
    hi = x.astype(BF16)
    r1 = x - hi.astype(F32)
    mid = r1.astype(BF16)
    lo = (r1 - mid.astype(F32)).astype(BF16)
    return hi, mid, lo


def _block_diag_rows(y, half):
    lane = lax.broadcasted_iota(jnp.int32, y.shape, 1)
    zero = jnp.zeros_like(y)
    return jnp.concatenate([jnp.where(lane < half, y, zero), jnp.where(lane >= half, y, zero)], axis=0)


def _inverse_masks(c):
    row = lax.broadcasted_iota(jnp.int32, (c, 2 * c), 0)
    col = lax.broadcasted_iota(jnp.int32, (c, 2 * c), 1) & (c - 1)
    blk = lambda x, m: x >> (m.bit_length() - 1)
    eye = (row == col).astype(F32)
    same_leaf = blk(row, LEAF) == blk(col, LEAF)
    offs = []
    m = 2 * LEAF
    while m <= c:
        offs.append((blk(row, m) == blk(col, m)) & (blk(row, m // 2) != blk(col, m // 2)))
        m *= 2
    return eye, same_leaf, offs


def _inv_unit_lower_packed(lps, c, masks):
    eye, same_leaf, offs = masks
    bd = lambda y: _block_diag_rows(y.astype(BF16), c)
    ps = [jnp.where(same_leaf, -l, 0.0) for l in lps]
    ts = [eye + p for p in ps]
    span = 2
    while span < LEAF:
        ps = [_mm(p, bd(p)) for p in ps]
        ts = [t + _mm(t, bd(p)) for t, p in zip(ts, ps)]
        span *= 2
    for off in offs:
        inner = [_mm(jnp.where(off, l, 0.0), bd(t)) for l, t in zip(lps, ts)]
        ts = [t - _mm(t, bd(x)) for t, x in zip(ts, inner)]
    return ts


def _deltanet_kernel(q_ref, k_ref, v_ref, z_ref, gates_ref, mqkv_ref, mgates_ref, cw_ref, dnw_ref,
                     expand_ref, y_ref, stage_ref, gstage_ref, state_ref, *, tt):
    c = CHUNK
    hd = DN_HEAD_DIM
    n_pairs = DN_HEADS // 2
    pad = SUBLANES
    t_idx = pl.program_id(1)
    pairs = range(n_pairs)

    @pl.when(t_idx == 0)
    def _():
        for idx in range(3):
            stage_ref[idx, 0:pad, :] = jnp.zeros((pad, DN_WIDTH), F32)
            stage_ref[idx, pad:pad + c, :] = mqkv_ref[:, idx * DN_WIDTH:(idx + 1) * DN_WIDTH].astype(F32)
        gstage_ref[0:c, :] = mgates_ref[...]
        state_ref[...] = jnp.zeros(state_ref.shape, F32)

    @pl.when(t_idx > 0)
    def _():
        for idx in range(3):
            stage_ref[idx, pad + c - pad:pad + c, :] = stage_ref[idx, pad + c + tt - pad:pad + c + tt, :]

    for idx, x_ref in enumerate((q_ref, k_ref, v_ref)):
        stage_ref[idx, pad + c:pad + c + tt, :] = x_ref[...].astype(F32)
    gstage_ref[c:c + tt, :] = gates_ref[...]

    rowp = lax.broadcasted_iota(jnp.int32, (c, 2 * c), 0)
    colp = lax.broadcasted_iota(jnp.int32, (c, 2 * c), 1) & (c - 1)
    causal = rowp >= colp
    strict = rowp > colp
    lane_p = lax.broadcasted_iota(jnp.int32, (c, 2 * c), 1)
    inv_masks = _inverse_masks(c)
    head_a = (lax.broadcasted_iota(jnp.int32, (c, 2 * PAIR), 1) & (PAIR - 1)) < hd
    tril = (lax.broadcasted_iota(jnp.int32, (c, c), 0) >= lax.broadcasted_iota(jnp.int32, (c, c), 1))
    tril3 = jnp.concatenate([tril.astype(BF16)] * 3, axis=1)
    bd_mask = (lax.broadcasted_iota(jnp.int32, (PAIR, PAIR), 0) < hd) == \
              (lax.broadcasted_iota(jnp.int32, (PAIR, PAIR), 1) < hd)
    dnw = dnw_ref[...]

    def conv_silu(idx, base):
        blk = stage_ref[idx, pl.ds(base, c + pad), :]
        w = cw_ref[:, idx * DN_WIDTH:(idx + 1) * DN_WIDTH]
        y = blk[pad:pad + c] * w[CONV_WIDTH - 1:CONV_WIDTH]
        for kk in range(CONV_WIDTH - 1):
            off = pad - (CONV_WIDTH - 1) + kk
            y = y + blk[off:off + c] * w[kk:kk + 1]
        return _silu(y)

    def l2n_heads(x):
        out = []
        for h in range(DN_HEADS):
            xh = x[:, h * hd:(h + 1) * hd]
            out.append(xh * lax.rsqrt(jnp.sum(xh * xh, axis=-1, keepdims=True) + NORM_EPS))
        return jnp.concatenate(out, axis=1)

    def exact_rows_dot(lhs3, x):
        hi, mid, lo = _split3(x)
        return jnp.dot(lhs3, jnp.concatenate([hi, mid, lo], axis=0), preferred_element_type=F32)

    def chunk_body(j, carry):
        base = pl.multiple_of(j * c, c)
        q = l2n_heads(conv_silu(0, base)) * (hd ** -0.5)
        k = l2n_heads(conv_silu(1, base))
        v = conv_silu(2, base)
        gt = gstage_ref[pl.ds(base, c), :]
        hi, mid, lo = _split3(gt)
        rep = jnp.dot(jnp.concatenate([hi, mid, lo], axis=1), expand_ref[...],
                      preferred_element_type=F32)
        beta_rep = rep[:, :DN_WIDTH]
        g_rep = rep[:, DN_WIDTH:]
        gcum = exact_rows_dot(tril3, g_rep)
        g_last = gcum[c - 1:c, :]
        eg = jnp.exp(gcum)
        kb = k * beta_rep
        vb = v * beta_rep
        kbe = kb * eg
        qd = q * eg
        kd = k * jnp.exp(g_last - gcum)
        egl = jnp.exp(g_last)

        sl = [slice(p * PAIR, (p + 1) * PAIR) for p in pairs]
        aq = [_mm_nt(jnp.concatenate([kb[:, s], q[:, s]], axis=0).astype(BF16),
                     _block_diag_rows(k[:, s].astype(BF16), hd)) for s in sl]
        gp = [jnp.where(lane_p < c, g_rep[:, s][:, :hd], g_rep[:, s][:, hd:]) for s in sl]
        diff = [exact_rows_dot(tril3, jnp.where(strict, g, 0.0)) for g in gp]
        dec = [jnp.where(causal, jnp.exp(d), 0.0) for d in diff]
        lps = [jnp.where(strict, aq[p][:c] * dec[p], 0.0) for p in pairs]
        pps = [aq[p][c:] * dec[p] for p in pairs]
        tps = _inv_unit_lower_packed(lps, c, inv_masks)
        uw = []
        for p in pairs:
            x = jnp.concatenate([vb[:, sl[p]], kbe[:, sl[p]]], axis=1).astype(BF16)
            zero = jnp.zeros_like(x)
            rhs = jnp.concatenate([jnp.where(head_a, x, zero), jnp.where(head_a, zero, x)], axis=0)
            uw.append(_mm(tps[p], rhs))
        s_old = [state_ref[p] for p in pairs]
        ws = [_mm(jnp.concatenate([uw[p][:, PAIR:], qd[:, sl[p]]], axis=0), s_old[p]) for p in pairs]
        v_new = [uw[p][:, :PAIR] - ws[p][:c] for p in pairs]
        upd = [_mm_tn(kd[:, sl[p]], v_new[p]) for p in pairs]
        o = [ws[p][c:] + _mm(pps[p], _block_diag_rows(v_new[p].astype(BF16), hd)) for p in pairs]
        for p in pairs:
            state_ref[p] = s_old[p] * egl[:, sl[p]] + jnp.where(bd_mask, upd[p], 0.0)

        @pl.when(j > 0)
        def _():
            rows = pl.ds(pl.multiple_of((j - 1) * c, c), c)
            for p in pairs:
                for half in range(2):
                    oh = o[p][:, half * hd:(half + 1) * hd]
                    oh = oh * lax.rsqrt(jnp.mean(oh * oh, axis=-1, keepdims=True) + NORM_EPS) * dnw
                    cs = slice(p * PAIR + half * hd, p * PAIR + (half + 1) * hd)
                    zh = z_ref[rows, cs].astype(F32)
                    y_ref[rows, cs] = (oh * _silu(zh)).astype(y_ref.dtype)

        return carry

    start = jnp.where(t_idx == 0, 0, 1)
    lax.fori_loop(start, tt // c + 1, chunk_body, 0)


def _expand_matrix():
    r = jnp.arange(LANES)[:, None]
    cidx = jnp.arange(2 * DN_WIDTH)[None, :]
    src = jnp.where(cidx < DN_WIDTH, cidx // DN_HEAD_DIM, DN_HEADS + (cidx - DN_WIDTH) // DN_HEAD_DIM)
    e = (r == src).astype(BF16)
    return jnp.concatenate([e, e, e], axis=0)


def _deltanet(proj, meta_qkv, gates, meta_gates, conv_w, dn_norm_w, *, batch, seq_len):
    tt = TT_DN
    nt = seq_len // tt
    c = CHUNK
    kern = functools.partial(_deltanet_kernel, tt=tt)
    row = lambda b, t: b * nt + t
    const = lambda shape: pl.BlockSpec(shape, lambda b, t: (0,) * len(shape), pipeline_mode=pl.Buffered(1))
    return pl.pallas_call(
        kern,
        out_shape=jax.ShapeDtypeStruct((batch * seq_len, DN_WIDTH), BF16),
        grid=(batch, nt),
        in_specs=[
            pl.BlockSpec((tt, DN_WIDTH), lambda b, t: (row(b, t), COL_Q // DN_WIDTH)),
            pl.BlockSpec((tt, DN_WIDTH), lambda b, t: (row(b, t), COL_K // DN_WIDTH)),
            pl.BlockSpec((tt, DN_WIDTH), lambda b, t: (row(b, t), COL_V // DN_WIDTH)),
            pl.BlockSpec((tt, DN_WIDTH), lambda b, t: (row(b, t), COL_ZD // DN_WIDTH)),
            pl.BlockSpec((tt, LANES), lambda b, t: (row(b, t), 0)),
            const((c, 3 * DN_WIDTH)),
            const((c, LANES)),
            const((CONV_WIDTH, 3 * DN_WIDTH)),
            const((1, DN_HEAD_DIM)),
            const((3 * LANES, 2 * DN_WIDTH)),
        ],
        out_specs=pl.BlockSpec((tt, DN_WIDTH), lambda b, t: (row(b, t), 0)),
        scratch_shapes=[
            pltpu.VMEM((3, SUBLANES + c + tt, DN_WIDTH), F32),
            pltpu.VMEM((c + tt, LANES), F32),
            pltpu.VMEM((DN_HEADS // 2, PAIR, PAIR), F32),
        ],
        compiler_params=pltpu.CompilerParams(
            dimension_semantics=("arbitrary", "arbitrary"), vmem_limit_bytes=VMEM_LIMIT),
        name="deltanet",
    )(proj, proj, proj, proj, gates, meta_qkv, meta_gates, conv_w, dn_norm_w, _expand_matrix())


def _out_merge_kernel(u_ref, zp_ref, gp_ref, gd_ref, ydn_ref, x_ref, mu_ref, mix_ref, scale_ref,
                      wpo_ref, wdo_ref, wo_ref, fnw_ref, out_ref, ubuf_ref, *, tm):
    t = pl.program_id(1)
    hist = N_META

    @pl.when(t == 0)
    def _():
        ubuf_ref[0:hist, :] = mu_ref[...].astype(F32)

    @pl.when(t > 0)
    def _():
        ubuf_ref[0:hist, :] = ubuf_ref[tm:tm + hist, :]

    ubuf_ref[hist:hist + tm, :] = u_ref[...].astype(F32)

    pooled = []
    for gi, w in enumerate(POOL_WINDOWS):
        cs = slice(gi * POOL_GROUP_DIM, (gi + 1) * POOL_GROUP_DIM)
        acc = ubuf_ref[hist:hist + tm, cs]
        cur = acc
        for s in range(1, w):
            acc = acc + ubuf_ref[hist - s:hist - s + tm, cs]
        d = acc * (1.0 / w) - cur
        mixed = _mm(d, mix_ref[gi])
        zp = zp_ref[:, cs].astype(F32)
        pooled.append((mixed * scale_ref[:, cs] * _silu(zp)).astype(BF16))
    y_pool = jnp.concatenate(pooled, axis=1)

    t_pool = jnp.dot(y_pool, wpo_ref[...], preferred_element_type=F32)
    t_dn = jnp.dot(ydn_ref[...], wdo_ref[...], preferred_element_type=F32)
    merged = _sigmoid(gp_ref[...].astype(F32)) * t_pool + _sigmoid(gd_ref[...].astype(F32)) * t_dn
    h = x_ref[...] + jnp.dot(merged.astype(BF16), wo_ref[...], preferred_element_type=F32)
    ms = jnp.mean(h * h, axis=-1, keepdims=True)
    out_ref[...] = h * lax.rsqrt(ms + NORM_EPS) * fnw_ref[...]


def _resident(shape, index_map):
    return pl.BlockSpec(shape, index_map, pipeline_mode=pl.Buffered(1))


def _out_merge(proj, y_dn, x2d, meta_u, mix, scale, wpo, wdo, wo, fnw, *, batch, seq_len):
    tm = TM_OUT
    tiles = seq_len // tm
    kern = functools.partial(_out_merge_kernel, tm=tm)
    row = lambda b, t: b * tiles + t
    return pl.pallas_call(
        kern,
        out_shape=jax.ShapeDtypeStruct((batch * seq_len, D_MODEL), F32),
        grid=(batch, tiles),
        in_specs=[
            pl.BlockSpec((tm, POOL_WIDTH), lambda b, t: (row(b, t), COL_U // POOL_WIDTH)),
            pl.BlockSpec((tm, POOL_WIDTH), lambda b, t: (row(b, t), COL_ZP // POOL_WIDTH)),
            pl.BlockSpec((tm, D_MODEL), lambda b, t: (row(b, t), COL_GP // D_MODEL)),
            pl.BlockSpec((tm, D_MODEL), lambda b, t: (row(b, t), COL_GD // D_MODEL)),
            pl.BlockSpec((tm, DN_WIDTH), lambda b, t: (row(b, t), 0)),
            pl.BlockSpec((tm, D_MODEL), lambda b, t: (row(b, t), 0)),
            _resident((N_META, POOL_WIDTH), lambda b, t: (0, 0)),
            _resident((POOL_GROUPS, POOL_GROUP_DIM, POOL_GROUP_DIM), lambda b, t: (0, 0, 0)),
            _resident((1, POOL_WIDTH), lambda b, t: (0, 0)),
            _resident((POOL_WIDTH, D_MODEL), lambda b, t: (0, 0)),
            _resident((DN_WIDTH, D_MODEL), lambda b, t: (0, 0)),
            _resident((D_MODEL, D_MODEL), lambda b, t: (0, 0)),
            _resident((1, D_MODEL), lambda b, t: (0, 0)),
        ],
        out_specs=pl.BlockSpec((tm, D_MODEL), lambda b, t: (row(b, t), 0)),
        scratch_shapes=[pltpu.VMEM((N_META + tm, POOL_WIDTH), F32)],
        compiler_params=pltpu.CompilerParams(
            dimension_semantics=("arbitrary", "arbitrary"), vmem_limit_bytes=VMEM_LIMIT),
        name="out_merge",
    )(proj, proj, proj, proj, y_dn, x2d, meta_u, mix, scale, wpo, wdo, wo, fnw)


def kernel(x, meta_tokens, norm_w, w_in, conv_w, A_log, dt_bias, pool_mix, pool_scale, dn_norm_w,
           w_pool_out, w_dn_out, w_o, final_norm_w):
    batch, seq_len, _ = x.shape
    assert norm_w.shape[0] == 1, "single layer block"
    x2d = x.reshape(batch * seq_len, D_MODEL)

    w = w_in[0]
    w_main = jnp.concatenate([w[:, :BA_OFFSET], w[:, BA_OFFSET + 2 * DN_HEADS:]], axis=1).astype(BF16)
    w_ba = jnp.pad(w[:, BA_OFFSET:BA_OFFSET + 2 * DN_HEADS], ((0, 0), (0, LANES - 2 * DN_HEADS))).astype(BF16)
    gpar = jnp.zeros((2, LANES), F32)
    gpar = gpar.at[0, DN_HEADS:2 * DN_HEADS].set(A_log[0]).at[1, DN_HEADS:2 * DN_HEADS].set(dt_bias[0])
    nw = norm_w[0].reshape(1, D_MODEL)

    proj, gates = _in_proj(x2d, nw, w_main, w_ba, gpar, tm=TM_PROJ, tn=TN_PROJ)
    meta_proj, meta_gates = _in_proj(meta_tokens.astype(F32), nw, w_main, w_ba, gpar, tm=N_META, tn=TN_PROJ)

    front = CHUNK - N_META
    meta_qkv = jnp.pad(meta_proj[:, COL_Q:COL_ZD], ((front, 0), (0, 0)))
    meta_gates = jnp.pad(meta_gates, ((front, 0), (0, 0)))
    meta_u = meta_proj[:, COL_U:COL_U + POOL_WIDTH]

    y_dn = _deltanet(proj, meta_qkv, gates, meta_gates, conv_w[0], dn_norm_w[0].reshape(1, DN_HEAD_DIM),
                     batch=batch, seq_len=seq_len)

    out = _out_merge(proj, y_dn, x2d, meta_u, pool_mix[0].astype(BF16), pool_scale[0].reshape(1, POOL_WIDTH),
                     w_pool_out[0].astype(BF16), w_dn_out[0].astype(BF16), w_o[0].astype(BF16),
                     final_norm_w.reshape(1, D_MODEL), batch=batch, seq_len=seq_len)
    return out.reshape(batch, seq_len, D_MODEL)
```

```python
import functools

import jax
import jax.numpy as jnp
from jax import lax
from jax.experimental import pallas as pl
from jax.experimental.pallas import tpu as pltpu

D_MODEL = 2048
N_META = 16
POOL_GROUPS = 4
POOL_WINDOWS = (2, 4, 8, 16)
POOL_WIDTH = D_MODEL // 2
POOL_GROUP_DIM = POOL_WIDTH // POOL_GROUPS
DN_HEADS = 16
DN_HEAD_DIM = 128
DN_WIDTH = DN_HEADS * DN_HEAD_DIM
CONV_WIDTH = 4
NORM_EPS = 1e-6

COL_U = 0
COL_ZP = COL_U + POOL_WIDTH
COL_Q = COL_ZP + POOL_WIDTH
COL_K = COL_Q + DN_WIDTH
COL_V = COL_K + DN_WIDTH
COL_ZD = COL_V + DN_WIDTH
BA_OFFSET = COL_ZD + DN_WIDTH
COL_GP = BA_OFFSET + 2 * DN_HEADS
COL_GD = COL_GP + D_MODEL
LANES = 128
SUBLANES = 8
GATE_SHIFT = COL_GP % LANES
GATE_FRAME = D_MODEL + LANES

PROJ_DTYPE = jnp.bfloat16
CHUNK = 64
LEAF = 16
TT_DN = 512
TM_NORM = 512
TN_PROJ = 256
ROW_BLK_PROJ = 2048
TM_OUT = 256
VMEM_LIMIT = 56 * 1024 * 1024

BF16 = jnp.bfloat16
F32 = jnp.float32


def _mm(a, b):
    return jnp.dot(a.astype(BF16), b.astype(BF16), preferred_element_type=F32)


def _mm_nt(a, b):
    return lax.dot_general(a.astype(BF16), b.astype(BF16), (((1,), (1,)), ((), ())),
                           preferred_element_type=F32)


def _mm_tn(a, b):
    return lax.dot_general(a.astype(BF16), b.astype(BF16), (((0,), (0,)), ((), ())),
                           preferred_element_type=F32)


def _sigmoid(x):
    return 1.0 / (1.0 + jnp.exp(-x))


def _silu(x):
    return x * _sigmoid(x)


def _softplus(x):
    return jnp.maximum(x, 0.0) + jnp.log1p(jnp.exp(-jnp.abs(x)))


def _norm_gates_kernel(x_ref, meta_ref, nw_ref, wba_ref, gpar_ref, xn_ref, gates_ref, *, x_tiles, sub):
    i = pl.program_id(0)
    nw = nw_ref[...]
    wba = wba_ref[...].astype(BF16)
    a_rate = jnp.exp(gpar_ref[0:1, :])
    dt_bias = gpar_ref[1:2, :]

    def rows_block(x, rows):
        ms = jnp.mean(x * x, axis=-1, keepdims=True)
        xn = (x * lax.rsqrt(ms + NORM_EPS) * nw).astype(BF16)
        xn_ref[rows, :] = xn
        ba = jnp.dot(xn, wba, preferred_element_type=F32)
        lane = lax.broadcasted_iota(jnp.int32, ba.shape, 1)
        g = -a_rate * _softplus(ba + dt_bias)
        gates_ref[rows, :] = jnp.where(lane < DN_HEADS, _sigmoid(ba), jnp.where(lane < 2 * DN_HEADS, g, 0.0))

    @pl.when(i < x_tiles)
    def _():
        def body(r, carry):
            rows = pl.ds(pl.multiple_of(r * sub, sub), sub)
            rows_block(x_ref[rows, :], rows)
            return carry

        lax.fori_loop(0, x_ref.shape[0] // sub, body, 0)

    @pl.when(i == x_tiles)
    def _():
        rows_block(meta_ref[...], pl.ds(0, N_META))


def _norm_gates(x2d, meta, norm_w, w2d, gpar):
    rows = x2d.shape[0]
    tm = TM_NORM
    x_tiles = rows // tm
    kern = functools.partial(_norm_gates_kernel, x_tiles=x_tiles, sub=128)
    return pl.pallas_call(
        kern,
        out_shape=(jax.ShapeDtypeStruct((rows + N_META, D_MODEL), BF16),
                   jax.ShapeDtypeStruct((rows + N_META, LANES), F32)),
        grid=(x_tiles + 1,),
        in_specs=[
            pl.BlockSpec((tm, D_MODEL), lambda i: (jnp.minimum(i, x_tiles - 1), 0)),
            pl.BlockSpec((N_META, D_MODEL), lambda i: (0, 0)),
            pl.BlockSpec((1, D_MODEL), lambda i: (0, 0)),
            pl.BlockSpec((D_MODEL, LANES), lambda i: (0, BA_OFFSET // LANES)),
            pl.BlockSpec((2, LANES), lambda i: (0, 0)),
        ],
        out_specs=(pl.BlockSpec((tm, D_MODEL), lambda i: (i, 0)),
                   pl.BlockSpec((tm, LANES), lambda i: (i, 0))),
        compiler_params=pltpu.CompilerParams(
            dimension_semantics=("arbitrary",), vmem_limit_bytes=VMEM_LIMIT),
        name="norm_gates",
    )(x2d, meta, norm_w, w2d, gpar)


def _in_proj_kernel(xn_ref, w_ref, proj_ref, *, rows, row_blk):
    w = w_ref[...].astype(BF16)
    for r in range(rows // row_blk):
        rs = slice(r * row_blk, (r + 1) * row_blk)
        proj_ref[rs, :] = jnp.dot(xn_ref[rs, :], w, preferred_element_type=F32).astype(proj_ref.dtype)
    ms = slice(rows, rows + N_META)
    proj_ref[ms, :] = jnp.dot(xn_ref[ms, :], w, preferred_element_type=F32).astype(proj_ref.dtype)


def _in_proj(xn_all, w2d):
    rows = xn_all.shape[0] - N_META
    cols = w2d.shape[1]
    tn = TN_PROJ
    kern = functools.partial(_in_proj_kernel, rows=rows, row_blk=ROW_BLK_PROJ)
    return pl.pallas_call(
        kern,
        out_shape=jax.ShapeDtypeStruct((rows + N_META, cols), PROJ_DTYPE),
        grid=(pl.cdiv(cols, tn),),
        in_specs=[
            pl.BlockSpec((rows + N_META, D_MODEL), lambda j: (0, 0), pipeline_mode=pl.Buffered(1)),
            pl.BlockSpec((D_MODEL, tn), lambda j: (0, j)),
        ],
        out_specs=pl.BlockSpec((rows + N_META, tn), lambda j: (0, j)),
        compiler_params=pltpu.CompilerParams(
            dimension_semantics=("arbitrary",), vmem_limit_bytes=VMEM_LIMIT),
        name="in_proj",
    )(xn_all, w2d)


PAIR = 2 * DN_HEAD_DIM


def _split3(x):
    hi = x.astype(BF16)
    r1 = x - hi.astype(F32)
    mid = r1.astype(BF16)
    lo = (r1 - mid.astype(F32)).astype(BF16)
    return hi, mid, lo


def _block_diag_rows(y, half):
    lane = lax.broadcasted_iota(jnp.int32, y.shape, 1)
    zero = jnp.zeros_like(y)
    return jnp.concatenate([jnp.where(lane < half, y, zero), jnp.where(lane >= half, y, zero)], axis=0)


def _inverse_masks(c):
    row = lax.broadcasted_iota(jnp.int32, (c, 2 * c), 0)
    col = lax.broadcasted_iota(jnp.int32, (c, 2 * c), 1) & (c - 1)
    blk = lambda x, m: x >> (m.bit_length() - 1)
    eye = (row == col).astype(F32)
    same_leaf = blk(row, LEAF) == blk(col, LEAF)
    offs = []
    m = 2 * LEAF
    while m <= c:
        offs.append((blk(row, m) == blk(col, m)) & (blk(row, m // 2) != blk(col, m // 2)))
        m *= 2
    return eye, same_leaf, offs


def _inv_unit_lower_packed(lps, c, masks):
    eye, same_leaf, offs = masks
    bd = lambda y: _block_diag_rows(y.astype(BF16), c)
    ps = [jnp.where(same_leaf, -l, 0.0) for l in lps]
    ts = [eye + p for p in ps]
    span = 2
    while span < LEAF:
        ps = [_mm(p, bd(p)) for p in ps]
        ts = [t + _mm(t, bd(p)) for t, p in zip(ts, ps)]
        span *= 2
    for off in offs:
        inner = [_mm(jnp.where(off, l, 0.0), bd(t)) for l, t in zip(lps, ts)]
        ts = [t - _mm(t, bd(x)) for t, x in zip(ts, inner)]
    return ts


def _deltanet_kernel(q_ref, k_ref, v_ref, z_ref, gates_ref, mqkv_ref, mgates_ref, cw_ref, dnw_ref,
                     expand_ref, y_ref, stage_ref, gstage_ref, state_ref, *, tt):
    c = CHUNK
    hd = DN_HEAD_DIM
    n_pairs = DN_HEADS // 2
    pad = SUBLANES
    t_idx = pl.program_id(1)
    pairs = range(n_pairs)

    @pl.when(t_idx == 0)
    def _():
        for idx in range(3):
            stage_ref[idx, 0:pad, :] = jnp.zeros((pad, DN_WIDTH), F32)
            stage_ref[idx, pad:pad + c, :] = mqkv_ref[:, idx * DN_WIDTH:(idx + 1) * DN_WIDTH].astype(F32)
        gstage_ref[0:c, :] = mgates_ref[...]
        state_ref[...] = jnp.zeros(state_ref.shape, F32)

    @pl.when(t_idx > 0)
    def _():
        for idx in range(3):
            stage_ref[idx, pad + c - pad:pad + c, :] = stage_ref[idx, pad + c + tt - pad:pad + c + tt, :]

    for idx, x_ref in enumerate((q_ref, k_ref, v_ref)):
        stage_ref[idx, pad + c:pad + c + tt, :] = x_ref[...].astype(F32)
    gstage_ref[c:c + tt, :] = gates_ref[...]

    rowp = lax.broadcasted_iota(jnp.int32, (c, 2 * c), 0)
    colp = lax.broadcasted_iota(jnp.int32, (c, 2 * c), 1) & (c - 1)
    causal = rowp >= colp
    strict = rowp > colp
    lane_p = lax.broadcasted_iota(jnp.int32, (c, 2 * c), 1)
    inv_masks = _inverse_masks(c)
    head_a = (lax.broadcasted_iota(jnp.int32, (c, 2 * PAIR), 1) & (PAIR - 1)) < hd
    tril = (lax.broadcasted_iota(jnp.int32, (c, c), 0) >= lax.broadcasted_iota(jnp.int32, (c, c), 1))
    tril3 = jnp.concatenate([tril.astype(BF16)] * 3, axis=1)
    bd_mask = (lax.broadcasted_iota(jnp.int32, (PAIR, PAIR), 0) < hd) == \
              (lax.broadcasted_iota(jnp.int32, (PAIR, PAIR), 1) < hd)
    dnw = dnw_ref[...]

    def conv_silu(idx, base):
        blk = stage_ref[idx, pl.ds(base, c + pad), :]
        w = cw_ref[:, idx * DN_WIDTH:(idx + 1) * DN_WIDTH]
        y = blk[pad:pad + c] * w[CONV_WIDTH - 1:CONV_WIDTH]
        for kk in range(CONV_WIDTH - 1):
            off = pad - (CONV_WIDTH - 1) + kk
            y = y + blk[off:off + c] * w[kk:kk + 1]
        return _silu(y)

    def l2n_heads(x):
        out = []
        for h in range(DN_HEADS):
            xh = x[:, h * hd:(h + 1) * hd]
            out.append(xh * lax.rsqrt(jnp.sum(xh * xh, axis=-1, keepdims=True) + NORM_EPS))
        return jnp.concatenate(out, axis=1)

    def exact_rows_dot(lhs3, x):
        hi, mid, lo = _split3(x)
        return jnp.dot(lhs3, jnp.concatenate([hi, mid, lo], axis=0), preferred_element_type=F32)

    def chunk_body(j, carry):
        base = pl.multiple_of(j * c, c)
        q = l2n_heads(conv_silu(0, base)) * (hd ** -0.5)
        k = l2n_heads(conv_silu(1, base))
        v = conv_silu(2, base)
        gt = gstage_ref[pl.ds(base, c), :]
        hi, mid, lo = _split3(gt)
        rep = jnp.dot(jnp.concatenate([hi, mid, lo], axis=1), expand_ref[...],
                      preferred_element_type=F32)
        beta_rep = rep[:, :DN_WIDTH]
        g_rep = rep[:, DN_WIDTH:]
        gcum = exact_rows_dot(tril3, g_rep)
        g_last = gcum[c - 1:c, :]
        eg = jnp.exp(gcum)
        kb = k * beta_rep
        vb = v * beta_rep
        kbe = kb * eg
        qd = q * eg
        kd = k * jnp.exp(g_last - gcum)
        egl = jnp.exp(g_last)

        sl = [slice(p * PAIR, (p + 1) * PAIR) for p in pairs]
        aq = [_mm_nt(jnp.concatenate([kb[:, s], q[:, s]], axis=0).astype(BF16),
                     _block_diag_rows(k[:, s].astype(BF16), hd)) for s in sl]
        gp = [jnp.where(lane_p < c, g_rep[:, s][:, :hd], g_rep[:, s][:, hd:]) for s in sl]
        diff = [exact_rows_dot(tril3, jnp.where(strict, g, 0.0)) for g in gp]
        dec = [jnp.where(causal, jnp.exp(d), 0.0) for d in diff]
        lps = [jnp.where(strict, aq[p][:c] * dec[p], 0.0) for p in pairs]
        pps = [aq[p][c:] * dec[p] for p in pairs]
        tps = _inv_unit_lower_packed(lps, c, inv_masks)
        uw = []
        for p in pairs:
            x = jnp.concatenate([vb[:, sl[p]], kbe[:, sl[p]]], axis=1).astype(BF16)
            zero = jnp.zeros_like(x)
            rhs = jnp.concatenate([jnp.where(head_a, x, zero), jnp.where(head_a, zero, x)], axis=0)
            uw.append(_mm(tps[p], rhs))
        s_old = [state_ref[p] for p in pairs]
        ws = [_mm(jnp.concatenate([uw[p][:, PAIR:], qd[:, sl[p]]], axis=0), s_old[p]) for p in pairs]
        v_new = [uw[p][:, :PAIR] - ws[p][:c] for p in pairs]
        upd = [_mm_tn(kd[:, sl[p]], v_new[p]) for p in pairs]
        o = [ws[p][c:] + _mm(pps[p], _block_diag_rows(v_new[p].astype(BF16), hd)) for p in pairs]
        for p in pairs:
            state_ref[p] = s_old[p] * egl[:, sl[p]] + jnp.where(bd_mask, upd[p], 0.0)

        @pl.when(j > 0)
        def _():
            rows = pl.ds(pl.multiple_of((j - 1) * c, c), c)
            for p in pairs:
                for half in range(2):
                    oh = o[p][:, half * hd:(half + 1) * hd]
                    oh = oh * lax.rsqrt(jnp.mean(oh * oh, axis=-1, keepdims=True) + NORM_EPS) * dnw
                    cs = slice(p * PAIR + half * hd, p * PAIR + (half + 1) * hd)
                    zh = z_ref[rows, cs].astype(F32)
                    y_ref[rows, cs] = (oh * _silu(zh)).astype(y_ref.dtype)

        return carry

    start = jnp.where(t_idx == 0, 0, 1)
    lax.fori_loop(start, tt // c + 1, chunk_body, 0)


def _expand_matrix():
    r = jnp.arange(LANES)[:, None]
    cidx = jnp.arange(2 * DN_WIDTH)[None, :]
    src = jnp.where(cidx < DN_WIDTH, cidx // DN_HEAD_DIM, DN_HEADS + (cidx - DN_WIDTH) // DN_HEAD_DIM)
    e = (r == src).astype(BF16)
    return jnp.concatenate([e, e, e], axis=0)


def _deltanet(proj, meta_qkv, gates, meta_gates, conv_w, dn_norm_w, *, batch, seq_len):
    tt = TT_DN
    nt = seq_len // tt
    c = CHUNK
    kern = functools.partial(_deltanet_kernel, tt=tt)
    row = lambda b, t: b * nt + t
    const = lambda shape: pl.BlockSpec(shape, lambda b, t: (0,) * len(shape), pipeline_mode=pl.Buffered(1))
    return pl.pallas_call(
        kern,
        out_shape=jax.ShapeDtypeStruct((batch * seq_len, DN_WIDTH), BF16),
        grid=(batch, nt),
        in_specs=[
            pl.BlockSpec((tt, DN_WIDTH), lambda b, t: (row(b, t), COL_Q // DN_WIDTH)),
            pl.BlockSpec((tt, DN_WIDTH), lambda b, t: (row(b, t), COL_K // DN_WIDTH)),
            pl.BlockSpec((tt, DN_WIDTH), lambda b, t: (row(b, t), COL_V // DN_WIDTH)),
            pl.BlockSpec((tt, DN_WIDTH), lambda b, t: (row(b, t), COL_ZD // DN_WIDTH)),
            pl.BlockSpec((tt, LANES), lambda b, t: (row(b, t), 0)),
            const((c, 3 * DN_WIDTH)),
            const((c, LANES)),
            const((CONV_WIDTH, 3 * DN_WIDTH)),
            const((1, DN_HEAD_DIM)),
            const((3 * LANES, 2 * DN_WIDTH)),
        ],
        out_specs=pl.BlockSpec((tt, DN_WIDTH), lambda b, t: (row(b, t), 0)),
        scratch_shapes=[
            pltpu.VMEM((3, SUBLANES + c + tt, DN_WIDTH), F32),
            pltpu.VMEM((c + tt, LANES), F32),
            pltpu.VMEM((DN_HEADS // 2, PAIR, PAIR), F32),
        ],
        compiler_params=pltpu.CompilerParams(
            dimension_semantics=("arbitrary", "arbitrary"), vmem_limit_bytes=VMEM_LIMIT),
        name="deltanet",
    )(proj, proj, proj, proj, gates, meta_qkv, meta_gates, conv_w, dn_norm_w, _expand_matrix())


def _out_merge_kernel(u_ref, zp_ref, gpa_ref, gpb_ref, gda_ref, gdb_ref, ydn_ref, x_ref, mu_ref, mix_ref,
                      scale_ref, wpo_ref, wdo_ref, wo_ref, fnw_ref, out_ref, ubuf_ref, *, tm):
    t = pl.program_id(1)
    hist = N_META

    @pl.when(t == 0)
    def _():
        ubuf_ref[0:hist, :] = mu_ref[...].astype(F32)

    @pl.when(t > 0)
    def _():
        ubuf_ref[0:hist, :] = ubuf_ref[tm:tm + hist, :]

    ubuf_ref[hist:hist + tm, :] = u_ref[...].astype(F32)

    pooled = []
    for gi, w in enumerate(POOL_WINDOWS):
        cs = slice(gi * POOL_GROUP_DIM, (gi + 1) * POOL_GROUP_DIM)
        acc = ubuf_ref[hist:hist + tm, cs]
        cur = acc
        for s in range(1, w):
            acc = acc + ubuf_ref[hist - s:hist - s + tm, cs]
        d = acc * (1.0 / w) - cur
        mixed = _mm(d, mix_ref[gi])
        zp = zp_ref[:, cs].astype(F32)
        pooled.append((mixed * scale_ref[:, cs] * _silu(zp)).astype(BF16))
    y_pool = jnp.concatenate(pooled, axis=1)

    gp = jnp.concatenate([gpa_ref[...], gpb_ref[...]], axis=1).astype(F32)
    gd = jnp.concatenate([gda_ref[...], gdb_ref[...]], axis=1).astype(F32)
    t_pool = jnp.dot(y_pool, wpo_ref[...], preferred_element_type=F32)
    t_dn = jnp.dot(ydn_ref[...], wdo_ref[...], preferred_element_type=F32)
    col = lax.broadcasted_iota(jnp.int32, gp.shape, 1)
    in_frame = (col >= GATE_SHIFT) & (col < GATE_SHIFT + D_MODEL)
    merged = jnp.where(in_frame, _sigmoid(gp) * t_pool + _sigmoid(gd) * t_dn, 0.0)
    h = x_ref[...] + jnp.dot(merged.astype(BF16), wo_ref[...], preferred_element_type=F32)
    ms = jnp.mean(h * h, axis=-1, keepdims=True)
    out_ref[...] = h * lax.rsqrt(ms + NORM_EPS) * fnw_ref[...]


def _resident(shape, index_map):
    return pl.BlockSpec(shape, index_map, pipeline_mode=pl.Buffered(1))


def _out_merge(proj, y_dn, x2d, meta_u, mix, scale, wpo, wdo, wo, fnw, *, batch, seq_len):
    tm = TM_OUT
    tiles = seq_len // tm
    kern = functools.partial(_out_merge_kernel, tm=tm)
    row = lambda b, t: b * tiles + t
    gp0, gd0 = COL_GP - GATE_SHIFT, COL_GD - GATE_SHIFT
    assert gp0 % D_MODEL == 0 and gd0 % D_MODEL == 0
    return pl.pallas_call(
        kern,
        out_shape=jax.ShapeDtypeStruct((batch * seq_len, D_MODEL), F32),
        grid=(batch, tiles),
        in_specs=[
            pl.BlockSpec((tm, POOL_WIDTH), lambda b, t: (row(b, t), COL_U // POOL_WIDTH)),
            pl.BlockSpec((tm, POOL_WIDTH), lambda b, t: (row(b, t), COL_ZP // POOL_WIDTH)),
            pl.BlockSpec((tm, D_MODEL), lambda b, t: (row(b, t), gp0 // D_MODEL)),
            pl.BlockSpec((tm, LANES), lambda b, t: (row(b, t), (gp0 + D_MODEL) // LANES)),
            pl.BlockSpec((tm, D_MODEL), lambda b, t: (row(b, t), gd0 // D_MODEL)),
            pl.BlockSpec((tm, LANES), lambda b, t: (row(b, t), (gd0 + D_MODEL) // LANES)),
            pl.BlockSpec((tm, DN_WIDTH), lambda b, t: (row(b, t), 0)),
            pl.BlockSpec((tm, D_MODEL), lambda b, t: (row(b, t), 0)),
            _resident((N_META, POOL_WIDTH), lambda b, t: (0, 0)),
            _resident((POOL_GROUPS, POOL_GROUP_DIM, POOL_GROUP_DIM), lambda b, t: (0, 0, 0)),
            _resident((1, POOL_WIDTH), lambda b, t: (0, 0)),
            _resident((POOL_WIDTH, GATE_FRAME), lambda b, t: (0, 0)),
            _resident((DN_WIDTH, GATE_FRAME), lambda b, t: (0, 0)),
            _resident((GATE_FRAME, D_MODEL), lambda b, t: (0, 0)),
            _resident((1, D_MODEL), lambda b, t: (0, 0)),
        ],
        out_specs=pl.BlockSpec((tm, D_MODEL), lambda b, t: (row(b, t), 0)),
        scratch_shapes=[pltpu.VMEM((N_META + tm, POOL_WIDTH), F32)],
        compiler_params=pltpu.CompilerParams(
            dimension_semantics=("arbitrary", "arbitrary"), vmem_limit_bytes=VMEM_LIMIT),
        name="out_merge",
    )(proj, proj, proj, proj, proj, proj, y_dn, x2d, meta_u, mix, scale, wpo, wdo, wo, fnw)


def kernel(x, meta_tokens, norm_w, w_in, conv_w, A_log, dt_bias, pool_mix, pool_scale, dn_norm_w,
           w_pool_out, w_dn_out, w_o, final_norm_w):
    batch, seq_len, _ = x.shape
    assert norm_w.shape[0] == 1, "single layer block"
    x2d = x.reshape(batch * seq_len, D_MODEL)

    rows = batch * seq_len
    w2d = w_in[0]
    gpar = jnp.zeros((2, LANES), F32)
    gpar = gpar.at[0, DN_HEADS:2 * DN_HEADS].set(A_log[0]).at[1, DN_HEADS:2 * DN_HEADS].set(dt_bias[0])
    nw = norm_w[0].reshape(1, D_MODEL)
    frame = (GATE_SHIFT, GATE_FRAME - D_MODEL - GATE_SHIFT)
    wpo = jnp.pad(w_pool_out[0].astype(BF16), ((0, 0), frame))
    wdo = jnp.pad(w_dn_out[0].astype(BF16), ((0, 0), frame))
    wo = jnp.pad(w_o[0].astype(BF16), (frame, (0, 0)))

    xn_all, gates = _norm_gates(x2d, meta_tokens.astype(F32), nw, w2d, gpar)
    proj = _in_proj(xn_all, w2d)

    front = CHUNK - N_META
    meta_proj = proj[rows:]
    meta_qkv = jnp.pad(meta_proj[:, COL_Q:COL_ZD], ((front, 0), (0, 0)))
    meta_gates = jnp.pad(gates[rows:], ((front, 0), (0, 0)))
    meta_u = meta_proj[:, COL_U:COL_U + POOL_WIDTH]

    y_dn = _deltanet(proj, meta_qkv, gates, meta_gates, conv_w[0], dn_norm_w[0].reshape(1, DN_HEAD_DIM),
                     batch=batch, seq_len=seq_len)

    out = _out_merge(proj, y_dn, x2d, meta_u, pool_mix[0].astype(BF16), pool_scale[0].reshape(1, POOL_WIDTH),
                     wpo, wdo, wo, final_norm_w.reshape(1, D_MODEL), batch=batch, seq_len=seq_len)
    return out.reshape(batch, seq_len, D_MODEL)
```

```python
import functools

import jax
import jax.numpy as jnp
from jax import lax
from jax.experimental import pallas as pl
from jax.experimental.pallas import tpu as pltpu

D_MODEL = 2048
N_META = 16
POOL_GROUPS = 4
POOL_WINDOWS = (2, 4, 8, 16)
POOL_WIDTH = D_MODEL // 2
POOL_GROUP_DIM = POOL_WIDTH // POOL_GROUPS
DN_HEADS = 16
DN_HEAD_DIM = 128
DN_WIDTH = DN_HEADS * DN_HEAD_DIM
CONV_WIDTH = 4
NORM_EPS = 1e-6

COL_U = 0
COL_ZP = COL_U + POOL_WIDTH
COL_Q = COL_ZP + POOL_WIDTH
COL_K = COL_Q + DN_WIDTH
COL_V = COL_K + DN_WIDTH
COL_ZD = COL_V + DN_WIDTH
BA_OFFSET = COL_ZD + DN_WIDTH
COL_GP = BA_OFFSET + 2 * DN_HEADS
COL_GD = COL_GP + D_MODEL
LANES = 128
SUBLANES = 8
GATE_SHIFT = COL_GP % LANES

PROJ_DTYPE = jnp.bfloat16
CHUNK = 64
LEAF = 16
TT_DN = 512
TM_NORM = 512
TN_PROJ = 256
ROW_BLK_PROJ = 2048
TM_OUT = 256
VMEM_LIMIT = 56 * 1024 * 1024

BF16 = jnp.bfloat16
F32 = jnp.float32


def _mm(a, b):
    return jnp.dot(a.astype(BF16), b.astype(BF16), preferred_element_type=F32)


def _mm_nt(a, b):
    return lax.dot_general(a.astype(BF16), b.astype(BF16), (((1,), (1,)), ((), ())),
                           preferred_element_type=F32)


def _mm_tn(a, b):
    return lax.dot_general(a.astype(BF16), b.astype(BF16), (((0,), (0,)), ((), ())),
                           preferred_element_type=F32)


def _sigmoid(x):
    return 1.0 / (1.0 + jnp.exp(-x))


def _silu(x):
    return x * _sigmoid(x)


def _softplus(x):
    return jnp.maximum(x, 0.0) + jnp.log1p(jnp.exp(-jnp.abs(x)))


def _norm_gates_kernel(x_ref, meta_ref, nw_ref, wba_ref, gpar_ref, xn_ref, gates_ref, *, x_tiles, sub):
    i = pl.program_id(0)
    nw = nw_ref[...]
    wba = wba_ref[...].astype(BF16)
    a_rate = jnp.exp(gpar_ref[0:1, :])
    dt_bias = gpar_ref[1:2, :]

    def rows_block(x, rows):
        ms = jnp.mean(x * x, axis=-1, keepdims=True)
        xn = (x * lax.rsqrt(ms + NORM_EPS) * nw).astype(BF16)
        xn_ref[rows, :] = xn
        ba = _mm_nt(xn, wba)
        lane = lax.broadcasted_iota(jnp.int32, ba.shape, 1)
        g = -a_rate * _softplus(ba + dt_bias)
        gates_ref[rows, :] = jnp.where(lane < DN_HEADS, _sigmoid(ba), jnp.where(lane < 2 * DN_HEADS, g, 0.0))

    @pl.when(i < x_tiles)
    def _():
        def body(r, carry):
            rows = pl.ds(pl.multiple_of(r * sub, sub), sub)
            rows_block(x_ref[rows, :], rows)
            return carry

        lax.fori_loop(0, x_ref.shape[0] // sub, body, 0)

    @pl.when(i == x_tiles)
    def _():
        rows_block(meta_ref[...], pl.ds(0, N_META))


def _norm_gates(x2d, meta, norm_w, w_t, gpar):
    rows = x2d.shape[0]
    tm = TM_NORM
    x_tiles = rows // tm
    kern = functools.partial(_norm_gates_kernel, x_tiles=x_tiles, sub=128)
    return pl.pallas_call(
        kern,
        out_shape=(jax.ShapeDtypeStruct((rows + N_META, D_MODEL), BF16),
                   jax.ShapeDtypeStruct((rows + N_META, LANES), F32)),
        grid=(x_tiles + 1,),
        in_specs=[
            pl.BlockSpec((tm, D_MODEL), lambda i: (jnp.minimum(i, x_tiles - 1), 0)),
            pl.BlockSpec((N_META, D_MODEL), lambda i: (0, 0)),
            pl.BlockSpec((1, D_MODEL), lambda i: (0, 0)),
            pl.BlockSpec((LANES, D_MODEL), lambda i: (BA_OFFSET // LANES, 0)),
            pl.BlockSpec((2, LANES), lambda i: (0, 0)),
        ],
        out_specs=(pl.BlockSpec((tm, D_MODEL), lambda i: (i, 0)),
                   pl.BlockSpec((tm, LANES), lambda i: (i, 0))),
        compiler_params=pltpu.CompilerParams(
            dimension_semantics=("arbitrary",), vmem_limit_bytes=VMEM_LIMIT),
        name="norm_gates",
    )(x2d, meta, norm_w, w_t, gpar)


def _in_proj_kernel(xn_ref, w_ref, proj_ref, *, rows, row_blk):
    w = w_ref[...].astype(BF16)
    for r in range(rows // row_blk):
        rs = slice(r * row_blk, (r + 1) * row_blk)
        proj_ref[rs, :] = _mm_nt(xn_ref[rs, :], w).astype(proj_ref.dtype)
    ms = slice(rows, rows + N_META)
    proj_ref[ms, :] = _mm_nt(xn_ref[ms, :], w).astype(proj_ref.dtype)


def _in_proj(xn_all, w_t):
    rows = xn_all.shape[0] - N_META
    cols = w_t.shape[0]
    tn = TN_PROJ
    kern = functools.partial(_in_proj_kernel, rows=rows, row_blk=ROW_BLK_PROJ)
    return pl.pallas_call(
        kern,
        out_shape=jax.ShapeDtypeStruct((rows + N_META, cols), PROJ_DTYPE),
        grid=(pl.cdiv(cols, tn),),
        in_specs=[
            pl.BlockSpec((rows + N_META, D_MODEL), lambda j: (0, 0), pipeline_mode=pl.Buffered(1)),
            pl.BlockSpec((tn, D_MODEL), lambda j: (j, 0)),
        ],
        out_specs=pl.BlockSpec((rows + N_META, tn), lambda j: (0, j)),
        compiler_params=pltpu.CompilerParams(
            dimension_semantics=("arbitrary",), vmem_limit_bytes=VMEM_LIMIT),
        name="in_proj",
    )(xn_all, w_t)


PAIR = 2 * DN_HEAD_DIM


def _split3(x):
    hi = x.astype(BF16)
    r1 = x - hi.astype(F32)
    mid = r1.astype(BF16)
    lo = (r1 - mid.astype(F32)).astype(BF16)
    return hi, mid, lo


def _block_diag_rows(y, half):
    lane = lax.broadcasted_iota(jnp.int32, y.shape, 1)
    zero = jnp.zeros_like(y)
    return jnp.concatenate([jnp.where(lane < half, y, zero), jnp.where(lane >= half, y, zero)], axis=0)


def _inverse_masks(c):
    row = lax.broadcasted_iota(jnp.int32, (c, 2 * c), 0)
    col = lax.broadcasted_iota(jnp.int32, (c, 2 * c), 1) & (c - 1)
    blk = lambda x, m: x >> (m.bit_length() - 1)
    eye = (row == col).astype(F32)
    same_leaf = blk(row, LEAF) == blk(col, LEAF)
    offs = []
    m = 2 * LEAF
    while m <= c:
        offs.append((blk(row, m) == blk(col, m)) & (blk(row, m // 2) != blk(col, m // 2)))
        m *= 2
    return eye, same_leaf, offs


def _inv_unit_lower_packed(lps, c, masks):
    eye, same_leaf, offs = masks
    bd = lambda y: _block_diag_rows(y.astype(BF16), c)
    ps = [jnp.where(same_leaf, -l, 0.0) for l in lps]
    ts = [eye + p for p in ps]
    span = 2
    while span < LEAF:
        ps = [_mm(p, bd(p)) for p in ps]
        ts = [t + _mm(t, bd(p)) for t, p in zip(ts, ps)]
        span *= 2
    for off in offs:
        inner = [_mm(jnp.where(off, l, 0.0), bd(t)) for l, t in zip(lps, ts)]
        ts = [t - _mm(t, bd(x)) for t, x in zip(ts, inner)]
    return ts


def _deltanet_kernel(q_ref, k_ref, v_ref, z_ref, gates_ref, mqkv_ref, mgates_ref, cw_ref, dnw_ref,
                     expand_ref, y_ref, stage_ref, gstage_ref, state_ref, *, tt):
    c = CHUNK
    hd = DN_HEAD_DIM
    n_pairs = DN_HEADS // 2
    pad = SUBLANES
    t_idx = pl.program_id(1)
    pairs = range(n_pairs)

    @pl.when(t_idx == 0)
    def _():
        for idx in range(3):
            stage_ref[idx, 0:pad, :] = jnp.zeros((pad, DN_WIDTH), F32)
            stage_ref[idx, pad:pad + c, :] = mqkv_ref[:, idx * DN_WIDTH:(idx + 1) * DN_WIDTH].astype(F32)
        gstage_ref[0:c, :] = mgates_ref[...]
        state_ref[...] = jnp.zeros(state_ref.shape, F32)

    @pl.when(t_idx > 0)
    def _():
        for idx in range(3):
            stage_ref[idx, pad + c - pad:pad + c, :] = stage_ref[idx, pad + c + tt - pad:pad + c + tt, :]

    for idx, x_ref in enumerate((q_ref, k_ref, v_ref)):
        stage_ref[idx, pad + c:pad + c + tt, :] = x_ref[...].astype(F32)
    gstage_ref[c:c + tt, :] = gates_ref[...]

    rowp = lax.broadcasted_iota(jnp.int32, (c, 2 * c), 0)
    colp = lax.broadcasted_iota(jnp.int32, (c, 2 * c), 1) & (c - 1)
    causal = rowp >= colp
    strict = rowp > colp
    lane_p = lax.broadcasted_iota(jnp.int32, (c, 2 * c), 1)
    inv_masks = _inverse_masks(c)
    head_a = (lax.broadcasted_iota(jnp.int32, (c, 2 * PAIR), 1) & (PAIR - 1)) < hd
    tril = (lax.broadcasted_iota(jnp.int32, (c, c), 0) >= lax.broadcasted_iota(jnp.int32, (c, c), 1))
    tril3 = jnp.concatenate([tril.astype(BF16)] * 3, axis=1)
    bd_mask = (lax.broadcasted_iota(jnp.int32, (PAIR, PAIR), 0) < hd) == \
              (lax.broadcasted_iota(jnp.int32, (PAIR, PAIR), 1) < hd)
    dnw = dnw_ref[...]

    def conv_silu(idx, base):
        blk = stage_ref[idx, pl.ds(base, c + pad), :]
        w = cw_ref[:, idx * DN_WIDTH:(idx + 1) * DN_WIDTH]
        y = blk[pad:pad + c] * w[CONV_WIDTH - 1:CONV_WIDTH]
        for kk in range(CONV_WIDTH - 1):
            off = pad - (CONV_WIDTH - 1) + kk
            y = y + blk[off:off + c] * w[kk:kk + 1]
        return _silu(y)

    def l2n_heads(x):
        out = []
        for h in range(DN_HEADS):
            xh = x[:, h * hd:(h + 1) * hd]
            out.append(xh * lax.rsqrt(jnp.sum(xh * xh, axis=-1, keepdims=True) + NORM_EPS))
        return jnp.concatenate(out, axis=1)

    def exact_rows_dot(lhs3, x):
        hi, mid, lo = _split3(x)
        return jnp.dot(lhs3, jnp.concatenate([hi, mid, lo], axis=0), preferred_element_type=F32)

    def chunk_body(j, carry):
        base = pl.multiple_of(j * c, c)
        q = l2n_heads(conv_silu(0, base)) * (hd ** -0.5)
        k = l2n_heads(conv_silu(1, base))
        v = conv_silu(2, base)
        gt = gstage_ref[pl.ds(base, c), :]
        hi, mid, lo = _split3(gt)
        rep = jnp.dot(jnp.concatenate([hi, mid, lo], axis=1), expand_ref[...],
                      preferred_element_type=F32)
        beta_rep = rep[:, :DN_WIDTH]
        g_rep = rep[:, DN_WIDTH:]
        gcum = exact_rows_dot(tril3, g_rep)
        g_last = gcum[c - 1:c, :]
        eg = jnp.exp(gcum)
        kb = k * beta_rep
        vb = v * beta_rep
        kbe = kb * eg
        qd = q * eg
        kd = k * jnp.exp(g_last - gcum)
        egl = jnp.exp(g_last)

        sl = [slice(p * PAIR, (p + 1) * PAIR) for p in pairs]
        aq = [_mm_nt(jnp.concatenate([kb[:, s], q[:, s]], axis=0).astype(BF16),
                     _block_diag_rows(k[:, s].astype(BF16), hd)) for s in sl]
        gp = [jnp.where(lane_p < c, g_rep[:, s][:, :hd], g_rep[:, s][:, hd:]) for s in sl]
        diff = [exact_rows_dot(tril3, jnp.where(strict, g, 0.0)) for g in gp]
        dec = [jnp.where(causal, jnp.exp(d), 0.0) for d in diff]
        lps = [jnp.where(strict, aq[p][:c] * dec[p], 0.0) for p in pairs]
        pps = [aq[p][c:] * dec[p] for p in pairs]
        tps = _inv_unit_lower_packed(lps, c, inv_masks)
        uw = []
        for p in pairs:
            x = jnp.concatenate([vb[:, sl[p]], kbe[:, sl[p]]], axis=1).astype(BF16)
            zero = jnp.zeros_like(x)
            rhs = jnp.concatenate([jnp.where(head_a, x, zero), jnp.where(head_a, zero, x)], axis=0)
            uw.append(_mm(tps[p], rhs))
        s_old = [state_ref[p] for p in pairs]
        ws = [_mm(jnp.concatenate([uw[p][:, PAIR:], qd[:, sl[p]]], axis=0), s_old[p]) for p in pairs]
        v_new = [uw[p][:, :PAIR] - ws[p][:c] for p in pairs]
        upd = [_mm_tn(kd[:, sl[p]], v_new[p]) for p in pairs]
        o = [ws[p][c:] + _mm(pps[p], _block_diag_rows(v_new[p].astype(BF16), hd)) for p in pairs]
        for p in pairs:
            state_ref[p] = s_old[p] * egl[:, sl[p]] + jnp.where(bd_mask, upd[p], 0.0)

        @pl.when(j > 0)
        def _():
            rows = pl.ds(pl.multiple_of((j - 1) * c, c), c)
            for p in pairs:
                for half in range(2):
                    oh = o[p][:, half * hd:(half + 1) * hd]
                    oh = oh * lax.rsqrt(jnp.mean(oh * oh, axis=-1, keepdims=True) + NORM_EPS) * dnw
                    cs = slice(p * PAIR + half * hd, p * PAIR + (half + 1) * hd)
                    zh = z_ref[rows, cs].astype(F32)
                    y_ref[rows, cs] = (oh * _silu(zh)).astype(y_ref.dtype)

        return carry

    start = jnp.where(t_idx == 0, 0, 1)
    lax.fori_loop(start, tt // c + 1, chunk_body, 0)


def _expand_matrix():
    r = jnp.arange(LANES)[:, None]
    cidx = jnp.arange(2 * DN_WIDTH)[None, :]
    src = jnp.where(cidx < DN_WIDTH, cidx // DN_HEAD_DIM, DN_HEADS + (cidx - DN_WIDTH) // DN_HEAD_DIM)
    e = (r == src).astype(BF16)
    return jnp.concatenate([e, e, e], axis=0)


def _deltanet(proj, meta_qkv, gates, meta_gates, conv_w, dn_norm_w, *, batch, seq_len):
    tt = TT_DN
    nt = seq_len // tt
    c = CHUNK
    kern = functools.partial(_deltanet_kernel, tt=tt)
    row = lambda b, t: b * nt + t
    const = lambda shape: pl.BlockSpec(shape, lambda b, t: (0,) * len(shape), pipeline_mode=pl.Buffered(1))
    return pl.pallas_call(
        kern,
        out_shape=jax.ShapeDtypeStruct((batch * seq_len, DN_WIDTH), BF16),
        grid=(batch, nt),
        in_specs=[
            pl.BlockSpec((tt, DN_WIDTH), lambda b, t: (row(b, t), COL_Q // DN_WIDTH)),
            pl.BlockSpec((tt, DN_WIDTH), lambda b, t: (row(b, t), COL_K // DN_WIDTH)),
            pl.BlockSpec((tt, DN_WIDTH), lambda b, t: (row(b, t), COL_V // DN_WIDTH)),
            pl.BlockSpec((tt, DN_WIDTH), lambda b, t: (row(b, t), COL_ZD // DN_WIDTH)),
            pl.BlockSpec((tt, LANES), lambda b, t: (row(b, t), 0)),
            const((c, 3 * DN_WIDTH)),
            const((c, LANES)),
            const((CONV_WIDTH, 3 * DN_WIDTH)),
            const((1, DN_HEAD_DIM)),
            const((3 * LANES, 2 * DN_WIDTH)),
        ],
        out_specs=pl.BlockSpec((tt, DN_WIDTH), lambda b, t: (row(b, t), 0)),
        scratch_shapes=[
            pltpu.VMEM((3, SUBLANES + c + tt, DN_WIDTH), F32),
            pltpu.VMEM((c + tt, LANES), F32),
            pltpu.VMEM((DN_HEADS // 2, PAIR, PAIR), F32),
        ],
        compiler_params=pltpu.CompilerParams(
            dimension_semantics=("arbitrary", "arbitrary"), vmem_limit_bytes=VMEM_LIMIT),
        name="deltanet",
    )(proj, proj, proj, proj, gates, meta_qkv, meta_gates, conv_w, dn_norm_w, _expand_matrix())


def _out_merge_kernel(u_ref, zp_ref, gpa_ref, gpb_ref, gda_ref, gdb_ref, ydn_ref, x_ref, mu_ref, mix_ref,
                      scale_ref, wpo_ref, wdo_ref, wo_ref, fnw_ref, out_ref, ubuf_ref, *, tm):
    t = pl.program_id(1)
    hist = N_META

    @pl.when(t == 0)
    def _():
        ubuf_ref[0:hist, :] = mu_ref[...].astype(F32)

    @pl.when(t > 0)
    def _():
        ubuf_ref[0:hist, :] = ubuf_ref[tm:tm + hist, :]

    ubuf_ref[hist:hist + tm, :] = u_ref[...].astype(F32)

    pooled = []
    for gi, w in enumerate(POOL_WINDOWS):
        cs = slice(gi * POOL_GROUP_DIM, (gi + 1) * POOL_GROUP_DIM)
        acc = ubuf_ref[hist:hist + tm, cs]
        cur = acc
        for s in range(1, w):
            acc = acc + ubuf_ref[hist - s:hist - s + tm, cs]
        d = acc * (1.0 / w) - cur
        mixed = _mm(d, mix_ref[gi])
        zp = zp_ref[:, cs].astype(F32)
        pooled.append((mixed * scale_ref[:, cs] * _silu(zp)).astype(BF16))
    y_pool = jnp.concatenate(pooled, axis=1)

    frame = slice(GATE_SHIFT, GATE_SHIFT + D_MODEL)
    gp = jnp.concatenate([gpa_ref[...], gpb_ref[...]], axis=1).astype(F32)[:, frame]
    gd = jnp.concatenate([gda_ref[...], gdb_ref[...]], axis=1).astype(F32)[:, frame]
    t_pool = jnp.dot(y_pool, wpo_ref[...], preferred_element_type=F32)
    t_dn = jnp.dot(ydn_ref[...], wdo_ref[...], preferred_element_type=F32)
    merged = _sigmoid(gp) * t_pool + _sigmoid(gd) * t_dn
    h = x_ref[...] + jnp.dot(merged.astype(BF16), wo_ref[...], preferred_element_type=F32)
    ms = jnp.mean(h * h, axis=-1, keepdims=True)
    out_ref[...] = h * lax.rsqrt(ms + NORM_EPS) * fnw_ref[...]


def _resident(shape, index_map):
    return pl.BlockSpec(shape, index_map, pipeline_mode=pl.Buffered(1))


def _out_merge(proj, y_dn, x2d, meta_u, mix, scale, wpo, wdo, wo, fnw, *, batch, seq_len):
    tm = TM_OUT
    tiles = seq_len // tm
    kern = functools.partial(_out_merge_kernel, tm=tm)
    row = lambda b, t: b * tiles + t
    gp0, gd0 = COL_GP - GATE_SHIFT, COL_GD - GATE_SHIFT
    assert gp0 % D_MODEL == 0 and gd0 % D_MODEL == 0
    return pl.pallas_call(
        kern,
        out_shape=jax.ShapeDtypeStruct((batch * seq_len, D_MODEL), F32),
        grid=(batch, tiles),
        in_specs=[
            pl.BlockSpec((tm, POOL_WIDTH), lambda b, t: (row(b, t), COL_U // POOL_WIDTH)),
            pl.BlockSpec((tm, POOL_WIDTH), lambda b, t: (row(b, t), COL_ZP // POOL_WIDTH)),
            pl.BlockSpec((tm, D_MODEL), lambda b, t: (row(b, t), gp0 // D_MODEL)),
            pl.BlockSpec((tm, LANES), lambda b, t: (row(b, t), (gp0 + D_MODEL) // LANES)),
            pl.BlockSpec((tm, D_MODEL), lambda b, t: (row(b, t), gd0 // D_MODEL)),
            pl.BlockSpec((tm, LANES), lambda b, t: (row(b, t), (gd0 + D_MODEL) // LANES)),
            pl.BlockSpec((tm, DN_WIDTH), lambda b, t: (row(b, t), 0)),
            pl.BlockSpec((tm, D_MODEL), lambda b, t: (row(b, t), 0)),
            _resident((N_META, POOL_WIDTH), lambda b, t: (0, 0)),
            _resident((POOL_GROUPS, POOL_GROUP_DIM, POOL_GROUP_DIM), lambda b, t: (0, 0, 0)),
            _resident((1, POOL_WIDTH), lambda b, t: (0, 0)),
            _resident((POOL_WIDTH, D_MODEL), lambda b, t: (0, 0)),
            _resident((DN_WIDTH, D_MODEL), lambda b, t: (0, 0)),
            _resident((D_MODEL, D_MODEL), lambda b, t: (0, 0)),
            _resident((1, D_MODEL), lambda b, t: (0, 0)),
        ],
        out_specs=pl.BlockSpec((tm, D_MODEL), lambda b, t: (row(b, t), 0)),
        scratch_shapes=[pltpu.VMEM((N_META + tm, POOL_WIDTH), F32)],
        compiler_params=pltpu.CompilerParams(
            dimension_semantics=("arbitrary", "arbitrary"), vmem_limit_bytes=VMEM_LIMIT),
        name="out_merge",
    )(proj, proj, proj, proj, proj, proj, y_dn, x2d, meta_u, mix, scale, wpo, wdo, wo, fnw)


def kernel(x, meta_tokens, norm_w, w_in, conv_w, A_log, dt_bias, pool_mix, pool_scale, dn_norm_w,
           w_pool_out, w_dn_out, w_o, final_norm_w):
    batch, seq_len, _ = x.shape
    assert norm_w.shape[0] == 1, "single layer block"
    x2d = x.reshape(batch * seq_len, D_MODEL)

    rows = batch * seq_len
    w_t = jnp.swapaxes(w_in, 1, 2).reshape(w_in.shape[2], D_MODEL)
    gpar = jnp.zeros((2, LANES), F32)
    gpar = gpar.at[0, DN_HEADS:2 * DN_HEADS].set(A_log[0]).at[1, DN_HEADS:2 * DN_HEADS].set(dt_bias[0])
    nw = norm_w[0].reshape(1, D_MODEL)
    wpo = w_pool_out[0].astype(BF16)
    wdo = w_dn_out[0].astype(BF16)
    wo = w_o[0].astype(BF16)

    xn_all, gates = _norm_gates(x2d, meta_tokens.astype(F32), nw, w_t, gpar)
    proj = _in_proj(xn_all, w_t)

    front = CHUNK - N_META
    meta_proj = proj[rows:]
    meta_qkv = jnp.pad(meta_proj[:, COL_Q:COL_ZD], ((front, 0), (0, 0)))
    meta_gates = jnp.pad(gates[rows:], ((front, 0), (0, 0)))
    meta_u = meta_proj[:, COL_U:COL_U + POOL_WIDTH]

    y_dn = _deltanet(proj, meta_qkv, gates, meta_gates, conv_w[0], dn_norm_w[0].reshape(1, DN_HEAD_DIM),
                     batch=batch, seq_len=seq_len)

    out = _out_merge(proj, y_dn, x2d, meta_u, pool_mix[0].astype(BF16), pool_scale[0].reshape(1, POOL_WIDTH),
                     wpo, wdo, wo, final_norm_w.reshape(1, D_MODEL), batch=batch, seq_len=seq_len)
    return out.reshape(batch, seq_len, D_MODEL)
```

```python
import functools

import jax
import jax.numpy as jnp
from jax import lax
from jax.experimental import pallas as pl
from jax.experimental.pallas import tpu as pltpu

D_MODEL = 2048
N_META = 16
POOL_GROUPS = 4
POOL_WINDOWS = (2, 4, 8, 16)
POOL_WIDTH = D_MODEL // 2
POOL_GROUP_DIM = POOL_WIDTH // POOL_GROUPS
DN_HEADS = 16
DN_HEAD_DIM = 128
DN_WIDTH = DN_HEADS * DN_HEAD_DIM
CONV_WIDTH = 4
NORM_EPS = 1e-6

COL_U = 0
COL_ZP = COL_U + POOL_WIDTH
COL_Q = COL_ZP + POOL_WIDTH
COL_K = COL_Q + DN_WIDTH
COL_V = COL_K + DN_WIDTH
COL_ZD = COL_V + DN_WIDTH
BA_OFFSET = COL_ZD + DN_WIDTH
COL_GP = BA_OFFSET + 2 * DN_HEADS
COL_GD = COL_GP + D_MODEL
LANES = 128
SUBLANES = 8
GATE_SHIFT = COL_GP % LANES

PROJ_DTYPE = jnp.bfloat16
CHUNK = 64
LEAF = 16
TT_DN = 512
TM_NORM = 512
TN_PROJ = 256
ROW_BLK_PROJ = 2048
TM_OUT = 256
VMEM_LIMIT = 56 * 1024 * 1024

BF16 = jnp.bfloat16
F32 = jnp.float32


def _mm(a, b):
    return jnp.dot(a.astype(BF16), b.astype(BF16), preferred_element_type=F32)


def _mm_nt(a, b):
    return lax.dot_general(a.astype(BF16), b.astype(BF16), (((1,), (1,)), ((), ())),
                           preferred_element_type=F32)


def _mm_tn(a, b):
    return lax.dot_general(a.astype(BF16), b.astype(BF16), (((0,), (0,)), ((), ())),
                           preferred_element_type=F32)


def _sigmoid(x):
    return 0.5 * jnp.tanh(0.5 * x) + 0.5


def _silu(x):
    h = 0.5 * x
    return h * jnp.tanh(h) + h


def _softplus(x):
    return jnp.maximum(x, 0.0) + jnp.log1p(jnp.exp(-jnp.abs(x)))


def _norm_gates_kernel(x_ref, meta_ref, nw_ref, wba_ref, gpar_ref, xn_ref, gates_ref, *, x_tiles, sub):
    i = pl.program_id(0)
    nw = nw_ref[...]
    wba = wba_ref[...].astype(BF16)
    a_rate = jnp.exp(gpar_ref[0:1, :])
    dt_bias = gpar_ref[1:2, :]

    def rows_block(x, rows):
        ms = jnp.mean(x * x, axis=-1, keepdims=True)
        xn = (x * lax.rsqrt(ms + NORM_EPS) * nw).astype(BF16)
        xn_ref[rows, :] = xn
        ba = _mm_nt(xn, wba)
        lane = lax.broadcasted_iota(jnp.int32, ba.shape, 1)
        g = -a_rate * _softplus(ba + dt_bias)
        gates_ref[rows, :] = jnp.where(lane < DN_HEADS, _sigmoid(ba), jnp.where(lane < 2 * DN_HEADS, g, 0.0))

    @pl.when(i < x_tiles)
    def _():
        def body(r, carry):
            rows = pl.ds(pl.multiple_of(r * sub, sub), sub)
            rows_block(x_ref[rows, :], rows)
            return carry

        lax.fori_loop(0, x_ref.shape[0] // sub, body, 0)

    @pl.when(i == x_tiles)
    def _():
        rows_block(meta_ref[...], pl.ds(0, N_META))


def _norm_gates(x2d, meta, norm_w, w_t, gpar):
    rows = x2d.shape[0]
    tm = TM_NORM
    x_tiles = rows // tm
    kern = functools.partial(_norm_gates_kernel, x_tiles=x_tiles, sub=128)
    return pl.pallas_call(
        kern,
        out_shape=(jax.ShapeDtypeStruct((rows + N_META, D_MODEL), BF16),
                   jax.ShapeDtypeStruct((rows + N_META, LANES), F32)),
        grid=(x_tiles + 1,),
        in_specs=[
            pl.BlockSpec((tm, D_MODEL), lambda i: (jnp.minimum(i, x_tiles - 1), 0)),
            pl.BlockSpec((N_META, D_MODEL), lambda i: (0, 0)),
            pl.BlockSpec((1, D_MODEL), lambda i: (0, 0)),
            pl.BlockSpec((LANES, D_MODEL), lambda i: (BA_OFFSET // LANES, 0)),
            pl.BlockSpec((2, LANES), lambda i: (0, 0)),
        ],
        out_specs=(pl.BlockSpec((tm, D_MODEL), lambda i: (i, 0)),
                   pl.BlockSpec((tm, LANES), lambda i: (i, 0))),
        compiler_params=pltpu.CompilerParams(
            dimension_semantics=("arbitrary",), vmem_limit_bytes=VMEM_LIMIT),
        name="norm_gates",
    )(x2d, meta, norm_w, w_t, gpar)


def _in_proj_kernel(xn_ref, w_ref, proj_ref, *, rows, row_blk):
    w = w_ref[...].astype(BF16)
    for r in range(rows // row_blk):
        rs = slice(r * row_blk, (r + 1) * row_blk)
        proj_ref[rs, :] = _mm_nt(xn_ref[rs, :], w).astype(proj_ref.dtype)
    ms = slice(rows, rows + N_META)
    proj_ref[ms, :] = _mm_nt(xn_ref[ms, :], w).astype(proj_ref.dtype)


def _in_proj(xn_all, w_t):
    rows = xn_all.shape[0] - N_META
    cols = w_t.shape[0]
    tn = TN_PROJ
    kern = functools.partial(_in_proj_kernel, rows=rows, row_blk=ROW_BLK_PROJ)
    return pl.pallas_call(
        kern,
        out_shape=jax.ShapeDtypeStruct((rows + N_META, cols), PROJ_DTYPE),
        grid=(pl.cdiv(cols, tn),),
        in_specs=[
            pl.BlockSpec((rows + N_META, D_MODEL), lambda j: (0, 0), pipeline_mode=pl.Buffered(1)),
            pl.BlockSpec((tn, D_MODEL), lambda j: (j, 0)),
        ],
        out_specs=pl.BlockSpec((rows + N_META, tn), lambda j: (0, j)),
        compiler_params=pltpu.CompilerParams(
            dimension_semantics=("arbitrary",), vmem_limit_bytes=VMEM_LIMIT),
        name="in_proj",
    )(xn_all, w_t)


PAIR = 2 * DN_HEAD_DIM


def _split3(x):
    hi = x.astype(BF16)
    r1 = x - hi.astype(F32)
    mid = r1.astype(BF16)
    lo = (r1 - mid.astype(F32)).astype(BF16)
    return hi, mid, lo


def _block_diag_rows(y, half):
    lane = lax.broadcasted_iota(jnp.int32, y.shape, 1)
    zero = jnp.zeros_like(y)
    return jnp.concatenate([jnp.where(lane < half, y, zero), jnp.where(lane >= half, y, zero)], axis=0)


def _inverse_masks(c):
    row = lax.broadcasted_iota(jnp.int32, (c, 2 * c), 0)
    col = lax.broadcasted_iota(jnp.int32, (c, 2 * c), 1) & (c - 1)
    blk = lambda x, m: x >> (m.bit_length() - 1)
    eye = (row == col).astype(F32)
    same_leaf = blk(row, LEAF) == blk(col, LEAF)
    offs = []
    m = 2 * LEAF
    while m <= c:
        offs.append((blk(row, m) == blk(col, m)) & (blk(row, m // 2) != blk(col, m // 2)))
        m *= 2
    return eye, same_leaf, offs


def _inv_unit_lower_packed(lps, c, masks, fill):
    eye, same_leaf, offs = masks
    bd = lambda y: _block_diag_rows(y.astype(BF16), c)
    ps = [jnp.where(same_leaf, -l, 0.0) for l in lps]
    ts = [eye + p for p in ps]
    span = 2
    while span < LEAF:
        dep = ps[-1]
        ps = [_mm(p, bd(p)) for p in ps]
        fill(dep, ps)
        dep = ts[-1]
        ts = [t + _mm(t, bd(p)) for t, p in zip(ts, ps)]
        fill(dep, ts)
        span *= 2
    for off in offs:
        dep = ts[-1]
        inner = [_mm(jnp.where(off, l, 0.0), bd(t)) for l, t in zip(lps, ts)]
        fill(dep, inner)
        dep = inner[-1]
        ts = [t - _mm(t, bd(x)) for t, x in zip(ts, inner)]
        fill(dep, ts)
    return ts


def _deltanet_kernel(q_ref, k_ref, v_ref, z_ref, gates_ref, mqkv_ref, mgates_ref, cw_ref, dnw_ref,
                     y_ref, stage_ref, gstage_ref, state_ref, qn_s, kn_s, vn_s, o_s, *, tt):
    c = CHUNK
    hd = DN_HEAD_DIM
    n_pairs = DN_HEADS // 2
    pad = SUBLANES
    t_idx = pl.program_id(1)
    pairs = range(n_pairs)
    n_chunks = tt // c

    @pl.when(t_idx == 0)
    def _():
        for idx in range(3):
            stage_ref[idx, 0:pad, :] = jnp.zeros((pad, DN_WIDTH), F32)
            stage_ref[idx, pad:pad + c, :] = mqkv_ref[:, idx * DN_WIDTH:(idx + 1) * DN_WIDTH].astype(F32)
            stage_ref[idx, pad + c + tt:pad + 2 * c + tt, :] = jnp.zeros((c, DN_WIDTH), F32)
        gstage_ref[0:c, :] = mgates_ref[...]
        state_ref[...] = jnp.zeros(state_ref.shape, F32)
        o_s[...] = jnp.zeros(o_s.shape, F32)

    @pl.when(t_idx > 0)
    def _():
        for idx in range(3):
            stage_ref[idx, pad + c - pad:pad + c, :] = stage_ref[idx, pad + c + tt - pad:pad + c + tt, :]

    for idx, x_ref in enumerate((q_ref, k_ref, v_ref)):
        stage_ref[idx, pad + c:pad + c + tt, :] = x_ref[...].astype(F32)
    gstage_ref[c:c + tt, :] = gates_ref[...]

    rowp = lax.broadcasted_iota(jnp.int32, (c, 2 * c), 0)
    colp = lax.broadcasted_iota(jnp.int32, (c, 2 * c), 1) & (c - 1)
    causal = rowp >= colp
    strict = rowp > colp
    lane_p = lax.broadcasted_iota(jnp.int32, (c, 2 * c), 1)
    inv_masks = _inverse_masks(c)
    tril = (lax.broadcasted_iota(jnp.int32, (c, c), 0) >= lax.broadcasted_iota(jnp.int32, (c, c), 1))
    tril3 = jnp.concatenate([tril.astype(BF16)] * 3, axis=1)
    dnw = dnw_ref[...]
    zero_blk = jnp.zeros((hd, hd), BF16)

    def l2n(xh, scale):
        return xh * (lax.rsqrt(jnp.sum(xh * xh, axis=-1, keepdims=True) + NORM_EPS) * scale)

    def prepare_pair(j, p, zero=None):
        base = pl.multiple_of(j * c, c)
        cols = slice(p * PAIR, (p + 1) * PAIR)
        outs = []
        for idx in range(3):
            blk = stage_ref[idx, pl.ds(base, c + pad), cols]
            w = cw_ref[:, idx * DN_WIDTH + p * PAIR:idx * DN_WIDTH + (p + 1) * PAIR]
            if zero is not None:
                w = w + jnp.concatenate([zero[0:CONV_WIDTH], zero[0:CONV_WIDTH]], axis=1)
            y = blk[pad:pad + c] * w[CONV_WIDTH - 1:CONV_WIDTH]
            for kk in range(CONV_WIDTH - 1):
                off = pad - (CONV_WIDTH - 1) + kk
                y = y + blk[off:off + c] * w[kk:kk + 1]
            outs.append(_silu(y))
        slot = j & 1
        qn_s[slot, :, cols] = jnp.concatenate(
            [l2n(outs[0][:, :hd], hd ** -0.5), l2n(outs[0][:, hd:], hd ** -0.5)], axis=1)
        kq = jnp.concatenate([l2n(outs[1][:, :hd], 1.0), l2n(outs[1][:, hd:], 1.0)], axis=1)
        kn_s[slot, :, cols] = kq
        vn_s[slot, :, cols] = outs[2].astype(BF16)
        return kq[0:SUBLANES, 0:LANES]

    def finish_pair(rows, p, zero=None):
        for half in range(2):
            cs = slice(p * PAIR + half * hd, p * PAIR + (half + 1) * hd)
            oh = o_s[:, cs]
            if zero is not None:
                oh = oh + zero[0:1]
            oh = oh * lax.rsqrt(jnp.mean(oh * oh, axis=-1, keepdims=True) + NORM_EPS) * dnw
            zh = z_ref[rows, cs].astype(F32)
            res = oh * _silu(zh)
            y_ref[rows, cs] = res.astype(y_ref.dtype)
        return res[0:SUBLANES, :]

    def out_rows(j):
        return pl.ds(pl.multiple_of(jnp.maximum(j - 1, 0) * c, c), c)

    def exact_rows_dot(lhs3, x):
        hi, mid, lo = _split3(x)
        return jnp.dot(lhs3, jnp.concatenate([hi, mid, lo], axis=0), preferred_element_type=F32)

    def pair_cols(mat, la, lb):
        return jnp.where(lane_p < c, mat[:, la:la + 1], mat[:, lb:lb + 1])

    def pair_rows(mat_t, ra, rb):
        return jnp.concatenate([mat_t[ra:ra + 1, :], mat_t[rb:rb + 1, :]], axis=1)

    def zero_of(v):
        return jnp.where(v != v, v, 0.0)

    def chunk_body(j, carry):
        base = pl.multiple_of(j * c, c)
        prev_rows = out_rows(j - 1)
        fillers = []
        for p in pairs:
            fillers.append(functools.partial(finish_pair, prev_rows, p))
            fillers.append(functools.partial(prepare_pair, j + 1, p))
        fillers.reverse()

        def fill(dep, into, n=1):
            zero = None if dep is None else zero_of(dep[0:SUBLANES, 0:LANES].astype(F32))
            for _ in range(n):
                if fillers:
                    done = zero_of(fillers.pop()(zero=zero))[0:1]
                    if into[0].shape[1] == PAIR:
                        done = jnp.concatenate([done, done], axis=1)
                    into[0] = into[0] + done

        slot = j & 1
        head_cols = lambda h: slice(h * hd, (h + 1) * hd)
        pair_cols_of = lambda p: slice(p * PAIR, (p + 1) * PAIR)
        gt =gstage_ref[pl.ds(base, c), :]
        gcum = exact_rows_dot(tril3, gt)
        g_last = gcum[c - 1:c, :]
        eg = jnp.exp(gcum)
        e_rest = jnp.exp(g_last - gcum)
        e_last = jnp.exp(g_last)
        gt_t = gt.T
        gcum_t = gcum.T

        ha = [2 * p for p in pairs]
        hb = [2 * p + 1 for p in pairs]
        kpair = [kn_s[slot, :, pair_cols_of(p)].astype(BF16) for p in pairs]
        kbd = [_block_diag_rows(kp, hd) for kp in kpair]
        qpair = [qn_s[slot, :, pair_cols_of(p)].astype(BF16) for p in pairs]
        aq = [_mm_nt(jnp.concatenate([kpair[p], qpair[p]], axis=0), kbd[p]) for p in pairs]
        g_col = [pair_cols(gcum, DN_HEADS + ha[p], DN_HEADS + hb[p]) for p in pairs]
        g_row = [pair_rows(gcum_t, DN_HEADS + ha[p], DN_HEADS + hb[p]) for p in pairs]
        b_col = [pair_cols(gt, ha[p], hb[p]) for p in pairs]
        b_row = [pair_rows(gt_t, ha[p], hb[p]) for p in pairs]
        dec = [jnp.where(causal, jnp.exp(jnp.where(causal, g_col[p] - g_row[p], 0.0)), 0.0) for p in pairs]
        lps = [jnp.where(strict, aq[p][:c] * b_col[p] * dec[p], 0.0) for p in pairs]
        pps = [aq[p][c:] * dec[p] for p in pairs]
        fill(None, lps, 4)
        tps = _inv_unit_lower_packed(lps, c, inv_masks, fill)
        vbd = [_block_diag_rows(vn_s[slot, :, pair_cols_of(p)], hd) for p in pairs]
        u = [_mm(tps[p] * b_row[p], vbd[p]) for p in pairs]
        w = [_mm(tps[p] * (b_row[p] * jnp.exp(g_row[p])), kbd[p]) for p in pairs]
        fill(tps[-1], w)
        s_old = [state_ref[h] for h in range(DN_HEADS)]
        qd = [jnp.concatenate(
            [qn_s[slot, :, head_cols(ha[p])] * eg[:, DN_HEADS + ha[p]:DN_HEADS + ha[p] + 1],
             qn_s[slot, :, head_cols(hb[p])] * eg[:, DN_HEADS + hb[p]:DN_HEADS + hb[p] + 1]], axis=1)
            for p in pairs]
        sbd = [jnp.concatenate([jnp.concatenate([s_old[ha[p]].astype(BF16), zero_blk], axis=1),
                                jnp.concatenate([zero_blk, s_old[hb[p]].astype(BF16)], axis=1)], axis=0)
               for p in pairs]
        ws = [_mm(jnp.concatenate([w[p], qd[p]], axis=0), sbd[p]) for p in pairs]
        fill(w[-1], ws)
        v_new = [u[p] - ws[p][:c] for p in pairs]
        o = [ws[p][c:] + _mm(pps[p], _block_diag_rows(v_new[p].astype(BF16), hd)) for p in pairs]
        for h in range(DN_HEADS):
            p, half = divmod(h, 2)
            lane_h = DN_HEADS + h
            kd = kn_s[slot, :, head_cols(h)] * e_rest[:, lane_h:lane_h + 1]
            upd = _mm_tn(kd, v_new[p][:, half * hd:(half + 1) * hd])
            state_ref[h] = s_old[h] * e_last[:, lane_h:lane_h + 1] + upd
        fill(ws[-1], o, len(fillers))
        for p in pairs:
            o_s[:, p * PAIR:(p + 1) * PAIR] = o[p]
        return carry

    start = jnp.where(t_idx == 0, 0, 1)
    for p in pairs:
        prepare_pair(start, p)
    lax.fori_loop(start, n_chunks + 1, chunk_body, 0)
    last_rows = out_rows(n_chunks)
    for p in pairs:
        finish_pair(last_rows, p)


def _deltanet(proj, meta_qkv, gates, meta_gates, conv_w, dn_norm_w, *, batch, seq_len):
    tt = TT_DN
    nt = seq_len // tt
    c = CHUNK
    kern = functools.partial(_deltanet_kernel, tt=tt)
    row = lambda b, t: b * nt + t
    const = lambda shape: pl.BlockSpec(shape, lambda b, t: (0,) * len(shape), pipeline_mode=pl.Buffered(1))
    return pl.pallas_call(
        kern,
        out_shape=jax.ShapeDtypeStruct((batch * seq_len, DN_WIDTH), BF16),
        grid=(batch, nt),
        in_specs=[
            pl.BlockSpec((tt, DN_WIDTH), lambda b, t: (row(b, t), COL_Q // DN_WIDTH)),
            pl.BlockSpec((tt, DN_WIDTH), lambda b, t: (row(b, t), COL_K // DN_WIDTH)),
            pl.BlockSpec((tt, DN_WIDTH), lambda b, t: (row(b, t), COL_V // DN_WIDTH)),
            pl.BlockSpec((tt, DN_WIDTH), lambda b, t: (row(b, t), COL_ZD // DN_WIDTH)),
            pl.BlockSpec((tt, LANES), lambda b, t: (row(b, t), 0)),
            const((c, 3 * DN_WIDTH)),
            const((c, LANES)),
            const((CONV_WIDTH, 3 * DN_WIDTH)),
            const((1, DN_HEAD_DIM)),
        ],
        out_specs=pl.BlockSpec((tt, DN_WIDTH), lambda b, t: (row(b, t), 0)),
        scratch_shapes=[
            pltpu.VMEM((3, SUBLANES + 2 * c + tt, DN_WIDTH), F32),
            pltpu.VMEM((c + tt, LANES), F32),
            pltpu.VMEM((DN_HEADS, DN_HEAD_DIM, DN_HEAD_DIM), F32),
            pltpu.VMEM((2, c, DN_WIDTH), F32),
            pltpu.VMEM((2, c, DN_WIDTH), F32),
            pltpu.VMEM((2, c, DN_WIDTH), BF16),
            pltpu.VMEM((c, DN_WIDTH), F32),
        ],
        compiler_params=pltpu.CompilerParams(
            dimension_semantics=("arbitrary", "arbitrary"), vmem_limit_bytes=VMEM_LIMIT),
        name="deltanet",
    )(proj, proj, proj, proj, gates, meta_qkv, meta_gates, conv_w, dn_norm_w)


def _out_merge_kernel(u_ref, zp_ref, gpa_ref, gpb_ref, gda_ref, gdb_ref, ydn_ref, x_ref, mu_ref, mix_ref,
                      scale_ref, wpo_ref, wdo_ref, wo_ref, fnw_ref, out_ref, ubuf_ref, *, tm):
    t = pl.program_id(1)
    hist = N_META

    @pl.when(t == 0)
    def _():
        ubuf_ref[0:hist, :] = mu_ref[...].astype(F32)

    @pl.when(t > 0)
    def _():
        ubuf_ref[0:hist, :] = ubuf_ref[tm:tm + hist, :]

    ubuf_ref[hist:hist + tm, :] = u_ref[...].astype(F32)

    pooled = []
    for gi, w in enumerate(POOL_WINDOWS):
        cs = slice(gi * POOL_GROUP_DIM, (gi + 1) * POOL_GROUP_DIM)
        acc = ubuf_ref[hist:hist + tm, cs]
        cur = acc
        for s in range(1, w):
            acc = acc + ubuf_ref[hist - s:hist - s + tm, cs]
        d = acc * (1.0 / w) - cur
        mixed = _mm(d, mix_ref[gi])
        zp = zp_ref[:, cs].astype(F32)
        pooled.append((mixed * scale_ref[:, cs] * _silu(zp)).astype(BF16))
    y_pool = jnp.concatenate(pooled, axis=1)

    frame = slice(GATE_SHIFT, GATE_SHIFT + D_MODEL)
    gp = jnp.concatenate([gpa_ref[...], gpb_ref[...]], axis=1).astype(F32)[:, frame]
    gd = jnp.concatenate([gda_ref[...], gdb_ref[...]], axis=1).astype(F32)[:, frame]
    t_pool = jnp.dot(y_pool, wpo_ref[...], preferred_element_type=F32)
    t_dn = jnp.dot(ydn_ref[...], wdo_ref[...], preferred_element_type=F32)
    merged = _sigmoid(gp) * t_pool + _sigmoid(gd) * t_dn
    h = x_ref[...] + jnp.dot(merged.astype(BF16), wo_ref[...], preferred_element_type=F32)
    ms = jnp.mean(h * h, axis=-1, keepdims=True)
    out_ref[...] = h * lax.rsqrt(ms + NORM_EPS) * fnw_ref[...]


def _resident(shape, index_map):
    return pl.BlockSpec(shape, index_map, pipeline_mode=pl.Buffered(1))


def _out_merge(proj, y_dn, x2d, meta_u, mix, scale, wpo, wdo, wo, fnw, *, batch, seq_len):
    tm = TM_OUT
    tiles = seq_len // tm
    kern = functools.partial(_out_merge_kernel, tm=tm)
    row = lambda b, t: b * tiles + t
    gp0, gd0 = COL_GP - GATE_SHIFT, COL_GD - GATE_SHIFT
    assert gp0 % D_MODEL == 0 and gd0 % D_MODEL == 0
    return pl.pallas_call(
        kern,
        out_shape=jax.ShapeDtypeStruct((batch * seq_len, D_MODEL), F32),
        grid=(batch, tiles),
        in_specs=[
            pl.BlockSpec((tm, POOL_WIDTH), lambda b, t: (row(b, t), COL_U // POOL_WIDTH)),
            pl.BlockSpec((tm, POOL_WIDTH), lambda b, t: (row(b, t), COL_ZP // POOL_WIDTH)),
            pl.BlockSpec((tm, D_MODEL), lambda b, t: (row(b, t), gp0 // D_MODEL)),
            pl.BlockSpec((tm, LANES), lambda b, t: (row(b, t), (gp0 + D_MODEL) // LANES)),
            pl.BlockSpec((tm, D_MODEL), lambda b, t: (row(b, t), gd0 // D_MODEL)),
            pl.BlockSpec((tm, LANES), lambda b, t: (row(b, t), (gd0 + D_MODEL) // LANES)),
            pl.BlockSpec((tm, DN_WIDTH), lambda b, t: (row(b, t), 0)),
            pl.BlockSpec((tm, D_MODEL), lambda b, t: (row(b, t), 0)),
            _resident((N_META, POOL_WIDTH), lambda b, t: (0, 0)),
            _resident((POOL_GROUPS, POOL_GROUP_DIM, POOL_GROUP_DIM), lambda b, t: (0, 0, 0)),
            _resident((1, POOL_WIDTH), lambda b, t: (0, 0)),
            _resident((POOL_WIDTH, D_MODEL), lambda b, t: (0, 0)),
            _resident((DN_WIDTH, D_MODEL), lambda b, t: (0, 0)),
            _resident((D_MODEL, D_MODEL), lambda b, t: (0, 0)),
            _resident((1, D_MODEL), lambda b, t: (0, 0)),
        ],
        out_specs=pl.BlockSpec((tm, D_MODEL), lambda b, t: (row(b, t), 0)),
        scratch_shapes=[pltpu.VMEM((N_META + tm, POOL_WIDTH), F32)],
        compiler_params=pltpu.CompilerParams(
            dimension_semantics=("arbitrary", "arbitrary"), vmem_limit_bytes=VMEM_LIMIT),
        name="out_merge",
    )(proj, proj, proj, proj, proj, proj, y_dn, x2d, meta_u, mix, scale, wpo, wdo, wo, fnw)


def kernel(x, meta_tokens, norm_w, w_in, conv_w, A_log, dt_bias, pool_mix, pool_scale, dn_norm_w,
           w_pool_out, w_dn_out, w_o, final_norm_w):
    batch, seq_len, _ = x.shape
    assert norm_w.shape[0] == 1, "single layer block"
    x2d = x.reshape(batch * seq_len, D_MODEL)

    rows = batch * seq_len
    w_t = jnp.swapaxes(w_in, 1, 2).reshape(w_in.shape[2], D_MODEL)
    gpar = jnp.zeros((2, LANES), F32)
    gpar = gpar.at[0, DN_HEADS:2 * DN_HEADS].set(A_log[0]).at[1, DN_HEADS:2 * DN_HEADS].set(dt_bias[0])
    nw = norm_w[0].reshape(1, D_MODEL)
    wpo = w_pool_out[0].astype(BF16)
    wdo = w_dn_out[0].astype(BF16)
    wo = w_o[0].astype(BF16)

    xn_all, gates = _norm_gates(x2d, meta_tokens.astype(F32), nw, w_t, gpar)
    proj = _in_proj(xn_all, w_t)

    front = CHUNK - N_META
    meta_proj = proj[rows:]
    meta_qkv = jnp.pad(meta_proj[:, COL_Q:COL_ZD], ((front, 0), (0, 0)))
    meta_gates = jnp.pad(gates[rows:], ((front, 0), (0, 0)))
    meta_u = meta_proj[:, COL_U:COL_U + POOL_WIDTH]

    y_dn = _deltanet(proj, meta_qkv, gates, meta_gates, conv_w[0], dn_norm_w[0].reshape(1, DN_HEAD_DIM),
                     batch=batch, seq_len=seq_len)

    out = _out_merge(proj, y_dn, x2d, meta_u, pool_mix[0].astype(BF16), pool_scale[0].reshape(1, POOL_WIDTH),
                     wpo, wdo, wo, final_norm_w.reshape(1, D_MODEL), batch=batch, seq_len=seq_len)
    return out.reshape(batch, seq_len, D_MODEL)
```

```python
import functools

import jax
import jax.numpy as jnp
from jax import lax
from jax.experimental import pallas as pl
from jax.experimental.pallas import tpu as pltpu

D_MODEL = 2048
N_META = 16
POOL_GROUPS = 4
POOL_WINDOWS = (2, 4, 8, 16)
POOL_WIDTH = D_MODEL // 2
POOL_GROUP_DIM = POOL_WIDTH // POOL_GROUPS
DN_HEADS = 16
DN_HEAD_DIM = 128
DN_WIDTH = DN_HEADS * DN_HEAD_DIM
CONV_WIDTH = 4
NORM_EPS = 1e-6

COL_U = 0
COL_ZP = COL_U + POOL_WIDTH
COL_Q = COL_ZP + POOL_WIDTH
COL_K = COL_Q + DN_WIDTH
COL_V = COL_K + DN_WIDTH
COL_ZD = COL_V + DN_WIDTH
BA_OFFSET = COL_ZD + DN_WIDTH
COL_GP = BA_OFFSET + 2 * DN_HEADS
COL_GD = COL_GP + D_MODEL
LANES = 128
SUBLANES = 8
GATE_SHIFT = COL_GP % LANES

PROJ_DTYPE = jnp.bfloat16
CHUNK = 64
LEAF = 16
TT_DN = 512
TM_NORM = 512
TN_PROJ = 256
ROW_BLK_PROJ = 2048
TM_OUT = 256
VMEM_LIMIT = 56 * 1024 * 1024

BF16 = jnp.bfloat16
F32 = jnp.float32


def _mm(a, b):
    return jnp.dot(a.astype(BF16), b.astype(BF16), preferred_element_type=F32)


def _mm_nt(a, b):
    return lax.dot_general(a.astype(BF16), b.astype(BF16), (((1,), (1,)), ((), ())),
                           preferred_element_type=F32)


def _mm_tn(a, b):
    return lax.dot_general(a.astype(BF16), b.astype(BF16), (((0,), (0,)), ((), ())),
                           preferred_element_type=F32)


def _sigmoid(x):
    return 0.5 * jnp.tanh(0.5 * x) + 0.5


def _silu(x):
    h = 0.5 * x
    return h * jnp.tanh(h) + h


def _softplus(x):
    return jnp.maximum(x, 0.0) + jnp.log1p(jnp.exp(-jnp.abs(x)))


def _norm_gates_kernel(x_ref, meta_ref, nw_ref, wba_ref, gpar_ref, xn_ref, gates_ref, *, x_tiles, sub):
    i = pl.program_id(0)
    nw = nw_ref[...]
    wba = wba_ref[...].astype(BF16)
    a_rate = jnp.exp(gpar_ref[0:1, :])
    dt_bias = gpar_ref[1:2, :]

    def rows_block(x, rows):
        ms = jnp.mean(x * x, axis=-1, keepdims=True)
        xn = (x * lax.rsqrt(ms + NORM_EPS) * nw).astype(BF16)
        xn_ref[rows, :] = xn
        ba = _mm_nt(xn, wba)
        lane = lax.broadcasted_iota(jnp.int32, ba.shape, 1)
        g = -a_rate * _softplus(ba + dt_bias)
        gates_ref[rows, :] = jnp.where(lane < DN_HEADS, _sigmoid(ba), jnp.where(lane < 2 * DN_HEADS, g, 0.0))

    @pl.when(i < x_tiles)
    def _():
        def body(r, carry):
            rows = pl.ds(pl.multiple_of(r * sub, sub), sub)
            rows_block(x_ref[rows, :], rows)
            return carry

        lax.fori_loop(0, x_ref.shape[0] // sub, body, 0)

    @pl.when(i == x_tiles)
    def _():
        rows_block(meta_ref[...], pl.ds(0, N_META))


def _norm_gates(x2d, meta, norm_w, w_t, gpar):
    rows = x2d.shape[0]
    tm = TM_NORM
    x_tiles = rows // tm
    kern = functools.partial(_norm_gates_kernel, x_tiles=x_tiles, sub=128)
    return pl.pallas_call(
        kern,
        out_shape=(jax.ShapeDtypeStruct((rows + N_META, D_MODEL), BF16),
                   jax.ShapeDtypeStruct((rows + N_META, LANES), F32)),
        grid=(x_tiles + 1,),
        in_specs=[
            pl.BlockSpec((tm, D_MODEL), lambda i: (jnp.minimum(i, x_tiles - 1), 0)),
            pl.BlockSpec((N_META, D_MODEL), lambda i: (0, 0)),
            pl.BlockSpec((1, D_MODEL), lambda i: (0, 0)),
            pl.BlockSpec((LANES, D_MODEL), lambda i: (BA_OFFSET // LANES, 0)),
            pl.BlockSpec((2, LANES), lambda i: (0, 0)),
        ],
        out_specs=(pl.BlockSpec((tm, D_MODEL), lambda i: (i, 0)),
                   pl.BlockSpec((tm, LANES), lambda i: (i, 0))),
        compiler_params=pltpu.CompilerParams(
            dimension_semantics=("arbitrary",), vmem_limit_bytes=VMEM_LIMIT),
        name="norm_gates",
    )(x2d, meta, norm_w, w_t, gpar)


def _in_proj_kernel(xn_ref, w_ref, proj_ref, *, rows, row_blk):
    w = w_ref[...].astype(BF16)
    for r in range(rows // row_blk):
        rs = slice(r * row_blk, (r + 1) * row_blk)
        proj_ref[rs, :] = _mm_nt(xn_ref[rs, :], w).astype(proj_ref.dtype)
    ms = slice(rows, rows + N_META)
    proj_ref[ms, :] = _mm_nt(xn_ref[ms, :], w).astype(proj_ref.dtype)


def _in_proj(xn_all, w_t):
    rows = xn_all.shape[0] - N_META
    cols = w_t.shape[0]
    tn = TN_PROJ
    kern = functools.partial(_in_proj_kernel, rows=rows, row_blk=ROW_BLK_PROJ)
    return pl.pallas_call(
        kern,
        out_shape=jax.ShapeDtypeStruct((rows + N_META, cols), PROJ_DTYPE),
        grid=(pl.cdiv(cols, tn),),
        in_specs=[
            pl.BlockSpec((rows + N_META, D_MODEL), lambda j: (0, 0), pipeline_mode=pl.Buffered(1)),
            pl.BlockSpec((tn, D_MODEL), lambda j: (j, 0)),
        ],
        out_specs=pl.BlockSpec((rows + N_META, tn), lambda j: (0, j)),
        compiler_params=pltpu.CompilerParams(
            dimension_semantics=("arbitrary",), vmem_limit_bytes=VMEM_LIMIT),
        name="in_proj",
    )(xn_all, w_t)


PAIR = 2 * DN_HEAD_DIM


def _split3(x):
    hi = x.astype(BF16)
    r1 = x - hi.astype(F32)
    mid = r1.astype(BF16)
    lo = (r1 - mid.astype(F32)).astype(BF16)
    return hi, mid, lo


def _block_diag_rows(y, half):
    lane = lax.broadcasted_iota(jnp.int32, y.shape, 1)
    zero = jnp.zeros_like(y)
    return jnp.concatenate([jnp.where(lane < half, y, zero), jnp.where(lane >= half, y, zero)], axis=0)


def _inverse_masks(c):
    row = lax.broadcasted_iota(jnp.int32, (c, 2 * c), 0)
    col = lax.broadcasted_iota(jnp.int32, (c, 2 * c), 1) & (c - 1)
    blk = lambda x, m: x >> (m.bit_length() - 1)
    eye = (row == col).astype(F32)
    same_leaf = blk(row, LEAF) == blk(col, LEAF)
    offs = []
    m = 2 * LEAF
    while m <= c:
        offs.append((blk(row, m) == blk(col, m)) & (blk(row, m // 2) != blk(col, m // 2)))
        m *= 2
    return eye, same_leaf, offs


def _inv_unit_lower_packed(lps, c, masks, fill):
    eye, same_leaf, offs = masks
    bd = lambda y: _block_diag_rows(y.astype(BF16), c)
    ps = [jnp.where(same_leaf, -l, 0.0) for l in lps]
    ts = [eye + p for p in ps]
    dep = ps[0]
    ps = [_mm(p, bd(p)) for p in ps]
    fill(dep, ps, 2)
    span = 4
    while span <= LEAF:
        dep = ps[0]
        last = span == LEAF
        prod = [_mm(t if last else jnp.concatenate([t, p], axis=0), bd(p)) for t, p in zip(ts, ps)]
        ts = [t + x[:c] for t, x in zip(ts, prod)]
        if not last:
            ps = [x[c:] for x in prod]
        fill(dep, ts)
        span *= 2
    for off in offs:
        dep = ts[0]
        inner = [_mm(jnp.where(off, l, 0.0), bd(t)) for l, t in zip(lps, ts)]
        fill(dep, inner)
        dep = inner[0]
        ts = [t - _mm(t, bd(x)) for t, x in zip(ts, inner)]
        fill(dep, ts)
    return ts


def _deltanet_kernel(q_ref, k_ref, v_ref, z_ref, gates_ref, mqkv_ref, mgates_ref, cw_ref, dnw_ref,
                     y_ref, stage_ref, gstage_ref, state_ref, qn_s, kn_s, vn_s, o_s, *, tt):
    c = CHUNK
    hd = DN_HEAD_DIM
    n_pairs = DN_HEADS // 2
    pad = SUBLANES
    t_idx = pl.program_id(1)
    pairs = range(n_pairs)
    n_chunks = tt // c

    @pl.when(t_idx == 0)
    def _():
        for idx in range(3):
            stage_ref[idx, 0:pad, :] = jnp.zeros((pad, DN_WIDTH), F32)
            stage_ref[idx, pad:pad + c, :] = mqkv_ref[:, idx * DN_WIDTH:(idx + 1) * DN_WIDTH].astype(F32)
            stage_ref[idx, pad + c + tt:pad + 2 * c + tt, :] = jnp.zeros((c, DN_WIDTH), F32)
        gstage_ref[0:c, :] = mgates_ref[...]
        state_ref[...] = jnp.zeros(state_ref.shape, F32)
        o_s[...] = jnp.zeros(o_s.shape, F32)

    @pl.when(t_idx > 0)
    def _():
        for idx in range(3):
            stage_ref[idx, pad + c - pad:pad + c, :] = stage_ref[idx, pad + c + tt - pad:pad + c + tt, :]

    for idx, x_ref in enumerate((q_ref, k_ref, v_ref)):
        stage_ref[idx, pad + c:pad + c + tt, :] = x_ref[...].astype(F32)
    gstage_ref[c:c + tt, :] = gates_ref[...]

    rowp = lax.broadcasted_iota(jnp.int32, (c, 2 * c), 0)
    colp = lax.broadcasted_iota(jnp.int32, (c, 2 * c), 1) & (c - 1)
    causal = rowp >= colp
    strict = rowp > colp
    lane_p = lax.broadcasted_iota(jnp.int32, (c, 2 * c), 1)
    inv_masks = _inverse_masks(c)
    tril = (lax.broadcasted_iota(jnp.int32, (c, c), 0) >= lax.broadcasted_iota(jnp.int32, (c, c), 1))
    tril3 = jnp.concatenate([tril.astype(BF16)] * 3, axis=1)
    dnw = dnw_ref[...]
    zero_blk = jnp.zeros((hd, hd), BF16)

    def l2n(xh, scale):
        return xh * (lax.rsqrt(jnp.sum(xh * xh, axis=-1, keepdims=True) + NORM_EPS) * scale)

    def prepare_pair(j, p, zero=None):
        base = pl.multiple_of(j * c, c)
        cols = slice(p * PAIR, (p + 1) * PAIR)
        outs = []
        for idx in range(3):
            blk = stage_ref[idx, pl.ds(base, c + pad), cols]
            w = cw_ref[:, idx * DN_WIDTH + p * PAIR:idx * DN_WIDTH + (p + 1) * PAIR]
            if zero is not None:
                w = w + jnp.concatenate([zero[0:CONV_WIDTH], zero[0:CONV_WIDTH]], axis=1)
            y = blk[pad:pad + c] * w[CONV_WIDTH - 1:CONV_WIDTH]
            for kk in range(CONV_WIDTH - 1):
                off = pad - (CONV_WIDTH - 1) + kk
                y = y + blk[off:off + c] * w[kk:kk + 1]
            outs.append(_silu(y))
        slot = j & 1
        qn_s[slot, :, cols] = jnp.concatenate(
            [l2n(outs[0][:, :hd], hd ** -0.5), l2n(outs[0][:, hd:], hd ** -0.5)], axis=1)
        kn_s[slot, :, cols] = jnp.concatenate([l2n(outs[1][:, :hd], 1.0), l2n(outs[1][:, hd:], 1.0)], axis=1)
        vn_s[slot, :, cols] = outs[2].astype(BF16)
        r0 = pl.multiple_of((j >> 24) * (2 * SUBLANES), 2 * SUBLANES)
        back = vn_s[slot, pl.ds(r0, 2 * SUBLANES), cols].astype(F32)[0:SUBLANES]
        back = back + qn_s[slot, pl.ds(r0, SUBLANES), cols] + kn_s[slot, pl.ds(r0, SUBLANES), cols]
        return back[:, :hd] + back[:, hd:]

    def finish_pair(row0, p, zero=None):
        rows = pl.ds(row0, c)
        for half in range(2):
            cs = slice(p * PAIR + half * hd, p * PAIR + (half + 1) * hd)
            oh = o_s[:, cs]
            if zero is not None:
                oh = oh + zero[0:1]
            oh = oh * lax.rsqrt(jnp.mean(oh * oh, axis=-1, keepdims=True) + NORM_EPS) * dnw
            zh = z_ref[rows, cs].astype(F32)
            y_ref[rows, cs] = (oh * _silu(zh)).astype(y_ref.dtype)
        r0 = pl.multiple_of((row0 >> 24) * (2 * SUBLANES), 2 * SUBLANES)
        back = y_ref[pl.ds(r0, 2 * SUBLANES), p * PAIR:(p + 1) * PAIR].astype(F32)[0:SUBLANES]
        return back[:, :hd] + back[:, hd:]

    def out_rows(j):
        return pl.multiple_of(jnp.maximum(j - 1, 0) * c, c)

    def exact_rows_dot(lhs3, x):
        hi, mid, lo = _split3(x)
        return jnp.dot(lhs3, jnp.concatenate([hi, mid, lo], axis=0), preferred_element_type=F32)

    def pair_cols(mat, la, lb):
        return jnp.where(lane_p < c, mat[:, la:la + 1], mat[:, lb:lb + 1])

    def pair_rows(mat_t, ra, rb):
        return jnp.concatenate([mat_t[ra:ra + 1, :], mat_t[rb:rb + 1, :]], axis=1)

    def zero_of(v):
        return jnp.where(v != v, v, 0.0)

    def chunk_body(j, carry):
        base = pl.multiple_of(j * c, c)
        prev_rows = out_rows(j - 1)
        fillers = []
        for p in pairs:
            fillers.append(functools.partial(finish_pair, prev_rows, p))
            fillers.append(functools.partial(prepare_pair, j + 1, p))
        fillers.reverse()

        def fill(dep, into, n=1):
            zero = None if dep is None else zero_of(dep[0:SUBLANES, 0:LANES].astype(F32))
            for _ in range(n):
                if fillers:
                    done = zero_of(fillers.pop()(zero=zero))[0:1]
                    if into[-1].shape[1] == PAIR:
                        done = jnp.concatenate([done, done], axis=1)
                    into[-1] = into[-1] + done

        slot = j & 1
        head_cols = lambda h: slice(h * hd, (h + 1) * hd)
        pair_cols_of = lambda p: slice(p * PAIR, (p + 1) * PAIR)
        gt =gstage_ref[pl.ds(base, c), :]
        gcum = exact_rows_dot(tril3, gt)
        g_last = gcum[c - 1:c, :]
        eg = jnp.exp(gcum)
        e_rest = jnp.exp(g_last - gcum)
        e_last = jnp.exp(g_last)
        gt_t = gt.T
        gcum_t = gcum.T

        ha = [2 * p for p in pairs]
        hb = [2 * p + 1 for p in pairs]
        kpair = [kn_s[slot, :, pair_cols_of(p)].astype(BF16) for p in pairs]
        kbd = [_block_diag_rows(kp, hd) for kp in kpair]
        qpair = [qn_s[slot, :, pair_cols_of(p)].astype(BF16) for p in pairs]
        for p in (1, 3, 5, 7):
            staged = [qn_s[slot, :, pair_cols_of(p)]]
            fill(None, staged)
            qpair[p] = staged[0].astype(BF16)
        aq = [_mm_nt(jnp.concatenate([kpair[p], qpair[p]], axis=0), kbd[p]) for p in pairs]
        g_col = [pair_cols(gcum, DN_HEADS + ha[p], DN_HEADS + hb[p]) for p in pairs]
        g_row = [pair_rows(gcum_t, DN_HEADS + ha[p], DN_HEADS + hb[p]) for p in pairs]
        b_col = [pair_cols(gt, ha[p], hb[p]) for p in pairs]
        b_row = [pair_rows(gt_t, ha[p], hb[p]) for p in pairs]
        dec = [jnp.where(causal, jnp.exp(jnp.where(causal, g_col[p] - g_row[p], 0.0)), 0.0) for p in pairs]
        lps = [jnp.where(strict, aq[p][:c] * b_col[p] * dec[p], 0.0) for p in pairs]
        pps = [aq[p][c:] * dec[p] for p in pairs]
        tps = _inv_unit_lower_packed(lps, c, inv_masks, fill)
        vbd = [_block_diag_rows(vn_s[slot, :, pair_cols_of(p)], hd) for p in pairs]
        u = [_mm(tps[p] * b_row[p], vbd[p]) for p in pairs]
        w = [_mm(tps[p] * (b_row[p] * jnp.exp(g_row[p])), kbd[p]) for p in pairs]
        fill(tps[0], w)
        s_old = [state_ref[h] for h in range(DN_HEADS)]
        qd = [jnp.concatenate(
            [qn_s[slot, :, head_cols(ha[p])] * eg[:, DN_HEADS + ha[p]:DN_HEADS + ha[p] + 1],
             qn_s[slot, :, head_cols(hb[p])] * eg[:, DN_HEADS + hb[p]:DN_HEADS + hb[p] + 1]], axis=1)
            for p in pairs]
        sbd = [jnp.concatenate([jnp.concatenate([s_old[ha[p]].astype(BF16), zero_blk], axis=1),
                                jnp.concatenate([zero_blk, s_old[hb[p]].astype(BF16)], axis=1)], axis=0)
               for p in pairs]
        ws = [_mm(jnp.concatenate([w[p], qd[p]], axis=0), sbd[p]) for p in pairs]
        fill(w[0], ws)
        v_new = [u[p] - ws[p][:c] for p in pairs]
        o = [ws[p][c:] + _mm(pps[p], _block_diag_rows(v_new[p].astype(BF16), hd)) for p in pairs]
        for h in range(DN_HEADS):
            p, half = divmod(h, 2)
            lane_h = DN_HEADS + h
            kd = kn_s[slot, :, head_cols(h)] * e_rest[:, lane_h:lane_h + 1]
            upd = _mm_tn(kd, v_new[p][:, half * hd:(half + 1) * hd])
            state_ref[h] = s_old[h] * e_last[:, lane_h:lane_h + 1] + upd
        fill(ws[0], o, len(fillers))
        for p in pairs:
            o_s[:, p * PAIR:(p + 1) * PAIR] = o[p]
        return carry

    start = jnp.where(t_idx == 0, 0, 1)
    for p in pairs:
        prepare_pair(start, p)
    lax.fori_loop(start, n_chunks + 1, chunk_body, 0)
    last_rows = out_rows(jnp.int32(n_chunks))
    for p in pairs:
        finish_pair(last_rows, p)


def _deltanet(proj, meta_qkv, gates, meta_gates, conv_w, dn_norm_w, *, batch, seq_len):
    tt = TT_DN
    nt = seq_len // tt
    c = CHUNK
    kern = functools.partial(_deltanet_kernel, tt=tt)
    row = lambda b, t: b * nt + t
    const = lambda shape: pl.BlockSpec(shape, lambda b, t: (0,) * len(shape), pipeline_mode=pl.Buffered(1))
    return pl.pallas_call(
        kern,
        out_shape=jax.ShapeDtypeStruct((batch * seq_len, DN_WIDTH), BF16),
        grid=(batch, nt),
        in_specs=[
            pl.BlockSpec((tt, DN_WIDTH), lambda b, t: (row(b, t), COL_Q // DN_WIDTH)),
            pl.BlockSpec((tt, DN_WIDTH), lambda b, t: (row(b, t), COL_K // DN_WIDTH)),
            pl.BlockSpec((tt, DN_WIDTH), lambda b, t: (row(b, t), COL_V // DN_WIDTH)),
            pl.BlockSpec((tt, DN_WIDTH), lambda b, t: (row(b, t), COL_ZD // DN_WIDTH)),
            pl.BlockSpec((tt, LANES), lambda b, t: (row(b, t), 0)),
            const((c, 3 * DN_WIDTH)),
            const((c, LANES)),
            const((CONV_WIDTH, 3 * DN_WIDTH)),
            const((1, DN_HEAD_DIM)),
        ],
        out_specs=pl.BlockSpec((tt, DN_WIDTH), lambda b, t: (row(b, t), 0)),
        scratch_shapes=[
            pltpu.VMEM((3, SUBLANES + 2 * c + tt, DN_WIDTH), F32),
            pltpu.VMEM((c + tt, LANES), F32),
            pltpu.VMEM((DN_HEADS, DN_HEAD_DIM, DN_HEAD_DIM), F32),
            pltpu.VMEM((2, c, DN_WIDTH), F32),
            pltpu.VMEM((2, c, DN_WIDTH), F32),
            pltpu.VMEM((2, c, DN_WIDTH), BF16),
            pltpu.VMEM((c, DN_WIDTH), F32),
        ],
        compiler_params=pltpu.CompilerParams(
            dimension_semantics=("arbitrary", "arbitrary"), vmem_limit_bytes=VMEM_LIMIT),
        name="deltanet",
    )(proj, proj, proj, proj, gates, meta_qkv, meta_gates, conv_w, dn_norm_w)


def _out_merge_kernel(u_ref, zp_ref, gpa_ref, gpb_ref, gda_ref, gdb_ref, ydn_ref, x_ref, mu_ref, mix_ref,
                      scale_ref, wpo_ref, wdo_ref, wo_ref, fnw_ref, out_ref, ubuf_ref, *, tm):
    t = pl.program_id(1)
    hist = N_META

    @pl.when(t == 0)
    def _():
        ubuf_ref[0:hist, :] = mu_ref[...].astype(F32)

    @pl.when(t > 0)
    def _():
        ubuf_ref[0:hist, :] = ubuf_ref[tm:tm + hist, :]

    ubuf_ref[hist:hist + tm, :] = u_ref[...].astype(F32)

    pooled = []
    for gi, w in enumerate(POOL_WINDOWS):
        cs = slice(gi * POOL_GROUP_DIM, (gi + 1) * POOL_GROUP_DIM)
        acc = ubuf_ref[hist:hist + tm, cs]
        cur = acc
        for s in range(1, w):
            acc = acc + ubuf_ref[hist - s:hist - s + tm, cs]
        d = acc * (1.0 / w) - cur
        mixed = _mm(d, mix_ref[gi])
        zp = zp_ref[:, cs].astype(F32)
        pooled.append((mixed * scale_ref[:, cs] * _silu(zp)).astype(BF16))
    y_pool = jnp.concatenate(pooled, axis=1)

    frame = slice(GATE_SHIFT, GATE_SHIFT + D_MODEL)
    gp = jnp.concatenate([gpa_ref[...], gpb_ref[...]], axis=1).astype(F32)[:, frame]
    gd = jnp.concatenate([gda_ref[...], gdb_ref[...]], axis=1).astype(F32)[:, frame]
    t_pool = jnp.dot(y_pool, wpo_ref[...], preferred_element_type=F32)
    t_dn = jnp.dot(ydn_ref[...], wdo_ref[...], preferred_element_type=F32)
    merged = _sigmoid(gp) * t_pool + _sigmoid(gd) * t_dn
    h = x_ref[...] + jnp.dot(merged.astype(BF16), wo_ref[...], preferred_element_type=F32)
    ms = jnp.mean(h * h, axis=-1, keepdims=True)
    out_ref[...] = h * lax.rsqrt(ms + NORM_EPS) * fnw_ref[...]


def _resident(shape, index_map):
    return pl.BlockSpec(shape, index_map, pipeline_mode=pl.Buffered(1))


def _out_merge(proj, y_dn, x2d, meta_u, mix, scale, wpo, wdo, wo, fnw, *, batch, seq_len):
    tm = TM_OUT
    tiles = seq_len // tm
    kern = functools.partial(_out_merge_kernel, tm=tm)
    row = lambda b, t: b * tiles + t
    gp0, gd0 = COL_GP - GATE_SHIFT, COL_GD - GATE_SHIFT
    assert gp0 % D_MODEL == 0 and gd0 % D_MODEL == 0
    return pl.pallas_call(
        kern,
        out_shape=jax.ShapeDtypeStruct((batch * seq_len, D_MODEL), F32),
        grid=(batch, tiles),
        in_specs=[
            pl.BlockSpec((tm, POOL_WIDTH), lambda b, t: (row(b, t), COL_U // POOL_WIDTH)),
            pl.BlockSpec((tm, POOL_WIDTH), lambda b, t: (row(b, t), COL_ZP // POOL_WIDTH)),
            pl.BlockSpec((tm, D_MODEL), lambda b, t: (row(b, t), gp0 // D_MODEL)),
            pl.BlockSpec((tm, LANES), lambda b, t: (row(b, t), (gp0 + D_MODEL) // LANES)),
            pl.BlockSpec((tm, D_MODEL), lambda b, t: (row(b, t), gd0 // D_MODEL)),
            pl.BlockSpec((tm, LANES), lambda b, t: (row(b, t), (gd0 + D_MODEL) // LANES)),
            pl.BlockSpec((tm, DN_WIDTH), lambda b, t: (row(b, t), 0)),
            pl.BlockSpec((tm, D_MODEL), lambda b, t: (row(b, t), 0)),
            _resident((N_META, POOL_WIDTH), lambda b, t: (0, 0)),
            _resident((POOL_GROUPS, POOL_GROUP_DIM, POOL_GROUP_DIM), lambda b, t: (0, 0, 0)),
            _resident((1, POOL_WIDTH), lambda b, t: (0, 0)),
            _resident((POOL_WIDTH, D_MODEL), lambda b, t: (0, 0)),
            _resident((DN_WIDTH, D_MODEL), lambda b, t: (0, 0)),
            _resident((D_MODEL, D_MODEL), lambda b, t: (0, 0)),
            _resident((1, D_MODEL), lambda b, t: (0, 0)),
        ],
        out_specs=pl.BlockSpec((tm, D_MODEL), lambda b, t: (row(b, t), 0)),
        scratch_shapes=[pltpu.VMEM((N_META + tm, POOL_WIDTH), F32)],
        compiler_params=pltpu.CompilerParams(
            dimension_semantics=("arbitrary", "arbitrary"), vmem_limit_bytes=VMEM_LIMIT),
        name="out_merge",
    )(proj, proj, proj, proj, proj, proj, y_dn, x2d, meta_u, mix, scale, wpo, wdo, wo, fnw)


def kernel(x, meta_tokens, norm_w, w_in, conv_w, A_log, dt_bias, pool_mix, pool_scale, dn_norm_w,
           w_pool_out, w_dn_out, w_o, final_norm_w):
    batch, seq_len, _ = x.shape
    assert norm_w.shape[0] == 1, "single layer block"
    x2d = x.reshape(batch * seq_len, D_MODEL)

    rows = batch * seq_len
    w_t = jnp.swapaxes(w_in, 1, 2).reshape(w_in.shape[2], D_MODEL)
    gpar = jnp.zeros((2, LANES), F32)
    gpar = gpar.at[0, DN_HEADS:2 * DN_HEADS].set(A_log[0]).at[1, DN_HEADS:2 * DN_HEADS].set(dt_bias[0])
    nw = norm_w[0].reshape(1, D_MODEL)
    wpo = w_pool_out[0].astype(BF16)
    wdo = w_dn_out[0].astype(BF16)
    wo = w_o[0].astype(BF16)

    xn_all, gates = _norm_gates(x2d, meta_tokens.astype(F32), nw, w_t, gpar)
    proj = _in_proj(xn_all, w_t)

    front = CHUNK - N_META
    meta_proj = proj[rows:]
    meta_qkv = jnp.pad(meta_proj[:, COL_Q:COL_ZD], ((front, 0), (0, 0)))
    meta_gates = jnp.pad(gates[rows:], ((front, 0), (0, 0)))
    meta_u = meta_proj[:, COL_U:COL_U + POOL_WIDTH]

    y_dn = _deltanet(proj, meta_qkv, gates, meta_gates, conv_w[0], dn_norm_w[0].reshape(1, DN_HEAD_DIM),
                     batch=batch, seq_len=seq_len)

    out = _out_merge(proj, y_dn, x2d, meta_u, pool_mix[0].astype(BF16), pool_scale[0].reshape(1, POOL_WIDTH),
                     wpo, wdo, wo, final_norm_w.reshape(1, D_MODEL), batch=batch, seq_len=seq_len)
    return out.reshape(batch, seq_len, D_MODEL)
```

```python
import functools

import jax
import jax.numpy as jnp
from jax import lax
from jax.experimental import pallas as pl
from jax.experimental.pallas import tpu as pltpu

D_MODEL = 2048
N_META = 16
POOL_GROUPS = 4
POOL_WINDOWS = (2, 4, 8, 16)
POOL_WIDTH = D_MODEL // 2
POOL_GROUP_DIM = POOL_WIDTH // POOL_GROUPS
DN_HEADS = 16
DN_HEAD_DIM = 128
DN_WIDTH = DN_HEADS * DN_HEAD_DIM
CONV_WIDTH = 4
NORM_EPS = 1e-6

COL_U = 0
COL_ZP = COL_U + POOL_WIDTH
COL_Q = COL_ZP + POOL_WIDTH
COL_K = COL_Q + DN_WIDTH
COL_V = COL_K + DN_WIDTH
COL_ZD = COL_V + DN_WIDTH
BA_OFFSET = COL_ZD + DN_WIDTH
COL_GP = BA_OFFSET + 2 * DN_HEADS
COL_GD = COL_GP + D_MODEL
LANES = 128
SUBLANES = 8
GATE_SHIFT = COL_GP % LANES

PROJ_DTYPE = jnp.bfloat16
CHUNK = 64
LEAF = 16
TT_DN = 512
TM_NORM = 512
TN_PROJ = 256
ROW_BLK_PROJ = 1024
TM_OUT = 256
VMEM_LIMIT = 56 * 1024 * 1024

BF16 = jnp.bfloat16
F32 = jnp.float32


def _mm(a, b):
    return jnp.dot(a.astype(BF16), b.astype(BF16), preferred_element_type=F32)


def _mm_nt(a, b):
    return lax.dot_general(a.astype(BF16), b.astype(BF16), (((1,), (1,)), ((), ())),
                           preferred_element_type=F32)


def _mm_tn(a, b):
    return lax.dot_general(a.astype(BF16), b.astype(BF16), (((0,), (0,)), ((), ())),
                           preferred_element_type=F32)


def _sigmoid(x):
    return 0.5 * jnp.tanh(0.5 * x) + 0.5


def _silu(x):
    h = 0.5 * x
    return h * jnp.tanh(h) + h


def _softplus(x):
    return jnp.maximum(x, 0.0) + jnp.log1p(jnp.exp(-jnp.abs(x)))


def _norm_gates_kernel(x_ref, meta_ref, nw_ref, wba_ref, gpar_ref, xn_ref, gates_ref, *, x_tiles, sub):
    i = pl.program_id(0)
    nw = nw_ref[...]
    wba = wba_ref[...].astype(BF16)
    a_rate = jnp.exp(gpar_ref[0:1, :])
    dt_bias = gpar_ref[1:2, :]

    def rows_block(x, rows):
        ms = jnp.mean(x * x, axis=-1, keepdims=True)
        xn = (x * lax.rsqrt(ms + NORM_EPS) * nw).astype(BF16)
        xn_ref[rows, :] = xn
        ba = _mm_nt(xn, wba)
        lane = lax.broadcasted_iota(jnp.int32, ba.shape, 1)
        g = -a_rate * _softplus(ba + dt_bias)
        gates_ref[rows, :] = jnp.where(lane < DN_HEADS, _sigmoid(ba), jnp.where(lane < 2 * DN_HEADS, g, 0.0))

    @pl.when(i < x_tiles)
    def _():
        def body(r, carry):
            rows = pl.ds(pl.multiple_of(r * sub, sub), sub)
            rows_block(x_ref[rows, :], rows)
            return carry

        lax.fori_loop(0, x_ref.shape[0] // sub, body, 0)

    @pl.when(i == x_tiles)
    def _():
        rows_block(meta_ref[...], pl.ds(0, N_META))


def _norm_gates(x2d, meta, norm_w, w_t, gpar):
    rows = x2d.shape[0]
    tm = TM_NORM
    x_tiles = rows // tm
    kern = functools.partial(_norm_gates_kernel, x_tiles=x_tiles, sub=128)
    return pl.pallas_call(
        kern,
        out_shape=(jax.ShapeDtypeStruct((rows + N_META, D_MODEL), BF16),
                   jax.ShapeDtypeStruct((rows + N_META, LANES), F32)),
        grid=(x_tiles + 1,),
        in_specs=[
            pl.BlockSpec((tm, D_MODEL), lambda i: (jnp.minimum(i, x_tiles - 1), 0)),
            pl.BlockSpec((N_META, D_MODEL), lambda i: (0, 0)),
            pl.BlockSpec((1, D_MODEL), lambda i: (0, 0)),
            pl.BlockSpec((LANES, D_MODEL), lambda i: (BA_OFFSET // LANES, 0)),
            pl.BlockSpec((2, LANES), lambda i: (0, 0)),
        ],
        out_specs=(pl.BlockSpec((tm, D_MODEL), lambda i: (i, 0)),
                   pl.BlockSpec((tm, LANES), lambda i: (i, 0))),
        compiler_params=pltpu.CompilerParams(
            dimension_semantics=("arbitrary",), vmem_limit_bytes=VMEM_LIMIT),
        name="norm_gates",
    )(x2d, meta, norm_w, w_t, gpar)


def _in_proj_kernel(xn_ref, w_ref, cw_ref, proj_ref, acc_ref, *, rows, row_blk, seq_len):
    j = pl.program_id(0)
    tn = w_ref.shape[0]
    w = w_ref[...].astype(BF16)
    ms = slice(rows, rows + N_META)
    conv_tile = (j >= COL_Q // tn) & (j < COL_ZD // tn)
    silu_tile = ((j >= COL_ZP // tn) & (j < COL_Q // tn)) | ((j >= COL_ZD // tn) & (j < BA_OFFSET // tn))

    def plain(fn):
        for r in range(rows // row_blk):
            rs = slice(r * row_blk, (r + 1) * row_blk)
            proj_ref[rs, :] = fn(_mm_nt(xn_ref[rs, :], w)).astype(proj_ref.dtype)
        proj_ref[ms, :] = fn(_mm_nt(xn_ref[ms, :], w)).astype(proj_ref.dtype)

    @pl.when(conv_tile)
    def _():
        cw = cw_ref[...]
        hist_rows = SUBLANES
        n_blk = rows // row_blk

        def conv_silu(slot, n):
            y = acc_ref[slot, hist_rows:hist_rows + n, :] * cw[CONV_WIDTH - 1:CONV_WIDTH]
            for kk in range(CONV_WIDTH - 1):
                off = hist_rows - (CONV_WIDTH - 1) + kk
                y = y + acc_ref[slot, off:off + n, :] * cw[kk:kk + 1]
            return _silu(y)

        meta_acc = _mm_nt(xn_ref[ms, :], w)
        acc_ref[0, 0:hist_rows, :] = jnp.zeros((hist_rows, tn), F32)
        acc_ref[0, hist_rows:hist_rows + N_META, :] = meta_acc
        proj_ref[ms, :] = conv_silu(0, N_META).astype(proj_ref.dtype)
        meta_tail = meta_acc[N_META - hist_rows:]
        for r in range(n_blk + 1):
            if r < n_blk:
                slot = r % 2
                rs = slice(r * row_blk, (r + 1) * row_blk)
                acc = _mm_nt(xn_ref[rs, :], w)
                batch_start = (r * row_blk) % seq_len == 0
                acc_ref[slot, 0:hist_rows, :] = (
                    meta_tail if batch_start else acc_ref[1 - slot, row_blk:row_blk + hist_rows, :])
                acc_ref[slot, hist_rows:hist_rows + row_blk, :] = acc
            if r > 0:
                ps = slice((r - 1) * row_blk, r * row_blk)
                proj_ref[ps, :] = conv_silu((r - 1) % 2, row_blk).astype(proj_ref.dtype)

    @pl.when(silu_tile)
    def _():
        plain(_silu)

    @pl.when(jnp.logical_not(conv_tile | silu_tile))
    def _():
        plain(lambda a: a)


def _in_proj(xn_all, w_t, conv_w, *, seq_len):
    rows = xn_all.shape[0] - N_META
    cols = w_t.shape[0]
    tn = TN_PROJ
    n_conv_tiles = (COL_ZD - COL_Q) // tn
    kern = functools.partial(_in_proj_kernel, rows=rows, row_blk=ROW_BLK_PROJ, seq_len=seq_len)
    return pl.pallas_call(
        kern,
        out_shape=jax.ShapeDtypeStruct((rows + N_META, cols), PROJ_DTYPE),
        grid=(pl.cdiv(cols, tn),),
        in_specs=[
            pl.BlockSpec((rows + N_META, D_MODEL), lambda j: (0, 0), pipeline_mode=pl.Buffered(1)),
            pl.BlockSpec((tn, D_MODEL), lambda j: (j, 0)),
            pl.BlockSpec((CONV_WIDTH, tn), lambda j: (0, jnp.clip(j - COL_Q // tn, 0, n_conv_tiles - 1))),
        ],
        out_specs=pl.BlockSpec((rows + N_META, tn), lambda j: (0, j)),
        scratch_shapes=[pltpu.VMEM((2, SUBLANES + ROW_BLK_PROJ, tn), F32)],
        compiler_params=pltpu.CompilerParams(
            dimension_semantics=("arbitrary",), vmem_limit_bytes=VMEM_LIMIT),
        name="in_proj",
    )(xn_all, w_t, conv_w)


PAIR = 2 * DN_HEAD_DIM


def _split3(x):
    hi = x.astype(BF16)
    r1 = x - hi.astype(F32)
    mid = r1.astype(BF16)
    lo = (r1 - mid.astype(F32)).astype(BF16)
    return hi, mid, lo


def _block_diag_rows(y, half):
    lane = lax.broadcasted_iota(jnp.int32, y.shape, 1)
    zero = jnp.zeros_like(y)
    return jnp.concatenate([jnp.where(lane < half, y, zero), jnp.where(lane >= half, y, zero)], axis=0)


def _inverse_masks(c):
    row = lax.broadcasted_iota(jnp.int32, (c, 2 * c), 0)
    col = lax.broadcasted_iota(jnp.int32, (c, 2 * c), 1) & (c - 1)
    blk = lambda x, m: x >> (m.bit_length() - 1)
    eye = (row == col).astype(F32)
    same_leaf = blk(row, LEAF) == blk(col, LEAF)
    offs = []
    m = 2 * LEAF
    while m <= c:
        offs.append((blk(row, m) == blk(col, m)) & (blk(row, m // 2) != blk(col, m // 2)))
        m *= 2
    return eye, same_leaf, offs


def _inv_unit_lower_packed(lps, c, masks, fill):
    eye, same_leaf, offs = masks
    bd = lambda y: _block_diag_rows(y.astype(BF16), c)
    ps = [jnp.where(same_leaf, -l, 0.0) for l in lps]
    ts = [eye + p for p in ps]
    dep = ps[0]
    ps = [_mm(p, bd(p)) for p in ps]
    fill(dep, ps, 2)
    span = 4
    while span <= LEAF:
        dep = ps[0]
        last = span == LEAF
        prod = [_mm(t if last else jnp.concatenate([t, p], axis=0), bd(p)) for t, p in zip(ts, ps)]
        ts = [t + x[:c] for t, x in zip(ts, prod)]
        if not last:
            ps = [x[c:] for x in prod]
        fill(dep, ts)
        span *= 2
    for off in offs:
        dep = ts[0]
        inner = [_mm(jnp.where(off, l, 0.0), bd(t)) for l, t in zip(lps, ts)]
        fill(dep, inner)
        dep = inner[0]
        ts = [t - _mm(t, bd(x)) for t, x in zip(ts, inner)]
        fill(dep, ts)
    return ts


def _deltanet_kernel(q_ref, k_ref, v_ref, z_ref, gates_ref, mqkv_ref, mgates_ref, dnw_ref,
                     y_ref, stage_ref, gstage_ref, state_ref, qn_s, kn_s, o_s, *, tt):
    c = CHUNK
    hd = DN_HEAD_DIM
    n_pairs = DN_HEADS // 2
    t_idx = pl.program_id(1)
    pairs = range(n_pairs)
    n_chunks = tt // c

    @pl.when(t_idx == 0)
    def _():
        for idx in range(3):
            stage_ref[idx, 0:c, :] = mqkv_ref[:, idx * DN_WIDTH:(idx + 1) * DN_WIDTH]
            stage_ref[idx, c + tt:2 * c + tt, :] = jnp.zeros((c, DN_WIDTH), BF16)
        gstage_ref[0:c, :] = mgates_ref[...]
        state_ref[...] = jnp.zeros(state_ref.shape, F32)
        o_s[...] = jnp.zeros(o_s.shape, F32)

    for idx, x_ref in enumerate((q_ref, k_ref, v_ref)):
        stage_ref[idx, c:c + tt, :] = x_ref[...]
    gstage_ref[c:c + tt, :] = gates_ref[...]

    rowp = lax.broadcasted_iota(jnp.int32, (c, 2 * c), 0)
    colp = lax.broadcasted_iota(jnp.int32, (c, 2 * c), 1) & (c - 1)
    causal = rowp >= colp
    strict = rowp > colp
    lane_p = lax.broadcasted_iota(jnp.int32, (c, 2 * c), 1)
    inv_masks = _inverse_masks(c)
    tril = (lax.broadcasted_iota(jnp.int32, (c, c), 0) >= lax.broadcasted_iota(jnp.int32, (c, c), 1))
    tril3 = jnp.concatenate([tril.astype(BF16)] * 3, axis=1)
    dnw = dnw_ref[...]
    zero_blk = jnp.zeros((hd, hd), BF16)

    def l2n(xh, scale):
        return xh * (lax.rsqrt(jnp.sum(xh * xh, axis=-1, keepdims=True) + NORM_EPS) * scale)

    def prepare_pair(j, p, zero=None):
        base = pl.multiple_of(j * c, c)
        cols = slice(p * PAIR, (p + 1) * PAIR)
        slot = j & 1
        for idx, (dst, scale) in enumerate(((qn_s, hd ** -0.5), (kn_s, 1.0))):
            x = stage_ref[idx, pl.ds(base, c), cols].astype(F32)
            if zero is not None:
                x = x + jnp.concatenate([zero[0:1], zero[0:1]], axis=1)
            dst[slot, :, cols] = jnp.concatenate([l2n(x[:, :hd], scale), l2n(x[:, hd:], scale)], axis=1)
        r0 = pl.multiple_of((j >> 24) * SUBLANES, SUBLANES)
        back = qn_s[slot, pl.ds(r0, SUBLANES), cols] + kn_s[slot, pl.ds(r0, SUBLANES), cols]
        return back[:, :hd] + back[:, hd:]

    def finish_pair(row0, p, zero=None):
        rows = pl.ds(row0, c)
        for half in range(2):
            cs = slice(p * PAIR + half * hd, p * PAIR + (half + 1) * hd)
            oh = o_s[:, cs]
            if zero is not None:
                oh = oh + zero[0:1]
            oh = oh * lax.rsqrt(jnp.mean(oh * oh, axis=-1, keepdims=True) + NORM_EPS) * dnw
            y_ref[rows, cs] = (oh * z_ref[rows, cs].astype(F32)).astype(y_ref.dtype)
        r0 = pl.multiple_of((row0 >> 24) * (2 * SUBLANES), 2 * SUBLANES)
        back = y_ref[pl.ds(r0, 2 * SUBLANES), p * PAIR:(p + 1) * PAIR].astype(F32)[0:SUBLANES]
        return back[:, :hd] + back[:, hd:]

    def out_rows(j):
        return pl.multiple_of(jnp.maximum(j - 1, 0) * c, c)

    def exact_rows_dot(lhs3, x):
        hi, mid, lo = _split3(x)
        return jnp.dot(lhs3, jnp.concatenate([hi, mid, lo], axis=0), preferred_element_type=F32)

    def pair_cols(mat, la, lb):
        return jnp.where(lane_p < c, mat[:, la:la + 1], mat[:, lb:lb + 1])

    def pair_rows(mat_t, ra, rb):
        return jnp.concatenate([mat_t[ra:ra + 1, :], mat_t[rb:rb + 1, :]], axis=1)

    def zero_of(v):
        return jnp.where(v != v, v, 0.0)

    def chunk_body(j, carry):
        base = pl.multiple_of(j * c, c)
        prev_rows = out_rows(j - 1)
        fillers = []
        for p in pairs:
            fillers.append(functools.partial(finish_pair, prev_rows, p))
            fillers.append(functools.partial(prepare_pair, j + 1, p))
        fillers.reverse()

        def fill(dep, into, n=1):
            zero = None if dep is None else zero_of(dep[0:SUBLANES, 0:LANES].astype(F32))
            for _ in range(n):
                if fillers:
                    done = zero_of(fillers.pop()(zero=zero))[0:1]
                    if into[-1].shape[1] == PAIR:
                        done = jnp.concatenate([done, done], axis=1)
                    into[-1] = into[-1] + done

        slot = j & 1
        head_cols = lambda h: slice(h * hd, (h + 1) * hd)
        pair_cols_of = lambda p: slice(p * PAIR, (p + 1) * PAIR)
        gt =gstage_ref[pl.ds(base, c), :]
        gcum = exact_rows_dot(tril3, gt)
        g_last = gcum[c - 1:c, :]
        eg = jnp.exp(gcum)
        e_rest = jnp.exp(g_last - gcum)
        e_last = jnp.exp(g_last)
        gt_t = gt.T
        gcum_t = gcum.T

        ha = [2 * p for p in pairs]
        hb = [2 * p + 1 for p in pairs]
        kpair = [kn_s[slot, :, pair_cols_of(p)].astype(BF16) for p in pairs]
        kbd = [_block_diag_rows(kp, hd) for kp in kpair]
        qpair = [qn_s[slot, :, pair_cols_of(p)].astype(BF16) for p in pairs]
        for p in (1, 3, 5, 7):
            staged = [qn_s[slot, :, pair_cols_of(p)]]
            fill(None, staged)
            qpair[p] = staged[0].astype(BF16)
        aq = [_mm_nt(jnp.concatenate([kpair[p], qpair[p]], axis=0), kbd[p]) for p in pairs]
        g_col = [pair_cols(gcum, DN_HEADS + ha[p], DN_HEADS + hb[p]) for p in pairs]
        g_row = [pair_rows(gcum_t, DN_HEADS + ha[p], DN_HEADS + hb[p]) for p in pairs]
        b_col = [pair_cols(gt, ha[p], hb[p]) for p in pairs]
        b_row = [pair_rows(gt_t, ha[p], hb[p]) for p in pairs]
        dec = [jnp.where(causal, jnp.exp(jnp.where(causal, g_col[p] - g_row[p], 0.0)), 0.0) for p in pairs]
        lps = [jnp.where(strict, aq[p][:c] * b_col[p] * dec[p], 0.0) for p in pairs]
        pps = [aq[p][c:] * dec[p] for p in pairs]
        tps = _inv_unit_lower_packed(lps, c, inv_masks, fill)
        vbd = [_block_diag_rows(stage_ref[2, pl.ds(base, c), pair_cols_of(p)], hd) for p in pairs]
        u = [_mm(tps[p] * b_row[p], vbd[p]) for p in pairs]
        w = [_mm(tps[p] * (b_row[p] * jnp.exp(g_row[p])), kbd[p]) for p in pairs]
        fill(tps[0], w)
        s_old = [state_ref[h] for h in range(DN_HEADS)]
        qd = [jnp.concatenate(
            [qn_s[slot, :, head_cols(ha[p])] * eg[:, DN_HEADS + ha[p]:DN_HEADS + ha[p] + 1],
             qn_s[slot, :, head_cols(hb[p])] * eg[:, DN_HEADS + hb[p]:DN_HEADS + hb[p] + 1]], axis=1)
            for p in pairs]
        sbd = [jnp.concatenate([jnp.concatenate([s_old[ha[p]].astype(BF16), zero_blk], axis=1),
                                jnp.concatenate([zero_blk, s_old[hb[p]].astype(BF16)], axis=1)], axis=0)
               for p in pairs]
        ws = [_mm(jnp.concatenate([w[p], qd[p]], axis=0), sbd[p]) for p in pairs]
        fill(w[0], ws)
        v_new = [u[p] - ws[p][:c] for p in pairs]
        o = [ws[p][c:] + _mm(pps[p], _block_diag_rows(v_new[p].astype(BF16), hd)) for p in pairs]
        for h in range(DN_HEADS):
            p, half = divmod(h, 2)
            lane_h = DN_HEADS + h
            kd = kn_s[slot, :, head_cols(h)] * e_rest[:, lane_h:lane_h + 1]
            upd = _mm_tn(kd, v_new[p][:, half * hd:(half + 1) * hd])
            state_ref[h] = s_old[h] * e_last[:, lane_h:lane_h + 1] + upd
        fill(ws[0], o, len(fillers))
        for p in pairs:
            o_s[:, p * PAIR:(p + 1) * PAIR] = o[p]
        return carry

    start = jnp.where(t_idx == 0, 0, 1)
    for p in pairs:
        prepare_pair(start, p)
    lax.fori_loop(start, n_chunks + 1, chunk_body, 0)
    last_rows = out_rows(jnp.int32(n_chunks))
    for p in pairs:
        finish_pair(last_rows, p)


def _deltanet(proj, meta_qkv, gates, meta_gates, dn_norm_w, *, batch, seq_len):
    tt = TT_DN
    nt = seq_len // tt
    c = CHUNK
    kern = functools.partial(_deltanet_kernel, tt=tt)
    row = lambda b, t: b * nt + t
    const = lambda shape: pl.BlockSpec(shape, lambda b, t: (0,) * len(shape), pipeline_mode=pl.Buffered(1))
    return pl.pallas_call(
        kern,
        out_shape=jax.ShapeDtypeStruct((batch * seq_len, DN_WIDTH), BF16),
        grid=(batch, nt),
        in_specs=[
            pl.BlockSpec((tt, DN_WIDTH), lambda b, t: (row(b, t), COL_Q // DN_WIDTH)),
            pl.BlockSpec((tt, DN_WIDTH), lambda b, t: (row(b, t), COL_K // DN_WIDTH)),
            pl.BlockSpec((tt, DN_WIDTH), lambda b, t: (row(b, t), COL_V // DN_WIDTH)),
            pl.BlockSpec((tt, DN_WIDTH), lambda b, t: (row(b, t), COL_ZD // DN_WIDTH)),
            pl.BlockSpec((tt, LANES), lambda b, t: (row(b, t), 0)),
            const((c, 3 * DN_WIDTH)),
            const((c, LANES)),
            const((1, DN_HEAD_DIM)),
        ],
        out_specs=pl.BlockSpec((tt, DN_WIDTH), lambda b, t: (row(b, t), 0)),
        scratch_shapes=[
            pltpu.VMEM((3, 2 * c + tt, DN_WIDTH), BF16),
            pltpu.VMEM((c + tt, LANES), F32),
            pltpu.VMEM((DN_HEADS, DN_HEAD_DIM, DN_HEAD_DIM), F32),
            pltpu.VMEM((2, c, DN_WIDTH), F32),
            pltpu.VMEM((2, c, DN_WIDTH), F32),
            pltpu.VMEM((c, DN_WIDTH), F32),
        ],
        compiler_params=pltpu.CompilerParams(
            dimension_semantics=("arbitrary", "arbitrary"), vmem_limit_bytes=VMEM_LIMIT),
        name="deltanet",
    )(proj, proj, proj, proj, gates, meta_qkv, meta_gates, dn_norm_w)


def _out_merge_kernel(u_ref, zp_ref, gpa_ref, gpb_ref, gda_ref, gdb_ref, ydn_ref, x_ref, mu_ref, mix_ref,
                      scale_ref, wpo_ref, wdo_ref, wo_ref, fnw_ref, out_ref, ubuf_ref, *, tm):
    t = pl.program_id(1)
    hist = N_META

    @pl.when(t == 0)
    def _():
        ubuf_ref[0:hist, :] = mu_ref[...].astype(F32)

    @pl.when(t > 0)
    def _():
        ubuf_ref[0:hist, :] = ubuf_ref[tm:tm + hist, :]

    ubuf_ref[hist:hist + tm, :] = u_ref[...].astype(F32)

    pooled = []
    for gi, w in enumerate(POOL_WINDOWS):
        cs = slice(gi * POOL_GROUP_DIM, (gi + 1) * POOL_GROUP_DIM)
        acc = ubuf_ref[hist:hist + tm, cs]
        cur = acc
        for s in range(1, w):
            acc = acc + ubuf_ref[hist - s:hist - s + tm, cs]
        d = acc * (1.0 / w) - cur
        mixed = _mm(d, mix_ref[gi])
        zp = zp_ref[:, cs].astype(F32)
        pooled.append((mixed * scale_ref[:, cs] * zp).astype(BF16))
    y_pool = jnp.concatenate(pooled, axis=1)

    frame = slice(GATE_SHIFT, GATE_SHIFT + D_MODEL)
    gp = jnp.concatenate([gpa_ref[...], gpb_ref[...]], axis=1).astype(F32)[:, frame]
    gd = jnp.concatenate([gda_ref[...], gdb_ref[...]], axis=1).astype(F32)[:, frame]
    t_pool = jnp.dot(y_pool, wpo_ref[...], preferred_element_type=F32)
    t_dn = jnp.dot(ydn_ref[...], wdo_ref[...], preferred_element_type=F32)
    merged = _sigmoid(gp) * t_pool + _sigmoid(gd) * t_dn
    h = x_ref[...] + jnp.dot(merged.astype(BF16), wo_ref[...], preferred_element_type=F32)
    ms = jnp.mean(h * h, axis=-1, keepdims=True)
    out_ref[...] = h * lax.rsqrt(ms + NORM_EPS) * fnw_ref[...]


def _resident(shape, index_map):
    return pl.BlockSpec(shape, index_map, pipeline_mode=pl.Buffered(1))


def _out_merge(proj, y_dn, x2d, meta_u, mix, scale, wpo, wdo, wo, fnw, *, batch, seq_len):
    tm = TM_OUT
    tiles = seq_len // tm
    kern = functools.partial(_out_merge_kernel, tm=tm)
    row = lambda b, t: b * tiles + t
    gp0, gd0 = COL_GP - GATE_SHIFT, COL_GD - GATE_SHIFT
    assert gp0 % D_MODEL == 0 and gd0 % D_MODEL == 0
    return pl.pallas_call(
        kern,
        out_shape=jax.ShapeDtypeStruct((batch * seq_len, D_MODEL), F32),
        grid=(batch, tiles),
        in_specs=[
            pl.BlockSpec((tm, POOL_WIDTH), lambda b, t: (row(b, t), COL_U // POOL_WIDTH)),
            pl.BlockSpec((tm, POOL_WIDTH), lambda b, t: (row(b, t), COL_ZP // POOL_WIDTH)),
            pl.BlockSpec((tm, D_MODEL), lambda b, t: (row(b, t), gp0 // D_MODEL)),
            pl.BlockSpec((tm, LANES), lambda b, t: (row(b, t), (gp0 + D_MODEL) // LANES)),
            pl.BlockSpec((tm, D_MODEL), lambda b, t: (row(b, t), gd0 // D_MODEL)),
            pl.BlockSpec((tm, LANES), lambda b, t: (row(b, t), (gd0 + D_MODEL) // LANES)),
            pl.BlockSpec((tm, DN_WIDTH), lambda b, t: (row(b, t), 0)),
            pl.BlockSpec((tm, D_MODEL), lambda b, t: (row(b, t), 0)),
            _resident((N_META, POOL_WIDTH), lambda b, t: (0, 0)),
            _resident((POOL_GROUPS, POOL_GROUP_DIM, POOL_GROUP_DIM), lambda b, t: (0, 0, 0)),
            _resident((1, POOL_WIDTH), lambda b, t: (0, 0)),
            _resident((POOL_WIDTH, D_MODEL), lambda b, t: (0, 0)),
            _resident((DN_WIDTH, D_MODEL), lambda b, t: (0, 0)),
            _resident((D_MODEL, D_MODEL), lambda b, t: (0, 0)),
            _resident((1, D_MODEL), lambda b, t: (0, 0)),
        ],
        out_specs=pl.BlockSpec((tm, D_MODEL), lambda b, t: (row(b, t), 0)),
        scratch_shapes=[pltpu.VMEM((N_META + tm, POOL_WIDTH), F32)],
        compiler_params=pltpu.CompilerParams(
            dimension_semantics=("arbitrary", "arbitrary"), vmem_limit_bytes=VMEM_LIMIT),
        name="out_merge",
    )(proj, proj, proj, proj, proj, proj, y_dn, x2d, meta_u, mix, scale, wpo, wdo, wo, fnw)


def kernel(x, meta_tokens, norm_w, w_in, conv_w, A_log, dt_bias, pool_mix, pool_scale, dn_norm_w,
           w_pool_out, w_dn_out, w_o, final_norm_w):
    batch, seq_len, _ = x.shape
    assert norm_w.shape[0] == 1, "single layer block"
    x2d = x.reshape(batch * seq_len, D_MODEL)

    rows = batch * seq_len
    w_t = jnp.swapaxes(w_in, 1, 2).reshape(w_in.shape[2], D_MODEL)
    gpar = jnp.zeros((2, LANES), F32)
    gpar = gpar.at[0, DN_HEADS:2 * DN_HEADS].set(A_log[0]).at[1, DN_HEADS:2 * DN_HEADS].set(dt_bias[0])
    nw = norm_w[0].reshape(1, D_MODEL)
    wpo = w_pool_out[0].astype(BF16)
    wdo = w_dn_out[0].astype(BF16)
    wo = w_o[0].astype(BF16)

    xn_all, gates = _norm_gates(x2d, meta_tokens.astype(F32), nw, w_t, gpar)
    proj = _in_proj(xn_all, w_t, conv_w[0], seq_len=seq_len)

    front = CHUNK - N_META
    meta_proj = proj[rows:]
    meta_qkv = jnp.pad(meta_proj[:, COL_Q:COL_ZD], ((front, 0), (0, 0)))
    meta_gates = jnp.pad(gates[rows:], ((front, 0), (0, 0)))
    meta_u = meta_proj[:, COL_U:COL_U + POOL_WIDTH]

    y_dn = _deltanet(proj, meta_qkv, gates, meta_gates, dn_norm_w[0].reshape(1, DN_HEAD_DIM),
                     batch=batch, seq_len=seq_len)

    out = _out_merge(proj, y_dn, x2d, meta_u, pool_mix[0].astype(BF16), pool_scale[0].reshape(1, POOL_WIDTH),
                     wpo, wdo, wo, final_norm_w.reshape(1, D_MODEL), batch=batch, seq_len=seq_len)
    return out.reshape(batch, seq_len, D_MODEL)
```

```python
import functools

import jax
import jax.numpy as jnp
from jax import lax
from jax.experimental import pallas as pl
from jax.experimental.pallas import tpu as pltpu

D_MODEL = 2048
N_META = 16
POOL_GROUPS = 4
POOL_WINDOWS = (2, 4, 8, 16)
POOL_WIDTH = D_MODEL // 2
POOL_GROUP_DIM = POOL_WIDTH // POOL_GROUPS
DN_HEADS = 16
DN_HEAD_DIM = 128
DN_WIDTH = DN_HEADS * DN_HEAD_DIM
CONV_WIDTH = 4
NORM_EPS = 1e-6

COL_U = 0
COL_ZP = COL_U + POOL_WIDTH
COL_Q = COL_ZP + POOL_WIDTH
COL_K = COL_Q + DN_WIDTH
COL_V = COL_K + DN_WIDTH
COL_ZD = COL_V + DN_WIDTH
BA_OFFSET = COL_ZD + DN_WIDTH
COL_GP = BA_OFFSET + 2 * DN_HEADS
COL_GD = COL_GP + D_MODEL
LANES = 128
SUBLANES = 8
GATE_SHIFT = COL_GP % LANES

PROJ_DTYPE = jnp.bfloat16
CHUNK = 64
LEAF = 16
TT_DN = 512
TM_NORM = 512
TN_PROJ = 256
ROW_BLK_PROJ = 1024
TM_OUT = 256
VMEM_LIMIT = 56 * 1024 * 1024

BF16 = jnp.bfloat16
F32 = jnp.float32


def _mm(a, b):
    return jnp.dot(a.astype(BF16), b.astype(BF16), preferred_element_type=F32)


def _mm_nt(a, b):
    return lax.dot_general(a.astype(BF16), b.astype(BF16), (((1,), (1,)), ((), ())),
                           preferred_element_type=F32)


def _mm_tn(a, b):
    return lax.dot_general(a.astype(BF16), b.astype(BF16), (((0,), (0,)), ((), ())),
                           preferred_element_type=F32)


def _sigmoid(x):
    return 0.5 * jnp.tanh(0.5 * x) + 0.5


def _silu(x):
    h = 0.5 * x
    return h * jnp.tanh(h) + h


def _softplus(x):
    return jnp.maximum(x, 0.0) + jnp.log1p(jnp.exp(-jnp.abs(x)))


def _norm_gates_kernel(x_ref, meta_ref, nw_ref, wba_ref, gpar_ref, xn_ref, gates_ref, *, x_tiles, sub):
    i = pl.program_id(0)
    nw = nw_ref[...]
    wba = wba_ref[...].astype(BF16)
    a_rate = jnp.exp(gpar_ref[0:1, :])
    dt_bias = gpar_ref[1:2, :]

    def rows_block(x, rows):
        ms = jnp.mean(x * x, axis=-1, keepdims=True)
        xn = (x * lax.rsqrt(ms + NORM_EPS) * nw).astype(BF16)
        xn_ref[rows, :] = xn
        ba = _mm_nt(xn, wba)
        lane = lax.broadcasted_iota(jnp.int32, ba.shape, 1)
        g = -a_rate * _softplus(ba + dt_bias)
        gates_ref[rows, :] = jnp.where(lane < DN_HEADS, _sigmoid(ba), jnp.where(lane < 2 * DN_HEADS, g, 0.0))

    @pl.when(i < x_tiles)
    def _():
        def body(r, carry):
            rows = pl.ds(pl.multiple_of(r * sub, sub), sub)
            rows_block(x_ref[rows, :], rows)
            return carry

        lax.fori_loop(0, x_ref.shape[0] // sub, body, 0)

    @pl.when(i == x_tiles)
    def _():
        rows_block(meta_ref[...], pl.ds(0, N_META))


def _norm_gates(x2d, meta, norm_w, w_t, gpar):
    rows = x2d.shape[0]
    tm = TM_NORM
    x_tiles = rows // tm
    kern = functools.partial(_norm_gates_kernel, x_tiles=x_tiles, sub=128)
    return pl.pallas_call(
        kern,
        out_shape=(jax.ShapeDtypeStruct((rows + N_META, D_MODEL), BF16),
                   jax.ShapeDtypeStruct((rows + N_META, LANES), F32)),
        grid=(x_tiles + 1,),
        in_specs=[
            pl.BlockSpec((tm, D_MODEL), lambda i: (jnp.minimum(i, x_tiles - 1), 0)),
            pl.BlockSpec((N_META, D_MODEL), lambda i: (0, 0)),
            pl.BlockSpec((1, D_MODEL), lambda i: (0, 0)),
            pl.BlockSpec((LANES, D_MODEL), lambda i: (BA_OFFSET // LANES, 0)),
            pl.BlockSpec((2, LANES), lambda i: (0, 0)),
        ],
        out_specs=(pl.BlockSpec((tm, D_MODEL), lambda i: (i, 0)),
                   pl.BlockSpec((tm, LANES), lambda i: (i, 0))),
        compiler_params=pltpu.CompilerParams(
            dimension_semantics=("arbitrary",), vmem_limit_bytes=VMEM_LIMIT),
        name="norm_gates",
    )(x2d, meta, norm_w, w_t, gpar)


def _in_proj_kernel(xn_ref, w_ref, cw_ref, proj_ref, acc_ref, *, rows, row_blk, seq_len):
    j = pl.program_id(0)
    tn = w_ref.shape[0]
    w = w_ref[...].astype(BF16)
    ms = slice(rows, rows + N_META)
    conv_tile = (j >= COL_Q // tn) & (j < COL_ZD // tn)
    silu_tile = ((j >= COL_ZP // tn) & (j < COL_Q // tn)) | ((j >= COL_ZD // tn) & (j < BA_OFFSET // tn))

    def plain(fn):
        for r in range(rows // row_blk):
            rs = slice(r * row_blk, (r + 1) * row_blk)
            proj_ref[rs, :] = fn(_mm_nt(xn_ref[rs, :], w)).astype(proj_ref.dtype)
        proj_ref[ms, :] = fn(_mm_nt(xn_ref[ms, :], w)).astype(proj_ref.dtype)

    @pl.when(conv_tile)
    def _():
        cw = cw_ref[...]
        hist_rows = SUBLANES
        n_blk = rows // row_blk

        def conv_silu(slot, n):
            y = acc_ref[slot, hist_rows:hist_rows + n, :] * cw[CONV_WIDTH - 1:CONV_WIDTH]
            for kk in range(CONV_WIDTH - 1):
                off = hist_rows - (CONV_WIDTH - 1) + kk
                y = y + acc_ref[slot, off:off + n, :] * cw[kk:kk + 1]
            return _silu(y)

        meta_acc = _mm_nt(xn_ref[ms, :], w)
        acc_ref[0, 0:hist_rows, :] = jnp.zeros((hist_rows, tn), F32)
        acc_ref[0, hist_rows:hist_rows + N_META, :] = meta_acc
        proj_ref[ms, :] = conv_silu(0, N_META).astype(proj_ref.dtype)
        meta_tail = meta_acc[N_META - hist_rows:]
        for r in range(n_blk + 1):
            if r < n_blk:
                slot = r % 2
                rs = slice(r * row_blk, (r + 1) * row_blk)
                acc = _mm_nt(xn_ref[rs, :], w)
                batch_start = (r * row_blk) % seq_len == 0
                acc_ref[slot, 0:hist_rows, :] = (
                    meta_tail if batch_start else acc_ref[1 - slot, row_blk:row_blk + hist_rows, :])
                acc_ref[slot, hist_rows:hist_rows + row_blk, :] = acc
            if r > 0:
                ps = slice((r - 1) * row_blk, r * row_blk)
                proj_ref[ps, :] = conv_silu((r - 1) % 2, row_blk).astype(proj_ref.dtype)

    @pl.when(silu_tile)
    def _():
        plain(_silu)

    @pl.when(jnp.logical_not(conv_tile | silu_tile))
    def _():
        plain(lambda a: a)


def _in_proj(xn_all, w_t, conv_w, *, seq_len):
    rows = xn_all.shape[0] - N_META
    cols = w_t.shape[0]
    tn = TN_PROJ
    n_conv_tiles = (COL_ZD - COL_Q) // tn
    kern = functools.partial(_in_proj_kernel, rows=rows, row_blk=ROW_BLK_PROJ, seq_len=seq_len)
    return pl.pallas_call(
        kern,
        out_shape=jax.ShapeDtypeStruct((rows + N_META, cols), PROJ_DTYPE),
        grid=(pl.cdiv(cols, tn),),
        in_specs=[
            pl.BlockSpec((rows + N_META, D_MODEL), lambda j: (0, 0), pipeline_mode=pl.Buffered(1)),
            pl.BlockSpec((tn, D_MODEL), lambda j: (j, 0)),
            pl.BlockSpec((CONV_WIDTH, tn), lambda j: (0, jnp.clip(j - COL_Q // tn, 0, n_conv_tiles - 1))),
        ],
        out_specs=pl.BlockSpec((rows + N_META, tn), lambda j: (0, j)),
        scratch_shapes=[pltpu.VMEM((2, SUBLANES + ROW_BLK_PROJ, tn), F32)],
        compiler_params=pltpu.CompilerParams(
            dimension_semantics=("arbitrary",), vmem_limit_bytes=VMEM_LIMIT),
        name="in_proj",
    )(xn_all, w_t, conv_w)


PAIR = 2 * DN_HEAD_DIM


def _split3(x):
    hi = x.astype(BF16)
    r1 = x - hi.astype(F32)
    mid = r1.astype(BF16)
    lo = (r1 - mid.astype(F32)).astype(BF16)
    return hi, mid, lo


def _block_diag_rows(y, half):
    lane = lax.broadcasted_iota(jnp.int32, y.shape, 1)
    zero = jnp.zeros_like(y)
    return jnp.concatenate([jnp.where(lane < half, y, zero), jnp.where(lane >= half, y, zero)], axis=0)


def _inverse_masks(c):
    row = lax.broadcasted_iota(jnp.int32, (c, 2 * c), 0)
    col = lax.broadcasted_iota(jnp.int32, (c, 2 * c), 1) & (c - 1)
    blk = lambda x, m: x >> (m.bit_length() - 1)
    eye = (row == col).astype(F32)
    same_leaf = blk(row, LEAF) == blk(col, LEAF)
    offs = []
    m = 2 * LEAF
    while m <= c:
        offs.append((blk(row, m) == blk(col, m)) & (blk(row, m // 2) != blk(col, m // 2)))
        m *= 2
    return eye, same_leaf, offs


def _inv_unit_lower_packed(lps, c, masks, fill):
    eye, same_leaf, offs = masks
    bd = lambda y: _block_diag_rows(y.astype(BF16), c)
    ps = [jnp.where(same_leaf, -l, 0.0) for l in lps]
    ts = [eye + p for p in ps]
    dep = ps[0]
    ps = [_mm(p, bd(p)) for p in ps]
    fill(dep, ps, 2)
    span = 4
    while span <= LEAF:
        dep = ps[0]
        last = span == LEAF
        prod = [_mm(t if last else jnp.concatenate([t, p], axis=0), bd(p)) for t, p in zip(ts, ps)]
        ts = [t + x[:c] for t, x in zip(ts, prod)]
        if not last:
            ps = [x[c:] for x in prod]
        fill(dep, ts)
        span *= 2
    for off in offs:
        dep = ts[0]
        inner = [_mm(jnp.where(off, l, 0.0), bd(t)) for l, t in zip(lps, ts)]
        fill(dep, inner)
        dep = inner[0]
        ts = [t - _mm(t, bd(x)) for t, x in zip(ts, inner)]
        fill(dep, ts)
    return ts


def _deltanet_kernel(q_ref, k_ref, v_ref, z_ref, gates_ref, mqkv_ref, mgates_ref, dnw_ref,
                     y_ref, stage_ref, gstage_ref, state_ref, qn_s, kn_s, o_s, gsm_s, gl_s, gp_s, gr_s, *, tt):
    c = CHUNK
    hd = DN_HEAD_DIM
    n_pairs = DN_HEADS // 2
    t_idx = pl.program_id(1)
    pairs = range(n_pairs)
    n_chunks = tt // c

    @pl.when(t_idx == 0)
    def _():
        for idx in range(3):
            stage_ref[idx, 0:c, :] = mqkv_ref[:, idx * DN_WIDTH:(idx + 1) * DN_WIDTH]
            stage_ref[idx, c + tt:2 * c + tt, :] = jnp.zeros((c, DN_WIDTH), BF16)
        gstage_ref[0:c, :] = mgates_ref[...]
        gstage_ref[c + tt:2 * c + tt, :] = jnp.zeros((c, LANES), F32)
        state_ref[...] = jnp.zeros(state_ref.shape, F32)
        o_s[...] = jnp.zeros(o_s.shape, F32)

    for idx, x_ref in enumerate((q_ref, k_ref, v_ref)):
        stage_ref[idx, c:c + tt, :] = x_ref[...]
    gstage_ref[c:c + tt, :] = gates_ref[...]

    rowp = lax.broadcasted_iota(jnp.int32, (c, 2 * c), 0)
    colp = lax.broadcasted_iota(jnp.int32, (c, 2 * c), 1) & (c - 1)
    causal = rowp >= colp
    strict = rowp > colp
    lane_p = lax.broadcasted_iota(jnp.int32, (c, 2 * c), 1)
    inv_masks = _inverse_masks(c)
    tril = (lax.broadcasted_iota(jnp.int32, (c, c), 0) >= lax.broadcasted_iota(jnp.int32, (c, c), 1))
    tril3 = jnp.concatenate([tril.astype(BF16)] * 3, axis=1)
    dnw = dnw_ref[...]
    zero_blk = jnp.zeros((hd, hd), BF16)

    def l2n(xh, scale):
        return xh * (lax.rsqrt(jnp.sum(xh * xh, axis=-1, keepdims=True) + NORM_EPS) * scale)

    def prepare_pair(j, p, zero=None):
        base = pl.multiple_of(j * c, c)
        cols = slice(p * PAIR, (p + 1) * PAIR)
        slot = j & 1
        for idx, (dst, scale) in enumerate(((qn_s, hd ** -0.5), (kn_s, 1.0))):
            x = stage_ref[idx, pl.ds(base, c), cols].astype(F32)
            if zero is not None:
                x = x + jnp.concatenate([zero[0:1], zero[0:1]], axis=1)
            dst[slot, :, cols] = jnp.concatenate([l2n(x[:, :hd], scale), l2n(x[:, hd:], scale)], axis=1)
        r0 = pl.multiple_of((j >> 24) * SUBLANES, SUBLANES)
        back = qn_s[slot, pl.ds(r0, SUBLANES), cols] + kn_s[slot, pl.ds(r0, SUBLANES), cols]
        return back[:, :hd] + back[:, hd:]

    def finish_pair(row0, p, zero=None):
        rows = pl.ds(row0, c)
        for half in range(2):
            cs = slice(p * PAIR + half * hd, p * PAIR + (half + 1) * hd)
            oh = o_s[:, cs]
            if zero is not None:
                oh = oh + zero[0:1]
            oh = oh * lax.rsqrt(jnp.mean(oh * oh, axis=-1, keepdims=True) + NORM_EPS) * dnw
            y_ref[rows, cs] = (oh * z_ref[rows, cs].astype(F32)).astype(y_ref.dtype)
        r0 = pl.multiple_of((row0 >> 24) * (2 * SUBLANES), 2 * SUBLANES)
        back = y_ref[pl.ds(r0, 2 * SUBLANES), p * PAIR:(p + 1) * PAIR].astype(F32)[0:SUBLANES]
        return back[:, :hd] + back[:, hd:]

    def out_rows(j):
        return pl.multiple_of(jnp.maximum(j - 1, 0) * c, c)

    def exact_rows_dot(lhs3, x):
        hi, mid, lo = _split3(x)
        return jnp.dot(lhs3, jnp.concatenate([hi, mid, lo], axis=0), preferred_element_type=F32)

    def pair_cols(mat, la, lb):
        return jnp.where(lane_p < c, mat[:, la:la + 1], mat[:, lb:lb + 1])

    def pair_rows(mat_t, ra, rb):
        return jnp.concatenate([mat_t[ra:ra + 1, :], mat_t[rb:rb + 1, :]], axis=1)

    def zero_of(v):
        return jnp.where(v != v, v, 0.0)

    ha = [2 * p for p in pairs]
    hb = [2 * p + 1 for p in pairs]

    def prepare_gates(j, zero=None):
        base = pl.multiple_of(j * c, c)
        slot = j & 1
        gt = gstage_ref[pl.ds(base, c), :]
        if zero is not None:
            gt = gt + zero[0:1]
        gcum = exact_rows_dot(tril3, gt)
        g_last = gcum[c - 1:c, :]
        gsm_s[slot, 0] = jnp.exp(gcum)
        gsm_s[slot, 1] = jnp.exp(g_last - gcum)
        gsm_s[slot, 2] = jnp.broadcast_to(jnp.exp(g_last), (c, LANES))
        gt_t = gt.T
        gcum_t = gcum.T
        for p in pairs:
            g_col = pair_cols(gcum, DN_HEADS + ha[p], DN_HEADS + hb[p])
            g_row = pair_rows(gcum_t, DN_HEADS + ha[p], DN_HEADS + hb[p])
            b_col = pair_cols(gt, ha[p], hb[p])
            b_row = pair_rows(gt_t, ha[p], hb[p])
            dec = jnp.where(causal, jnp.exp(jnp.where(causal, g_col - g_row, 0.0)), 0.0)
            gp_s[slot, p] = dec
            gl_s[slot, p] = jnp.where(strict, b_col * dec, 0.0)
            gr_s[slot, p] = jnp.concatenate(
                [b_row, b_row * jnp.exp(g_row), jnp.zeros((SUBLANES - 2, 2 * c), F32)], axis=0)
        r0 = pl.multiple_of((j >> 24) * SUBLANES, SUBLANES)
        back = gsm_s[slot, 2, pl.ds(r0, SUBLANES), :] + gr_s[slot, n_pairs - 1, pl.ds(r0, SUBLANES), :]
        return back + gl_s[slot, n_pairs - 1, pl.ds(r0, SUBLANES), :] + gp_s[slot, n_pairs - 1, pl.ds(r0, SUBLANES), :]

    def chunk_body(j, carry):
        base = pl.multiple_of(j * c, c)
        prev_rows = out_rows(j - 1)
        fillers = []
        for p in pairs:
            fillers.append(functools.partial(finish_pair, prev_rows, p))
            fillers.append(functools.partial(prepare_pair, j + 1, p))
        fillers.reverse()

        def fill(dep, into, n=1):
            zero = None if dep is None else zero_of(dep[0:SUBLANES, 0:LANES].astype(F32))
            for _ in range(n):
                if fillers:
                    done = zero_of(fillers.pop()(zero=zero))[0:1]
                    if into[-1].shape[1] == PAIR:
                        done = jnp.concatenate([done, done], axis=1)
                    into[-1] = into[-1] + done

        head_cols = lambda h: slice(h * hd, (h + 1) * hd)
        pair_cols_of = lambda p: slice(p * PAIR, (p + 1) * PAIR)
        slot = j & 1
        eg = gsm_s[slot, 0]
        e_rest = gsm_s[slot, 1]
        e_last = gsm_s[slot, 2, 0:1, :]
        gl = [gl_s[slot, p] for p in pairs]
        gp = [gp_s[slot, p] for p in pairs]
        gr = [gr_s[slot, p] for p in pairs]
        kpair = [kn_s[slot, :, pair_cols_of(p)].astype(BF16) for p in pairs]
        kbd = [_block_diag_rows(kp, hd) for kp in kpair]
        qpair = [qn_s[slot, :, pair_cols_of(p)].astype(BF16) for p in pairs]
        for p in (1, 3, 5, 7):
            staged = [qn_s[slot, :, pair_cols_of(p)]]
            fill(None, staged)
            qpair[p] = staged[0].astype(BF16)
        aq = [_mm_nt(jnp.concatenate([kpair[p], qpair[p]], axis=0), kbd[p]) for p in pairs]
        prepare_gates(j + 1)
        lps = [aq[p][:c] * gl[p] for p in pairs]
        pps = [aq[p][c:] * gp[p] for p in pairs]
        tps = _inv_unit_lower_packed(lps, c, inv_masks, fill)
        vbd = [_block_diag_rows(stage_ref[2, pl.ds(base, c), pair_cols_of(p)], hd) for p in pairs]
        u = [_mm(tps[p] * gr[p][0:1], vbd[p]) for p in pairs]
        w = [_mm(tps[p] * gr[p][1:2], kbd[p]) for p in pairs]
        fill(tps[0], w)
        s_old = [state_ref[h] for h in range(DN_HEADS)]
        qd = [jnp.concatenate(
            [qn_s[slot, :, head_cols(ha[p])] * eg[:, DN_HEADS + ha[p]:DN_HEADS + ha[p] + 1],
             qn_s[slot, :, head_cols(hb[p])] * eg[:, DN_HEADS + hb[p]:DN_HEADS + hb[p] + 1]], axis=1)
            for p in pairs]
        sbd = [jnp.concatenate([jnp.concatenate([s_old[ha[p]].astype(BF16), zero_blk], axis=1),
                                jnp.concatenate([zero_blk, s_old[hb[p]].astype(BF16)], axis=1)], axis=0)
               for p in pairs]
        ws = [_mm(jnp.concatenate([w[p], qd[p]], axis=0), sbd[p]) for p in pairs]
        fill(w[0], ws)
        v_new = [u[p] - ws[p][:c] for p in pairs]
        o = [ws[p][c:] + _mm(pps[p], _block_diag_rows(v_new[p].astype(BF16), hd)) for p in pairs]
        for h in range(DN_HEADS):
            p, half = divmod(h, 2)
            lane_h = DN_HEADS + h
            kd = kn_s[slot, :, head_cols(h)] * e_rest[:, lane_h:lane_h + 1]
            upd = _mm_tn(kd, v_new[p][:, half * hd:(half + 1) * hd])
            state_ref[h] = s_old[h] * e_last[:, lane_h:lane_h + 1] + upd
        fill(ws[0], o, len(fillers))
        for p in pairs:
            o_s[:, p * PAIR:(p + 1) * PAIR] = o[p]
        return carry

    start = jnp.where(t_idx == 0, 0, 1)
    for p in pairs:
        prepare_pair(start, p)
    prepare_gates(start)
    lax.fori_loop(start, n_chunks + 1, chunk_body, 0)
    last_rows = out_rows(jnp.int32(n_chunks))
    for p in pairs:
        finish_pair(last_rows, p)


def _deltanet(proj, meta_qkv, gates, meta_gates, dn_norm_w, *, batch, seq_len):
    tt = TT_DN
    nt = seq_len // tt
    c = CHUNK
    kern = functools.partial(_deltanet_kernel, tt=tt)
    row = lambda b, t: b * nt + t
    const = lambda shape: pl.BlockSpec(shape, lambda b, t: (0,) * len(shape), pipeline_mode=pl.Buffered(1))
    return pl.pallas_call(
        kern,
        out_shape=jax.ShapeDtypeStruct((batch * seq_len, DN_WIDTH), BF16),
        grid=(batch, nt),
        in_specs=[
            pl.BlockSpec((tt, DN_WIDTH), lambda b, t: (row(b, t), COL_Q // DN_WIDTH)),
            pl.BlockSpec((tt, DN_WIDTH), lambda b, t: (row(b, t), COL_K // DN_WIDTH)),
            pl.BlockSpec((tt, DN_WIDTH), lambda b, t: (row(b, t), COL_V // DN_WIDTH)),
            pl.BlockSpec((tt, DN_WIDTH), lambda b, t: (row(b, t), COL_ZD // DN_WIDTH)),
            pl.BlockSpec((tt, LANES), lambda b, t: (row(b, t), 0)),
            const((c, 3 * DN_WIDTH)),
            const((c, LANES)),
            const((1, DN_HEAD_DIM)),
        ],
        out_specs=pl.BlockSpec((tt, DN_WIDTH), lambda b, t: (row(b, t), 0)),
        scratch_shapes=[
            pltpu.VMEM((3, 2 * c + tt, DN_WIDTH), BF16),
            pltpu.VMEM((2 * c + tt, LANES), F32),
            pltpu.VMEM((DN_HEADS, DN_HEAD_DIM, DN_HEAD_DIM), F32),
            pltpu.VMEM((2, c, DN_WIDTH), F32),
            pltpu.VMEM((2, c, DN_WIDTH), F32),
            pltpu.VMEM((c, DN_WIDTH), F32),
            pltpu.VMEM((2, 3, c, LANES), F32),
            pltpu.VMEM((2, DN_HEADS // 2, c, 2 * c), F32),
            pltpu.VMEM((2, DN_HEADS // 2, c, 2 * c), F32),
            pltpu.VMEM((2, DN_HEADS // 2, SUBLANES, 2 * c), F32),
        ],
        compiler_params=pltpu.CompilerParams(
            dimension_semantics=("arbitrary", "arbitrary"), vmem_limit_bytes=VMEM_LIMIT),
        name="deltanet",
    )(proj, proj, proj, proj, gates, meta_qkv, meta_gates, dn_norm_w)


def _out_merge_kernel(u_ref, zp_ref, gpa_ref, gpb_ref, gda_ref, gdb_ref, ydn_ref, x_ref, mu_ref, mix_ref,
                      scale_ref, wpo_ref, wdo_ref, wo_ref, fnw_ref, out_ref, ubuf_ref, *, tm):
    t = pl.program_id(1)
    hist = N_META

    @pl.when(t == 0)
    def _():
        ubuf_ref[0:hist, :] = mu_ref[...].astype(F32)

    @pl.when(t > 0)
    def _():
        ubuf_ref[0:hist, :] = ubuf_ref[tm:tm + hist, :]

    ubuf_ref[hist:hist + tm, :] = u_ref[...].astype(F32)

    pooled = []
    for gi, w in enumerate(POOL_WINDOWS):
        cs = slice(gi * POOL_GROUP_DIM, (gi + 1) * POOL_GROUP_DIM)
        acc = ubuf_ref[hist:hist + tm, cs]
        cur = acc
        for s in range(1, w):
            acc = acc + ubuf_ref[hist - s:hist - s + tm, cs]
        d = acc * (1.0 / w) - cur
        mixed = _mm(d, mix_ref[gi])
        zp = zp_ref[:, cs].astype(F32)
        pooled.append((mixed * scale_ref[:, cs] * zp).astype(BF16))
    y_pool = jnp.concatenate(pooled, axis=1)

    frame = slice(GATE_SHIFT, GATE_SHIFT + D_MODEL)
    gp = jnp.concatenate([gpa_ref[...], gpb_ref[...]], axis=1).astype(F32)[:, frame]
    gd = jnp.concatenate([gda_ref[...], gdb_ref[...]], axis=1).astype(F32)[:, frame]
    t_pool = jnp.dot(y_pool, wpo_ref[...], preferred_element_type=F32)
    t_dn = jnp.dot(ydn_ref[...], wdo_ref[...], preferred_element_type=F32)
    merged = _sigmoid(gp) * t_pool + _sigmoid(gd) * t_dn
    h = x_ref[...] + jnp.dot(merged.astype(BF16), wo_ref[...], preferred_element_type=F32)
    ms = jnp.mean(h * h, axis=-1, keepdims=True)
    out_ref[...] = h * lax.rsqrt(ms + NORM_EPS) * fnw_ref[...]


def _resident(shape, index_map):
    return pl.BlockSpec(shape, index_map, pipeline_mode=pl.Buffered(1))


def _out_merge(proj, y_dn, x2d, meta_u, mix, scale, wpo, wdo, wo, fnw, *, batch, seq_len):
    tm = TM_OUT
    tiles = seq_len // tm
    kern = functools.partial(_out_merge_kernel, tm=tm)
    row = lambda b, t: b * tiles + t
    gp0, gd0 = COL_GP - GATE_SHIFT, COL_GD - GATE_SHIFT
    assert gp0 % D_MODEL == 0 and gd0 % D_MODEL == 0
    return pl.pallas_call(
        kern,
        out_shape=jax.ShapeDtypeStruct((batch * seq_len, D_MODEL), F32),
        grid=(batch, tiles),
        in_specs=[
            pl.BlockSpec((tm, POOL_WIDTH), lambda b, t: (row(b, t), COL_U // POOL_WIDTH)),
            pl.BlockSpec((tm, POOL_WIDTH), lambda b, t: (row(b, t), COL_ZP // POOL_WIDTH)),
            pl.BlockSpec((tm, D_MODEL), lambda b, t: (row(b, t), gp0 // D_MODEL)),
            pl.BlockSpec((tm, LANES), lambda b, t: (row(b, t), (gp0 + D_MODEL) // LANES)),
            pl.BlockSpec((tm, D_MODEL), lambda b, t: (row(b, t), gd0 // D_MODEL)),
            pl.BlockSpec((tm, LANES), lambda b, t: (row(b, t), (gd0 + D_MODEL) // LANES)),
            pl.BlockSpec((tm, DN_WIDTH), lambda b, t: (row(b, t), 0)),
            pl.BlockSpec((tm, D_MODEL), lambda b, t: (row(b, t), 0)),
            _resident((N_META, POOL_WIDTH), lambda b, t: (0, 0)),
            _resident((POOL_GROUPS, POOL_GROUP_DIM, POOL_GROUP_DIM), lambda b, t: (0, 0, 0)),
            _resident((1, POOL_WIDTH), lambda b, t: (0, 0)),
            _resident((POOL_WIDTH, D_MODEL), lambda b, t: (0, 0)),
            _resident((DN_WIDTH, D_MODEL), lambda b, t: (0, 0)),
            _resident((D_MODEL, D_MODEL), lambda b, t: (0, 0)),
            _resident((1, D_MODEL), lambda b, t: (0, 0)),
        ],
        out_specs=pl.BlockSpec((tm, D_MODEL), lambda b, t: (row(b, t), 0)),
        scratch_shapes=[pltpu.VMEM((N_META + tm, POOL_WIDTH), F32)],
        compiler_params=pltpu.CompilerParams(
            dimension_semantics=("arbitrary", "arbitrary"), vmem_limit_bytes=VMEM_LIMIT),
        name="out_merge",
    )(proj, proj, proj, proj, proj, proj, y_dn, x2d, meta_u, mix, scale, wpo, wdo, wo, fnw)


def kernel(x, meta_tokens, norm_w, w_in, conv_w, A_log, dt_bias, pool_mix, pool_scale, dn_norm_w,
           w_pool_out, w_dn_out, w_o, final_norm_w):
    batch, seq_len, _ = x.shape
    assert norm_w.shape[0] == 1, "single layer block"
    x2d = x.reshape(batch * seq_len, D_MODEL)

    rows = batch * seq_len
    w_t = jnp.swapaxes(w_in, 1, 2).reshape(w_in.shape[2], D_MODEL)
    gpar = jnp.zeros((2, LANES), F32)
    gpar = gpar.at[0, DN_HEADS:2 * DN_HEADS].set(A_log[0]).at[1, DN_HEADS:2 * DN_HEADS].set(dt_bias[0])
    nw = norm_w[0].reshape(1, D_MODEL)
    wpo = w_pool_out[0].astype(BF16)
    wdo = w_dn_out[0].astype(BF16)
    wo = w_o[0].astype(BF16)

    xn_all, gates = _norm_gates(x2d, meta_tokens.astype(F32), nw, w_t, gpar)
    proj = _in_proj(xn_all, w_t, conv_w[0], seq_len=seq_len)

    front = CHUNK - N_META
    meta_proj = proj[rows:]
    meta_qkv = jnp.pad(meta_proj[:, COL_Q:COL_ZD], ((front, 0), (0, 0)))
    meta_gates = jnp.pad(gates[rows:], ((front, 0), (0, 0)))
    meta_u = meta_proj[:, COL_U:COL_U + POOL_WIDTH]

    y_dn = _deltanet(proj, meta_qkv, gates, meta_gates, dn_norm_w[0].reshape(1, DN_HEAD_DIM),
                     batch=batch, seq_len=seq_len)

    out = _out_merge(proj, y_dn, x2d, meta_u, pool_mix[0].astype(BF16), pool_scale[0].reshape(1, POOL_WIDTH),
                     wpo, wdo, wo, final_norm_w.reshape(1, D_MODEL), batch=batch, seq_len=seq_len)
    return out.reshape(batch, seq_len, D_MODEL)
```

```python
import functools

import jax
import jax.numpy as jnp
from jax import lax
from jax.experimental import pallas as pl
from jax.experimental.pallas import tpu as pltpu

D_MODEL = 2048
N_META = 16
POOL_GROUPS = 4
POOL_WINDOWS = (2, 4, 8, 16)
POOL_WIDTH = D_MODEL // 2
POOL_GROUP_DIM = POOL_WIDTH // POOL_GROUPS
DN_HEADS = 16
DN_HEAD_DIM = 128
DN_WIDTH = DN_HEADS * DN_HEAD_DIM
CONV_WIDTH = 4
NORM_EPS = 1e-6

COL_U = 0
COL_ZP = COL_U + POOL_WIDTH
COL_Q = COL_ZP + POOL_WIDTH
COL_K = COL_Q + DN_WIDTH
COL_V = COL_K + DN_WIDTH
COL_ZD = COL_V + DN_WIDTH
BA_OFFSET = COL_ZD + DN_WIDTH
COL_GP = BA_OFFSET + 2 * DN_HEADS
COL_GD = COL_GP + D_MODEL
LANES = 128
SUBLANES = 8
GATE_SHIFT = COL_GP % LANES

PROJ_DTYPE = jnp.bfloat16
CHUNK = 64
LEAF = 16
TT_DN = 512
TM_NORM = 512
TN_PROJ = 256
ROW_BLK_PROJ = 1024
TM_OUT = 256
VMEM_LIMIT = 56 * 1024 * 1024

BF16 = jnp.bfloat16
F32 = jnp.float32


def _mm(a, b):
    return jnp.dot(a.astype(BF16), b.astype(BF16), preferred_element_type=F32)


def _mm_nt(a, b):
    return lax.dot_general(a.astype(BF16), b.astype(BF16), (((1,), (1,)), ((), ())),
                           preferred_element_type=F32)


def _mm_tn(a, b):
    return lax.dot_general(a.astype(BF16), b.astype(BF16), (((0,), (0,)), ((), ())),
                           preferred_element_type=F32)


def _sigmoid(x):
    return 0.5 * jnp.tanh(0.5 * x) + 0.5


def _silu(x):
    h = 0.5 * x
    return h * jnp.tanh(h) + h


def _softplus(x):
    return jnp.maximum(x, 0.0) + jnp.log1p(jnp.exp(-jnp.abs(x)))


def _norm_gates_kernel(x_ref, meta_ref, nw_ref, wba_ref, gpar_ref, xn_ref, gates_ref, *, x_tiles, sub):
    i = pl.program_id(0)
    nw = nw_ref[...]
    wba = wba_ref[...].astype(BF16)
    a_rate = jnp.exp(gpar_ref[0:1, :])
    dt_bias = gpar_ref[1:2, :]

    def rows_block(x, rows):
        ms = jnp.mean(x * x, axis=-1, keepdims=True)
        xn = (x * lax.rsqrt(ms + NORM_EPS) * nw).astype(BF16)
        xn_ref[rows, :] = xn
        ba = _mm_nt(xn, wba)
        lane = lax.broadcasted_iota(jnp.int32, ba.shape, 1)
        g = -a_rate * _softplus(ba + dt_bias)
        gates_ref[rows, :] = jnp.where(lane < DN_HEADS, _sigmoid(ba), jnp.where(lane < 2 * DN_HEADS, g, 0.0))

    @pl.when(i < x_tiles)
    def _():
        def body(r, carry):
            rows = pl.ds(pl.multiple_of(r * sub, sub), sub)
            rows_block(x_ref[rows, :], rows)
            return carry

        lax.fori_loop(0, x_ref.shape[0] // sub, body, 0)

    @pl.when(i == x_tiles)
    def _():
        rows_block(meta_ref[...], pl.ds(0, N_META))


def _norm_gates(x2d, meta, norm_w, w_t, gpar):
    rows = x2d.shape[0]
    tm = TM_NORM
    x_tiles = rows // tm
    kern = functools.partial(_norm_gates_kernel, x_tiles=x_tiles, sub=128)
    return pl.pallas_call(
        kern,
        out_shape=(jax.ShapeDtypeStruct((rows + N_META, D_MODEL), BF16),
                   jax.ShapeDtypeStruct((rows + N_META, LANES), F32)),
        grid=(x_tiles + 1,),
        in_specs=[
            pl.BlockSpec((tm, D_MODEL), lambda i: (jnp.minimum(i, x_tiles - 1), 0)),
            pl.BlockSpec((N_META, D_MODEL), lambda i: (0, 0)),
            pl.BlockSpec((1, D_MODEL), lambda i: (0, 0)),
            pl.BlockSpec((LANES, D_MODEL), lambda i: (BA_OFFSET // LANES, 0)),
            pl.BlockSpec((2, LANES), lambda i: (0, 0)),
        ],
        out_specs=(pl.BlockSpec((tm, D_MODEL), lambda i: (i, 0)),
                   pl.BlockSpec((tm, LANES), lambda i: (i, 0))),
        compiler_params=pltpu.CompilerParams(
            dimension_semantics=("arbitrary",), vmem_limit_bytes=VMEM_LIMIT),
        name="norm_gates",
    )(x2d, meta, norm_w, w_t, gpar)


def _in_proj_kernel(xn_ref, w_ref, cw_ref, proj_ref, acc_ref, *, rows, row_blk, seq_len):
    j = pl.program_id(0)
    tn = w_ref.shape[0]
    w = w_ref[...].astype(BF16)
    ms = slice(rows, rows + N_META)
    conv_tile = (j >= COL_Q // tn) & (j < COL_ZD // tn)
    silu_tile = ((j >= COL_ZP // tn) & (j < COL_Q // tn)) | ((j >= COL_ZD // tn) & (j < BA_OFFSET // tn))

    def plain(fn):
        for r in range(rows // row_blk):
            rs = slice(r * row_blk, (r + 1) * row_blk)
            proj_ref[rs, :] = fn(_mm_nt(xn_ref[rs, :], w)).astype(proj_ref.dtype)
        proj_ref[ms, :] = fn(_mm_nt(xn_ref[ms, :], w)).astype(proj_ref.dtype)

    @pl.when(conv_tile)
    def _():
        cw = cw_ref[...]
        hist_rows = SUBLANES
        n_blk = rows // row_blk

        def conv_silu(slot, n):
            y = acc_ref[slot, hist_rows:hist_rows + n, :] * cw[CONV_WIDTH - 1:CONV_WIDTH]
            for kk in range(CONV_WIDTH - 1):
                off = hist_rows - (CONV_WIDTH - 1) + kk
                y = y + acc_ref[slot, off:off + n, :] * cw[kk:kk + 1]
            return _silu(y)

        meta_acc = _mm_nt(xn_ref[ms, :], w)
        acc_ref[0, 0:hist_rows, :] = jnp.zeros((hist_rows, tn), F32)
        acc_ref[0, hist_rows:hist_rows + N_META, :] = meta_acc
        proj_ref[ms, :] = conv_silu(0, N_META).astype(proj_ref.dtype)
        meta_tail = meta_acc[N_META - hist_rows:]
        for r in range(n_blk + 1):
            if r < n_blk:
                slot = r % 2
                rs = slice(r * row_blk, (r + 1) * row_blk)
                acc = _mm_nt(xn_ref[rs, :], w)
                batch_start = (r * row_blk) % seq_len == 0
                acc_ref[slot, 0:hist_rows, :] = (
                    meta_tail if batch_start else acc_ref[1 - slot, row_blk:row_blk + hist_rows, :])
                acc_ref[slot, hist_rows:hist_rows + row_blk, :] = acc
            if r > 0:
                ps = slice((r - 1) * row_blk, r * row_blk)
                proj_ref[ps, :] = conv_silu((r - 1) % 2, row_blk).astype(proj_ref.dtype)

    @pl.when(silu_tile)
    def _():
        plain(_silu)

    @pl.when(jnp.logical_not(conv_tile | silu_tile))
    def _():
        plain(lambda a: a)


def _in_proj(xn_all, w_t, conv_w, *, seq_len):
    rows = xn_all.shape[0] - N_META
    cols = w_t.shape[0]
    tn = TN_PROJ
    n_conv_tiles = (COL_ZD - COL_Q) // tn
    kern = functools.partial(_in_proj_kernel, rows=rows, row_blk=ROW_BLK_PROJ, seq_len=seq_len)
    return pl.pallas_call(
        kern,
        out_shape=jax.ShapeDtypeStruct((rows + N_META, cols), PROJ_DTYPE),
        grid=(pl.cdiv(cols, tn),),
        in_specs=[
            pl.BlockSpec((rows + N_META, D_MODEL), lambda j: (0, 0), pipeline_mode=pl.Buffered(1)),
            pl.BlockSpec((tn, D_MODEL), lambda j: (j, 0)),
            pl.BlockSpec((CONV_WIDTH, tn), lambda j: (0, jnp.clip(j - COL_Q // tn, 0, n_conv_tiles - 1))),
        ],
        out_specs=pl.BlockSpec((rows + N_META, tn), lambda j: (0, j)),
        scratch_shapes=[pltpu.VMEM((2, SUBLANES + ROW_BLK_PROJ, tn), F32)],
        compiler_params=pltpu.CompilerParams(
            dimension_semantics=("arbitrary",), vmem_limit_bytes=VMEM_LIMIT),
        name="in_proj",
    )(xn_all, w_t, conv_w)


PAIR = 2 * DN_HEAD_DIM


def _split3(x):
    hi = x.astype(BF16)
    r1 = x - hi.astype(F32)
    mid = r1.astype(BF16)
    lo = (r1 - mid.astype(F32)).astype(BF16)
    return hi, mid, lo


def _block_diag_rows(y, half):
    lane = lax.broadcasted_iota(jnp.int32, y.shape, 1)
    zero = jnp.zeros_like(y)
    return jnp.concatenate([jnp.where(lane < half, y, zero), jnp.where(lane >= half, y, zero)], axis=0)


def _inverse_masks(c):
    row = lax.broadcasted_iota(jnp.int32, (c, 2 * c), 0)
    col = lax.broadcasted_iota(jnp.int32, (c, 2 * c), 1) & (c - 1)
    blk = lambda x, m: x >> (m.bit_length() - 1)
    eye = (row == col).astype(F32)
    same_leaf = blk(row, LEAF) == blk(col, LEAF)
    offs = []
    m = 2 * LEAF
    while m <= c:
        offs.append((blk(row, m) == blk(col, m)) & (blk(row, m // 2) != blk(col, m // 2)))
        m *= 2
    return eye, same_leaf, offs


def _inv_unit_lower_packed(lps, c, masks, fill, n_fill=1):
    eye, same_leaf, offs = masks
    bd = lambda y: _block_diag_rows(y.astype(BF16), c)
    ps = [jnp.where(same_leaf, -l, 0.0) for l in lps]
    ts = [eye + p for p in ps]
    dep = ps[0]
    ps = [_mm(p, bd(p)) for p in ps]
    fill(dep, ps, n_fill + 1)
    span = 4
    while span <= LEAF:
        dep = ps[0]
        last = span == LEAF
        prod = [_mm(t if last else jnp.concatenate([t, p], axis=0), bd(p)) for t, p in zip(ts, ps)]
        ts = [t + x[:c] for t, x in zip(ts, prod)]
        if not last:
            ps = [x[c:] for x in prod]
        fill(dep, ts, n_fill)
        span *= 2
    for off in offs:
        dep = ts[0]
        inner = [_mm(jnp.where(off, l, 0.0), bd(t)) for l, t in zip(lps, ts)]
        fill(dep, inner, n_fill)
        dep = inner[0]
        ts = [t - _mm(t, bd(x)) for t, x in zip(ts, inner)]
        fill(dep, ts, n_fill)
    return ts


def _deltanet_kernel(q_ref, k_ref, v_ref, z_ref, gates_ref, mqkv_ref, mgates_ref, dnw_ref,
                     y_ref, stage_ref, gstage_ref, state_ref, qn_s, kn_s, o_s, gsm_s, gl_s, gp_s, gr_s, *, tt):
    c = CHUNK
    hd = DN_HEAD_DIM
    n_pairs = DN_HEADS // 2
    t_idx = pl.program_id(1)
    pairs = range(n_pairs)
    n_chunks = tt // c

    head_rows = 2 * c
    @pl.when(t_idx == 0)
    def _():
        for idx in range(3):
            stage_ref[idx, 0:c, :] = jnp.zeros((c, DN_WIDTH), BF16)
            stage_ref[idx, c:head_rows, :] = mqkv_ref[:, idx * DN_WIDTH:(idx + 1) * DN_WIDTH]
            stage_ref[idx, head_rows + tt:2 * head_rows + tt, :] = jnp.zeros((head_rows, DN_WIDTH), BF16)
        gstage_ref[0:c, :] = jnp.zeros((c, LANES), F32)
        gstage_ref[c:head_rows, :] = mgates_ref[...]
        gstage_ref[head_rows + tt:2 * head_rows + tt, :] = jnp.zeros((head_rows, LANES), F32)
        state_ref[...] = jnp.zeros(state_ref.shape, F32)
        o_s[...] = jnp.zeros(o_s.shape, F32)

    for idx, x_ref in enumerate((q_ref, k_ref, v_ref)):
        stage_ref[idx, head_rows:head_rows + tt, :] = x_ref[...]
    gstage_ref[head_rows:head_rows + tt, :] = gates_ref[...]

    rowp = lax.broadcasted_iota(jnp.int32, (c, 2 * c), 0)
    colp = lax.broadcasted_iota(jnp.int32, (c, 2 * c), 1) & (c - 1)
    causal = rowp >= colp
    strict = rowp > colp
    lane_p = lax.broadcasted_iota(jnp.int32, (c, 2 * c), 1)
    inv_masks = _inverse_masks(c)
    tril = (lax.broadcasted_iota(jnp.int32, (c, c), 0) >= lax.broadcasted_iota(jnp.int32, (c, c), 1))
    tril3 = jnp.concatenate([tril.astype(BF16)] * 3, axis=1)
    dnw = dnw_ref[...]
    zero_blk = jnp.zeros((hd, hd), BF16)

    def l2n(xh, scale):
        return xh * (lax.rsqrt(jnp.sum(xh * xh, axis=-1, keepdims=True) + NORM_EPS) * scale)

    def prepare_pair(j, p, zero=None):
        base = pl.multiple_of(j * c, c)
        cols = slice(p * PAIR, (p + 1) * PAIR)
        slot = j & 3
        for idx, (dst, scale) in enumerate(((qn_s, hd ** -0.5), (kn_s, 1.0))):
            x = stage_ref[idx, pl.ds(base, c), cols].astype(F32)
            if zero is not None:
                x = x + jnp.concatenate([zero[0:1], zero[0:1]], axis=1)
            dst[slot, :, cols] = jnp.concatenate([l2n(x[:, :hd], scale), l2n(x[:, hd:], scale)], axis=1)
        r0 = pl.multiple_of((j >> 24) * SUBLANES, SUBLANES)
        back = qn_s[slot, pl.ds(r0, SUBLANES), cols] + kn_s[slot, pl.ds(r0, SUBLANES), cols]
        return back[:, :hd] + back[:, hd:]

    def finish_pair(src, row0, p, zero=None):
        rows = pl.ds(row0, c)
        for half in range(2):
            cs = slice(p * PAIR + half * hd, p * PAIR + (half + 1) * hd)
            oh = o_s[src, :, cs]
            if zero is not None:
                oh = oh + zero[0:1]
            oh = oh * lax.rsqrt(jnp.mean(oh * oh, axis=-1, keepdims=True) + NORM_EPS) * dnw
            y_ref[rows, cs] = (oh * z_ref[rows, cs].astype(F32)).astype(y_ref.dtype)
        r0 = pl.multiple_of((row0 >> 24) * (2 * SUBLANES), 2 * SUBLANES)
        back = y_ref[pl.ds(r0, 2 * SUBLANES), p * PAIR:(p + 1) * PAIR].astype(F32)[0:SUBLANES]
        return back[:, :hd] + back[:, hd:]

    def out_rows(j):
        return pl.multiple_of(jnp.maximum(j - 2, 0) * c, c)

    def exact_rows_dot(lhs3, x):
        hi, mid, lo = _split3(x)
        return jnp.dot(lhs3, jnp.concatenate([hi, mid, lo], axis=0), preferred_element_type=F32)

    def pair_cols(mat, la, lb):
        return jnp.where(lane_p < c, mat[:, la:la + 1], mat[:, lb:lb + 1])

    def pair_rows(mat_t, ra, rb):
        return jnp.concatenate([mat_t[ra:ra + 1, :], mat_t[rb:rb + 1, :]], axis=1)

    def zero_of(v):
        return jnp.where(v != v, v, 0.0)

    ha = [2 * p for p in pairs]
    hb = [2 * p + 1 for p in pairs]

    def prepare_gates(j, zero=None):
        base = pl.multiple_of(j * c, c)
        slot = j & 3
        gt = gstage_ref[pl.ds(base, c), :]
        if zero is not None:
            gt = gt + zero[0:1]
        gcum = exact_rows_dot(tril3, gt)
        g_last = gcum[c - 1:c, :]
        gsm_s[slot, 0] = jnp.exp(gcum)
        gsm_s[slot, 1] = jnp.exp(g_last - gcum)
        gsm_s[slot, 2] = jnp.broadcast_to(jnp.exp(g_last), (c, LANES))
        gt_t = gt.T
        gcum_t = gcum.T
        for p in pairs:
            g_col = pair_cols(gcum, DN_HEADS + ha[p], DN_HEADS + hb[p])
            g_row = pair_rows(gcum_t, DN_HEADS + ha[p], DN_HEADS + hb[p])
            b_col = pair_cols(gt, ha[p], hb[p])
            b_row = pair_rows(gt_t, ha[p], hb[p])
            dec = jnp.where(causal, jnp.exp(jnp.where(causal, g_col - g_row, 0.0)), 0.0)
            gp_s[slot, p] = dec
            gl_s[slot, p] = jnp.where(strict, b_col * dec, 0.0)
            gr_s[slot, p] = jnp.concatenate(
                [b_row, b_row * jnp.exp(g_row), jnp.zeros((SUBLANES - 2, 2 * c), F32)], axis=0)
        r0 = pl.multiple_of((j >> 24) * SUBLANES, SUBLANES)
        back = gsm_s[slot, 2, pl.ds(r0, SUBLANES), :] + gr_s[slot, n_pairs - 1, pl.ds(r0, SUBLANES), :]
        return back + gl_s[slot, n_pairs - 1, pl.ds(r0, SUBLANES), :] + gp_s[slot, n_pairs - 1, pl.ds(r0, SUBLANES), :]

    head_cols = lambda h: slice(h * hd, (h + 1) * hd)
    pair_cols_of = lambda p: slice(p * PAIR, (p + 1) * PAIR)
    both = (0, 1)

    def pair_body(i, carry):
        pos = [2 * i, 2 * i + 1]
        base = [pl.multiple_of(pz * c, c) for pz in pos]
        slot = [pz & 3 for pz in pos]
        fillers = []
        for p in pairs:
            for x in both:
                fillers.append(functools.partial(finish_pair, x, out_rows(pos[x] - 2), p))
                fillers.append(functools.partial(prepare_pair, pos[x] + 2, p))
        fillers.reverse()

        def fill(dep, into, n=1):
            zero = None if dep is None else zero_of(dep[0:SUBLANES, 0:LANES].astype(F32))
            for _ in range(n):
                if fillers:
                    done = zero_of(fillers.pop()(zero=zero))[0:1]
                    if into[-1].shape[1] == PAIR:
                        done = jnp.concatenate([done, done], axis=1)
                    into[-1] = into[-1] + done

        prob = [(x, p) for x in both for p in pairs]
        eg = [gsm_s[slot[x], 0] for x in both]
        e_rest = [gsm_s[slot[x], 1] for x in both]
        e_last = [gsm_s[slot[x], 2, 0:1, :] for x in both]
        gl = [gl_s[slot[x], p] for x, p in prob]
        gp = [gp_s[slot[x], p] for x, p in prob]
        gr = [gr_s[slot[x], p] for x, p in prob]
        kpair = [kn_s[slot[x], :, pair_cols_of(p)].astype(BF16) for x, p in prob]
        kbd = [_block_diag_rows(kp, hd) for kp in kpair]
        qpair = []
        for n, (x, p) in enumerate(prob):
            staged = [qn_s[slot[x], :, pair_cols_of(p)]]
            if n % 2:
                fill(None, staged)
            qpair.append(staged[0].astype(BF16))
        aq = [_mm_nt(jnp.concatenate([kpair[n], qpair[n]], axis=0), kbd[n]) for n in range(len(prob))]
        for x in both:
            prepare_gates(pos[x] + 2)
        lps = [aq[n][:c] * gl[n] for n in range(len(prob))]
        pps = [aq[n][c:] * gp[n] for n in range(len(prob))]
        tps = _inv_unit_lower_packed(lps, c, inv_masks, fill, 2)
        vbd = [_block_diag_rows(stage_ref[2, pl.ds(base[x], c), pair_cols_of(p)], hd) for x, p in prob]
        u = [_mm(tps[n] * gr[n][0:1], vbd[n]) for n in range(len(prob))]
        w = [_mm(tps[n] * gr[n][1:2], kbd[n]) for n in range(len(prob))]
        fill(tps[0], w, 2)
        state = [state_ref[h] for h in range(DN_HEADS)]
        outs = []
        for x in both:
            idx = [x * n_pairs + p for p in pairs]
            qd = [jnp.concatenate(
                [qn_s[slot[x], :, head_cols(h)] * eg[x][:, DN_HEADS + h:DN_HEADS + h + 1]
                 for h in (ha[p], hb[p])], axis=1) for p in pairs]
            sbd = [jnp.concatenate([jnp.concatenate([state[ha[p]].astype(BF16), zero_blk], axis=1),
                                    jnp.concatenate([zero_blk, state[hb[p]].astype(BF16)], axis=1)], axis=0)
                   for p in pairs]
            ws = [_mm(jnp.concatenate([w[idx[p]], qd[p]], axis=0), sbd[p]) for p in pairs]
            fill(w[idx[0]], ws, 2)
            v_new = [u[idx[p]] - ws[p][:c] for p in pairs]
            o = [ws[p][c:] + _mm(pps[idx[p]], _block_diag_rows(v_new[p].astype(BF16), hd)) for p in pairs]
            new_state = []
            for h in range(DN_HEADS):
                p, half = divmod(h, 2)
                lane_h = DN_HEADS + h
                kd = kn_s[slot[x], :, head_cols(h)] * e_rest[x][:, lane_h:lane_h + 1]
                upd = _mm_tn(kd, v_new[p][:, half * hd:(half + 1) * hd])
                new_state.append(state[h] * e_last[x][:, lane_h:lane_h + 1] + upd)
            state = new_state
            if x == 1:
                fill(ws[0], o, len(fillers))
            outs.append(o)
        for h in range(DN_HEADS):
            state_ref[h] = state[h]
        for x in both:
            for p in pairs:
                o_s[x, :, pair_cols_of(p)] = outs[x][p]
        return carry

    first = jnp.where(t_idx == 0, 0, 1)
    for x in both:
        for p in pairs:
            prepare_pair(2 * first + x, p)
        prepare_gates(2 * first + x)
    n_iter = n_chunks // 2 + 1
    lax.fori_loop(first, n_iter, pair_body, 0)
    for x in both:
        last_rows = out_rows(jnp.int32(2 * (n_iter - 1) + x))
        for p in pairs:
            finish_pair(x, last_rows, p)


def _deltanet(proj, meta_qkv, gates, meta_gates, dn_norm_w, *, batch, seq_len):
    tt = TT_DN
    nt = seq_len // tt
    c = CHUNK
    kern = functools.partial(_deltanet_kernel, tt=tt)
    row = lambda b, t: b * nt + t
    const = lambda shape: pl.BlockSpec(shape, lambda b, t: (0,) * len(shape), pipeline_mode=pl.Buffered(1))
    return pl.pallas_call(
        kern,
        out_shape=jax.ShapeDtypeStruct((batch * seq_len, DN_WIDTH), BF16),
        grid=(batch, nt),
        in_specs=[
            pl.BlockSpec((tt, DN_WIDTH), lambda b, t: (row(b, t), COL_Q // DN_WIDTH)),
            pl.BlockSpec((tt, DN_WIDTH), lambda b, t: (row(b, t), COL_K // DN_WIDTH)),
            pl.BlockSpec((tt, DN_WIDTH), lambda b, t: (row(b, t), COL_V // DN_WIDTH)),
            pl.BlockSpec((tt, DN_WIDTH), lambda b, t: (row(b, t), COL_ZD // DN_WIDTH)),
            pl.BlockSpec((tt, LANES), lambda b, t: (row(b, t), 0)),
            const((c, 3 * DN_WIDTH)),
            const((c, LANES)),
            const((1, DN_HEAD_DIM)),
        ],
        out_specs=pl.BlockSpec((tt, DN_WIDTH), lambda b, t: (row(b, t), 0)),
        scratch_shapes=[
            pltpu.VMEM((3, 4 * c + tt, DN_WIDTH), BF16),
            pltpu.VMEM((4 * c + tt, LANES), F32),
            pltpu.VMEM((DN_HEADS, DN_HEAD_DIM, DN_HEAD_DIM), F32),
            pltpu.VMEM((4, c, DN_WIDTH), F32),
            pltpu.VMEM((4, c, DN_WIDTH), F32),
            pltpu.VMEM((2, c, DN_WIDTH), F32),
            pltpu.VMEM((4, 3, c, LANES), F32),
            pltpu.VMEM((4, DN_HEADS // 2, c, 2 * c), F32),
            pltpu.VMEM((4, DN_HEADS // 2, c, 2 * c), F32),
            pltpu.VMEM((4, DN_HEADS // 2, SUBLANES, 2 * c), F32),
        ],
        compiler_params=pltpu.CompilerParams(
            dimension_semantics=("arbitrary", "arbitrary"), vmem_limit_bytes=VMEM_LIMIT),
        name="deltanet",
    )(proj, proj, proj, proj, gates, meta_qkv, meta_gates, dn_norm_w)


def _out_merge_kernel(u_ref, zp_ref, gpa_ref, gpb_ref, gda_ref, gdb_ref, ydn_ref, x_ref, mu_ref, mix_ref,
                      scale_ref, wpo_ref, wdo_ref, wo_ref, fnw_ref, out_ref, ubuf_ref, *, tm):
    t = pl.program_id(1)
    hist = N_META

    @pl.when(t == 0)
    def _():
        ubuf_ref[0:hist, :] = mu_ref[...].astype(F32)

    @pl.when(t > 0)
    def _():
        ubuf_ref[0:hist, :] = ubuf_ref[tm:tm + hist, :]

    ubuf_ref[hist:hist + tm, :] = u_ref[...].astype(F32)

    pooled = []
    for gi, w in enumerate(POOL_WINDOWS):
        cs = slice(gi * POOL_GROUP_DIM, (gi + 1) * POOL_GROUP_DIM)
        ext = ubuf_ref[:, cs]
        acc = ext
        span = 1
        while span < w:
            acc = acc + pltpu.roll(acc, span, axis=0)
            span *= 2
        d = acc[hist:] * (1.0 / w) - ext[hist:]
        mixed = _mm(d, mix_ref[gi])
        zp = zp_ref[:, cs].astype(F32)
        pooled.append((mixed * scale_ref[:, cs] * zp).astype(BF16))
    y_pool = jnp.concatenate(pooled, axis=1)

    frame = slice(GATE_SHIFT, GATE_SHIFT + D_MODEL)
    gp = jnp.concatenate([gpa_ref[...], gpb_ref[...]], axis=1).astype(F32)[:, frame]
    gd = jnp.concatenate([gda_ref[...], gdb_ref[...]], axis=1).astype(F32)[:, frame]
    t_pool = jnp.dot(y_pool, wpo_ref[...], preferred_element_type=F32)
    t_dn = jnp.dot(ydn_ref[...], wdo_ref[...], preferred_element_type=F32)
    merged = _sigmoid(gp) * t_pool + _sigmoid(gd) * t_dn
    h = x_ref[...] + jnp.dot(merged.astype(BF16), wo_ref[...], preferred_element_type=F32)
    ms = jnp.mean(h * h, axis=-1, keepdims=True)
    out_ref[...] = h * lax.rsqrt(ms + NORM_EPS) * fnw_ref[...]


def _resident(shape, index_map):
    return pl.BlockSpec(shape, index_map, pipeline_mode=pl.Buffered(1))


def _out_merge(proj, y_dn, x2d, meta_u, mix, scale, wpo, wdo, wo, fnw, *, batch, seq_len):
    tm = TM_OUT
    tiles = seq_len // tm
    kern = functools.partial(_out_merge_kernel, tm=tm)
    row = lambda b, t: b * tiles + t
    gp0, gd0 = COL_GP - GATE_SHIFT, COL_GD - GATE_SHIFT
    assert gp0 % D_MODEL == 0 and gd0 % D_MODEL == 0
    return pl.pallas_call(
        kern,
        out_shape=jax.ShapeDtypeStruct((batch * seq_len, D_MODEL), F32),
        grid=(batch, tiles),
        in_specs=[
            pl.BlockSpec((tm, POOL_WIDTH), lambda b, t: (row(b, t), COL_U // POOL_WIDTH)),
            pl.BlockSpec((tm, POOL_WIDTH), lambda b, t: (row(b, t), COL_ZP // POOL_WIDTH)),
            pl.BlockSpec((tm, D_MODEL), lambda b, t: (row(b, t), gp0 // D_MODEL)),
            pl.BlockSpec((tm, LANES), lambda b, t: (row(b, t), (gp0 + D_MODEL) // LANES)),
            pl.BlockSpec((tm, D_MODEL), lambda b, t: (row(b, t), gd0 // D_MODEL)),
            pl.BlockSpec((tm, LANES), lambda b, t: (row(b, t), (gd0 + D_MODEL) // LANES)),
            pl.BlockSpec((tm, DN_WIDTH), lambda b, t: (row(b, t), 0)),
            pl.BlockSpec((tm, D_MODEL), lambda b, t: (row(b, t), 0)),
            _resident((N_META, POOL_WIDTH), lambda b, t: (0, 0)),
            _resident((POOL_GROUPS, POOL_GROUP_DIM, POOL_GROUP_DIM), lambda b, t: (0, 0, 0)),
            _resident((1, POOL_WIDTH), lambda b, t: (0, 0)),
            _resident((POOL_WIDTH, D_MODEL), lambda b, t: (0, 0)),
            _resident((DN_WIDTH, D_MODEL), lambda b, t: (0, 0)),
            _resident((D_MODEL, D_MODEL), lambda b, t: (0, 0)),
            _resident((1, D_MODEL), lambda b, t: (0, 0)),
        ],
        out_specs=pl.BlockSpec((tm, D_MODEL), lambda b, t: (row(b, t), 0)),
        scratch_shapes=[pltpu.VMEM((N_META + tm, POOL_WIDTH), F32)],
        compiler_params=pltpu.CompilerParams(
            dimension_semantics=("arbitrary", "arbitrary"), vmem_limit_bytes=VMEM_LIMIT),
        name="out_merge",
    )(proj, proj, proj, proj, proj, proj, y_dn, x2d, meta_u, mix, scale, wpo, wdo, wo, fnw)


def kernel(x, meta_tokens, norm_w, w_in, conv_w, A_log, dt_bias, pool_mix, pool_scale, dn_norm_w,
           w_pool_out, w_dn_out, w_o, final_norm_w):
    batch, seq_len, _ = x.shape
    assert norm_w.shape[0] == 1, "single layer block"
    x2d = x.reshape(batch * seq_len, D_MODEL)

    rows = batch * seq_len
    w_t = jnp.swapaxes(w_in, 1, 2).reshape(w_in.shape[2], D_MODEL)
    gpar = jnp.zeros((2, LANES), F32)
    gpar = gpar.at[0, DN_HEADS:2 * DN_HEADS].set(A_log[0]).at[1, DN_HEADS:2 * DN_HEADS].set(dt_bias[0])
    nw = norm_w[0].reshape(1, D_MODEL)
    wpo = w_pool_out[0].astype(BF16)
    wdo = w_dn_out[0].astype(BF16)
    wo = w_o[0].astype(BF16)

    xn_all, gates = _norm_gates(x2d, meta_tokens.astype(F32), nw, w_t, gpar)
    proj = _in_proj(xn_all, w_t, conv_w[0], seq_len=seq_len)

    front = CHUNK - N_META
    meta_proj = proj[rows:]
    meta_qkv = jnp.pad(meta_proj[:, COL_Q:COL_ZD], ((front, 0), (0, 0)))
    meta_gates = jnp.pad(gates[rows:], ((front, 0), (0, 0)))
    meta_u = meta_proj[:, COL_U:COL_U + POOL_WIDTH]

    y_dn = _deltanet(proj, meta_qkv, gates, meta_gates, dn_norm_w[0].reshape(1, DN_HEAD_DIM),
                     batch=batch, seq_len=seq_len)

    out = _out_merge(proj, y_dn, x2d, meta_u, pool_mix[0].astype(BF16), pool_scale[0].reshape(1, POOL_WIDTH),
                     wpo, wdo, wo, final_norm_w.reshape(1, D_MODEL), batch=batch, seq_len=seq_len)
    return out.reshape(batch, seq_len, D_MODEL)
```

```python
import functools

import jax
import jax.numpy as jnp
from jax import lax
from jax.experimental import pallas as pl
from jax.experimental.pallas import tpu as pltpu

D_MODEL = 2048
N_META = 16
POOL_GROUPS = 4
POOL_WINDOWS = (2, 4, 8, 16)
POOL_WIDTH = D_MODEL // 2
POOL_GROUP_DIM = POOL_WIDTH // POOL_GROUPS
DN_HEADS = 16
DN_HEAD_DIM = 128
DN_WIDTH = DN_HEADS * DN_HEAD_DIM
CONV_WIDTH = 4
NORM_EPS = 1e-6

COL_U = 0
COL_ZP = COL_U + POOL_WIDTH
COL_Q = COL_ZP + POOL_WIDTH
COL_K = COL_Q + DN_WIDTH
COL_V = COL_K + DN_WIDTH
COL_ZD = COL_V + DN_WIDTH
BA_OFFSET = COL_ZD + DN_WIDTH
COL_GP = BA_OFFSET + 2 * DN_HEADS
COL_GD = COL_GP + D_MODEL
LANES = 128
SUBLANES = 8
GATE_SHIFT = COL_GP % LANES

PROJ_DTYPE = jnp.bfloat16
CHUNK = 64
LEAF = 16
TT_DN = 1024
TM_NORM = 512
TN_PROJ = 256
ROW_BLK_PROJ = 1024
TM_OUT = 256
VMEM_LIMIT = 58 * 1024 * 1024

BF16 = jnp.bfloat16
F32 = jnp.float32


def _mm(a, b):
    return jnp.dot(a.astype(BF16), b.astype(BF16), preferred_element_type=F32)


def _mm_nt(a, b):
    return lax.dot_general(a.astype(BF16), b.astype(BF16), (((1,), (1,)), ((), ())),
                           preferred_element_type=F32)


def _mm_tn(a, b):
    return lax.dot_general(a.astype(BF16), b.astype(BF16), (((0,), (0,)), ((), ())),
                           preferred_element_type=F32)


def _sigmoid(x):
    return 0.5 * jnp.tanh(0.5 * x) + 0.5


def _silu(x):
    h = 0.5 * x
    return h * jnp.tanh(h) + h


def _softplus(x):
    return jnp.maximum(x, 0.0) + jnp.log1p(jnp.exp(-jnp.abs(x)))


def _norm_gates_kernel(x_ref, meta_ref, nw_ref, wba_ref, gpar_ref, xn_ref, gates_ref, *, x_tiles, sub):
    i = pl.program_id(0)
    nw = nw_ref[...]
    wba = wba_ref[...].astype(BF16)
    a_rate = jnp.exp(gpar_ref[0:1, :])
    dt_bias = gpar_ref[1:2, :]

    def rows_block(x, rows):
        ms = jnp.mean(x * x, axis=-1, keepdims=True)
        xn = (x * lax.rsqrt(ms + NORM_EPS) * nw).astype(BF16)
        xn_ref[rows, :] = xn
        ba = _mm_nt(xn, wba)
        lane = lax.broadcasted_iota(jnp.int32, ba.shape, 1)
        g = -a_rate * _softplus(ba + dt_bias)
        gates_ref[rows, :] = jnp.where(lane < DN_HEADS, _sigmoid(ba), jnp.where(lane < 2 * DN_HEADS, g, 0.0))

    @pl.when(i < x_tiles)
    def _():
        def body(r, carry):
            rows = pl.ds(pl.multiple_of(r * sub, sub), sub)
            rows_block(x_ref[rows, :], rows)
            return carry

        lax.fori_loop(0, x_ref.shape[0] // sub, body, 0)

    @pl.when(i == x_tiles)
    def _():
        rows_block(meta_ref[...], pl.ds(0, N_META))


def _norm_gates(x2d, meta, norm_w, w_t, gpar):
    rows = x2d.shape[0]
    tm = TM_NORM
    x_tiles = rows // tm
    kern = functools.partial(_norm_gates_kernel, x_tiles=x_tiles, sub=128)
    return pl.pallas_call(
        kern,
        out_shape=(jax.ShapeDtypeStruct((rows + N_META, D_MODEL), BF16),
                   jax.ShapeDtypeStruct((rows + N_META, LANES), F32)),
        grid=(x_tiles + 1,),
        in_specs=[
            pl.BlockSpec((tm, D_MODEL), lambda i: (jnp.minimum(i, x_tiles - 1), 0)),
            pl.BlockSpec((N_META, D_MODEL), lambda i: (0, 0)),
            pl.BlockSpec((1, D_MODEL), lambda i: (0, 0)),
            pl.BlockSpec((LANES, D_MODEL), lambda i: (BA_OFFSET // LANES, 0)),
            pl.BlockSpec((2, LANES), lambda i: (0, 0)),
        ],
        out_specs=(pl.BlockSpec((tm, D_MODEL), lambda i: (i, 0)),
                   pl.BlockSpec((tm, LANES), lambda i: (i, 0))),
        compiler_params=pltpu.CompilerParams(
            dimension_semantics=("arbitrary",), vmem_limit_bytes=VMEM_LIMIT),
        name="norm_gates",
    )(x2d, meta, norm_w, w_t, gpar)


def _in_proj_kernel(xn_ref, w_ref, cw_ref, proj_ref, acc_ref, *, rows, row_blk, seq_len):
    j = pl.program_id(0)
    tn = w_ref.shape[0]
    w = w_ref[...].astype(BF16)
    ms = slice(rows, rows + N_META)
    conv_tile = (j >= COL_Q // tn) & (j < COL_ZD // tn)
    silu_tile = ((j >= COL_ZP // tn) & (j < COL_Q // tn)) | ((j >= COL_ZD // tn) & (j < BA_OFFSET // tn))

    def plain(fn):
        for r in range(rows // row_blk):
            rs = slice(r * row_blk, (r + 1) * row_blk)
            proj_ref[rs, :] = fn(_mm_nt(xn_ref[rs, :], w)).astype(proj_ref.dtype)
        proj_ref[ms, :] = fn(_mm_nt(xn_ref[ms, :], w)).astype(proj_ref.dtype)

    @pl.when(conv_tile)
    def _():
        cw = cw_ref[...]
        hist_rows = SUBLANES
        n_blk = rows // row_blk

        def conv_silu(slot, n):
            y = acc_ref[slot, hist_rows:hist_rows + n, :] * cw[CONV_WIDTH - 1:CONV_WIDTH]
            for kk in range(CONV_WIDTH - 1):
                off = hist_rows - (CONV_WIDTH - 1) + kk
                y = y + acc_ref[slot, off:off + n, :] * cw[kk:kk + 1]
            return _silu(y)

        meta_acc = _mm_nt(xn_ref[ms, :], w)
        acc_ref[0, 0:hist_rows, :] = jnp.zeros((hist_rows, tn), F32)
        acc_ref[0, hist_rows:hist_rows + N_META, :] = meta_acc
        proj_ref[ms, :] = conv_silu(0, N_META).astype(proj_ref.dtype)
        meta_tail = meta_acc[N_META - hist_rows:]
        for r in range(n_blk + 1):
            if r < n_blk:
                slot = r % 2
                rs = slice(r * row_blk, (r + 1) * row_blk)
                acc = _mm_nt(xn_ref[rs, :], w)
                batch_start = (r * row_blk) % seq_len == 0
                acc_ref[slot, 0:hist_rows, :] = (
                    meta_tail if batch_start else acc_ref[1 - slot, row_blk:row_blk + hist_rows, :])
                acc_ref[slot, hist_rows:hist_rows + row_blk, :] = acc
            if r > 0:
                ps = slice((r - 1) * row_blk, r * row_blk)
                proj_ref[ps, :] = conv_silu((r - 1) % 2, row_blk).astype(proj_ref.dtype)

    @pl.when(silu_tile)
    def _():
        plain(_silu)

    @pl.when(jnp.logical_not(conv_tile | silu_tile))
    def _():
        plain(lambda a: a)


def _in_proj(xn_all, w_t, conv_w, *, seq_len):
    rows = xn_all.shape[0] - N_META
    cols = w_t.shape[0]
    tn = TN_PROJ
    n_conv_tiles = (COL_ZD - COL_Q) // tn
    kern = functools.partial(_in_proj_kernel, rows=rows, row_blk=ROW_BLK_PROJ, seq_len=seq_len)
    return pl.pallas_call(
        kern,
        out_shape=jax.ShapeDtypeStruct((rows + N_META, cols), PROJ_DTYPE),
        grid=(pl.cdiv(cols, tn),),
        in_specs=[
            pl.BlockSpec((rows + N_META, D_MODEL), lambda j: (0, 0), pipeline_mode=pl.Buffered(1)),
            pl.BlockSpec((tn, D_MODEL), lambda j: (j, 0)),
            pl.BlockSpec((CONV_WIDTH, tn), lambda j: (0, jnp.clip(j - COL_Q // tn, 0, n_conv_tiles - 1))),
        ],
        out_specs=pl.BlockSpec((rows + N_META, tn), lambda j: (0, j)),
        scratch_shapes=[pltpu.VMEM((2, SUBLANES + ROW_BLK_PROJ, tn), F32)],
        compiler_params=pltpu.CompilerParams(
            dimension_semantics=("arbitrary",), vmem_limit_bytes=VMEM_LIMIT),
        name="in_proj",
    )(xn_all, w_t, conv_w)


PAIR = 2 * DN_HEAD_DIM


def _split3(x):
    hi = x.astype(BF16)
    r1 = x - hi.astype(F32)
    mid = r1.astype(BF16)
    lo = (r1 - mid.astype(F32)).astype(BF16)
    return hi, mid, lo


def _block_diag_rows(y, half):
    lane = lax.broadcasted_iota(jnp.int32, y.shape, 1)
    zero = jnp.zeros_like(y)
    return jnp.concatenate([jnp.where(lane < half, y, zero), jnp.where(lane >= half, y, zero)], axis=0)


def _inverse_masks(c):
    row = lax.broadcasted_iota(jnp.int32, (c, 2 * c), 0)
    col = lax.broadcasted_iota(jnp.int32, (c, 2 * c), 1) & (c - 1)
    blk = lambda x, m: x >> (m.bit_length() - 1)
    eye = (row == col).astype(F32)
    same_leaf = blk(row, LEAF) == blk(col, LEAF)
    offs = []
    m = 2 * LEAF
    while m <= c:
        offs.append((blk(row, m) == blk(col, m)) & (blk(row, m // 2) != blk(col, m // 2)))
        m *= 2
    return eye, same_leaf, offs


def _inv_unit_lower_packed(lps, c, masks, fill, n_fill=1):
    eye, same_leaf, offs = masks
    bd = lambda y: _block_diag_rows(y.astype(BF16), c)
    ps = [jnp.where(same_leaf, -l, 0.0) for l in lps]
    ts = [eye + p for p in ps]
    dep = ps[0]
    ps = [_mm(p, bd(p)) for p in ps]
    fill(dep, ps, n_fill + 1)
    span = 4
    while span <= LEAF:
        dep = ps[0]
        last = span == LEAF
        prod = [_mm(t if last else jnp.concatenate([t, p], axis=0), bd(p)) for t, p in zip(ts, ps)]
        ts = [t + x[:c] for t, x in zip(ts, prod)]
        if not last:
            ps = [x[c:] for x in prod]
        fill(dep, ts, n_fill)
        span *= 2
    for off in offs:
        dep = ts[0]
        inner = [_mm(jnp.where(off, l, 0.0), bd(t)) for l, t in zip(lps, ts)]
        fill(dep, inner, n_fill)
        dep = inner[0]
        ts = [t - _mm(t, bd(x)) for t, x in zip(ts, inner)]
        fill(dep, ts, n_fill)
    return ts


def _delta_rule_core(read_rows, read_gates, n_pos, state_ref, qn_s, kn_s, gsm_s, gl_s, gp_s, gr_s,
                     finish=None, o_s=None):
    c = CHUNK
    hd = DN_HEAD_DIM
    n_pairs = DN_HEADS // 2
    pairs = range(n_pairs)

    rowp = lax.broadcasted_iota(jnp.int32, (c, 2 * c), 0)
    colp = lax.broadcasted_iota(jnp.int32, (c, 2 * c), 1) & (c - 1)
    causal = rowp >= colp
    strict = rowp > colp
    lane_p = lax.broadcasted_iota(jnp.int32, (c, 2 * c), 1)
    inv_masks = _inverse_masks(c)
    tril = (lax.broadcasted_iota(jnp.int32, (c, c), 0) >= lax.broadcasted_iota(jnp.int32, (c, c), 1))
    tril3 = jnp.concatenate([tril.astype(BF16)] * 3, axis=1)
    zero_blk = jnp.zeros((hd, hd), BF16)
    head_cols = lambda h: slice(h * hd, (h + 1) * hd)
    pair_cols_of = lambda p: slice(p * PAIR, (p + 1) * PAIR)
    both = (0, 1)
    ha = [2 * p for p in pairs]
    hb = [2 * p + 1 for p in pairs]

    def l2n(xh, scale):
        return xh * (lax.rsqrt(jnp.sum(xh * xh, axis=-1, keepdims=True) + NORM_EPS) * scale)

    def prepare_pair(j, p, zero=None):
        cols = pair_cols_of(p)
        slot = j & 3
        for idx, (dst, scale) in enumerate(((qn_s, hd ** -0.5), (kn_s, 1.0))):
            x = read_rows(idx, j, cols).astype(F32)
            if zero is not None:
                x = x + jnp.concatenate([zero[0:1], zero[0:1]], axis=1)
            dst[slot, :, cols] = jnp.concatenate([l2n(x[:, :hd], scale), l2n(x[:, hd:], scale)], axis=1)
        r0 = pl.multiple_of((j >> 24) * SUBLANES, SUBLANES)
        back = qn_s[slot, pl.ds(r0, SUBLANES), cols] + kn_s[slot, pl.ds(r0, SUBLANES), cols]
        return back[:, :hd] + back[:, hd:]

    def out_rows(j):
        return pl.multiple_of(jnp.maximum(j, 0) * c, c)

    def exact_rows_dot(lhs3, x):
        hi, mid, lo = _split3(x)
        return jnp.dot(lhs3, jnp.concatenate([hi, mid, lo], axis=0), preferred_element_type=F32)

    def pair_cols(mat, la, lb):
        return jnp.where(lane_p < c, mat[:, la:la + 1], mat[:, lb:lb + 1])

    def pair_rows(mat_t, ra, rb):
        return jnp.concatenate([mat_t[ra:ra + 1, :], mat_t[rb:rb + 1, :]], axis=1)

    def zero_of(v):
        return jnp.where(v != v, v, 0.0)

    def prepare_gates(j):
        slot = j & 3
        gt = read_gates(j)
        gcum = exact_rows_dot(tril3, gt)
        g_last = gcum[c - 1:c, :]
        gsm_s[slot, 0] = jnp.exp(gcum)
        gsm_s[slot, 1] = jnp.exp(g_last - gcum)
        gsm_s[slot, 2] = jnp.broadcast_to(jnp.exp(g_last), (c, LANES))
        gt_t = gt.T
        gcum_t = gcum.T
        for p in pairs:
            g_col = pair_cols(gcum, DN_HEADS + ha[p], DN_HEADS + hb[p])
            g_row = pair_rows(gcum_t, DN_HEADS + ha[p], DN_HEADS + hb[p])
            b_col = pair_cols(gt, ha[p], hb[p])
            b_row = pair_rows(gt_t, ha[p], hb[p])
            dec = jnp.where(causal, jnp.exp(jnp.where(causal, g_col - g_row, 0.0)), 0.0)
            gp_s[slot, p] = dec
            gl_s[slot, p] = jnp.where(strict, b_col * dec, 0.0)
            gr_s[slot, p] = jnp.concatenate(
                [b_row, b_row * jnp.exp(g_row), jnp.zeros((SUBLANES - 2, 2 * c), F32)], axis=0)

    def pair_body(i, carry):
        pos = [2 * i, 2 * i + 1]
        slot = [pz & 3 for pz in pos]
        fillers = []
        for p in pairs:
            for x in both:
                if finish is not None:
                    fillers.append(functools.partial(finish, x, out_rows(pos[x] - 2), p))
                fillers.append(functools.partial(prepare_pair, pos[x] + 2, p))
        n_slices = len(fillers)
        fillers.reverse()

        def fill(dep, into, n=1):
            zero = None if dep is None else zero_of(dep[0:SUBLANES, 0:LANES].astype(F32))
            for _ in range(n):
                if fillers:
                    done = zero_of(fillers.pop()(zero=zero))[0:1]
                    if into[-1].shape[1] == PAIR:
                        done = jnp.concatenate([done, done], axis=1)
                    into[-1] = into[-1] + done

        prob = [(x, p) for x in both for p in pairs]
        eg = [gsm_s[slot[x], 0] for x in both]
        e_rest = [gsm_s[slot[x], 1] for x in both]
        e_last = [gsm_s[slot[x], 2, 0:1, :] for x in both]
        gl = [gl_s[slot[x], p] for x, p in prob]
        gp = [gp_s[slot[x], p] for x, p in prob]
        gr = [gr_s[slot[x], p] for x, p in prob]
        kpair = [kn_s[slot[x], :, pair_cols_of(p)].astype(BF16) for x, p in prob]
        kbd = [_block_diag_rows(kp, hd) for kp in kpair]
        qpair = []
        for n, (x, p) in enumerate(prob):
            staged = [qn_s[slot[x], :, pair_cols_of(p)]]
            if n % 2:
                fill(None, staged)
            qpair.append(staged[0].astype(BF16))
        aq = [_mm_nt(jnp.concatenate([kpair[n], qpair[n]], axis=0), kbd[n]) for n in range(len(prob))]
        for x in both:
            prepare_gates(pos[x] + 2)
        lps = [aq[n][:c] * gl[n] for n in range(len(prob))]
        pps = [aq[n][c:] * gp[n] for n in range(len(prob))]
        per_group = max(1, (n_slices - len(prob) // 2) // 12)
        tps = _inv_unit_lower_packed(lps, c, inv_masks, fill, per_group)
        vbd = [_block_diag_rows(read_rows(2, pos[x], pair_cols_of(p)), hd) for x, p in prob]
        u = [_mm(tps[n] * gr[n][0:1], vbd[n]) for n in range(len(prob))]
        w = [_mm(tps[n] * gr[n][1:2], kbd[n]) for n in range(len(prob))]
        fill(tps[0], w, per_group)
        state = [state_ref[h] for h in range(DN_HEADS)]
        outs = []
        for x in both:
            idx = [x * n_pairs + p for p in pairs]
            qd = [jnp.concatenate(
                [qn_s[slot[x], :, head_cols(h)] * eg[x][:, DN_HEADS + h:DN_HEADS + h + 1]
                 for h in (ha[p], hb[p])], axis=1) for p in pairs]
            sbd = [jnp.concatenate([jnp.concatenate([state[ha[p]].astype(BF16), zero_blk], axis=1),
                                    jnp.concatenate([zero_blk, state[hb[p]].astype(BF16)], axis=1)], axis=0)
                   for p in pairs]
            ws = [_mm(jnp.concatenate([w[idx[p]], qd[p]], axis=0), sbd[p]) for p in pairs]
            fill(w[idx[0]], ws, per_group)
            v_new = [u[idx[p]] - ws[p][:c] for p in pairs]
            o = [ws[p][c:] + _mm(pps[idx[p]], _block_diag_rows(v_new[p].astype(BF16), hd)) for p in pairs]
            new_state = []
            for h in range(DN_HEADS):
                p, half = divmod(h, 2)
                lane_h = DN_HEADS + h
                kd = kn_s[slot[x], :, head_cols(h)] * e_rest[x][:, lane_h:lane_h + 1]
                upd = _mm_tn(kd, v_new[p][:, half * hd:(half + 1) * hd])
                new_state.append(state[h] * e_last[x][:, lane_h:lane_h + 1] + upd)
            state = new_state
            if x == 1:
                fill(ws[0], o, len(fillers))
            outs.append(o)
        for h in range(DN_HEADS):
            state_ref[h] = state[h]
        if o_s is not None:
            for x in both:
                for p in pairs:
                    o_s[x, :, pair_cols_of(p)] = outs[x][p]
        return carry

    for x in both:
        for p in pairs:
            prepare_pair(jnp.int32(x), p)
        prepare_gates(jnp.int32(x))
    n_iter = n_pos // 2
    if n_iter == 1:
        pair_body(jnp.int32(0), 0)
    else:
        lax.fori_loop(0, n_iter, pair_body, 0)
    if finish is not None:
        for x in both:
            last_rows = out_rows(jnp.int32(n_pos - 2 + x))
            for p in pairs:
                finish(x, last_rows, p)


def _deltanet_kernel(q_ref, k_ref, v_ref, z_ref, gates_ref, s0_ref, dnw_ref,
                     y_ref, state_ref, qn_s, kn_s, o_s, gsm_s, gl_s, gp_s, gr_s, *, tt):
    c = CHUNK
    hd = DN_HEAD_DIM
    n_chunks = tt // c
    dnw = dnw_ref[...]

    @pl.when(pl.program_id(1) == 0)
    def _():
        state_ref[...] = s0_ref[...]
        o_s[...] = jnp.zeros(o_s.shape, F32)

    srcs = (q_ref, k_ref, v_ref)

    def rows_at(pos):
        return pl.ds(pl.multiple_of(jnp.minimum(pos, n_chunks - 1) * c, c), c)

    def read_rows(idx, pos, cols):
        return srcs[idx][rows_at(pos), cols]

    def read_gates(pos):
        return gates_ref[rows_at(pos), :]

    def finish(src, row0, p, zero=None):
        rows = pl.ds(row0, c)
        for half in range(2):
            cs = slice(p * PAIR + half * hd, p * PAIR + (half + 1) * hd)
            oh = o_s[src, :, cs]
            if zero is not None:
                oh = oh + zero[0:1]
            oh = oh * lax.rsqrt(jnp.mean(oh * oh, axis=-1, keepdims=True) + NORM_EPS) * dnw
            y_ref[rows, cs] = (oh * z_ref[rows, cs].astype(F32)).astype(y_ref.dtype)
        r0 = pl.multiple_of((row0 >> 24) * (2 * SUBLANES), 2 * SUBLANES)
        back = y_ref[pl.ds(r0, 2 * SUBLANES), p * PAIR:(p + 1) * PAIR].astype(F32)[0:SUBLANES]
        return back[:, :hd] + back[:, hd:]

    _delta_rule_core(read_rows, read_gates, n_chunks, state_ref, qn_s, kn_s, gsm_s, gl_s, gp_s, gr_s,
                     finish=finish, o_s=o_s)


def _deltanet_meta_kernel(mqkv_ref, mgates_ref, s_out_ref, state_ref, qn_s, kn_s, gsm_s, gl_s, gp_s, gr_s):
    c = CHUNK
    state_ref[...] = jnp.zeros(state_ref.shape, F32)

    def rows_at(pos):
        return pl.ds(pl.multiple_of(jnp.minimum(pos, 1) * c, c), c)

    def read_rows(idx, pos, cols):
        return mqkv_ref[rows_at(pos), slice(idx * DN_WIDTH + cols.start, idx * DN_WIDTH + cols.stop)]

    def read_gates(pos):
        return mgates_ref[rows_at(pos), :]

    _delta_rule_core(read_rows, read_gates, 2, state_ref, qn_s, kn_s, gsm_s, gl_s, gp_s, gr_s)
    s_out_ref[...] = state_ref[...]


def _lookahead_scratch():
    c = CHUNK
    return [
        pltpu.VMEM((4, c, DN_WIDTH), F32),
        pltpu.VMEM((4, c, DN_WIDTH), F32),
        pltpu.VMEM((4, 3, c, LANES), F32),
        pltpu.VMEM((4, DN_HEADS // 2, c, 2 * c), F32),
        pltpu.VMEM((4, DN_HEADS // 2, c, 2 * c), F32),
        pltpu.VMEM((4, DN_HEADS // 2, SUBLANES, 2 * c), F32),
    ]


def _deltanet_meta(meta_qkv, meta_gates):
    state_shape = (DN_HEADS, DN_HEAD_DIM, DN_HEAD_DIM)
    qn, kn, gsm, gl, gp, gr = _lookahead_scratch()
    return pl.pallas_call(
        _deltanet_meta_kernel,
        out_shape=jax.ShapeDtypeStruct(state_shape, F32),
        scratch_shapes=[pltpu.VMEM(state_shape, F32), qn, kn, gsm, gl, gp, gr],
        compiler_params=pltpu.CompilerParams(vmem_limit_bytes=VMEM_LIMIT),
        name="deltanet_meta",
    )(meta_qkv, meta_gates)


def _deltanet(proj, gates, state0, dn_norm_w, *, batch, seq_len):
    tt = TT_DN
    nt = seq_len // tt
    c = CHUNK
    kern = functools.partial(_deltanet_kernel, tt=tt)
    row = lambda b, t: b * nt + t
    const = lambda shape: pl.BlockSpec(shape, lambda b, t: (0,) * len(shape), pipeline_mode=pl.Buffered(1))
    state_shape = (DN_HEADS, DN_HEAD_DIM, DN_HEAD_DIM)
    qn, kn, gsm, gl, gp, gr = _lookahead_scratch()
    return pl.pallas_call(
        kern,
        out_shape=jax.ShapeDtypeStruct((batch * seq_len, DN_WIDTH), BF16),
        grid=(batch, nt),
        in_specs=[
            pl.BlockSpec((tt, DN_WIDTH), lambda b, t: (row(b, t), COL_Q // DN_WIDTH)),
            pl.BlockSpec((tt, DN_WIDTH), lambda b, t: (row(b, t), COL_K // DN_WIDTH)),
            pl.BlockSpec((tt, DN_WIDTH), lambda b, t: (row(b, t), COL_V // DN_WIDTH)),
            pl.BlockSpec((tt, DN_WIDTH), lambda b, t: (row(b, t), COL_ZD // DN_WIDTH)),
            pl.BlockSpec((tt, LANES), lambda b, t: (row(b, t), 0)),
            const(state_shape),
            const((1, DN_HEAD_DIM)),
        ],
        out_specs=pl.BlockSpec((tt, DN_WIDTH), lambda b, t: (row(b, t), 0)),
        scratch_shapes=[
            pltpu.VMEM(state_shape, F32),
            qn, kn,
            pltpu.VMEM((2, c, DN_WIDTH), F32),
            gsm, gl, gp, gr,
        ],
        compiler_params=pltpu.CompilerParams(
            dimension_semantics=("arbitrary", "arbitrary"), vmem_limit_bytes=VMEM_LIMIT),
        name="deltanet",
    )(proj, proj, proj, proj, gates, state0, dn_norm_w)


def _out_merge_kernel(u_ref, zp_ref, gpa_ref, gpb_ref, gda_ref, gdb_ref, ydn_ref, x_ref, mu_ref, mix_ref,
                      scale_ref, wpo_ref, wdo_ref, wo_ref, fnw_ref, out_ref, ubuf_ref, *, tm):
    t = pl.program_id(1)
    hist = N_META

    @pl.when(t == 0)
    def _():
        ubuf_ref[0:hist, :] = mu_ref[...].astype(F32)

    @pl.when(t > 0)
    def _():
        ubuf_ref[0:hist, :] = ubuf_ref[tm:tm + hist, :]

    ubuf_ref[hist:hist + tm, :] = u_ref[...].astype(F32)

    pooled = []
    for gi, w in enumerate(POOL_WINDOWS):
        cs = slice(gi * POOL_GROUP_DIM, (gi + 1) * POOL_GROUP_DIM)
        ext = ubuf_ref[:, cs]
        acc = ext
        span = 1
        while span < w:
            acc = acc + pltpu.roll(acc, span, axis=0)
            span *= 2
        d = acc[hist:] * (1.0 / w) - ext[hist:]
        mixed = _mm(d, mix_ref[gi])
        zp = zp_ref[:, cs].astype(F32)
        pooled.append((mixed * scale_ref[:, cs] * zp).astype(BF16))
    y_pool = jnp.concatenate(pooled, axis=1)

    frame = slice(GATE_SHIFT, GATE_SHIFT + D_MODEL)
    gp = jnp.concatenate([gpa_ref[...], gpb_ref[...]], axis=1).astype(F32)[:, frame]
    gd = jnp.concatenate([gda_ref[...], gdb_ref[...]], axis=1).astype(F32)[:, frame]
    t_pool = jnp.dot(y_pool, wpo_ref[...], preferred_element_type=F32)
    t_dn = jnp.dot(ydn_ref[...], wdo_ref[...], preferred_element_type=F32)
    merged = _sigmoid(gp) * t_pool + _sigmoid(gd) * t_dn
    h = x_ref[...] + jnp.dot(merged.astype(BF16), wo_ref[...], preferred_element_type=F32)
    ms = jnp.mean(h * h, axis=-1, keepdims=True)
    out_ref[...] = h * lax.rsqrt(ms + NORM_EPS) * fnw_ref[...]


def _resident(shape, index_map):
    return pl.BlockSpec(shape, index_map, pipeline_mode=pl.Buffered(1))


def _out_merge(proj, y_dn, x2d, meta_u, mix, scale, wpo, wdo, wo, fnw, *, batch, seq_len):
    tm = TM_OUT
    tiles = seq_len // tm
    kern = functools.partial(_out_merge_kernel, tm=tm)
    row = lambda b, t: b * tiles + t
    gp0, gd0 = COL_GP - GATE_SHIFT, COL_GD - GATE_SHIFT
    assert gp0 % D_MODEL == 0 and gd0 % D_MODEL == 0
    return pl.pallas_call(
        kern,
        out_shape=jax.ShapeDtypeStruct((batch * seq_len, D_MODEL), F32),
        grid=(batch, tiles),
        in_specs=[
            pl.BlockSpec((tm, POOL_WIDTH), lambda b, t: (row(b, t), COL_U // POOL_WIDTH)),
            pl.BlockSpec((tm, POOL_WIDTH), lambda b, t: (row(b, t), COL_ZP // POOL_WIDTH)),
            pl.BlockSpec((tm, D_MODEL), lambda b, t: (row(b, t), gp0 // D_MODEL)),
            pl.BlockSpec((tm, LANES), lambda b, t: (row(b, t), (gp0 + D_MODEL) // LANES)),
            pl.BlockSpec((tm, D_MODEL), lambda b, t: (row(b, t), gd0 // D_MODEL)),
            pl.BlockSpec((tm, LANES), lambda b, t: (row(b, t), (gd0 + D_MODEL) // LANES)),
            pl.BlockSpec((tm, DN_WIDTH), lambda b, t: (row(b, t), 0)),
            pl.BlockSpec((tm, D_MODEL), lambda b, t: (row(b, t), 0)),
            _resident((N_META, POOL_WIDTH), lambda b, t: (0, 0)),
            _resident((POOL_GROUPS, POOL_GROUP_DIM, POOL_GROUP_DIM), lambda b, t: (0, 0, 0)),
            _resident((1, POOL_WIDTH), lambda b, t: (0, 0)),
            _resident((POOL_WIDTH, D_MODEL), lambda b, t: (0, 0)),
            _resident((DN_WIDTH, D_MODEL), lambda b, t: (0, 0)),
            _resident((D_MODEL, D_MODEL), lambda b, t: (0, 0)),
            _resident((1, D_MODEL), lambda b, t: (0, 0)),
        ],
        out_specs=pl.BlockSpec((tm, D_MODEL), lambda b, t: (row(b, t), 0)),
        scratch_shapes=[pltpu.VMEM((N_META + tm, POOL_WIDTH), F32)],
        compiler_params=pltpu.CompilerParams(
            dimension_semantics=("arbitrary", "arbitrary"), vmem_limit_bytes=VMEM_LIMIT),
        name="out_merge",
    )(proj, proj, proj, proj, proj, proj, y_dn, x2d, meta_u, mix, scale, wpo, wdo, wo, fnw)


def kernel(x, meta_tokens, norm_w, w_in, conv_w, A_log, dt_bias, pool_mix, pool_scale, dn_norm_w,
           w_pool_out, w_dn_out, w_o, final_norm_w):
    batch, seq_len, _ = x.shape
    assert norm_w.shape[0] == 1, "single layer block"
    x2d = x.reshape(batch * seq_len, D_MODEL)

    rows = batch * seq_len
    w_t = jnp.swapaxes(w_in, 1, 2).reshape(w_in.shape[2], D_MODEL)
    gpar = jnp.zeros((2, LANES), F32)
    gpar = gpar.at[0, DN_HEADS:2 * DN_HEADS].set(A_log[0]).at[1, DN_HEADS:2 * DN_HEADS].set(dt_bias[0])
    nw = norm_w[0].reshape(1, D_MODEL)
    wpo = w_pool_out[0].astype(BF16)
    wdo = w_dn_out[0].astype(BF16)
    wo = w_o[0].astype(BF16)

    xn_all, gates = _norm_gates(x2d, meta_tokens.astype(F32), nw, w_t, gpar)
    proj = _in_proj(xn_all, w_t, conv_w[0], seq_len=seq_len)

    front = 2 * CHUNK - N_META
    meta_proj = proj[rows:]
    meta_qkv = jnp.pad(meta_proj[:, COL_Q:COL_ZD], ((front, 0), (0, 0)))
    meta_gates = jnp.pad(gates[rows:], ((front, 0), (0, 0)))
    meta_u = meta_proj[:, COL_U:COL_U + POOL_WIDTH]

    state0 = _deltanet_meta(meta_qkv, meta_gates)
    y_dn = _deltanet(proj, gates, state0, dn_norm_w[0].reshape(1, DN_HEAD_DIM), batch=batch, seq_len=seq_len)

    out = _out_merge(proj, y_dn, x2d, meta_u, pool_mix[0].astype(BF16), pool_scale[0].reshape(1, POOL_WIDTH),
                     wpo, wdo, wo, final_norm_w.reshape(1, D_MODEL), batch=batch, seq_len=seq_len)
    return out.reshape(batch, seq_len, D_MODEL)
```

```python
import functools

import jax
import jax.numpy as jnp
from jax import lax
from jax.experimental import pallas as pl
from jax.experimental.pallas import tpu as pltpu

D_MODEL = 2048
N_META = 16
POOL_GROUPS = 4
POOL_WINDOWS = (2, 4, 8, 16)
POOL_WIDTH = D_MODEL // 2
POOL_GROUP_DIM = POOL_WIDTH // POOL_GROUPS
DN_HEADS = 16
DN_HEAD_DIM = 128
DN_WIDTH = DN_HEADS * DN_HEAD_DIM
CONV_WIDTH = 4
NORM_EPS = 1e-6

COL_U = 0
COL_ZP = COL_U + POOL_WIDTH
COL_Q = COL_ZP + POOL_WIDTH
COL_K = COL_Q + DN_WIDTH
COL_V = COL_K + DN_WIDTH
COL_ZD = COL_V + DN_WIDTH
BA_OFFSET = COL_ZD + DN_WIDTH
COL_GP = BA_OFFSET + 2 * DN_HEADS
COL_GD = COL_GP + D_MODEL
LANES = 128
SUBLANES = 8
GATE_SHIFT = COL_GP % LANES

PROJ_DTYPE = jnp.bfloat16
CHUNK = 64
LEAF = 16
TT_DN = 1024
TM_NORM = 1024
TN_PROJ = 256
ROW_BLK_PROJ = 1024
TM_OUT = 256
VMEM_LIMIT = 58 * 1024 * 1024

BF16 = jnp.bfloat16
F32 = jnp.float32


def _mm(a, b):
    return jnp.dot(a.astype(BF16), b.astype(BF16), preferred_element_type=F32)


def _mm_nt(a, b):
    return lax.dot_general(a.astype(BF16), b.astype(BF16), (((1,), (1,)), ((), ())),
                           preferred_element_type=F32)


def _mm_tn(a, b):
    return lax.dot_general(a.astype(BF16), b.astype(BF16), (((0,), (0,)), ((), ())),
                           preferred_element_type=F32)


def _sigmoid(x):
    return 0.5 * jnp.tanh(0.5 * x) + 0.5


def _silu(x):
    h = 0.5 * x
    return h * jnp.tanh(h) + h


def _softplus(x):
    return jnp.maximum(x, 0.0) + jnp.log1p(jnp.exp(-jnp.abs(x)))


def _norm_gates_kernel(x_ref, meta_ref, nw_ref, wba_ref, gpar_ref, xn_ref, gates_ref, *, x_tiles, sub):
    i = pl.program_id(0)
    nw = nw_ref[...]
    wba = wba_ref[...].astype(BF16)
    a_rate = jnp.exp(gpar_ref[0:1, :])
    dt_bias = gpar_ref[1:2, :]

    def rows_block(x, rows):
        ms = jnp.mean(x * x, axis=-1, keepdims=True)
        xn = (x * lax.rsqrt(ms + NORM_EPS) * nw).astype(BF16)
        xn_ref[rows, :] = xn
        ba = _mm_nt(xn, wba)
        lane = lax.broadcasted_iota(jnp.int32, ba.shape, 1)
        g = -a_rate * _softplus(ba + dt_bias)
        gates_ref[rows, :] = jnp.where(lane < DN_HEADS, _sigmoid(ba), jnp.where(lane < 2 * DN_HEADS, g, 0.0))

    @pl.when(i < x_tiles)
    def _():
        def body(r, carry):
            rows = pl.ds(pl.multiple_of(r * sub, sub), sub)
            rows_block(x_ref[rows, :], rows)
            return carry

        lax.fori_loop(0, x_ref.shape[0] // sub, body, 0)

    @pl.when(i == x_tiles)
    def _():
        rows_block(meta_ref[...], pl.ds(0, N_META))


def _norm_gates(x2d, meta, norm_w, w_t, gpar):
    rows = x2d.shape[0]
    tm = TM_NORM
    x_tiles = rows // tm
    kern = functools.partial(_norm_gates_kernel, x_tiles=x_tiles, sub=128)
    return pl.pallas_call(
        kern,
        out_shape=(jax.ShapeDtypeStruct((rows + N_META, D_MODEL), BF16),
                   jax.ShapeDtypeStruct((rows + N_META, LANES), F32)),
        grid=(x_tiles + 1,),
        in_specs=[
            pl.BlockSpec((tm, D_MODEL), lambda i: (jnp.minimum(i, x_tiles - 1), 0)),
            pl.BlockSpec((N_META, D_MODEL), lambda i: (0, 0)),
            pl.BlockSpec((1, D_MODEL), lambda i: (0, 0)),
            pl.BlockSpec((LANES, D_MODEL), lambda i: (BA_OFFSET // LANES, 0)),
            pl.BlockSpec((2, LANES), lambda i: (0, 0)),
        ],
        out_specs=(pl.BlockSpec((tm, D_MODEL), lambda i: (i, 0)),
                   pl.BlockSpec((tm, LANES), lambda i: (i, 0))),
        compiler_params=pltpu.CompilerParams(
            dimension_semantics=("arbitrary",), vmem_limit_bytes=VMEM_LIMIT),
        name="norm_gates",
    )(x2d, meta, norm_w, w_t, gpar)


def _in_proj_kernel(xn_ref, w_ref, cw_ref, proj_ref, acc_ref, *, rows, row_blk, seq_len):
    j = pl.program_id(0)
    tn = w_ref.shape[0]
    w = w_ref[...].astype(BF16)
    ms = slice(rows, rows + N_META)
    conv_tile = (j >= COL_Q // tn) & (j < COL_ZD // tn)
    silu_tile = ((j >= COL_ZP // tn) & (j < COL_Q // tn)) | ((j >= COL_ZD // tn) & (j < BA_OFFSET // tn))

    def plain(fn):
        for r in range(rows // row_blk):
            rs = slice(r * row_blk, (r + 1) * row_blk)
            proj_ref[rs, :] = fn(_mm_nt(xn_ref[rs, :], w)).astype(proj_ref.dtype)
        proj_ref[ms, :] = fn(_mm_nt(xn_ref[ms, :], w)).astype(proj_ref.dtype)

    @pl.when(conv_tile)
    def _():
        cw = cw_ref[...]
        hist_rows = SUBLANES
        n_blk = rows // row_blk

        def conv_silu(slot, n):
            y = acc_ref[slot, hist_rows:hist_rows + n, :] * cw[CONV_WIDTH - 1:CONV_WIDTH]
            for kk in range(CONV_WIDTH - 1):
                off = hist_rows - (CONV_WIDTH - 1) + kk
                y = y + acc_ref[slot, off:off + n, :] * cw[kk:kk + 1]
            return _silu(y)

        meta_acc = _mm_nt(xn_ref[ms, :], w)
        acc_ref[0, 0:hist_rows, :] = jnp.zeros((hist_rows, tn), F32)
        acc_ref[0, hist_rows:hist_rows + N_META, :] = meta_acc
        proj_ref[ms, :] = conv_silu(0, N_META).astype(proj_ref.dtype)
        meta_tail = meta_acc[N_META - hist_rows:]
        for r in range(n_blk + 1):
            if r < n_blk:
                slot = r % 2
                rs = slice(r * row_blk, (r + 1) * row_blk)
                acc = _mm_nt(xn_ref[rs, :], w)
                batch_start = (r * row_blk) % seq_len == 0
                acc_ref[slot, 0:hist_rows, :] = (
                    meta_tail if batch_start else acc_ref[1 - slot, row_blk:row_blk + hist_rows, :])
                acc_ref[slot, hist_rows:hist_rows + row_blk, :] = acc
            if r > 0:
                ps = slice((r - 1) * row_blk, r * row_blk)
                proj_ref[ps, :] = conv_silu((r - 1) % 2, row_blk).astype(proj_ref.dtype)

    @pl.when(silu_tile)
    def _():
        plain(_silu)

    @pl.when(jnp.logical_not(conv_tile | silu_tile))
    def _():
        plain(lambda a: a)


def _in_proj(xn_all, w_t, conv_w, *, seq_len):
    rows = xn_all.shape[0] - N_META
    cols = w_t.shape[0]
    tn = TN_PROJ
    n_conv_tiles = (COL_ZD - COL_Q) // tn
    kern = functools.partial(_in_proj_kernel, rows=rows, row_blk=ROW_BLK_PROJ, seq_len=seq_len)
    return pl.pallas_call(
        kern,
        out_shape=jax.ShapeDtypeStruct((rows + N_META, cols), PROJ_DTYPE),
        grid=(pl.cdiv(cols, tn),),
        in_specs=[
            pl.BlockSpec((rows + N_META, D_MODEL), lambda j: (0, 0), pipeline_mode=pl.Buffered(1)),
            pl.BlockSpec((tn, D_MODEL), lambda j: (j, 0)),
            pl.BlockSpec((CONV_WIDTH, tn), lambda j: (0, jnp.clip(j - COL_Q // tn, 0, n_conv_tiles - 1))),
        ],
        out_specs=pl.BlockSpec((rows + N_META, tn), lambda j: (0, j)),
        scratch_shapes=[pltpu.VMEM((2, SUBLANES + ROW_BLK_PROJ, tn), F32)],
        compiler_params=pltpu.CompilerParams(
            dimension_semantics=("arbitrary",), vmem_limit_bytes=VMEM_LIMIT),
        name="in_proj",
    )(xn_all, w_t, conv_w)


PAIR = 2 * DN_HEAD_DIM


def _split3(x):
    hi = x.astype(BF16)
    r1 = x - hi.astype(F32)
    mid = r1.astype(BF16)
    lo = (r1 - mid.astype(F32)).astype(BF16)
    return hi, mid, lo


def _block_diag_rows(y, half):
    lane = lax.broadcasted_iota(jnp.int32, y.shape, 1)
    zero = jnp.zeros_like(y)
    return jnp.concatenate([jnp.where(lane < half, y, zero), jnp.where(lane >= half, y, zero)], axis=0)


def _inverse_masks(c):
    row = lax.broadcasted_iota(jnp.int32, (c, 2 * c), 0)
    col = lax.broadcasted_iota(jnp.int32, (c, 2 * c), 1) & (c - 1)
    blk = lambda x, m: x >> (m.bit_length() - 1)
    eye = (row == col).astype(F32)
    same_leaf = blk(row, LEAF) == blk(col, LEAF)
    offs = []
    m = 2 * LEAF
    while m <= c:
        offs.append((blk(row, m) == blk(col, m)) & (blk(row, m // 2) != blk(col, m // 2)))
        m *= 2
    return eye, same_leaf, offs


def _inv_unit_lower_packed(lps, c, masks, fill, n_fill=1):
    eye, same_leaf, offs = masks
    bd = lambda y: _block_diag_rows(y.astype(BF16), c)
    ps = [jnp.where(same_leaf, -l, 0.0) for l in lps]
    ts = [eye + p for p in ps]
    dep = ps[0]
    ps = [_mm(p, bd(p)) for p in ps]
    fill(dep, ps, n_fill + 1)
    span = 4
    while span <= LEAF:
        dep = ps[0]
        last = span == LEAF
        prod = [_mm(t if last else jnp.concatenate([t, p], axis=0), bd(p)) for t, p in zip(ts, ps)]
        ts = [t + x[:c] for t, x in zip(ts, prod)]
        if not last:
            ps = [x[c:] for x in prod]
        fill(dep, ts, n_fill)
        span *= 2
    for off in offs:
        dep = ts[0]
        inner = [_mm(jnp.where(off, l, 0.0), bd(t)) for l, t in zip(lps, ts)]
        fill(dep, inner, n_fill)
        dep = inner[0]
        ts = [t - _mm(t, bd(x)) for t, x in zip(ts, inner)]
        fill(dep, ts, n_fill)
    return ts


def _delta_rule_core(read_rows, read_gates, n_pos, state_ref, qn_s, kn_s, gsm_s, gl_s, gp_s, gr_s,
                     finish=None, o_s=None):
    c = CHUNK
    hd = DN_HEAD_DIM
    n_pairs = DN_HEADS // 2
    pairs = range(n_pairs)

    rowp = lax.broadcasted_iota(jnp.int32, (c, 2 * c), 0)
    colp = lax.broadcasted_iota(jnp.int32, (c, 2 * c), 1) & (c - 1)
    causal = rowp >= colp
    strict = rowp > colp
    lane_p = lax.broadcasted_iota(jnp.int32, (c, 2 * c), 1)
    inv_masks = _inverse_masks(c)
    tril = (lax.broadcasted_iota(jnp.int32, (c, c), 0) >= lax.broadcasted_iota(jnp.int32, (c, c), 1))
    tril3 = jnp.concatenate([tril.astype(BF16)] * 3, axis=1)
    zero_blk = jnp.zeros((hd, hd), BF16)
    head_cols = lambda h: slice(h * hd, (h + 1) * hd)
    pair_cols_of = lambda p: slice(p * PAIR, (p + 1) * PAIR)
    both = (0, 1)
    ha = [2 * p for p in pairs]
    hb = [2 * p + 1 for p in pairs]

    def l2n(xh, scale):
        return xh * (lax.rsqrt(jnp.sum(xh * xh, axis=-1, keepdims=True) + NORM_EPS) * scale)

    def prepare_pair(j, p, zero=None):
        cols = pair_cols_of(p)
        slot = j & 3
        for idx, (dst, scale) in enumerate(((qn_s, hd ** -0.5), (kn_s, 1.0))):
            x = read_rows(idx, j, cols).astype(F32)
            if zero is not None:
                x = x + jnp.concatenate([zero[0:1], zero[0:1]], axis=1)
            dst[slot, :, cols] = jnp.concatenate([l2n(x[:, :hd], scale), l2n(x[:, hd:], scale)], axis=1)
        r0 = pl.multiple_of((j >> 24) * SUBLANES, SUBLANES)
        back = qn_s[slot, pl.ds(r0, SUBLANES), cols] + kn_s[slot, pl.ds(r0, SUBLANES), cols]
        return back[:, :hd] + back[:, hd:]

    def out_rows(j):
        return pl.multiple_of(jnp.maximum(j, 0) * c, c)

    def exact_rows_dot(lhs3, x):
        hi, mid, lo = _split3(x)
        return jnp.dot(lhs3, jnp.concatenate([hi, mid, lo], axis=0), preferred_element_type=F32)

    def pair_cols(mat, la, lb):
        return jnp.where(lane_p < c, mat[:, la:la + 1], mat[:, lb:lb + 1])

    def pair_rows(mat_t, ra, rb):
        return jnp.concatenate([mat_t[ra:ra + 1, :], mat_t[rb:rb + 1, :]], axis=1)

    def zero_of(v):
        return jnp.where(v != v, v, 0.0)

    def prepare_gates(j):
        slot = j & 3
        gt = read_gates(j)
        gcum = exact_rows_dot(tril3, gt)
        g_last = gcum[c - 1:c, :]
        gsm_s[slot, 0] = jnp.exp(gcum)
        gsm_s[slot, 1] = jnp.exp(g_last - gcum)
        gsm_s[slot, 2] = jnp.broadcast_to(jnp.exp(g_last), (c, LANES))
        gt_t = gt.T
        gcum_t = gcum.T
        for p in pairs:
            g_col = pair_cols(gcum, DN_HEADS + ha[p], DN_HEADS + hb[p])
            g_row = pair_rows(gcum_t, DN_HEADS + ha[p], DN_HEADS + hb[p])
            b_col = pair_cols(gt, ha[p], hb[p])
            b_row = pair_rows(gt_t, ha[p], hb[p])
            dec = jnp.where(causal, jnp.exp(jnp.where(causal, g_col - g_row, 0.0)), 0.0)
            gp_s[slot, p] = dec
            gl_s[slot, p] = jnp.where(strict, b_col * dec, 0.0)
            gr_s[slot, p] = jnp.concatenate(
                [b_row, b_row * jnp.exp(g_row), jnp.zeros((SUBLANES - 2, 2 * c), F32)], axis=0)

    def pair_body(i, carry):
        pos = [2 * i, 2 * i + 1]
        slot = [pz & 3 for pz in pos]
        fillers = []
        for p in pairs:
            for x in both:
                if finish is not None:
                    fillers.append(functools.partial(finish, x, out_rows(pos[x] - 2), p))
                fillers.append(functools.partial(prepare_pair, pos[x] + 2, p))
        n_slices = len(fillers)
        fillers.reverse()

        def fill(dep, into, n=1):
            zero = None if dep is None else zero_of(dep[0:SUBLANES, 0:LANES].astype(F32))
            for _ in range(n):
                if fillers:
                    done = zero_of(fillers.pop()(zero=zero))[0:1]
                    if into[-1].shape[1] == PAIR:
                        done = jnp.concatenate([done, done], axis=1)
                    into[-1] = into[-1] + done

        prob = [(x, p) for x in both for p in pairs]
        eg = [gsm_s[slot[x], 0] for x in both]
        e_rest = [gsm_s[slot[x], 1] for x in both]
        e_last = [gsm_s[slot[x], 2, 0:1, :] for x in both]
        gl = [gl_s[slot[x], p] for x, p in prob]
        gp = [gp_s[slot[x], p] for x, p in prob]
        gr = [gr_s[slot[x], p] for x, p in prob]
        kpair = [kn_s[slot[x], :, pair_cols_of(p)].astype(BF16) for x, p in prob]
        kbd = [_block_diag_rows(kp, hd) for kp in kpair]
        qpair = []
        for n, (x, p) in enumerate(prob):
            staged = [qn_s[slot[x], :, pair_cols_of(p)]]
            if n % 2:
                fill(None, staged)
            qpair.append(staged[0].astype(BF16))
        aq = [_mm_nt(jnp.concatenate([kpair[n], qpair[n]], axis=0), kbd[n]) for n in range(len(prob))]
        for x in both:
            prepare_gates(pos[x] + 2)
        lps = [aq[n][:c] * gl[n] for n in range(len(prob))]
        pps = [aq[n][c:] * gp[n] for n in range(len(prob))]
        per_group = max(1, (n_slices - len(prob) // 2) // 12)
        tps = _inv_unit_lower_packed(lps, c, inv_masks, fill, per_group)
        vbd = [_block_diag_rows(read_rows(2, pos[x], pair_cols_of(p)), hd) for x, p in prob]
        u = [_mm(tps[n] * gr[n][0:1], vbd[n]) for n in range(len(prob))]
        w = [_mm(tps[n] * gr[n][1:2], kbd[n]) for n in range(len(prob))]
        fill(tps[0], w, per_group)
        state = [state_ref[h] for h in range(DN_HEADS)]
        outs = []
        for x in both:
            idx = [x * n_pairs + p for p in pairs]
            qd = [jnp.concatenate(
                [qn_s[slot[x], :, head_cols(h)] * eg[x][:, DN_HEADS + h:DN_HEADS + h + 1]
                 for h in (ha[p], hb[p])], axis=1) for p in pairs]
            sbd = [jnp.concatenate([jnp.concatenate([state[ha[p]].astype(BF16), zero_blk], axis=1),
                                    jnp.concatenate([zero_blk, state[hb[p]].astype(BF16)], axis=1)], axis=0)
                   for p in pairs]
            ws = [_mm(jnp.concatenate([w[idx[p]], qd[p]], axis=0), sbd[p]) for p in pairs]
            fill(w[idx[0]], ws, per_group)
            v_new = [u[idx[p]] - ws[p][:c] for p in pairs]
            o = [ws[p][c:] + _mm(pps[idx[p]], _block_diag_rows(v_new[p].astype(BF16), hd)) for p in pairs]
            new_state = []
            for h in range(DN_HEADS):
                p, half = divmod(h, 2)
                lane_h = DN_HEADS + h
                kd = kn_s[slot[x], :, head_cols(h)] * e_rest[x][:, lane_h:lane_h + 1]
                upd = _mm_tn(kd, v_new[p][:, half * hd:(half + 1) * hd])
                new_state.append(state[h] * e_last[x][:, lane_h:lane_h + 1] + upd)
            state = new_state
            if x == 1:
                fill(ws[0], o, len(fillers))
            outs.append(o)
        for h in range(DN_HEADS):
            state_ref[h] = state[h]
        if o_s is not None:
            for x in both:
                for p in pairs:
                    o_s[x, :, pair_cols_of(p)] = outs[x][p]
        return carry

    for x in both:
        for p in pairs:
            prepare_pair(jnp.int32(x), p)
        prepare_gates(jnp.int32(x))
    n_iter = n_pos // 2
    if n_iter == 1:
        pair_body(jnp.int32(0), 0)
    else:
        lax.fori_loop(0, n_iter, pair_body, 0)
    if finish is not None:
        for x in both:
            last_rows = out_rows(jnp.int32(n_pos - 2 + x))
            for p in pairs:
                finish(x, last_rows, p)


def _deltanet_kernel(q_ref, k_ref, v_ref, z_ref, gates_ref, s0_ref, dnw_ref,
                     y_ref, state_ref, qn_s, kn_s, o_s, gsm_s, gl_s, gp_s, gr_s, *, tt):
    c = CHUNK
    hd = DN_HEAD_DIM
    n_chunks = tt // c
    dnw = dnw_ref[...]

    @pl.when(pl.program_id(1) == 0)
    def _():
        state_ref[...] = s0_ref[...]
        o_s[...] = jnp.zeros(o_s.shape, F32)

    srcs = (q_ref, k_ref, v_ref)

    def rows_at(pos):
        return pl.ds(pl.multiple_of(jnp.minimum(pos, n_chunks - 1) * c, c), c)

    def read_rows(idx, pos, cols):
        return srcs[idx][rows_at(pos), cols]

    def read_gates(pos):
        return gates_ref[rows_at(pos), :]

    def finish(src, row0, p, zero=None):
        rows = pl.ds(row0, c)
        for half in range(2):
            cs = slice(p * PAIR + half * hd, p * PAIR + (half + 1) * hd)
            oh = o_s[src, :, cs]
            if zero is not None:
                oh = oh + zero[0:1]
            oh = oh * lax.rsqrt(jnp.mean(oh * oh, axis=-1, keepdims=True) + NORM_EPS) * dnw
            y_ref[rows, cs] = (oh * z_ref[rows, cs].astype(F32)).astype(y_ref.dtype)
        r0 = pl.multiple_of((row0 >> 24) * (2 * SUBLANES), 2 * SUBLANES)
        back = y_ref[pl.ds(r0, 2 * SUBLANES), p * PAIR:(p + 1) * PAIR].astype(F32)[0:SUBLANES]
        return back[:, :hd] + back[:, hd:]

    _delta_rule_core(read_rows, read_gates, n_chunks, state_ref, qn_s, kn_s, gsm_s, gl_s, gp_s, gr_s,
                     finish=finish, o_s=o_s)


def _deltanet_meta_kernel(mqkv_ref, mgates_ref, s_out_ref, state_ref, qn_s, kn_s, gsm_s, gl_s, gp_s, gr_s):
    c = CHUNK
    state_ref[...] = jnp.zeros(state_ref.shape, F32)

    def rows_at(pos):
        return pl.ds(pl.multiple_of(jnp.minimum(pos, 1) * c, c), c)

    def read_rows(idx, pos, cols):
        return mqkv_ref[rows_at(pos), slice(idx * DN_WIDTH + cols.start, idx * DN_WIDTH + cols.stop)]

    def read_gates(pos):
        return mgates_ref[rows_at(pos), :]

    _delta_rule_core(read_rows, read_gates, 2, state_ref, qn_s, kn_s, gsm_s, gl_s, gp_s, gr_s)
    s_out_ref[...] = state_ref[...]


def _lookahead_scratch():
    c = CHUNK
    return [
        pltpu.VMEM((4, c, DN_WIDTH), F32),
        pltpu.VMEM((4, c, DN_WIDTH), F32),
        pltpu.VMEM((4, 3, c, LANES), F32),
        pltpu.VMEM((4, DN_HEADS // 2, c, 2 * c), F32),
        pltpu.VMEM((4, DN_HEADS // 2, c, 2 * c), F32),
        pltpu.VMEM((4, DN_HEADS // 2, SUBLANES, 2 * c), F32),
    ]


def _deltanet_meta(meta_qkv, meta_gates):
    state_shape = (DN_HEADS, DN_HEAD_DIM, DN_HEAD_DIM)
    qn, kn, gsm, gl, gp, gr = _lookahead_scratch()
    return pl.pallas_call(
        _deltanet_meta_kernel,
        out_shape=jax.ShapeDtypeStruct(state_shape, F32),
        scratch_shapes=[pltpu.VMEM(state_shape, F32), qn, kn, gsm, gl, gp, gr],
        compiler_params=pltpu.CompilerParams(vmem_limit_bytes=VMEM_LIMIT),
        name="deltanet_meta",
    )(meta_qkv, meta_gates)


def _deltanet(proj, gates, state0, dn_norm_w, *, batch, seq_len):
    tt = TT_DN
    nt = seq_len // tt
    c = CHUNK
    kern = functools.partial(_deltanet_kernel, tt=tt)
    row = lambda b, t: b * nt + t
    const = lambda shape: pl.BlockSpec(shape, lambda b, t: (0,) * len(shape), pipeline_mode=pl.Buffered(1))
    state_shape = (DN_HEADS, DN_HEAD_DIM, DN_HEAD_DIM)
    qn, kn, gsm, gl, gp, gr = _lookahead_scratch()
    return pl.pallas_call(
        kern,
        out_shape=jax.ShapeDtypeStruct((batch * seq_len, DN_WIDTH), BF16),
        grid=(batch, nt),
        in_specs=[
            pl.BlockSpec((tt, DN_WIDTH), lambda b, t: (row(b, t), COL_Q // DN_WIDTH)),
            pl.BlockSpec((tt, DN_WIDTH), lambda b, t: (row(b, t), COL_K // DN_WIDTH)),
            pl.BlockSpec((tt, DN_WIDTH), lambda b, t: (row(b, t), COL_V // DN_WIDTH)),
            pl.BlockSpec((tt, DN_WIDTH), lambda b, t: (row(b, t), COL_ZD // DN_WIDTH)),
            pl.BlockSpec((tt, LANES), lambda b, t: (row(b, t), 0)),
            const(state_shape),
            const((1, DN_HEAD_DIM)),
        ],
        out_specs=pl.BlockSpec((tt, DN_WIDTH), lambda b, t: (row(b, t), 0)),
        scratch_shapes=[
            pltpu.VMEM(state_shape, F32),
            qn, kn,
            pltpu.VMEM((2, c, DN_WIDTH), F32),
            gsm, gl, gp, gr,
        ],
        compiler_params=pltpu.CompilerParams(
            dimension_semantics=("arbitrary", "arbitrary"), vmem_limit_bytes=VMEM_LIMIT),
        name="deltanet",
    )(proj, proj, proj, proj, gates, state0, dn_norm_w)


def _out_merge_kernel(u_ref, zp_ref, gpa_ref, gpb_ref, gda_ref, gdb_ref, ydn_ref, x_ref, mu_ref, mix_ref,
                      scale_ref, wpo_ref, wdo_ref, wo_ref, fnw_ref, out_ref, ubuf_ref, *, tm):
    t = pl.program_id(1)
    hist = N_META

    @pl.when(t == 0)
    def _():
        ubuf_ref[0:hist, :] = mu_ref[...].astype(F32)

    @pl.when(t > 0)
    def _():
        ubuf_ref[0:hist, :] = ubuf_ref[tm:tm + hist, :]

    ubuf_ref[hist:hist + tm, :] = u_ref[...].astype(F32)

    pooled = []
    for gi, w in enumerate(POOL_WINDOWS):
        cs = slice(gi * POOL_GROUP_DIM, (gi + 1) * POOL_GROUP_DIM)
        ext = ubuf_ref[:, cs]
        acc = ext
        span = 1
        while span < w:
            acc = acc + pltpu.roll(acc, span, axis=0)
            span *= 2
        d = acc[hist:] * (1.0 / w) - ext[hist:]
        mixed = _mm(d, mix_ref[gi])
        zp = zp_ref[:, cs].astype(F32)
        pooled.append((mixed * scale_ref[:, cs] * zp).astype(BF16))
    y_pool = jnp.concatenate(pooled, axis=1)

    frame = slice(GATE_SHIFT, GATE_SHIFT + D_MODEL)
    gp = jnp.concatenate([gpa_ref[...], gpb_ref[...]], axis=1).astype(F32)[:, frame]
    gd = jnp.concatenate([gda_ref[...], gdb_ref[...]], axis=1).astype(F32)[:, frame]
    t_pool = jnp.dot(y_pool, wpo_ref[...], preferred_element_type=F32)
    t_dn = jnp.dot(ydn_ref[...], wdo_ref[...], preferred_element_type=F32)
    merged = _sigmoid(gp) * t_pool + _sigmoid(gd) * t_dn
    h = x_ref[...] + jnp.dot(merged.astype(BF16), wo_ref[...], preferred_element_type=F32)
    ms = jnp.mean(h * h, axis=-1, keepdims=True)
    out_ref[...] = h * lax.rsqrt(ms + NORM_EPS) * fnw_ref[...]


def _resident(shape, index_map):
    return pl.BlockSpec(shape, index_map, pipeline_mode=pl.Buffered(1))


def _out_merge(proj, y_dn, x2d, meta_u, mix, scale, wpo, wdo, wo, fnw, *, batch, seq_len):
    tm = TM_OUT
    tiles = seq_len // tm
    kern = functools.partial(_out_merge_kernel, tm=tm)
    row = lambda b, t: b * tiles + t
    gp0, gd0 = COL_GP - GATE_SHIFT, COL_GD - GATE_SHIFT
    assert gp0 % D_MODEL == 0 and gd0 % D_MODEL == 0
    return pl.pallas_call(
        kern,
        out_shape=jax.ShapeDtypeStruct((batch * seq_len, D_MODEL), F32),
        grid=(batch, tiles),
        in_specs=[
            pl.BlockSpec((tm, POOL_WIDTH), lambda b, t: (row(b, t), COL_U // POOL_WIDTH)),
            pl.BlockSpec((tm, POOL_WIDTH), lambda b, t: (row(b, t), COL_ZP // POOL_WIDTH)),
            pl.BlockSpec((tm, D_MODEL), lambda b, t: (row(b, t), gp0 // D_MODEL)),
            pl.BlockSpec((tm, LANES), lambda b, t: (row(b, t), (gp0 + D_MODEL) // LANES)),
            pl.BlockSpec((tm, D_MODEL), lambda b, t: (row(b, t), gd0 // D_MODEL)),
            pl.BlockSpec((tm, LANES), lambda b, t: (row(b, t), (gd0 + D_MODEL) // LANES)),
            pl.BlockSpec((tm, DN_WIDTH), lambda b, t: (row(b, t), 0)),
            pl.BlockSpec((tm, D_MODEL), lambda b, t: (row(b, t), 0)),
            _resident((N_META, POOL_WIDTH), lambda b, t: (0, 0)),
            _resident((POOL_GROUPS, POOL_GROUP_DIM, POOL_GROUP_DIM), lambda b, t: (0, 0, 0)),
            _resident((1, POOL_WIDTH), lambda b, t: (0, 0)),
            _resident((POOL_WIDTH, D_MODEL), lambda b, t: (0, 0)),
            _resident((DN_WIDTH, D_MODEL), lambda b, t: (0, 0)),
            _resident((D_MODEL, D_MODEL), lambda b, t: (0, 0)),
            _resident((1, D_MODEL), lambda b, t: (0, 0)),
        ],
        out_specs=pl.BlockSpec((tm, D_MODEL), lambda b, t: (row(b, t), 0)),
        scratch_shapes=[pltpu.VMEM((N_META + tm, POOL_WIDTH), F32)],
        compiler_params=pltpu.CompilerParams(
            dimension_semantics=("arbitrary", "arbitrary"), vmem_limit_bytes=VMEM_LIMIT),
        name="out_merge",
    )(proj, proj, proj, proj, proj, proj, y_dn, x2d, meta_u, mix, scale, wpo, wdo, wo, fnw)


def kernel(x, meta_tokens, norm_w, w_in, conv_w, A_log, dt_bias, pool_mix, pool_scale, dn_norm_w,
           w_pool_out, w_dn_out, w_o, final_norm_w):
    batch, seq_len, _ = x.shape
    assert norm_w.shape[0] == 1, "single layer block"
    x2d = x.reshape(batch * seq_len, D_MODEL)

    rows = batch * seq_len
    w_t = jnp.swapaxes(w_in, 1, 2).reshape(w_in.shape[2], D_MODEL)
    gpar = jnp.zeros((2, LANES), F32)
    gpar = gpar.at[0, DN_HEADS:2 * DN_HEADS].set(A_log[0]).at[1, DN_HEADS:2 * DN_HEADS].set(dt_bias[0])
    nw = norm_w[0].reshape(1, D_MODEL)
    wpo = w_pool_out[0].astype(BF16)
    wdo = w_dn_out[0].astype(BF16)
    wo = w_o[0].astype(BF16)

    xn_all, gates = _norm_gates(x2d, meta_tokens.astype(F32), nw, w_t, gpar)
    proj = _in_proj(xn_all, w_t, conv_w[0], seq_len=seq_len)

    front = 2 * CHUNK - N_META
    meta_proj = proj[rows:]
    meta_qkv = jnp.pad(meta_proj[:, COL_Q:COL_ZD], ((front, 0), (0, 0)))
    meta_gates = jnp.pad(gates[rows:], ((front, 0), (0, 0)))
    meta_u = meta_proj[:, COL_U:COL_U + POOL_WIDTH]

    state0 = _deltanet_meta(meta_qkv, meta_gates)
    y_dn = _deltanet(proj, gates, state0, dn_norm_w[0].reshape(1, DN_HEAD_DIM), batch=batch, seq_len=seq_len)

    out = _out_merge(proj, y_dn, x2d, meta_u, pool_mix[0].astype(BF16), pool_scale[0].reshape(1, POOL_WIDTH),
                     wpo, wdo, wo, final_norm_w.reshape(1, D_MODEL), batch=batch, seq_len=seq_len)
    return out.reshape(batch, seq_len, D_MODEL)
```

```python
import functools

import jax
import jax.numpy as jnp
from jax import lax
from jax.experimental import pallas as pl
from jax.experimental.pallas import tpu as pltpu

D_MODEL = 2048
N_META = 16
POOL_GROUPS = 4
POOL_WINDOWS = (2, 4, 8, 16)
POOL_WIDTH = D_MODEL // 2
POOL_GROUP_DIM = POOL_WIDTH // POOL_GROUPS
DN_HEADS = 16
DN_HEAD_DIM = 128
DN_WIDTH = DN_HEADS * DN_HEAD_DIM
CONV_WIDTH = 4
NORM_EPS = 1e-6

COL_U = 0
COL_ZP = COL_U + POOL_WIDTH
COL_Q = COL_ZP + POOL_WIDTH
COL_K = COL_Q + DN_WIDTH
COL_V = COL_K + DN_WIDTH
COL_ZD = COL_V + DN_WIDTH
BA_OFFSET = COL_ZD + DN_WIDTH
COL_GP = BA_OFFSET + 2 * DN_HEADS
COL_GD = COL_GP + D_MODEL
LANES = 128
SUBLANES = 8
GATE_SHIFT = COL_GP % LANES

PROJ_DTYPE = jnp.bfloat16
CHUNK = 64
LEAF = 16
TT_DN = 1024
TM_NORM = 1024
TN_PROJ = 256
ROW_BLK_PROJ = 1024
W_CAST_SLABS = 32
TM_OUT = 256
VMEM_LIMIT = 58 * 1024 * 1024

BF16 = jnp.bfloat16
F32 = jnp.float32


def _mm(a, b):
    return jnp.dot(a.astype(BF16), b.astype(BF16), preferred_element_type=F32)


def _mm_nt(a, b):
    return lax.dot_general(a.astype(BF16), b.astype(BF16), (((1,), (1,)), ((), ())),
                           preferred_element_type=F32)


def _mm_tn(a, b):
    return lax.dot_general(a.astype(BF16), b.astype(BF16), (((0,), (0,)), ((), ())),
                           preferred_element_type=F32)


def _sigmoid(x):
    return 0.5 * jnp.tanh(0.5 * x) + 0.5


def _silu(x):
    h = 0.5 * x
    return h * jnp.tanh(h) + h


def _softplus(x):
    return jnp.maximum(x, 0.0) + jnp.log1p(jnp.exp(-jnp.abs(x)))


def _norm_gates_kernel(x_ref, meta_ref, nw_ref, wba_ref, gpar_ref, xn_ref, gates_ref, *, x_tiles, sub):
    i = pl.program_id(0)
    nw = nw_ref[...]
    wba = wba_ref[...].astype(BF16)
    a_rate = jnp.exp(gpar_ref[0:1, :])
    dt_bias = gpar_ref[1:2, :]

    def rows_block(x, rows):
        ms = jnp.mean(x * x, axis=-1, keepdims=True)
        xn = (x * lax.rsqrt(ms + NORM_EPS) * nw).astype(BF16)
        xn_ref[rows, :] = xn
        ba = _mm_nt(xn, wba)
        lane = lax.broadcasted_iota(jnp.int32, ba.shape, 1)
        g = -a_rate * _softplus(ba + dt_bias)
        gates_ref[rows, :] = jnp.where(lane < DN_HEADS, _sigmoid(ba), jnp.where(lane < 2 * DN_HEADS, g, 0.0))

    @pl.when(i < x_tiles)
    def _():
        def body(r, carry):
            rows = pl.ds(pl.multiple_of(r * sub, sub), sub)
            rows_block(x_ref[rows, :], rows)
            return carry

        lax.fori_loop(0, x_ref.shape[0] // sub, body, 0)

    @pl.when(i == x_tiles)
    def _():
        rows_block(meta_ref[...], pl.ds(0, N_META))


def _norm_gates(x2d, meta, norm_w, w_t, gpar):
    rows = x2d.shape[0]
    tm = TM_NORM
    x_tiles = rows // tm
    kern = functools.partial(_norm_gates_kernel, x_tiles=x_tiles, sub=128)
    return pl.pallas_call(
        kern,
        out_shape=(jax.ShapeDtypeStruct((rows + N_META, D_MODEL), BF16),
                   jax.ShapeDtypeStruct((rows + N_META, LANES), F32)),
        grid=(x_tiles + 1,),
        in_specs=[
            pl.BlockSpec((tm, D_MODEL), lambda i: (jnp.minimum(i, x_tiles - 1), 0)),
            pl.BlockSpec((N_META, D_MODEL), lambda i: (0, 0)),
            pl.BlockSpec((1, D_MODEL), lambda i: (0, 0)),
            pl.BlockSpec((LANES, D_MODEL), lambda i: (BA_OFFSET // LANES, 0)),
            pl.BlockSpec((2, LANES), lambda i: (0, 0)),
        ],
        out_specs=(pl.BlockSpec((tm, D_MODEL), lambda i: (i, 0)),
                   pl.BlockSpec((tm, LANES), lambda i: (i, 0))),
        compiler_params=pltpu.CompilerParams(
            dimension_semantics=("arbitrary",), vmem_limit_bytes=VMEM_LIMIT),
        name="norm_gates",
    )(x2d, meta, norm_w, w_t, gpar)


def _in_proj_kernel(xn_ref, w_ref, cw_ref, wpo_ref, wdo_ref, wo_ref,
                    proj_ref, wpo_bf_ref, wdo_bf_ref, wo_bf_ref, acc_ref, *, rows, row_blk, seq_len):
    j = pl.program_id(0)
    tn = w_ref.shape[0]
    w = w_ref[...].astype(BF16)
    ms = slice(rows, rows + N_META)
    conv_tile = (j >= COL_Q // tn) & (j < COL_ZD // tn)
    silu_tile = ((j >= COL_ZP // tn) & (j < COL_Q // tn)) | ((j >= COL_ZD // tn) & (j < BA_OFFSET // tn))

    for src, dst in ((wpo_ref, wpo_bf_ref), (wdo_ref, wdo_bf_ref), (wo_ref, wo_bf_ref)):
        dst[...] = src[...].astype(BF16)

    def plain(fn):
        for r in range(rows // row_blk):
            rs = slice(r * row_blk, (r + 1) * row_blk)
            proj_ref[rs, :] = fn(_mm_nt(xn_ref[rs, :], w)).astype(proj_ref.dtype)
        proj_ref[ms, :] = fn(_mm_nt(xn_ref[ms, :], w)).astype(proj_ref.dtype)

    @pl.when(conv_tile)
    def _():
        cw = cw_ref[...]
        hist_rows = SUBLANES
        n_blk = rows // row_blk

        def conv_silu(slot, n):
            y = acc_ref[slot, hist_rows:hist_rows + n, :] * cw[CONV_WIDTH - 1:CONV_WIDTH]
            for kk in range(CONV_WIDTH - 1):
                off = hist_rows - (CONV_WIDTH - 1) + kk
                y = y + acc_ref[slot, off:off + n, :] * cw[kk:kk + 1]
            return _silu(y)

        meta_acc = _mm_nt(xn_ref[ms, :], w)
        acc_ref[0, 0:hist_rows, :] = jnp.zeros((hist_rows, tn), F32)
        acc_ref[0, hist_rows:hist_rows + N_META, :] = meta_acc
        proj_ref[ms, :] = conv_silu(0, N_META).astype(proj_ref.dtype)
        meta_tail = meta_acc[N_META - hist_rows:]
        for r in range(n_blk + 1):
            if r < n_blk:
                slot = r % 2
                rs = slice(r * row_blk, (r + 1) * row_blk)
                acc = _mm_nt(xn_ref[rs, :], w)
                batch_start = (r * row_blk) % seq_len == 0
                acc_ref[slot, 0:hist_rows, :] = (
                    meta_tail if batch_start else acc_ref[1 - slot, row_blk:row_blk + hist_rows, :])
                acc_ref[slot, hist_rows:hist_rows + row_blk, :] = acc
            if r > 0:
                ps = slice((r - 1) * row_blk, r * row_blk)
                proj_ref[ps, :] = conv_silu((r - 1) % 2, row_blk).astype(proj_ref.dtype)

    @pl.when(silu_tile)
    def _():
        plain(_silu)

    @pl.when(jnp.logical_not(conv_tile | silu_tile))
    def _():
        plain(lambda a: a)


def _in_proj(xn_all, w_t, conv_w, out_weights, *, seq_len):
    rows = xn_all.shape[0] - N_META
    cols = w_t.shape[0]
    tn = TN_PROJ
    n_conv_tiles = (COL_ZD - COL_Q) // tn
    n_steps = pl.cdiv(cols, tn)
    assert W_CAST_SLABS <= n_steps
    slab = lambda wgt: pl.BlockSpec((wgt.shape[0] // W_CAST_SLABS, D_MODEL),
                                    lambda j: (jnp.minimum(j, W_CAST_SLABS - 1), 0))
    kern = functools.partial(_in_proj_kernel, rows=rows, row_blk=ROW_BLK_PROJ, seq_len=seq_len)
    return pl.pallas_call(
        kern,
        out_shape=(jax.ShapeDtypeStruct((rows + N_META, cols), PROJ_DTYPE),
                   *[jax.ShapeDtypeStruct(wgt.shape, BF16) for wgt in out_weights]),
        grid=(n_steps,),
        in_specs=[
            pl.BlockSpec((rows + N_META, D_MODEL), lambda j: (0, 0), pipeline_mode=pl.Buffered(1)),
            pl.BlockSpec((tn, D_MODEL), lambda j: (j, 0)),
            pl.BlockSpec((CONV_WIDTH, tn), lambda j: (0, jnp.clip(j - COL_Q // tn, 0, n_conv_tiles - 1))),
            *[slab(wgt) for wgt in out_weights],
        ],
        out_specs=(pl.BlockSpec((rows + N_META, tn), lambda j: (0, j)),
                   *[slab(wgt) for wgt in out_weights]),
        scratch_shapes=[pltpu.VMEM((2, SUBLANES + ROW_BLK_PROJ, tn), F32)],
        compiler_params=pltpu.CompilerParams(
            dimension_semantics=("arbitrary",), vmem_limit_bytes=VMEM_LIMIT),
        name="in_proj",
    )(xn_all, w_t, conv_w, *out_weights)


PAIR = 2 * DN_HEAD_DIM


def _split3(x):
    hi = x.astype(BF16)
    r1 = x - hi.astype(F32)
    mid = r1.astype(BF16)
    lo = (r1 - mid.astype(F32)).astype(BF16)
    return hi, mid, lo


def _block_diag_rows(y, half):
    lane = lax.broadcasted_iota(jnp.int32, y.shape, 1)
    zero = jnp.zeros_like(y)
    return jnp.concatenate([jnp.where(lane < half, y, zero), jnp.where(lane >= half, y, zero)], axis=0)


def _inverse_masks(c):
    row = lax.broadcasted_iota(jnp.int32, (c, 2 * c), 0)
    col = lax.broadcasted_iota(jnp.int32, (c, 2 * c), 1) & (c - 1)
    blk = lambda x, m: x >> (m.bit_length() - 1)
    eye = (row == col).astype(F32)
    same_leaf = blk(row, LEAF) == blk(col, LEAF)
    offs = []
    m = 2 * LEAF
    while m <= c:
        offs.append((blk(row, m) == blk(col, m)) & (blk(row, m // 2) != blk(col, m // 2)))
        m *= 2
    return eye, same_leaf, offs


def _inv_unit_lower_packed(lps, c, masks, fill, n_fill=1):
    eye, same_leaf, offs = masks
    bd = lambda y: _block_diag_rows(y.astype(BF16), c)
    ps = [jnp.where(same_leaf, -l, 0.0) for l in lps]
    ts = [eye + p for p in ps]
    dep = ps[0]
    ps = [_mm(p, bd(p)) for p in ps]
    fill(dep, ps, n_fill + 1)
    span = 4
    while span <= LEAF:
        dep = ps[0]
        last = span == LEAF
        prod = [_mm(t if last else jnp.concatenate([t, p], axis=0), bd(p)) for t, p in zip(ts, ps)]
        ts = [t + x[:c] for t, x in zip(ts, prod)]
        if not last:
            ps = [x[c:] for x in prod]
        fill(dep, ts, n_fill)
        span *= 2
    for off in offs:
        dep = ts[0]
        inner = [_mm(jnp.where(off, l, 0.0), bd(t)) for l, t in zip(lps, ts)]
        fill(dep, inner, n_fill)
        dep = inner[0]
        ts = [t - _mm(t, bd(x)) for t, x in zip(ts, inner)]
        fill(dep, ts, n_fill)
    return ts


def _delta_rule_core(read_rows, read_gates, n_pos, state_ref, qn_s, kn_s, gsm_s, gl_s, gp_s, gr_s,
                     finish=None, o_s=None):
    c = CHUNK
    hd = DN_HEAD_DIM
    n_pairs = DN_HEADS // 2
    pairs = range(n_pairs)

    rowp = lax.broadcasted_iota(jnp.int32, (c, 2 * c), 0)
    colp = lax.broadcasted_iota(jnp.int32, (c, 2 * c), 1) & (c - 1)
    causal = rowp >= colp
    strict = rowp > colp
    lane_p = lax.broadcasted_iota(jnp.int32, (c, 2 * c), 1)
    inv_masks = _inverse_masks(c)
    tril = (lax.broadcasted_iota(jnp.int32, (c, c), 0) >= lax.broadcasted_iota(jnp.int32, (c, c), 1))
    tril3 = jnp.concatenate([tril.astype(BF16)] * 3, axis=1)
    zero_blk = jnp.zeros((hd, hd), BF16)
    head_cols = lambda h: slice(h * hd, (h + 1) * hd)
    pair_cols_of = lambda p: slice(p * PAIR, (p + 1) * PAIR)
    both = (0, 1)
    ha = [2 * p for p in pairs]
    hb = [2 * p + 1 for p in pairs]

    def l2n(xh, scale):
        return xh * (lax.rsqrt(jnp.sum(xh * xh, axis=-1, keepdims=True) + NORM_EPS) * scale)

    def prepare_pair(j, p, zero=None):
        cols = pair_cols_of(p)
        slot = j & 3
        for idx, (dst, scale) in enumerate(((qn_s, hd ** -0.5), (kn_s, 1.0))):
            x = read_rows(idx, j, cols).astype(F32)
            if zero is not None:
                x = x + jnp.concatenate([zero[0:1], zero[0:1]], axis=1)
            dst[slot, :, cols] = jnp.concatenate([l2n(x[:, :hd], scale), l2n(x[:, hd:], scale)], axis=1)
        r0 = pl.multiple_of((j >> 24) * SUBLANES, SUBLANES)
        back = qn_s[slot, pl.ds(r0, SUBLANES), cols] + kn_s[slot, pl.ds(r0, SUBLANES), cols]
        return back[:, :hd] + back[:, hd:]

    def out_rows(j):
        return pl.multiple_of(jnp.maximum(j, 0) * c, c)

    def exact_rows_dot(lhs3, x):
        hi, mid, lo = _split3(x)
        return jnp.dot(lhs3, jnp.concatenate([hi, mid, lo], axis=0), preferred_element_type=F32)

    def pair_cols(mat, la, lb):
        return jnp.where(lane_p < c, mat[:, la:la + 1], mat[:, lb:lb + 1])

    def pair_rows(mat_t, ra, rb):
        return jnp.concatenate([mat_t[ra:ra + 1, :], mat_t[rb:rb + 1, :]], axis=1)

    def zero_of(v):
        return jnp.where(v != v, v, 0.0)

    def prepare_gates(j):
        slot = j & 3
        gt = read_gates(j)
        gcum = exact_rows_dot(tril3, gt)
        g_last = gcum[c - 1:c, :]
        gsm_s[slot, 0] = jnp.exp(gcum)
        gsm_s[slot, 1] = jnp.exp(g_last - gcum)
        gsm_s[slot, 2] = jnp.broadcast_to(jnp.exp(g_last), (c, LANES))
        gt_t = gt.T
        gcum_t = gcum.T
        for p in pairs:
            g_col = pair_cols(gcum, DN_HEADS + ha[p], DN_HEADS + hb[p])
            g_row = pair_rows(gcum_t, DN_HEADS + ha[p], DN_HEADS + hb[p])
            b_col = pair_cols(gt, ha[p], hb[p])
            b_row = pair_rows(gt_t, ha[p], hb[p])
            dec = jnp.where(causal, jnp.exp(jnp.where(causal, g_col - g_row, 0.0)), 0.0)
            gp_s[slot, p] = dec
            gl_s[slot, p] = jnp.where(strict, b_col * dec, 0.0)
            gr_s[slot, p] = jnp.concatenate(
                [b_row, b_row * jnp.exp(g_row), jnp.zeros((SUBLANES - 2, 2 * c), F32)], axis=0)

    def pair_body(i, carry):
        pos = [2 * i, 2 * i + 1]
        slot = [pz & 3 for pz in pos]
        fillers = []
        for p in pairs:
            for x in both:
                if finish is not None:
                    fillers.append(functools.partial(finish, x, out_rows(pos[x] - 2), p))
                fillers.append(functools.partial(prepare_pair, pos[x] + 2, p))
        n_slices = len(fillers)
        fillers.reverse()

        def fill(dep, into, n=1):
            zero = None if dep is None else zero_of(dep[0:SUBLANES, 0:LANES].astype(F32))
            for _ in range(n):
                if fillers:
                    done = zero_of(fillers.pop()(zero=zero))[0:1]
                    if into[-1].shape[1] == PAIR:
                        done = jnp.concatenate([done, done], axis=1)
                    into[-1] = into[-1] + done

        prob = [(x, p) for x in both for p in pairs]
        eg = [gsm_s[slot[x], 0] for x in both]
        e_rest = [gsm_s[slot[x], 1] for x in both]
        e_last = [gsm_s[slot[x], 2, 0:1, :] for x in both]
        gl = [gl_s[slot[x], p] for x, p in prob]
        gp = [gp_s[slot[x], p] for x, p in prob]
        gr = [gr_s[slot[x], p] for x, p in prob]
        kpair = [kn_s[slot[x], :, pair_cols_of(p)].astype(BF16) for x, p in prob]
        kbd = [_block_diag_rows(kp, hd) for kp in kpair]
        qpair = []
        for n, (x, p) in enumerate(prob):
            staged = [qn_s[slot[x], :, pair_cols_of(p)]]
            if n % 2:
                fill(None, staged)
            qpair.append(staged[0].astype(BF16))
        aq = [_mm_nt(jnp.concatenate([kpair[n], qpair[n]], axis=0), kbd[n]) for n in range(len(prob))]
        for x in both:
            prepare_gates(pos[x] + 2)
        lps = [aq[n][:c] * gl[n] for n in range(len(prob))]
        pps = [aq[n][c:] * gp[n] for n in range(len(prob))]
        per_group = max(1, (n_slices - len(prob) // 2) // 12)
        tps = _inv_unit_lower_packed(lps, c, inv_masks, fill, per_group)
        vbd = [_block_diag_rows(read_rows(2, pos[x], pair_cols_of(p)), hd) for x, p in prob]
        u = [_mm(tps[n] * gr[n][0:1], vbd[n]) for n in range(len(prob))]
        w = [_mm(tps[n] * gr[n][1:2], kbd[n]) for n in range(len(prob))]
        fill(tps[0], w, per_group)
        state = [state_ref[h] for h in range(DN_HEADS)]
        outs = []
        for x in both:
            idx = [x * n_pairs + p for p in pairs]
            qd = [jnp.concatenate(
                [qn_s[slot[x], :, head_cols(h)] * eg[x][:, DN_HEADS + h:DN_HEADS + h + 1]
                 for h in (ha[p], hb[p])], axis=1) for p in pairs]
            sbd = [jnp.concatenate([jnp.concatenate([state[ha[p]].astype(BF16), zero_blk], axis=1),
                                    jnp.concatenate([zero_blk, state[hb[p]].astype(BF16)], axis=1)], axis=0)
                   for p in pairs]
            ws = [_mm(jnp.concatenate([w[idx[p]], qd[p]], axis=0), sbd[p]) for p in pairs]
            fill(w[idx[0]], ws, per_group)
            v_new = [u[idx[p]] - ws[p][:c] for p in pairs]
            o = [ws[p][c:] + _mm(pps[idx[p]], _block_diag_rows(v_new[p].astype(BF16), hd)) for p in pairs]
            new_state = []
            for h in range(DN_HEADS):
                p, half = divmod(h, 2)
                lane_h = DN_HEADS + h
                kd = kn_s[slot[x], :, head_cols(h)] * e_rest[x][:, lane_h:lane_h + 1]
                upd = _mm_tn(kd, v_new[p][:, half * hd:(half + 1) * hd])
                new_state.append(state[h] * e_last[x][:, lane_h:lane_h + 1] + upd)
            state = new_state
            if x == 1:
                fill(ws[0], o, len(fillers))
            outs.append(o)
        for h in range(DN_HEADS):
            state_ref[h] = state[h]
        if o_s is not None:
            for x in both:
                for p in pairs:
                    o_s[x, :, pair_cols_of(p)] = outs[x][p]
        return carry

    for x in both:
        for p in pairs:
            prepare_pair(jnp.int32(x), p)
        prepare_gates(jnp.int32(x))
    n_iter = n_pos // 2
    if n_iter == 1:
        pair_body(jnp.int32(0), 0)
    else:
        lax.fori_loop(0, n_iter, pair_body, 0)
    if finish is not None:
        for x in both:
            last_rows = out_rows(jnp.int32(n_pos - 2 + x))
            for p in pairs:
                finish(x, last_rows, p)


def _deltanet_kernel(q_ref, k_ref, v_ref, z_ref, gates_ref, s0_ref, dnw_ref,
                     y_ref, state_ref, qn_s, kn_s, o_s, gsm_s, gl_s, gp_s, gr_s, *, tt):
    c = CHUNK
    hd = DN_HEAD_DIM
    n_chunks = tt // c
    dnw = dnw_ref[...]

    @pl.when(pl.program_id(1) == 0)
    def _():
        state_ref[...] = s0_ref[...]
        o_s[...] = jnp.zeros(o_s.shape, F32)

    srcs = (q_ref, k_ref, v_ref)

    def rows_at(pos):
        return pl.ds(pl.multiple_of(jnp.minimum(pos, n_chunks - 1) * c, c), c)

    def read_rows(idx, pos, cols):
        return srcs[idx][rows_at(pos), cols]

    def read_gates(pos):
        return gates_ref[rows_at(pos), :]

    def finish(src, row0, p, zero=None):
        rows = pl.ds(row0, c)
        for half in range(2):
            cs = slice(p * PAIR + half * hd, p * PAIR + (half + 1) * hd)
            oh = o_s[src, :, cs]
            if zero is not None:
                oh = oh + zero[0:1]
            oh = oh * lax.rsqrt(jnp.mean(oh * oh, axis=-1, keepdims=True) + NORM_EPS) * dnw
            y_ref[rows, cs] = (oh * z_ref[rows, cs].astype(F32)).astype(y_ref.dtype)
        r0 = pl.multiple_of((row0 >> 24) * (2 * SUBLANES), 2 * SUBLANES)
        back = y_ref[pl.ds(r0, 2 * SUBLANES), p * PAIR:(p + 1) * PAIR].astype(F32)[0:SUBLANES]
        return back[:, :hd] + back[:, hd:]

    _delta_rule_core(read_rows, read_gates, n_chunks, state_ref, qn_s, kn_s, gsm_s, gl_s, gp_s, gr_s,
                     finish=finish, o_s=o_s)


def _deltanet_meta_kernel(mqkv_ref, mgates_ref, s_out_ref, state_ref, qn_s, kn_s, gsm_s, gl_s, gp_s, gr_s):
    c = CHUNK
    state_ref[...] = jnp.zeros(state_ref.shape, F32)

    def rows_at(pos):
        return pl.ds(pl.multiple_of(jnp.minimum(pos, 1) * c, c), c)

    def read_rows(idx, pos, cols):
        return mqkv_ref[rows_at(pos), slice(idx * DN_WIDTH + cols.start, idx * DN_WIDTH + cols.stop)]

    def read_gates(pos):
        return mgates_ref[rows_at(pos), :]

    _delta_rule_core(read_rows, read_gates, 2, state_ref, qn_s, kn_s, gsm_s, gl_s, gp_s, gr_s)
    s_out_ref[...] = state_ref[...]


def _lookahead_scratch():
    c = CHUNK
    return [
        pltpu.VMEM((4, c, DN_WIDTH), F32),
        pltpu.VMEM((4, c, DN_WIDTH), F32),
        pltpu.VMEM((4, 3, c, LANES), F32),
        pltpu.VMEM((4, DN_HEADS // 2, c, 2 * c), F32),
        pltpu.VMEM((4, DN_HEADS // 2, c, 2 * c), F32),
        pltpu.VMEM((4, DN_HEADS // 2, SUBLANES, 2 * c), F32),
    ]


def _deltanet_meta(meta_qkv, meta_gates):
    state_shape = (DN_HEADS, DN_HEAD_DIM, DN_HEAD_DIM)
    qn, kn, gsm, gl, gp, gr = _lookahead_scratch()
    return pl.pallas_call(
        _deltanet_meta_kernel,
        out_shape=jax.ShapeDtypeStruct(state_shape, F32),
        scratch_shapes=[pltpu.VMEM(state_shape, F32), qn, kn, gsm, gl, gp, gr],
        compiler_params=pltpu.CompilerParams(vmem_limit_bytes=VMEM_LIMIT),
        name="deltanet_meta",
    )(meta_qkv, meta_gates)


def _deltanet(proj, gates, state0, dn_norm_w, *, batch, seq_len):
    tt = TT_DN
    nt = seq_len // tt
    c = CHUNK
    kern = functools.partial(_deltanet_kernel, tt=tt)
    row = lambda b, t: b * nt + t
    const = lambda shape: pl.BlockSpec(shape, lambda b, t: (0,) * len(shape), pipeline_mode=pl.Buffered(1))
    state_shape = (DN_HEADS, DN_HEAD_DIM, DN_HEAD_DIM)
    qn, kn, gsm, gl, gp, gr = _lookahead_scratch()
    return pl.pallas_call(
        kern,
        out_shape=jax.ShapeDtypeStruct((batch * seq_len, DN_WIDTH), BF16),
        grid=(batch, nt),
        in_specs=[
            pl.BlockSpec((tt, DN_WIDTH), lambda b, t: (row(b, t), COL_Q // DN_WIDTH)),
            pl.BlockSpec((tt, DN_WIDTH), lambda b, t: (row(b, t), COL_K // DN_WIDTH)),
            pl.BlockSpec((tt, DN_WIDTH), lambda b, t: (row(b, t), COL_V // DN_WIDTH)),
            pl.BlockSpec((tt, DN_WIDTH), lambda b, t: (row(b, t), COL_ZD // DN_WIDTH)),
            pl.BlockSpec((tt, LANES), lambda b, t: (row(b, t), 0)),
            const(state_shape),
            const((1, DN_HEAD_DIM)),
        ],
        out_specs=pl.BlockSpec((tt, DN_WIDTH), lambda b, t: (row(b, t), 0)),
        scratch_shapes=[
            pltpu.VMEM(state_shape, F32),
            qn, kn,
            pltpu.VMEM((2, c, DN_WIDTH), F32),
            gsm, gl, gp, gr,
        ],
        compiler_params=pltpu.CompilerParams(
            dimension_semantics=("arbitrary", "arbitrary"), vmem_limit_bytes=VMEM_LIMIT),
        name="deltanet",
    )(proj, proj, proj, proj, gates, state0, dn_norm_w)


def _out_merge_kernel(u_ref, zp_ref, gpa_ref, gpb_ref, gda_ref, gdb_ref, ydn_ref, x_ref, mu_ref, mix_ref,
                      scale_ref, wpo_ref, wdo_ref, wo_ref, fnw_ref, out_ref, ubuf_ref, *, tm):
    t = pl.program_id(1)
    hist = N_META

    @pl.when(t == 0)
    def _():
        ubuf_ref[0:hist, :] = mu_ref[...].astype(F32)

    @pl.when(t > 0)
    def _():
        ubuf_ref[0:hist, :] = ubuf_ref[tm:tm + hist, :]

    ubuf_ref[hist:hist + tm, :] = u_ref[...].astype(F32)

    pooled = []
    for gi, w in enumerate(POOL_WINDOWS):
        cs = slice(gi * POOL_GROUP_DIM, (gi + 1) * POOL_GROUP_DIM)
        ext = ubuf_ref[:, cs]
        acc = ext
        span = 1
        while span < w:
            acc = acc + pltpu.roll(acc, span, axis=0)
            span *= 2
        d = acc[hist:] * (1.0 / w) - ext[hist:]
        mixed = _mm(d, mix_ref[gi])
        zp = zp_ref[:, cs].astype(F32)
        pooled.append((mixed * scale_ref[:, cs] * zp).astype(BF16))
    y_pool = jnp.concatenate(pooled, axis=1)

    frame = slice(GATE_SHIFT, GATE_SHIFT + D_MODEL)
    gp = jnp.concatenate([gpa_ref[...], gpb_ref[...]], axis=1).astype(F32)[:, frame]
    gd = jnp.concatenate([gda_ref[...], gdb_ref[...]], axis=1).astype(F32)[:, frame]
    t_pool = jnp.dot(y_pool, wpo_ref[...], preferred_element_type=F32)
    t_dn = jnp.dot(ydn_ref[...], wdo_ref[...], preferred_element_type=F32)
    merged = _sigmoid(gp) * t_pool + _sigmoid(gd) * t_dn
    h = x_ref[...] + jnp.dot(merged.astype(BF16), wo_ref[...], preferred_element_type=F32)
    ms = jnp.mean(h * h, axis=-1, keepdims=True)
    out_ref[...] = h * lax.rsqrt(ms + NORM_EPS) * fnw_ref[...]


def _resident(shape, index_map):
    return pl.BlockSpec(shape, index_map, pipeline_mode=pl.Buffered(1))


def _out_merge(proj, y_dn, x2d, meta_u, mix, scale, wpo, wdo, wo, fnw, *, batch, seq_len):
    tm = TM_OUT
    tiles = seq_len // tm
    kern = functools.partial(_out_merge_kernel, tm=tm)
    row = lambda b, t: b * tiles + t
    gp0, gd0 = COL_GP - GATE_SHIFT, COL_GD - GATE_SHIFT
    assert gp0 % D_MODEL == 0 and gd0 % D_MODEL == 0
    return pl.pallas_call(
        kern,
        out_shape=jax.ShapeDtypeStruct((batch * seq_len, D_MODEL), F32),
        grid=(batch, tiles),
        in_specs=[
            pl.BlockSpec((tm, POOL_WIDTH), lambda b, t: (row(b, t), COL_U // POOL_WIDTH)),
            pl.BlockSpec((tm, POOL_WIDTH), lambda b, t: (row(b, t), COL_ZP // POOL_WIDTH)),
            pl.BlockSpec((tm, D_MODEL), lambda b, t: (row(b, t), gp0 // D_MODEL)),
            pl.BlockSpec((tm, LANES), lambda b, t: (row(b, t), (gp0 + D_MODEL) // LANES)),
            pl.BlockSpec((tm, D_MODEL), lambda b, t: (row(b, t), gd0 // D_MODEL)),
            pl.BlockSpec((tm, LANES), lambda b, t: (row(b, t), (gd0 + D_MODEL) // LANES)),
            pl.BlockSpec((tm, DN_WIDTH), lambda b, t: (row(b, t), 0)),
            pl.BlockSpec((tm, D_MODEL), lambda b, t: (row(b, t), 0)),
            _resident((N_META, POOL_WIDTH), lambda b, t: (0, 0)),
            _resident((POOL_GROUPS, POOL_GROUP_DIM, POOL_GROUP_DIM), lambda b, t: (0, 0, 0)),
            _resident((1, POOL_WIDTH), lambda b, t: (0, 0)),
            _resident((POOL_WIDTH, D_MODEL), lambda b, t: (0, 0)),
            _resident((DN_WIDTH, D_MODEL), lambda b, t: (0, 0)),
            _resident((D_MODEL, D_MODEL), lambda b, t: (0, 0)),
            _resident((1, D_MODEL), lambda b, t: (0, 0)),
        ],
        out_specs=pl.BlockSpec((tm, D_MODEL), lambda b, t: (row(b, t), 0)),
        scratch_shapes=[pltpu.VMEM((N_META + tm, POOL_WIDTH), F32)],
        compiler_params=pltpu.CompilerParams(
            dimension_semantics=("arbitrary", "arbitrary"), vmem_limit_bytes=VMEM_LIMIT),
        name="out_merge",
    )(proj, proj, proj, proj, proj, proj, y_dn, x2d, meta_u, mix, scale, wpo, wdo, wo, fnw)


def kernel(x, meta_tokens, norm_w, w_in, conv_w, A_log, dt_bias, pool_mix, pool_scale, dn_norm_w,
           w_pool_out, w_dn_out, w_o, final_norm_w):
    batch, seq_len, _ = x.shape
    assert norm_w.shape[0] == 1, "single layer block"
    x2d = x.reshape(batch * seq_len, D_MODEL)

    rows = batch * seq_len
    w_t = jnp.swapaxes(w_in, 1, 2).reshape(w_in.shape[2], D_MODEL)
    gpar = jnp.zeros((2, LANES), F32)
    gpar = gpar.at[0, DN_HEADS:2 * DN_HEADS].set(A_log[0]).at[1, DN_HEADS:2 * DN_HEADS].set(dt_bias[0])
    nw = norm_w[0].reshape(1, D_MODEL)

    xn_all, gates = _norm_gates(x2d, meta_tokens.astype(F32), nw, w_t, gpar)
    proj, wpo, wdo, wo = _in_proj(xn_all, w_t, conv_w[0], (w_pool_out[0], w_dn_out[0], w_o[0]), seq_len=seq_len)

    front = 2 * CHUNK - N_META
    meta_proj = proj[rows:]
    meta_qkv = jnp.pad(meta_proj[:, COL_Q:COL_ZD], ((front, 0), (0, 0)))
    meta_gates = jnp.pad(gates[rows:], ((front, 0), (0, 0)))
    meta_u = meta_proj[:, COL_U:COL_U + POOL_WIDTH]

    state0 = _deltanet_meta(meta_qkv, meta_gates)
    y_dn = _deltanet(proj, gates, state0, dn_norm_w[0].reshape(1, DN_HEAD_DIM), batch=batch, seq_len=seq_len)

    out = _out_merge(proj, y_dn, x2d, meta_u, pool_mix[0].astype(BF16), pool_scale[0].reshape(1, POOL_WIDTH),
                     wpo, wdo, wo, final_norm_w.reshape(1, D_MODEL), batch=batch, seq_len=seq_len)
    return out.reshape(batch, seq_len, D_MODEL)
```

```python
import functools

import jax
import jax.numpy as jnp
from jax import lax
from jax.experimental import pallas as pl
from jax.experimental.pallas import tpu as pltpu

D_MODEL = 2048
N_META = 16
POOL_GROUPS = 4
POOL_WINDOWS = (2, 4, 8, 16)
POOL_WIDTH = D_MODEL // 2
POOL_GROUP_DIM = POOL_WIDTH // POOL_GROUPS
DN_HEADS = 16
DN_HEAD_DIM = 128
DN_WIDTH = DN_HEADS * DN_HEAD_DIM
CONV_WIDTH = 4
NORM_EPS = 1e-6

COL_U = 0
COL_ZP = COL_U + POOL_WIDTH
COL_Q = COL_ZP + POOL_WIDTH
COL_K = COL_Q + DN_WIDTH
COL_V = COL_K + DN_WIDTH
COL_ZD = COL_V + DN_WIDTH
BA_OFFSET = COL_ZD + DN_WIDTH
COL_GP = BA_OFFSET + 2 * DN_HEADS
COL_GD = COL_GP + D_MODEL
LANES = 128
SUBLANES = 8
GATE_SHIFT = COL_GP % LANES

PROJ_DTYPE = jnp.bfloat16
CHUNK = 64
LEAF = 16
TT_DN = 1024
TM_NORM = 1024
TN_PROJ = 256
ROW_BLK_PROJ = 1024
W_CAST_SLABS = 32
TM_OUT = 256
VMEM_LIMIT = 58 * 1024 * 1024

BF16 = jnp.bfloat16
F32 = jnp.float32


def _mm(a, b):
    return jnp.dot(a.astype(BF16), b.astype(BF16), preferred_element_type=F32)


def _mm_nt(a, b):
    return lax.dot_general(a.astype(BF16), b.astype(BF16), (((1,), (1,)), ((), ())),
                           preferred_element_type=F32)


def _mm_tn(a, b):
    return lax.dot_general(a.astype(BF16), b.astype(BF16), (((0,), (0,)), ((), ())),
                           preferred_element_type=F32)


def _sigmoid(x):
    return 0.5 * jnp.tanh(0.5 * x) + 0.5


def _silu(x):
    h = 0.5 * x
    return h * jnp.tanh(h) + h


def _softplus(x):
    return jnp.maximum(x, 0.0) + jnp.log1p(jnp.exp(-jnp.abs(x)))


def _norm_gates_kernel(x_ref, meta_ref, nw_ref, wba_ref, wtail_ref, gpar_ref, xn_ref, gates_ref, tail_ref,
                       *, x_tiles, sub):
    i = pl.program_id(0)
    nw = nw_ref[...]
    wboth = jnp.concatenate([wba_ref[...], wtail_ref[...]], axis=0).astype(BF16)
    a_rate = jnp.exp(gpar_ref[0:1, :])
    dt_bias = gpar_ref[1:2, :]

    def rows_block(x, rows):
        ms = jnp.mean(x * x, axis=-1, keepdims=True)
        xn = (x * lax.rsqrt(ms + NORM_EPS) * nw).astype(BF16)
        xn_ref[rows, :] = xn
        both = _mm_nt(xn, wboth)
        ba = both[:, :LANES]
        tail_ref[rows, :] = both[:, LANES:].astype(tail_ref.dtype)
        lane = lax.broadcasted_iota(jnp.int32, ba.shape, 1)
        g = -a_rate * _softplus(ba + dt_bias)
        gates_ref[rows, :] = jnp.where(lane < DN_HEADS, _sigmoid(ba), jnp.where(lane < 2 * DN_HEADS, g, 0.0))

    @pl.when(i < x_tiles)
    def _():
        def body(r, carry):
            rows = pl.ds(pl.multiple_of(r * sub, sub), sub)
            rows_block(x_ref[rows, :], rows)
            return carry

        lax.fori_loop(0, x_ref.shape[0] // sub, body, 0)

    @pl.when(i == x_tiles)
    def _():
        rows_block(meta_ref[...], pl.ds(0, N_META))


def _norm_gates(x2d, meta, norm_w, w_t, gpar):
    rows = x2d.shape[0]
    tm = TM_NORM
    tail_start = w_t.shape[0] // TN_PROJ * TN_PROJ
    assert tail_start % LANES == 0 and w_t.shape[0] - tail_start <= LANES
    x_tiles = rows // tm
    kern = functools.partial(_norm_gates_kernel, x_tiles=x_tiles, sub=128)
    return pl.pallas_call(
        kern,
        out_shape=(jax.ShapeDtypeStruct((rows + N_META, D_MODEL), BF16),
                   jax.ShapeDtypeStruct((rows + N_META, LANES), F32),
                   jax.ShapeDtypeStruct((rows + N_META, LANES), PROJ_DTYPE)),
        grid=(x_tiles + 1,),
        in_specs=[
            pl.BlockSpec((tm, D_MODEL), lambda i: (jnp.minimum(i, x_tiles - 1), 0)),
            pl.BlockSpec((N_META, D_MODEL), lambda i: (0, 0)),
            pl.BlockSpec((1, D_MODEL), lambda i: (0, 0)),
            pl.BlockSpec((LANES, D_MODEL), lambda i: (BA_OFFSET // LANES, 0)),
            pl.BlockSpec((LANES, D_MODEL), lambda i: (tail_start // LANES, 0)),
            pl.BlockSpec((2, LANES), lambda i: (0, 0)),
        ],
        out_specs=(pl.BlockSpec((tm, D_MODEL), lambda i: (i, 0)),
                   pl.BlockSpec((tm, LANES), lambda i: (i, 0)),
                   pl.BlockSpec((tm, LANES), lambda i: (i, 0))),
        compiler_params=pltpu.CompilerParams(
            dimension_semantics=("arbitrary",), vmem_limit_bytes=VMEM_LIMIT),
        name="norm_gates",
    )(x2d, meta, norm_w, w_t, w_t, gpar)


def _in_proj_kernel(xn_ref, w_ref, cw_ref, wpo_ref, wdo_ref, wo_ref,
                    proj_ref, wpo_bf_ref, wdo_bf_ref, wo_bf_ref, acc_ref, *, rows, row_blk, seq_len):
    j = pl.program_id(0)
    tn = w_ref.shape[0]
    w = w_ref[...].astype(BF16)
    ms = slice(rows, rows + N_META)
    conv_tile = (j >= COL_Q // tn) & (j < COL_ZD // tn)
    silu_tile = ((j >= COL_ZP // tn) & (j < COL_Q // tn)) | ((j >= COL_ZD // tn) & (j < BA_OFFSET // tn))

    for src, dst in ((wpo_ref, wpo_bf_ref), (wdo_ref, wdo_bf_ref), (wo_ref, wo_bf_ref)):
        dst[...] = src[...].astype(BF16)

    def plain(fn):
        for r in range(rows // row_blk):
            rs = slice(r * row_blk, (r + 1) * row_blk)
            proj_ref[rs, :] = fn(_mm_nt(xn_ref[rs, :], w)).astype(proj_ref.dtype)
        proj_ref[ms, :] = fn(_mm_nt(xn_ref[ms, :], w)).astype(proj_ref.dtype)

    @pl.when(conv_tile)
    def _():
        cw = cw_ref[...]
        hist_rows = SUBLANES
        n_blk = rows // row_blk

        def conv_silu(slot, n):
            y = acc_ref[slot, hist_rows:hist_rows + n, :] * cw[CONV_WIDTH - 1:CONV_WIDTH]
            for kk in range(CONV_WIDTH - 1):
                off = hist_rows - (CONV_WIDTH - 1) + kk
                y = y + acc_ref[slot, off:off + n, :] * cw[kk:kk + 1]
            return _silu(y)

        meta_acc = _mm_nt(xn_ref[ms, :], w)
        acc_ref[0, 0:hist_rows, :] = jnp.zeros((hist_rows, tn), F32)
        acc_ref[0, hist_rows:hist_rows + N_META, :] = meta_acc
        proj_ref[ms, :] = conv_silu(0, N_META).astype(proj_ref.dtype)
        meta_tail = meta_acc[N_META - hist_rows:]
        for r in range(n_blk + 1):
            if r < n_blk:
                slot = r % 2
                rs = slice(r * row_blk, (r + 1) * row_blk)
                acc = _mm_nt(xn_ref[rs, :], w)
                batch_start = (r * row_blk) % seq_len == 0
                acc_ref[slot, 0:hist_rows, :] = (
                    meta_tail if batch_start else acc_ref[1 - slot, row_blk:row_blk + hist_rows, :])
                acc_ref[slot, hist_rows:hist_rows + row_blk, :] = acc
            if r > 0:
                ps = slice((r - 1) * row_blk, r * row_blk)
                proj_ref[ps, :] = conv_silu((r - 1) % 2, row_blk).astype(proj_ref.dtype)

    @pl.when(silu_tile)
    def _():
        plain(_silu)

    @pl.when(jnp.logical_not(conv_tile | silu_tile))
    def _():
        plain(lambda a: a)


def _in_proj(xn_all, w_t, conv_w, out_weights, *, seq_len):
    rows = xn_all.shape[0] - N_META
    cols = w_t.shape[0]
    tn = TN_PROJ
    n_conv_tiles = (COL_ZD - COL_Q) // tn
    n_steps = cols // tn
    assert W_CAST_SLABS <= n_steps
    slab = lambda wgt: pl.BlockSpec((wgt.shape[0] // W_CAST_SLABS, D_MODEL),
                                    lambda j: (jnp.minimum(j, W_CAST_SLABS - 1), 0))
    kern = functools.partial(_in_proj_kernel, rows=rows, row_blk=ROW_BLK_PROJ, seq_len=seq_len)
    return pl.pallas_call(
        kern,
        out_shape=(jax.ShapeDtypeStruct((rows + N_META, n_steps * tn), PROJ_DTYPE),
                   *[jax.ShapeDtypeStruct(wgt.shape, BF16) for wgt in out_weights]),
        grid=(n_steps,),
        in_specs=[
            pl.BlockSpec((rows + N_META, D_MODEL), lambda j: (0, 0), pipeline_mode=pl.Buffered(1)),
            pl.BlockSpec((tn, D_MODEL), lambda j: (j, 0)),
            pl.BlockSpec((CONV_WIDTH, tn), lambda j: (0, jnp.clip(j - COL_Q // tn, 0, n_conv_tiles - 1))),
            *[slab(wgt) for wgt in out_weights],
        ],
        out_specs=(pl.BlockSpec((rows + N_META, tn), lambda j: (0, j)),
                   *[slab(wgt) for wgt in out_weights]),
        scratch_shapes=[pltpu.VMEM((2, SUBLANES + ROW_BLK_PROJ, tn), F32)],
        compiler_params=pltpu.CompilerParams(
            dimension_semantics=("arbitrary",), vmem_limit_bytes=VMEM_LIMIT),
        name="in_proj",
    )(xn_all, w_t, conv_w, *out_weights)


PAIR = 2 * DN_HEAD_DIM


def _split3(x):
    hi = x.astype(BF16)
    r1 = x - hi.astype(F32)
    mid = r1.astype(BF16)
    lo = (r1 - mid.astype(F32)).astype(BF16)
    return hi, mid, lo


def _block_diag_rows(y, half):
    lane = lax.broadcasted_iota(jnp.int32, y.shape, 1)
    zero = jnp.zeros_like(y)
    return jnp.concatenate([jnp.where(lane < half, y, zero), jnp.where(lane >= half, y, zero)], axis=0)


def _inverse_masks(c):
    row = lax.broadcasted_iota(jnp.int32, (c, 2 * c), 0)
    col = lax.broadcasted_iota(jnp.int32, (c, 2 * c), 1) & (c - 1)
    blk = lambda x, m: x >> (m.bit_length() - 1)
    eye = (row == col).astype(F32)
    same_leaf = blk(row, LEAF) == blk(col, LEAF)
    offs = []
    m = 2 * LEAF
    while m <= c:
        offs.append((blk(row, m) == blk(col, m)) & (blk(row, m // 2) != blk(col, m // 2)))
        m *= 2
    return eye, same_leaf, offs


def _inv_unit_lower_packed(lps, c, masks, fill, n_fill=1):
    eye, same_leaf, offs = masks
    bd = lambda y: _block_diag_rows(y.astype(BF16), c)
    ps = [jnp.where(same_leaf, -l, 0.0) for l in lps]
    ts = [eye + p for p in ps]
    dep = ps[0]
    ps = [_mm(p, bd(p)) for p in ps]
    fill(dep, ps, n_fill + 1)
    span = 4
    while span <= LEAF:
        dep = ps[0]
        last = span == LEAF
        prod = [_mm(t if last else jnp.concatenate([t, p], axis=0), bd(p)) for t, p in zip(ts, ps)]
        ts = [t + x[:c] for t, x in zip(ts, prod)]
        if not last:
            ps = [x[c:] for x in prod]
        fill(dep, ts, n_fill)
        span *= 2
    for off in offs:
        dep = ts[0]
        inner = [_mm(jnp.where(off, l, 0.0), bd(t)) for l, t in zip(lps, ts)]
        fill(dep, inner, n_fill)
        dep = inner[0]
        ts = [t - _mm(t, bd(x)) for t, x in zip(ts, inner)]
        fill(dep, ts, n_fill)
    return ts


def _delta_rule_core(read_rows, read_gates, n_pos, state_ref, qn_s, kn_s, gsm_s, gl_s, gp_s, gr_s,
                     finish=None, o_s=None):
    c = CHUNK
    hd = DN_HEAD_DIM
    n_pairs = DN_HEADS // 2
    pairs = range(n_pairs)

    rowp = lax.broadcasted_iota(jnp.int32, (c, 2 * c), 0)
    colp = lax.broadcasted_iota(jnp.int32, (c, 2 * c), 1) & (c - 1)
    causal = rowp >= colp
    strict = rowp > colp
    lane_p = lax.broadcasted_iota(jnp.int32, (c, 2 * c), 1)
    inv_masks = _inverse_masks(c)
    tril = (lax.broadcasted_iota(jnp.int32, (c, c), 0) >= lax.broadcasted_iota(jnp.int32, (c, c), 1))
    tril3 = jnp.concatenate([tril.astype(BF16)] * 3, axis=1)
    zero_blk = jnp.zeros((hd, hd), BF16)
    head_cols = lambda h: slice(h * hd, (h + 1) * hd)
    pair_cols_of = lambda p: slice(p * PAIR, (p + 1) * PAIR)
    both = (0, 1)
    ha = [2 * p for p in pairs]
    hb = [2 * p + 1 for p in pairs]

    def l2n(xh, scale):
        return xh * (lax.rsqrt(jnp.sum(xh * xh, axis=-1, keepdims=True) + NORM_EPS) * scale)

    def prepare_pair(j, p, zero=None):
        cols = pair_cols_of(p)
        slot = j & 3
        for idx, (dst, scale) in enumerate(((qn_s, hd ** -0.5), (kn_s, 1.0))):
            x = read_rows(idx, j, cols).astype(F32)
            if zero is not None:
                x = x + jnp.concatenate([zero[0:1], zero[0:1]], axis=1)
            dst[slot, :, cols] = jnp.concatenate([l2n(x[:, :hd], scale), l2n(x[:, hd:], scale)], axis=1)
        r0 = pl.multiple_of((j >> 24) * SUBLANES, SUBLANES)
        back = qn_s[slot, pl.ds(r0, SUBLANES), cols] + kn_s[slot, pl.ds(r0, SUBLANES), cols]
        return back[:, :hd] + back[:, hd:]

    def out_rows(j):
        return pl.multiple_of(jnp.maximum(j, 0) * c, c)

    def exact_rows_dot(lhs3, x):
        hi, mid, lo = _split3(x)
        return jnp.dot(lhs3, jnp.concatenate([hi, mid, lo], axis=0), preferred_element_type=F32)

    def pair_cols(mat, la, lb):
        return jnp.where(lane_p < c, mat[:, la:la + 1], mat[:, lb:lb + 1])

    def pair_rows(mat_t, ra, rb):
        return jnp.concatenate([mat_t[ra:ra + 1, :], mat_t[rb:rb + 1, :]], axis=1)

    def zero_of(v):
        return jnp.where(v != v, v, 0.0)

    def prepare_gates(j):
        slot = j & 3
        gt = read_gates(j)
        gcum = exact_rows_dot(tril3, gt)
        g_last = gcum[c - 1:c, :]
        gsm_s[slot, 0] = jnp.exp(gcum)
        gsm_s[slot, 1] = jnp.exp(g_last - gcum)
        gsm_s[slot, 2] = jnp.broadcast_to(jnp.exp(g_last), (c, LANES))
        gt_t = gt.T
        gcum_t = gcum.T
        for p in pairs:
            g_col = pair_cols(gcum, DN_HEADS + ha[p], DN_HEADS + hb[p])
            g_row = pair_rows(gcum_t, DN_HEADS + ha[p], DN_HEADS + hb[p])
            b_col = pair_cols(gt, ha[p], hb[p])
            b_row = pair_rows(gt_t, ha[p], hb[p])
            dec = jnp.where(causal, jnp.exp(jnp.where(causal, g_col - g_row, 0.0)), 0.0)
            gp_s[slot, p] = dec
            gl_s[slot, p] = jnp.where(strict, b_col * dec, 0.0)
            gr_s[slot, p] = jnp.concatenate(
                [b_row, b_row * jnp.exp(g_row), jnp.zeros((SUBLANES - 2, 2 * c), F32)], axis=0)

    def pair_body(i, carry):
        pos = [2 * i, 2 * i + 1]
        slot = [pz & 3 for pz in pos]
        fillers = []
        for p in pairs:
            for x in both:
                if finish is not None:
                    fillers.append(functools.partial(finish, x, out_rows(pos[x] - 2), p))
                fillers.append(functools.partial(prepare_pair, pos[x] + 2, p))
        n_slices = len(fillers)
        fillers.reverse()

        def fill(dep, into, n=1):
            zero = None if dep is None else zero_of(dep[0:SUBLANES, 0:LANES].astype(F32))
            for _ in range(n):
                if fillers:
                    done = zero_of(fillers.pop()(zero=zero))[0:1]
                    if into[-1].shape[1] == PAIR:
                        done = jnp.concatenate([done, done], axis=1)
                    into[-1] = into[-1] + done

        prob = [(x, p) for x in both for p in pairs]
        eg = [gsm_s[slot[x], 0] for x in both]
        e_rest = [gsm_s[slot[x], 1] for x in both]
        e_last = [gsm_s[slot[x], 2, 0:1, :] for x in both]
        gl = [gl_s[slot[x], p] for x, p in prob]
        gp = [gp_s[slot[x], p] for x, p in prob]
        gr = [gr_s[slot[x], p] for x, p in prob]
        kpair = [kn_s[slot[x], :, pair_cols_of(p)].astype(BF16) for x, p in prob]
        kbd = [_block_diag_rows(kp, hd) for kp in kpair]
        qpair = []
        for n, (x, p) in enumerate(prob):
            staged = [qn_s[slot[x], :, pair_cols_of(p)]]
            if n % 2:
                fill(None, staged)
            qpair.append(staged[0].astype(BF16))
        aq = [_mm_nt(jnp.concatenate([kpair[n], qpair[n]], axis=0), kbd[n]) for n in range(len(prob))]
        for x in both:
            prepare_gates(pos[x] + 2)
        lps = [aq[n][:c] * gl[n] for n in range(len(prob))]
        pps = [aq[n][c:] * gp[n] for n in range(len(prob))]
        per_group = max(1, (n_slices - len(prob) // 2) // 12)
        tps = _inv_unit_lower_packed(lps, c, inv_masks, fill, per_group)
        vbd = [_block_diag_rows(read_rows(2, pos[x], pair_cols_of(p)), hd) for x, p in prob]
        u = [_mm(tps[n] * gr[n][0:1], vbd[n]) for n in range(len(prob))]
        w = [_mm(tps[n] * gr[n][1:2], kbd[n]) for n in range(len(prob))]
        fill(tps[0], w, per_group)
        state = [state_ref[h] for h in range(DN_HEADS)]
        outs = []
        for x in both:
            idx = [x * n_pairs + p for p in pairs]
            qd = [jnp.concatenate(
                [qn_s[slot[x], :, head_cols(h)] * eg[x][:, DN_HEADS + h:DN_HEADS + h + 1]
                 for h in (ha[p], hb[p])], axis=1) for p in pairs]
            sbd = [jnp.concatenate([jnp.concatenate([state[ha[p]].astype(BF16), zero_blk], axis=1),
                                    jnp.concatenate([zero_blk, state[hb[p]].astype(BF16)], axis=1)], axis=0)
                   for p in pairs]
            ws = [_mm(jnp.concatenate([w[idx[p]], qd[p]], axis=0), sbd[p]) for p in pairs]
            fill(w[idx[0]], ws, per_group)
            v_new = [u[idx[p]] - ws[p][:c] for p in pairs]
            o = [ws[p][c:] + _mm(pps[idx[p]], _block_diag_rows(v_new[p].astype(BF16), hd)) for p in pairs]
            new_state = []
            for h in range(DN_HEADS):
                p, half = divmod(h, 2)
                lane_h = DN_HEADS + h
                kd = kn_s[slot[x], :, head_cols(h)] * e_rest[x][:, lane_h:lane_h + 1]
                upd = _mm_tn(kd, v_new[p][:, half * hd:(half + 1) * hd])
                new_state.append(state[h] * e_last[x][:, lane_h:lane_h + 1] + upd)
            state = new_state
            if x == 1:
                fill(ws[0], o, len(fillers))
            outs.append(o)
        for h in range(DN_HEADS):
            state_ref[h] = state[h]
        if o_s is not None:
            for x in both:
                for p in pairs:
                    o_s[x, :, pair_cols_of(p)] = outs[x][p]
        return carry

    for x in both:
        for p in pairs:
            prepare_pair(jnp.int32(x), p)
        prepare_gates(jnp.int32(x))
    n_iter = n_pos // 2
    if n_iter == 1:
        pair_body(jnp.int32(0), 0)
    else:
        lax.fori_loop(0, n_iter, pair_body, 0)
    if finish is not None:
        for x in both:
            last_rows = out_rows(jnp.int32(n_pos - 2 + x))
            for p in pairs:
                finish(x, last_rows, p)


def _deltanet_kernel(q_ref, k_ref, v_ref, z_ref, gates_ref, s0_ref, dnw_ref,
                     y_ref, state_ref, qn_s, kn_s, o_s, gsm_s, gl_s, gp_s, gr_s, *, tt):
    c = CHUNK
    hd = DN_HEAD_DIM
    n_chunks = tt // c
    dnw = dnw_ref[...]

    @pl.when(pl.program_id(1) == 0)
    def _():
        state_ref[...] = s0_ref[...]
        o_s[...] = jnp.zeros(o_s.shape, F32)

    srcs = (q_ref, k_ref, v_ref)

    def rows_at(pos):
        return pl.ds(pl.multiple_of(jnp.minimum(pos, n_chunks - 1) * c, c), c)

    def read_rows(idx, pos, cols):
        return srcs[idx][rows_at(pos), cols]

    def read_gates(pos):
        return gates_ref[rows_at(pos), :]

    def finish(src, row0, p, zero=None):
        rows = pl.ds(row0, c)
        for half in range(2):
            cs = slice(p * PAIR + half * hd, p * PAIR + (half + 1) * hd)
            oh = o_s[src, :, cs]
            if zero is not None:
                oh = oh + zero[0:1]
            oh = oh * lax.rsqrt(jnp.mean(oh * oh, axis=-1, keepdims=True) + NORM_EPS) * dnw
            y_ref[rows, cs] = (oh * z_ref[rows, cs].astype(F32)).astype(y_ref.dtype)
        r0 = pl.multiple_of((row0 >> 24) * (2 * SUBLANES), 2 * SUBLANES)
        back = y_ref[pl.ds(r0, 2 * SUBLANES), p * PAIR:(p + 1) * PAIR].astype(F32)[0:SUBLANES]
        return back[:, :hd] + back[:, hd:]

    _delta_rule_core(read_rows, read_gates, n_chunks, state_ref, qn_s, kn_s, gsm_s, gl_s, gp_s, gr_s,
                     finish=finish, o_s=o_s)


def _deltanet_meta_kernel(mqkv_ref, mgates_ref, s_out_ref, state_ref, qn_s, kn_s, gsm_s, gl_s, gp_s, gr_s):
    c = CHUNK
    state_ref[...] = jnp.zeros(state_ref.shape, F32)

    def rows_at(pos):
        return pl.ds(pl.multiple_of(jnp.minimum(pos, 1) * c, c), c)

    def read_rows(idx, pos, cols):
        return mqkv_ref[rows_at(pos), slice(idx * DN_WIDTH + cols.start, idx * DN_WIDTH + cols.stop)]

    def read_gates(pos):
        return mgates_ref[rows_at(pos), :]

    _delta_rule_core(read_rows, read_gates, 2, state_ref, qn_s, kn_s, gsm_s, gl_s, gp_s, gr_s)
    s_out_ref[...] = state_ref[...]


def _lookahead_scratch():
    c = CHUNK
    return [
        pltpu.VMEM((4, c, DN_WIDTH), F32),
        pltpu.VMEM((4, c, DN_WIDTH), F32),
        pltpu.VMEM((4, 3, c, LANES), F32),
        pltpu.VMEM((4, DN_HEADS // 2, c, 2 * c), F32),
        pltpu.VMEM((4, DN_HEADS // 2, c, 2 * c), F32),
        pltpu.VMEM((4, DN_HEADS // 2, SUBLANES, 2 * c), F32),
    ]


def _deltanet_meta(meta_qkv, meta_gates):
    state_shape = (DN_HEADS, DN_HEAD_DIM, DN_HEAD_DIM)
    qn, kn, gsm, gl, gp, gr = _lookahead_scratch()
    return pl.pallas_call(
        _deltanet_meta_kernel,
        out_shape=jax.ShapeDtypeStruct(state_shape, F32),
        scratch_shapes=[pltpu.VMEM(state_shape, F32), qn, kn, gsm, gl, gp, gr],
        compiler_params=pltpu.CompilerParams(vmem_limit_bytes=VMEM_LIMIT),
        name="deltanet_meta",
    )(meta_qkv, meta_gates)


def _deltanet(proj, gates, state0, dn_norm_w, *, batch, seq_len):
    tt = TT_DN
    nt = seq_len // tt
    c = CHUNK
    kern = functools.partial(_deltanet_kernel, tt=tt)
    row = lambda b, t: b * nt + t
    const = lambda shape: pl.BlockSpec(shape, lambda b, t: (0,) * len(shape), pipeline_mode=pl.Buffered(1))
    state_shape = (DN_HEADS, DN_HEAD_DIM, DN_HEAD_DIM)
    qn, kn, gsm, gl, gp, gr = _lookahead_scratch()
    return pl.pallas_call(
        kern,
        out_shape=jax.ShapeDtypeStruct((batch * seq_len, DN_WIDTH), BF16),
        grid=(batch, nt),
        in_specs=[
            pl.BlockSpec((tt, DN_WIDTH), lambda b, t: (row(b, t), COL_Q // DN_WIDTH)),
            pl.BlockSpec((tt, DN_WIDTH), lambda b, t: (row(b, t), COL_K // DN_WIDTH)),
            pl.BlockSpec((tt, DN_WIDTH), lambda b, t: (row(b, t), COL_V // DN_WIDTH)),
            pl.BlockSpec((tt, DN_WIDTH), lambda b, t: (row(b, t), COL_ZD // DN_WIDTH)),
            pl.BlockSpec((tt, LANES), lambda b, t: (row(b, t), 0)),
            const(state_shape),
            const((1, DN_HEAD_DIM)),
        ],
        out_specs=pl.BlockSpec((tt, DN_WIDTH), lambda b, t: (row(b, t), 0)),
        scratch_shapes=[
            pltpu.VMEM(state_shape, F32),
            qn, kn,
            pltpu.VMEM((2, c, DN_WIDTH), F32),
            gsm, gl, gp, gr,
        ],
        compiler_params=pltpu.CompilerParams(
            dimension_semantics=("arbitrary", "arbitrary"), vmem_limit_bytes=VMEM_LIMIT),
        name="deltanet",
    )(proj, proj, proj, proj, gates, state0, dn_norm_w)


def _out_merge_kernel(u_ref, zp_ref, gpa_ref, gpb_ref, gda_ref, gdb_ref, ydn_ref, x_ref, mu_ref, mix_ref,
                      scale_ref, wpo_ref, wdo_ref, wo_ref, fnw_ref, out_ref, ubuf_ref, *, tm):
    t = pl.program_id(1)
    hist = N_META

    @pl.when(t == 0)
    def _():
        ubuf_ref[0:hist, :] = mu_ref[...].astype(F32)

    @pl.when(t > 0)
    def _():
        ubuf_ref[0:hist, :] = ubuf_ref[tm:tm + hist, :]

    ubuf_ref[hist:hist + tm, :] = u_ref[...].astype(F32)

    pooled = []
    for gi, w in enumerate(POOL_WINDOWS):
        cs = slice(gi * POOL_GROUP_DIM, (gi + 1) * POOL_GROUP_DIM)
        ext = ubuf_ref[:, cs]
        acc = ext
        span = 1
        while span < w:
            acc = acc + pltpu.roll(acc, span, axis=0)
            span *= 2
        d = acc[hist:] * (1.0 / w) - ext[hist:]
        mixed = _mm(d, mix_ref[gi])
        zp = zp_ref[:, cs].astype(F32)
        pooled.append((mixed * scale_ref[:, cs] * zp).astype(BF16))
    y_pool = jnp.concatenate(pooled, axis=1)

    frame = slice(GATE_SHIFT, GATE_SHIFT + D_MODEL)
    gp = jnp.concatenate([gpa_ref[...], gpb_ref[...]], axis=1).astype(F32)[:, frame]
    gd = jnp.concatenate([gda_ref[...], gdb_ref[...]], axis=1).astype(F32)[:, frame]
    t_pool = jnp.dot(y_pool, wpo_ref[...], preferred_element_type=F32)
    t_dn = jnp.dot(ydn_ref[...], wdo_ref[...], preferred_element_type=F32)
    merged = _sigmoid(gp) * t_pool + _sigmoid(gd) * t_dn
    h = x_ref[...] + jnp.dot(merged.astype(BF16), wo_ref[...], preferred_element_type=F32)
    ms = jnp.mean(h * h, axis=-1, keepdims=True)
    out_ref[...] = h * lax.rsqrt(ms + NORM_EPS) * fnw_ref[...]


def _resident(shape, index_map):
    return pl.BlockSpec(shape, index_map, pipeline_mode=pl.Buffered(1))


def _out_merge(proj, proj_tail, y_dn, x2d, meta_u, mix, scale, wpo, wdo, wo, fnw, *, batch, seq_len):
    tm = TM_OUT
    tiles = seq_len // tm
    kern = functools.partial(_out_merge_kernel, tm=tm)
    row = lambda b, t: b * tiles + t
    gp0, gd0 = COL_GP - GATE_SHIFT, COL_GD - GATE_SHIFT
    assert gp0 % D_MODEL == 0 and gd0 % D_MODEL == 0
    assert gd0 + D_MODEL == proj.shape[1]
    return pl.pallas_call(
        kern,
        out_shape=jax.ShapeDtypeStruct((batch * seq_len, D_MODEL), F32),
        grid=(batch, tiles),
        in_specs=[
            pl.BlockSpec((tm, POOL_WIDTH), lambda b, t: (row(b, t), COL_U // POOL_WIDTH)),
            pl.BlockSpec((tm, POOL_WIDTH), lambda b, t: (row(b, t), COL_ZP // POOL_WIDTH)),
            pl.BlockSpec((tm, D_MODEL), lambda b, t: (row(b, t), gp0 // D_MODEL)),
            pl.BlockSpec((tm, LANES), lambda b, t: (row(b, t), (gp0 + D_MODEL) // LANES)),
            pl.BlockSpec((tm, D_MODEL), lambda b, t: (row(b, t), gd0 // D_MODEL)),
            pl.BlockSpec((tm, LANES), lambda b, t: (row(b, t), 0)),
            pl.BlockSpec((tm, DN_WIDTH), lambda b, t: (row(b, t), 0)),
            pl.BlockSpec((tm, D_MODEL), lambda b, t: (row(b, t), 0)),
            _resident((N_META, POOL_WIDTH), lambda b, t: (0, 0)),
            _resident((POOL_GROUPS, POOL_GROUP_DIM, POOL_GROUP_DIM), lambda b, t: (0, 0, 0)),
            _resident((1, POOL_WIDTH), lambda b, t: (0, 0)),
            _resident((POOL_WIDTH, D_MODEL), lambda b, t: (0, 0)),
            _resident((DN_WIDTH, D_MODEL), lambda b, t: (0, 0)),
            _resident((D_MODEL, D_MODEL), lambda b, t: (0, 0)),
            _resident((1, D_MODEL), lambda b, t: (0, 0)),
        ],
        out_specs=pl.BlockSpec((tm, D_MODEL), lambda b, t: (row(b, t), 0)),
        scratch_shapes=[pltpu.VMEM((N_META + tm, POOL_WIDTH), F32)],
        compiler_params=pltpu.CompilerParams(
            dimension_semantics=("arbitrary", "arbitrary"), vmem_limit_bytes=VMEM_LIMIT),
        name="out_merge",
    )(proj, proj, proj, proj, proj, proj_tail, y_dn, x2d, meta_u, mix, scale, wpo, wdo, wo, fnw)


def kernel(x, meta_tokens, norm_w, w_in, conv_w, A_log, dt_bias, pool_mix, pool_scale, dn_norm_w,
           w_pool_out, w_dn_out, w_o, final_norm_w):
    batch, seq_len, _ = x.shape
    assert norm_w.shape[0] == 1, "single layer block"
    x2d = x.reshape(batch * seq_len, D_MODEL)

    rows = batch * seq_len
    w_t = jnp.swapaxes(w_in, 1, 2).reshape(w_in.shape[2], D_MODEL)
    gpar = jnp.zeros((2, LANES), F32)
    gpar = gpar.at[0, DN_HEADS:2 * DN_HEADS].set(A_log[0]).at[1, DN_HEADS:2 * DN_HEADS].set(dt_bias[0])
    nw = norm_w[0].reshape(1, D_MODEL)

    xn_all, gates, proj_tail = _norm_gates(x2d, meta_tokens.astype(F32), nw, w_t, gpar)
    proj, wpo, wdo, wo = _in_proj(xn_all, w_t, conv_w[0], (w_pool_out[0], w_dn_out[0], w_o[0]), seq_len=seq_len)

    front = 2 * CHUNK - N_META
    meta_proj = proj[rows:]
    meta_qkv = jnp.pad(meta_proj[:, COL_Q:COL_ZD], ((front, 0), (0, 0)))
    meta_gates = jnp.pad(gates[rows:], ((front, 0), (0, 0)))
    meta_u = meta_proj[:, COL_U:COL_U + POOL_WIDTH]

    state0 = _deltanet_meta(meta_qkv, meta_gates)
    y_dn = _deltanet(proj, gates, state0, dn_norm_w[0].reshape(1, DN_HEAD_DIM), batch=batch, seq_len=seq_len)

    out = _out_merge(proj, proj_tail, y_dn, x2d, meta_u, pool_mix[0].astype(BF16), pool_scale[0].reshape(1, POOL_WIDTH),
                     wpo, wdo, wo, final_norm_w.reshape(1, D_MODEL), batch=batch, seq_len=seq_len)
    return out.reshape(batch, seq_len, D_MODEL)
```

```python
import functools

import jax
import jax.numpy as jnp
from jax import lax
from jax.experimental import pallas as pl
from jax.experimental.pallas import tpu as pltpu

D_MODEL = 2048
N_META = 16
POOL_GROUPS = 4
POOL_WINDOWS = (2, 4, 8, 16)
POOL_WIDTH = D_MODEL // 2
POOL_GROUP_DIM = POOL_WIDTH // POOL_GROUPS
DN_HEADS = 16
DN_HEAD_DIM = 128
DN_WIDTH = DN_HEADS * DN_HEAD_DIM
CONV_WIDTH = 4
NORM_EPS = 1e-6

COL_U = 0
COL_ZP = COL_U + POOL_WIDTH
COL_Q = COL_ZP + POOL_WIDTH
COL_K = COL_Q + DN_WIDTH
COL_V = COL_K + DN_WIDTH
COL_ZD = COL_V + DN_WIDTH
BA_OFFSET = COL_ZD + DN_WIDTH
COL_GP = BA_OFFSET + 2 * DN_HEADS
COL_GD = COL_GP + D_MODEL
LANES = 128
SUBLANES = 8
GATE_SHIFT = COL_GP % LANES

PROJ_DTYPE = jnp.bfloat16
CHUNK = 64
LEAF = 16
TT_DN = 1024
TM_NORM = 1024
TN_PROJ = 256
ROW_BLK_PROJ = 1024
W_CAST_SLABS = 32
TM_OUT = 256
VMEM_LIMIT = 58 * 1024 * 1024

BF16 = jnp.bfloat16
F32 = jnp.float32


def _mm(a, b):
    return jnp.dot(a.astype(BF16), b.astype(BF16), preferred_element_type=F32)


def _mm_nt(a, b):
    return lax.dot_general(a.astype(BF16), b.astype(BF16), (((1,), (1,)), ((), ())),
                           preferred_element_type=F32)


def _mm_tn(a, b):
    return lax.dot_general(a.astype(BF16), b.astype(BF16), (((0,), (0,)), ((), ())),
                           preferred_element_type=F32)


def _sigmoid(x):
    return 0.5 * jnp.tanh(0.5 * x) + 0.5


def _silu(x):
    h = 0.5 * x
    return h * jnp.tanh(h) + h


def _softplus(x):
    return jnp.maximum(x, 0.0) + jnp.log1p(jnp.exp(-jnp.abs(x)))


def _norm_gates_kernel(x_ref, meta_ref, nw_ref, wba_ref, wtail_ref, gpar_ref, xn_ref, gates_ref, tail_ref,
                       *, x_tiles, sub):
    i = pl.program_id(0)
    nw = nw_ref[...]
    wtail = wtail_ref[...]
    wboth = jnp.concatenate(
        [wba_ref[...], wtail, jnp.zeros((LANES - wtail.shape[0], D_MODEL), F32)], axis=0).astype(BF16)
    a_rate = jnp.exp(gpar_ref[0:1, :])
    dt_bias = gpar_ref[1:2, :]

    def rows_block(x, rows):
        ms = jnp.mean(x * x, axis=-1, keepdims=True)
        xn = (x * lax.rsqrt(ms + NORM_EPS) * nw).astype(BF16)
        xn_ref[rows, :] = xn
        both = _mm_nt(xn, wboth)
        ba = both[:, :LANES]
        tail_ref[rows, :] = both[:, LANES:].astype(tail_ref.dtype)
        lane = lax.broadcasted_iota(jnp.int32, ba.shape, 1)
        g = -a_rate * _softplus(ba + dt_bias)
        gates_ref[rows, :] = jnp.where(lane < DN_HEADS, _sigmoid(ba), jnp.where(lane < 2 * DN_HEADS, g, 0.0))

    @pl.when(i < x_tiles)
    def _():
        def body(r, carry):
            rows = pl.ds(pl.multiple_of(r * sub, sub), sub)
            rows_block(x_ref[rows, :], rows)
            return carry

        lax.fori_loop(0, x_ref.shape[0] // sub, body, 0)

    @pl.when(i == x_tiles)
    def _():
        rows_block(meta_ref[...], pl.ds(0, N_META))


def _norm_gates(x2d, meta, norm_w, w_t, gpar):
    rows = x2d.shape[0]
    tm = TM_NORM
    tail_start = w_t.shape[0] // TN_PROJ * TN_PROJ
    tail_cols = w_t.shape[0] - tail_start
    assert tail_start % LANES == 0 and 0 < tail_cols <= LANES
    assert tail_cols % SUBLANES == 0 and tail_start % tail_cols == 0
    x_tiles = rows // tm
    kern = functools.partial(_norm_gates_kernel, x_tiles=x_tiles, sub=512)
    return pl.pallas_call(
        kern,
        out_shape=(jax.ShapeDtypeStruct((rows + N_META, D_MODEL), BF16),
                   jax.ShapeDtypeStruct((rows + N_META, LANES), F32),
                   jax.ShapeDtypeStruct((rows + N_META, LANES), PROJ_DTYPE)),
        grid=(x_tiles + 1,),
        in_specs=[
            pl.BlockSpec((tm, D_MODEL), lambda i: (jnp.minimum(i, x_tiles - 1), 0)),
            pl.BlockSpec((N_META, D_MODEL), lambda i: (0, 0)),
            pl.BlockSpec((1, D_MODEL), lambda i: (0, 0)),
            pl.BlockSpec((LANES, D_MODEL), lambda i: (BA_OFFSET // LANES, 0)),
            pl.BlockSpec((tail_cols, D_MODEL), lambda i: (tail_start // tail_cols, 0)),
            pl.BlockSpec((2, LANES), lambda i: (0, 0)),
        ],
        out_specs=(pl.BlockSpec((tm, D_MODEL), lambda i: (i, 0)),
                   pl.BlockSpec((tm, LANES), lambda i: (i, 0)),
                   pl.BlockSpec((tm, LANES), lambda i: (i, 0))),
        compiler_params=pltpu.CompilerParams(
            dimension_semantics=("arbitrary",), vmem_limit_bytes=VMEM_LIMIT),
        name="norm_gates",
    )(x2d, meta, norm_w, w_t, w_t, gpar)


def _in_proj_kernel(xn_ref, w_ref, cw_ref, wpo_ref, wdo_ref, wo_ref,
                    proj_ref, wpo_bf_ref, wdo_bf_ref, wo_bf_ref, acc_ref, *, rows, row_blk, seq_len):
    j = pl.program_id(0)
    tn = w_ref.shape[0]
    w = w_ref[...].astype(BF16)
    ms = slice(rows, rows + N_META)
    conv_tile = (j >= COL_Q // tn) & (j < COL_ZD // tn)
    silu_tile = ((j >= COL_ZP // tn) & (j < COL_Q // tn)) | ((j >= COL_ZD // tn) & (j < BA_OFFSET // tn))

    for src, dst in ((wpo_ref, wpo_bf_ref), (wdo_ref, wdo_bf_ref), (wo_ref, wo_bf_ref)):
        dst[...] = src[...].astype(BF16)

    def plain(fn):
        for r in range(rows // row_blk):
            rs = slice(r * row_blk, (r + 1) * row_blk)
            proj_ref[rs, :] = fn(_mm_nt(xn_ref[rs, :], w)).astype(proj_ref.dtype)
        proj_ref[ms, :] = fn(_mm_nt(xn_ref[ms, :], w)).astype(proj_ref.dtype)

    @pl.when(conv_tile)
    def _():
        cw = cw_ref[...]
        hist_rows = SUBLANES
        n_blk = rows // row_blk

        def conv_silu(slot, n):
            y = acc_ref[slot, hist_rows:hist_rows + n, :] * cw[CONV_WIDTH - 1:CONV_WIDTH]
            for kk in range(CONV_WIDTH - 1):
                off = hist_rows - (CONV_WIDTH - 1) + kk
                y = y + acc_ref[slot, off:off + n, :] * cw[kk:kk + 1]
            return _silu(y)

        meta_acc = _mm_nt(xn_ref[ms, :], w)
        acc_ref[0, 0:hist_rows, :] = jnp.zeros((hist_rows, tn), F32)
        acc_ref[0, hist_rows:hist_rows + N_META, :] = meta_acc
        proj_ref[ms, :] = conv_silu(0, N_META).astype(proj_ref.dtype)
        meta_tail = meta_acc[N_META - hist_rows:]
        for r in range(n_blk + 1):
            if r < n_blk:
                slot = r % 2
                rs = slice(r * row_blk, (r + 1) * row_blk)
                acc = _mm_nt(xn_ref[rs, :], w)
                batch_start = (r * row_blk) % seq_len == 0
                acc_ref[slot, 0:hist_rows, :] = (
                    meta_tail if batch_start else acc_ref[1 - slot, row_blk:row_blk + hist_rows, :])
                acc_ref[slot, hist_rows:hist_rows + row_blk, :] = acc
            if r > 0:
                ps = slice((r - 1) * row_blk, r * row_blk)
                proj_ref[ps, :] = conv_silu((r - 1) % 2, row_blk).astype(proj_ref.dtype)

    @pl.when(silu_tile)
    def _():
        plain(_silu)

    @pl.when(jnp.logical_not(conv_tile | silu_tile))
    def _():
        plain(lambda a: a)


def _in_proj(xn_all, w_t, conv_w, out_weights, *, seq_len):
    rows = xn_all.shape[0] - N_META
    cols = w_t.shape[0]
    tn = TN_PROJ
    n_conv_tiles = (COL_ZD - COL_Q) // tn
    n_steps = cols // tn
    assert W_CAST_SLABS <= n_steps
    slab = lambda wgt: pl.BlockSpec((wgt.shape[0] // W_CAST_SLABS, D_MODEL),
                                    lambda j: (jnp.minimum(j, W_CAST_SLABS - 1), 0))
    kern = functools.partial(_in_proj_kernel, rows=rows, row_blk=ROW_BLK_PROJ, seq_len=seq_len)
    return pl.pallas_call(
        kern,
        out_shape=(jax.ShapeDtypeStruct((rows + N_META, n_steps * tn), PROJ_DTYPE),
                   *[jax.ShapeDtypeStruct(wgt.shape, BF16) for wgt in out_weights]),
        grid=(n_steps,),
        in_specs=[
            pl.BlockSpec((rows + N_META, D_MODEL), lambda j: (0, 0), pipeline_mode=pl.Buffered(1)),
            pl.BlockSpec((tn, D_MODEL), lambda j: (j, 0)),
            pl.BlockSpec((CONV_WIDTH, tn), lambda j: (0, jnp.clip(j - COL_Q // tn, 0, n_conv_tiles - 1))),
            *[slab(wgt) for wgt in out_weights],
        ],
        out_specs=(pl.BlockSpec((rows + N_META, tn), lambda j: (0, j)),
                   *[slab(wgt) for wgt in out_weights]),
        scratch_shapes=[pltpu.VMEM((2, SUBLANES + ROW_BLK_PROJ, tn), F32)],
        compiler_params=pltpu.CompilerParams(
            dimension_semantics=("arbitrary",), vmem_limit_bytes=VMEM_LIMIT),
        name="in_proj",
    )(xn_all, w_t, conv_w, *out_weights)


PAIR = 2 * DN_HEAD_DIM


def _split3(x):
    hi = x.astype(BF16)
    r1 = x - hi.astype(F32)
    mid = r1.astype(BF16)
    lo = (r1 - mid.astype(F32)).astype(BF16)
    return hi, mid, lo


def _block_diag_rows(y, half):
    lane = lax.broadcasted_iota(jnp.int32, y.shape, 1)
    zero = jnp.zeros_like(y)
    return jnp.concatenate([jnp.where(lane < half, y, zero), jnp.where(lane >= half, y, zero)], axis=0)


def _inverse_masks(c):
    row = lax.broadcasted_iota(jnp.int32, (c, 2 * c), 0)
    col = lax.broadcasted_iota(jnp.int32, (c, 2 * c), 1) & (c - 1)
    blk = lambda x, m: x >> (m.bit_length() - 1)
    eye = (row == col).astype(F32)
    same_leaf = blk(row, LEAF) == blk(col, LEAF)
    offs = []
    m = 2 * LEAF
    while m <= c:
        offs.append((blk(row, m) == blk(col, m)) & (blk(row, m // 2) != blk(col, m // 2)))
        m *= 2
    return eye, same_leaf, offs


def _inv_unit_lower_packed(lps, c, masks, fill, n_fill=1):
    eye, same_leaf, offs = masks
    bd = lambda y: _block_diag_rows(y.astype(BF16), c)
    ps = [jnp.where(same_leaf, -l, 0.0) for l in lps]
    ts = [eye + p for p in ps]
    dep = ps[0]
    ps = [_mm(p, bd(p)) for p in ps]
    fill(dep, ps, n_fill + 1)
    span = 4
    while span <= LEAF:
        dep = ps[0]
        last = span == LEAF
        prod = [_mm(t if last else jnp.concatenate([t, p], axis=0), bd(p)) for t, p in zip(ts, ps)]
        ts = [t + x[:c] for t, x in zip(ts, prod)]
        if not last:
            ps = [x[c:] for x in prod]
        fill(dep, ts, n_fill)
        span *= 2
    for off in offs:
        dep = ts[0]
        inner = [_mm(jnp.where(off, l, 0.0), bd(t)) for l, t in zip(lps, ts)]
        fill(dep, inner, n_fill)
        dep = inner[0]
        ts = [t - _mm(t, bd(x)) for t, x in zip(ts, inner)]
        fill(dep, ts, n_fill)
    return ts


def _delta_rule_core(read_rows, read_gates, n_pos, state_ref, qn_s, kn_s, gsm_s, gl_s, gp_s, gr_s,
                     finish=None, o_s=None):
    c = CHUNK
    hd = DN_HEAD_DIM
    n_pairs = DN_HEADS // 2
    pairs = range(n_pairs)

    rowp = lax.broadcasted_iota(jnp.int32, (c, 2 * c), 0)
    colp = lax.broadcasted_iota(jnp.int32, (c, 2 * c), 1) & (c - 1)
    causal = rowp >= colp
    strict = rowp > colp
    lane_p = lax.broadcasted_iota(jnp.int32, (c, 2 * c), 1)
    inv_masks = _inverse_masks(c)
    tril = (lax.broadcasted_iota(jnp.int32, (c, c), 0) >= lax.broadcasted_iota(jnp.int32, (c, c), 1))
    tril3 = jnp.concatenate([tril.astype(BF16)] * 3, axis=1)
    zero_blk = jnp.zeros((hd, hd), BF16)
    head_cols = lambda h: slice(h * hd, (h + 1) * hd)
    pair_cols_of = lambda p: slice(p * PAIR, (p + 1) * PAIR)
    both = (0, 1)
    ha = [2 * p for p in pairs]
    hb = [2 * p + 1 for p in pairs]

    def l2n(xh, scale):
        return xh * (lax.rsqrt(jnp.sum(xh * xh, axis=-1, keepdims=True) + NORM_EPS) * scale)

    def prepare_pair(j, p, zero=None):
        cols = pair_cols_of(p)
        slot = j & 3
        for idx, (dst, scale) in enumerate(((qn_s, hd ** -0.5), (kn_s, 1.0))):
            x = read_rows(idx, j, cols).astype(F32)
            if zero is not None:
                x = x + jnp.concatenate([zero[0:1], zero[0:1]], axis=1)
            dst[slot, :, cols] = jnp.concatenate([l2n(x[:, :hd], scale), l2n(x[:, hd:], scale)], axis=1)
        r0 = pl.multiple_of((j >> 24) * SUBLANES, SUBLANES)
        back = qn_s[slot, pl.ds(r0, SUBLANES), cols] + kn_s[slot, pl.ds(r0, SUBLANES), cols]
        return back[:, :hd] + back[:, hd:]

    def out_rows(j):
        return pl.multiple_of(jnp.maximum(j, 0) * c, c)

    def exact_rows_dot(lhs3, x):
        hi, mid, lo = _split3(x)
        return jnp.dot(lhs3, jnp.concatenate([hi, mid, lo], axis=0), preferred_element_type=F32)

    def pair_cols(mat, la, lb):
        return jnp.where(lane_p < c, mat[:, la:la + 1], mat[:, lb:lb + 1])

    def pair_rows(mat_t, ra, rb):
        return jnp.concatenate([mat_t[ra:ra + 1, :], mat_t[rb:rb + 1, :]], axis=1)

    def zero_of(v):
        return jnp.where(v != v, v, 0.0)

    def prepare_gates(j):
        slot = j & 3
        gt = read_gates(j)
        gcum = exact_rows_dot(tril3, gt)
        g_last = gcum[c - 1:c, :]
        gsm_s[slot, 0] = jnp.exp(gcum)
        gsm_s[slot, 1] = jnp.exp(g_last - gcum)
        gsm_s[slot, 2] = jnp.broadcast_to(jnp.exp(g_last), (c, LANES))
        gt_t = gt.T
        gcum_t = gcum.T
        for p in pairs:
            g_col = pair_cols(gcum, DN_HEADS + ha[p], DN_HEADS + hb[p])
            g_row = pair_rows(gcum_t, DN_HEADS + ha[p], DN_HEADS + hb[p])
            b_col = pair_cols(gt, ha[p], hb[p])
            b_row = pair_rows(gt_t, ha[p], hb[p])
            dec = jnp.where(causal, jnp.exp(jnp.where(causal, g_col - g_row, 0.0)), 0.0)
            gp_s[slot, p] = dec
            gl_s[slot, p] = jnp.where(strict, b_col * dec, 0.0)
            gr_s[slot, p] = jnp.concatenate(
                [b_row, b_row * jnp.exp(g_row), jnp.zeros((SUBLANES - 2, 2 * c), F32)], axis=0)

    def pair_body(i, carry):
        pos = [2 * i, 2 * i + 1]
        slot = [pz & 3 for pz in pos]
        fillers = []
        for p in pairs:
            for x in both:
                if finish is not None:
                    fillers.append(functools.partial(finish, x, out_rows(pos[x] - 2), p))
                fillers.append(functools.partial(prepare_pair, pos[x] + 2, p))
        n_slices = len(fillers)
        fillers.reverse()

        def fill(dep, into, n=1):
            zero = None if dep is None else zero_of(dep[0:SUBLANES, 0:LANES].astype(F32))
            for _ in range(n):
                if fillers:
                    done = zero_of(fillers.pop()(zero=zero))[0:1]
                    if into[-1].shape[1] == PAIR:
                        done = jnp.concatenate([done, done], axis=1)
                    into[-1] = into[-1] + done

        prob = [(x, p) for x in both for p in pairs]
        eg = [gsm_s[slot[x], 0] for x in both]
        e_rest = [gsm_s[slot[x], 1] for x in both]
        e_last = [gsm_s[slot[x], 2, 0:1, :] for x in both]
        gl = [gl_s[slot[x], p] for x, p in prob]
        gp = [gp_s[slot[x], p] for x, p in prob]
        gr = [gr_s[slot[x], p] for x, p in prob]
        kpair = [kn_s[slot[x], :, pair_cols_of(p)].astype(BF16) for x, p in prob]
        kbd = [_block_diag_rows(kp, hd) for kp in kpair]
        qpair = []
        for n, (x, p) in enumerate(prob):
            staged = [qn_s[slot[x], :, pair_cols_of(p)]]
            if n % 2:
                fill(None, staged)
            qpair.append(staged[0].astype(BF16))
        aq = [_mm_nt(jnp.concatenate([kpair[n], qpair[n]], axis=0), kbd[n]) for n in range(len(prob))]
        for x in both:
            prepare_gates(pos[x] + 2)
        lps = [aq[n][:c] * gl[n] for n in range(len(prob))]
        pps = [aq[n][c:] * gp[n] for n in range(len(prob))]
        per_group = max(1, (n_slices - len(prob) // 2) // 12)
        tps = _inv_unit_lower_packed(lps, c, inv_masks, fill, per_group)
        vbd = [_block_diag_rows(read_rows(2, pos[x], pair_cols_of(p)), hd) for x, p in prob]
        u = [_mm(tps[n] * gr[n][0:1], vbd[n]) for n in range(len(prob))]
        w = [_mm(tps[n] * gr[n][1:2], kbd[n]) for n in range(len(prob))]
        fill(tps[0], w, per_group)
        state = [state_ref[h] for h in range(DN_HEADS)]
        outs = []
        for x in both:
            idx = [x * n_pairs + p for p in pairs]
            qd = [jnp.concatenate(
                [qn_s[slot[x], :, head_cols(h)] * eg[x][:, DN_HEADS + h:DN_HEADS + h + 1]
                 for h in (ha[p], hb[p])], axis=1) for p in pairs]
            sbd = [jnp.concatenate([jnp.concatenate([state[ha[p]].astype(BF16), zero_blk], axis=1),
                                    jnp.concatenate([zero_blk, state[hb[p]].astype(BF16)], axis=1)], axis=0)
                   for p in pairs]
            ws = [_mm(jnp.concatenate([w[idx[p]], qd[p]], axis=0), sbd[p]) for p in pairs]
            fill(w[idx[0]], ws, per_group)
            v_new = [u[idx[p]] - ws[p][:c] for p in pairs]
            o = [ws[p][c:] + _mm(pps[idx[p]], _block_diag_rows(v_new[p].astype(BF16), hd)) for p in pairs]
            new_state = []
            for h in range(DN_HEADS):
                p, half = divmod(h, 2)
                lane_h = DN_HEADS + h
                kd = kn_s[slot[x], :, head_cols(h)] * e_rest[x][:, lane_h:lane_h + 1]
                upd = _mm_tn(kd, v_new[p][:, half * hd:(half + 1) * hd])
                new_state.append(state[h] * e_last[x][:, lane_h:lane_h + 1] + upd)
            state = new_state
            if x == 1:
                fill(ws[0], o, len(fillers))
            outs.append(o)
        for h in range(DN_HEADS):
            state_ref[h] = state[h]
        if o_s is not None:
            for x in both:
                for p in pairs:
                    o_s[x, :, pair_cols_of(p)] = outs[x][p]
        return carry

    for x in both:
        for p in pairs:
            prepare_pair(jnp.int32(x), p)
        prepare_gates(jnp.int32(x))
    n_iter = n_pos // 2
    if n_iter == 1:
        pair_body(jnp.int32(0), 0)
    else:
        lax.fori_loop(0, n_iter, pair_body, 0)
    if finish is not None:
        for x in both:
            last_rows = out_rows(jnp.int32(n_pos - 2 + x))
            for p in pairs:
                finish(x, last_rows, p)


def _deltanet_kernel(q_ref, k_ref, v_ref, z_ref, gates_ref, s0_ref, dnw_ref,
                     y_ref, state_ref, qn_s, kn_s, o_s, gsm_s, gl_s, gp_s, gr_s, *, tt):
    c = CHUNK
    hd = DN_HEAD_DIM
    n_chunks = tt // c
    dnw = dnw_ref[...]

    @pl.when(pl.program_id(1) == 0)
    def _():
        state_ref[...] = s0_ref[...]
        o_s[...] = jnp.zeros(o_s.shape, F32)

    srcs = (q_ref, k_ref, v_ref)

    def rows_at(pos):
        return pl.ds(pl.multiple_of(jnp.minimum(pos, n_chunks - 1) * c, c), c)

    def read_rows(idx, pos, cols):
        return srcs[idx][rows_at(pos), cols]

    def read_gates(pos):
        return gates_ref[rows_at(pos), :]

    def finish(src, row0, p, zero=None):
        rows = pl.ds(row0, c)
        for half in range(2):
            cs = slice(p * PAIR + half * hd, p * PAIR + (half + 1) * hd)
            oh = o_s[src, :, cs]
            if zero is not None:
                oh = oh + zero[0:1]
            oh = oh * lax.rsqrt(jnp.mean(oh * oh, axis=-1, keepdims=True) + NORM_EPS) * dnw
            y_ref[rows, cs] = (oh * z_ref[rows, cs].astype(F32)).astype(y_ref.dtype)
        r0 = pl.multiple_of((row0 >> 24) * (2 * SUBLANES), 2 * SUBLANES)
        back = y_ref[pl.ds(r0, 2 * SUBLANES), p * PAIR:(p + 1) * PAIR].astype(F32)[0:SUBLANES]
        return back[:, :hd] + back[:, hd:]

    _delta_rule_core(read_rows, read_gates, n_chunks, state_ref, qn_s, kn_s, gsm_s, gl_s, gp_s, gr_s,
                     finish=finish, o_s=o_s)


def _deltanet_meta_kernel(mqkv_ref, mgates_ref, s_out_ref, state_ref, qn_s, kn_s, gsm_s, gl_s, gp_s, gr_s):
    c = CHUNK
    state_ref[...] = jnp.zeros(state_ref.shape, F32)

    def rows_at(pos):
        return pl.ds(pl.multiple_of(jnp.minimum(pos, 1) * c, c), c)

    def read_rows(idx, pos, cols):
        return mqkv_ref[rows_at(pos), slice(idx * DN_WIDTH + cols.start, idx * DN_WIDTH + cols.stop)]

    def read_gates(pos):
        return mgates_ref[rows_at(pos), :]

    _delta_rule_core(read_rows, read_gates, 2, state_ref, qn_s, kn_s, gsm_s, gl_s, gp_s, gr_s)
    s_out_ref[...] = state_ref[...]


def _lookahead_scratch():
    c = CHUNK
    return [
        pltpu.VMEM((4, c, DN_WIDTH), F32),
        pltpu.VMEM((4, c, DN_WIDTH), F32),
        pltpu.VMEM((4, 3, c, LANES), F32),
        pltpu.VMEM((4, DN_HEADS // 2, c, 2 * c), F32),
        pltpu.VMEM((4, DN_HEADS // 2, c, 2 * c), F32),
        pltpu.VMEM((4, DN_HEADS // 2, SUBLANES, 2 * c), F32),
    ]


def _deltanet_meta(meta_qkv, meta_gates):
    state_shape = (DN_HEADS, DN_HEAD_DIM, DN_HEAD_DIM)
    qn, kn, gsm, gl, gp, gr = _lookahead_scratch()
    return pl.pallas_call(
        _deltanet_meta_kernel,
        out_shape=jax.ShapeDtypeStruct(state_shape, F32),
        scratch_shapes=[pltpu.VMEM(state_shape, F32), qn, kn, gsm, gl, gp, gr],
        compiler_params=pltpu.CompilerParams(vmem_limit_bytes=VMEM_LIMIT),
        name="deltanet_meta",
    )(meta_qkv, meta_gates)


def _deltanet(proj, gates, state0, dn_norm_w, *, batch, seq_len):
    tt = TT_DN
    nt = seq_len // tt
    c = CHUNK
    kern = functools.partial(_deltanet_kernel, tt=tt)
    row = lambda b, t: b * nt + t
    const = lambda shape: pl.BlockSpec(shape, lambda b, t: (0,) * len(shape), pipeline_mode=pl.Buffered(1))
    state_shape = (DN_HEADS, DN_HEAD_DIM, DN_HEAD_DIM)
    qn, kn, gsm, gl, gp, gr = _lookahead_scratch()
    return pl.pallas_call(
        kern,
        out_shape=jax.ShapeDtypeStruct((batch * seq_len, DN_WIDTH), BF16),
        grid=(batch, nt),
        in_specs=[
            pl.BlockSpec((tt, DN_WIDTH), lambda b, t: (row(b, t), COL_Q // DN_WIDTH)),
            pl.BlockSpec((tt, DN_WIDTH), lambda b, t: (row(b, t), COL_K // DN_WIDTH)),
            pl.BlockSpec((tt, DN_WIDTH), lambda b, t: (row(b, t), COL_V // DN_WIDTH)),
            pl.BlockSpec((tt, DN_WIDTH), lambda b, t: (row(b, t), COL_ZD // DN_WIDTH)),
            pl.BlockSpec((tt, LANES), lambda b, t: (row(b, t), 0)),
            const(state_shape),
            const((1, DN_HEAD_DIM)),
        ],
        out_specs=pl.BlockSpec((tt, DN_WIDTH), lambda b, t: (row(b, t), 0)),
        scratch_shapes=[
            pltpu.VMEM(state_shape, F32),
            qn, kn,
            pltpu.VMEM((2, c, DN_WIDTH), F32),
            gsm, gl, gp, gr,
        ],
        compiler_params=pltpu.CompilerParams(
            dimension_semantics=("arbitrary", "arbitrary"), vmem_limit_bytes=VMEM_LIMIT),
        name="deltanet",
    )(proj, proj, proj, proj, gates, state0, dn_norm_w)


def _out_merge_kernel(u_ref, zp_ref, gpa_ref, gpb_ref, gda_ref, gdb_ref, ydn_ref, x_ref, mu_ref, mix_ref,
                      scale_ref, wpo_ref, wdo_ref, wo_ref, fnw_ref, out_ref, ubuf_ref, *, tm):
    t = pl.program_id(1)
    hist = N_META

    @pl.when(t == 0)
    def _():
        ubuf_ref[0:hist, :] = mu_ref[...].astype(F32)

    @pl.when(t > 0)
    def _():
        ubuf_ref[0:hist, :] = ubuf_ref[tm:tm + hist, :]

    ubuf_ref[hist:hist + tm, :] = u_ref[...].astype(F32)

    pooled = []
    for gi, w in enumerate(POOL_WINDOWS):
        cs = slice(gi * POOL_GROUP_DIM, (gi + 1) * POOL_GROUP_DIM)
        ext = ubuf_ref[:, cs]
        acc = ext
        span = 1
        while span < w:
            acc = acc + pltpu.roll(acc, span, axis=0)
            span *= 2
        d = acc[hist:] * (1.0 / w) - ext[hist:]
        mixed = _mm(d, mix_ref[gi])
        zp = zp_ref[:, cs].astype(F32)
        pooled.append((mixed * scale_ref[:, cs] * zp).astype(BF16))
    y_pool = jnp.concatenate(pooled, axis=1)

    frame = slice(GATE_SHIFT, GATE_SHIFT + D_MODEL)
    gp = jnp.concatenate([gpa_ref[...], gpb_ref[...]], axis=1).astype(F32)[:, frame]
    gd = jnp.concatenate([gda_ref[...], gdb_ref[...]], axis=1).astype(F32)[:, frame]
    t_pool = jnp.dot(y_pool, wpo_ref[...], preferred_element_type=F32)
    t_dn = jnp.dot(ydn_ref[...], wdo_ref[...], preferred_element_type=F32)
    merged = _sigmoid(gp) * t_pool + _sigmoid(gd) * t_dn
    h = x_ref[...] + jnp.dot(merged.astype(BF16), wo_ref[...], preferred_element_type=F32)
    ms = jnp.mean(h * h, axis=-1, keepdims=True)
    out_ref[...] = h * lax.rsqrt(ms + NORM_EPS) * fnw_ref[...]


def _resident(shape, index_map):
    return pl.BlockSpec(shape, index_map, pipeline_mode=pl.Buffered(1))


def _out_merge(proj, proj_tail, y_dn, x2d, meta_u, mix, scale, wpo, wdo, wo, fnw, *, batch, seq_len):
    tm = TM_OUT
    tiles = seq_len // tm
    kern = functools.partial(_out_merge_kernel, tm=tm)
    row = lambda b, t: b * tiles + t
    gp0, gd0 = COL_GP - GATE_SHIFT, COL_GD - GATE_SHIFT
    assert gp0 % D_MODEL == 0 and gd0 % D_MODEL == 0
    assert gd0 + D_MODEL == proj.shape[1]
    return pl.pallas_call(
        kern,
        out_shape=jax.ShapeDtypeStruct((batch * seq_len, D_MODEL), F32),
        grid=(batch, tiles),
        in_specs=[
            pl.BlockSpec((tm, POOL_WIDTH), lambda b, t: (row(b, t), COL_U // POOL_WIDTH)),
            pl.BlockSpec((tm, POOL_WIDTH), lambda b, t: (row(b, t), COL_ZP // POOL_WIDTH)),
            pl.BlockSpec((tm, D_MODEL), lambda b, t: (row(b, t), gp0 // D_MODEL)),
            pl.BlockSpec((tm, LANES), lambda b, t: (row(b, t), (gp0 + D_MODEL) // LANES)),
            pl.BlockSpec((tm, D_MODEL), lambda b, t: (row(b, t), gd0 // D_MODEL)),
            pl.BlockSpec((tm, LANES), lambda b, t: (row(b, t), 0)),
            pl.BlockSpec((tm, DN_WIDTH), lambda b, t: (row(b, t), 0)),
            pl.BlockSpec((tm, D_MODEL), lambda b, t: (row(b, t), 0)),
            _resident((N_META, POOL_WIDTH), lambda b, t: (0, 0)),
            _resident((POOL_GROUPS, POOL_GROUP_DIM, POOL_GROUP_DIM), lambda b, t: (0, 0, 0)),
            _resident((1, POOL_WIDTH), lambda b, t: (0, 0)),
            _resident((POOL_WIDTH, D_MODEL), lambda b, t: (0, 0)),
            _resident((DN_WIDTH, D_MODEL), lambda b, t: (0, 0)),
            _resident((D_MODEL, D_MODEL), lambda b, t: (0, 0)),
            _resident((1, D_MODEL), lambda b, t: (0, 0)),
        ],
        out_specs=pl.BlockSpec((tm, D_MODEL), lambda b, t: (row(b, t), 0)),
        scratch_shapes=[pltpu.VMEM((N_META + tm, POOL_WIDTH), F32)],
        compiler_params=pltpu.CompilerParams(
            dimension_semantics=("arbitrary", "arbitrary"), vmem_limit_bytes=VMEM_LIMIT),
        name="out_merge",
    )(proj, proj, proj, proj, proj, proj_tail, y_dn, x2d, meta_u, mix, scale, wpo, wdo, wo, fnw)


def kernel(x, meta_tokens, norm_w, w_in, conv_w, A_log, dt_bias, pool_mix, pool_scale, dn_norm_w,
           w_pool_out, w_dn_out, w_o, final_norm_w):
    batch, seq_len, _ = x.shape
    assert norm_w.shape[0] == 1, "single layer block"
    x2d = x.reshape(batch * seq_len, D_MODEL)

    rows = batch * seq_len
    w_t = jnp.swapaxes(w_in, 1, 2).reshape(w_in.shape[2], D_MODEL)
    gpar = jnp.zeros((2, LANES), F32)
    gpar = gpar.at[0, DN_HEADS:2 * DN_HEADS].set(A_log[0]).at[1, DN_HEADS:2 * DN_HEADS].set(dt_bias[0])
    nw = norm_w[0].reshape(1, D_MODEL)

    xn_all, gates, proj_tail = _norm_gates(x2d, meta_tokens.astype(F32), nw, w_t, gpar)
    proj, wpo, wdo, wo = _in_proj(xn_all, w_t, conv_w[0], (w_pool_out[0], w_dn_out[0], w_o[0]), seq_len=seq_len)

    front = 2 * CHUNK - N_META
    meta_proj = proj[rows:]
    meta_qkv = jnp.pad(meta_proj[:, COL_Q:COL_ZD], ((front, 0), (0, 0)))
    meta_gates = jnp.pad(gates[rows:], ((front, 0), (0, 0)))
    meta_u = meta_proj[:, COL_U:COL_U + POOL_WIDTH]

    state0 = _deltanet_meta(meta_qkv, meta_gates)
    y_dn = _deltanet(proj, gates, state0, dn_norm_w[0].reshape(1, DN_HEAD_DIM), batch=batch, seq_len=seq_len)

    out = _out_merge(proj, proj_tail, y_dn, x2d, meta_u, pool_mix[0].astype(BF16), pool_scale[0].reshape(1, POOL_WIDTH),
                     wpo, wdo, wo, final_norm_w.reshape(1, D_MODEL), batch=batch, seq_len=seq_len)
    return out.reshape(batch, seq_len, D_MODEL)
```

```python
import functools

import jax
import jax.numpy as jnp
from jax import lax
from jax.experimental import pallas as pl
from jax.experimental.pallas import tpu as pltpu

D_MODEL = 2048
N_META = 16
POOL_GROUPS = 4
POOL_WINDOWS = (2, 4, 8, 16)
POOL_WIDTH = D_MODEL // 2
POOL_GROUP_DIM = POOL_WIDTH // POOL_GROUPS
DN_HEADS = 16
DN_HEAD_DIM = 128
DN_WIDTH = DN_HEADS * DN_HEAD_DIM
CONV_WIDTH = 4
NORM_EPS = 1e-6

COL_U = 0
COL_ZP = COL_U + POOL_WIDTH
COL_Q = COL_ZP + POOL_WIDTH
COL_K = COL_Q + DN_WIDTH
COL_V = COL_K + DN_WIDTH
COL_ZD = COL_V + DN_WIDTH
BA_OFFSET = COL_ZD + DN_WIDTH
COL_GP = BA_OFFSET + 2 * DN_HEADS
COL_GD = COL_GP + D_MODEL
LANES = 128
SUBLANES = 8
GATE_SHIFT = COL_GP % LANES

PROJ_DTYPE = jnp.bfloat16
CHUNK = 64
LEAF = 16
TT_DN = 1024
TM_NORM = 1024
TN_PROJ = 256
ROW_BLK_PROJ = 1024
W_CAST_SLABS = 32
TM_OUT = 256
VMEM_LIMIT = 58 * 1024 * 1024

BF16 = jnp.bfloat16
F32 = jnp.float32


def _mm(a, b):
    return jnp.dot(a.astype(BF16), b.astype(BF16), preferred_element_type=F32)


def _mm_nt(a, b):
    return lax.dot_general(a.astype(BF16), b.astype(BF16), (((1,), (1,)), ((), ())),
                           preferred_element_type=F32)


def _mm_tn(a, b):
    return lax.dot_general(a.astype(BF16), b.astype(BF16), (((0,), (0,)), ((), ())),
                           preferred_element_type=F32)


def _sigmoid(x):
    return 0.5 * jnp.tanh(0.5 * x) + 0.5


def _silu(x):
    h = 0.5 * x
    return h * jnp.tanh(h) + h


def _softplus(x):
    return jnp.maximum(x, 0.0) + jnp.log1p(jnp.exp(-jnp.abs(x)))


def _norm_gates_kernel(x_ref, meta_ref, nw_ref, wba_ref, wtail_ref, gpar_ref, xn_ref, gates_ref, tail_ref,
                       *, x_tiles, sub):
    i = pl.program_id(0)
    nw = nw_ref[...]
    wtail = wtail_ref[...]
    wboth = jnp.concatenate(
        [wba_ref[...], wtail, jnp.zeros((LANES - wtail.shape[0], D_MODEL), F32)], axis=0).astype(BF16)
    a_rate = jnp.exp(gpar_ref[0:1, :])
    dt_bias = gpar_ref[1:2, :]

    def rows_block(x, rows):
        ms = jnp.mean(x * x, axis=-1, keepdims=True)
        xn = (x * lax.rsqrt(ms + NORM_EPS) * nw).astype(BF16)
        xn_ref[rows, :] = xn
        both = _mm_nt(xn, wboth)
        ba = both[:, :LANES]
        tail_ref[rows, :] = both[:, LANES:].astype(tail_ref.dtype)
        lane = lax.broadcasted_iota(jnp.int32, ba.shape, 1)
        g = -a_rate * _softplus(ba + dt_bias)
        gates_ref[rows, :] = jnp.where(lane < DN_HEADS, _sigmoid(ba), jnp.where(lane < 2 * DN_HEADS, g, 0.0))

    @pl.when(i < x_tiles)
    def _():
        def body(r, carry):
            rows = pl.ds(pl.multiple_of(r * sub, sub), sub)
            rows_block(x_ref[rows, :], rows)
            return carry

        lax.fori_loop(0, x_ref.shape[0] // sub, body, 0)

    @pl.when(i == x_tiles)
    def _():
        rows_block(meta_ref[...], pl.ds(0, N_META))


def _norm_gates(x2d, meta, norm_w, w_t, gpar):
    rows = x2d.shape[0]
    tm = TM_NORM
    tail_start = w_t.shape[0] // TN_PROJ * TN_PROJ
    tail_cols = w_t.shape[0] - tail_start
    assert tail_start % LANES == 0 and 0 < tail_cols <= LANES
    assert tail_cols % SUBLANES == 0 and tail_start % tail_cols == 0
    x_tiles = rows // tm
    kern = functools.partial(_norm_gates_kernel, x_tiles=x_tiles, sub=512)
    return pl.pallas_call(
        kern,
        out_shape=(jax.ShapeDtypeStruct((rows + N_META, D_MODEL), BF16),
                   jax.ShapeDtypeStruct((rows + N_META, LANES), F32),
                   jax.ShapeDtypeStruct((rows + N_META, LANES), PROJ_DTYPE)),
        grid=(x_tiles + 1,),
        in_specs=[
            pl.BlockSpec((tm, D_MODEL), lambda i: (jnp.minimum(i, x_tiles - 1), 0)),
            pl.BlockSpec((N_META, D_MODEL), lambda i: (0, 0)),
            pl.BlockSpec((1, D_MODEL), lambda i: (0, 0)),
            pl.BlockSpec((LANES, D_MODEL), lambda i: (BA_OFFSET // LANES, 0)),
            pl.BlockSpec((tail_cols, D_MODEL), lambda i: (tail_start // tail_cols, 0)),
            pl.BlockSpec((2, LANES), lambda i: (0, 0)),
        ],
        out_specs=(pl.BlockSpec((tm, D_MODEL), lambda i: (i, 0)),
                   pl.BlockSpec((tm, LANES), lambda i: (i, 0)),
                   pl.BlockSpec((tm, LANES), lambda i: (i, 0))),
        compiler_params=pltpu.CompilerParams(
            dimension_semantics=("arbitrary",), vmem_limit_bytes=VMEM_LIMIT),
        name="norm_gates",
    )(x2d, meta, norm_w, w_t, w_t, gpar)


def _in_proj_kernel(xn_ref, w_ref, cw_ref, wpo_ref, wdo_ref, wo_ref,
                    proj_ref, wpo_bf_ref, wdo_bf_ref, wo_bf_ref, acc_ref, *, rows, row_blk, seq_len):
    j = pl.program_id(0)
    tn = w_ref.shape[0]
    ms = slice(rows, rows + N_META)
    conv_tile = (j >= COL_Q // tn) & (j < COL_ZD // tn)
    silu_tile = ((j >= COL_ZP // tn) & (j < COL_Q // tn)) | ((j >= COL_ZD // tn) & (j < BA_OFFSET // tn))

    def cast_weights():
        for src, dst in ((wpo_ref, wpo_bf_ref), (wdo_ref, wdo_bf_ref), (wo_ref, wo_bf_ref)):
            dst[...] = src[...].astype(BF16)
        return w_ref[...].astype(BF16)

    def plain(fn):
        w = cast_weights()
        for r in range(rows // row_blk):
            rs = slice(r * row_blk, (r + 1) * row_blk)
            proj_ref[rs, :] = fn(_mm_nt(xn_ref[rs, :], w)).astype(proj_ref.dtype)
        proj_ref[ms, :] = fn(_mm_nt(xn_ref[ms, :], w)).astype(proj_ref.dtype)

    @pl.when(conv_tile)
    def _():
        w = cast_weights()
        cw = cw_ref[...]
        hist_rows = SUBLANES
        n_blk = rows // row_blk

        def conv_silu(slot, n):
            y = acc_ref[slot, hist_rows:hist_rows + n, :] * cw[CONV_WIDTH - 1:CONV_WIDTH]
            for kk in range(CONV_WIDTH - 1):
                off = hist_rows - (CONV_WIDTH - 1) + kk
                y = y + acc_ref[slot, off:off + n, :] * cw[kk:kk + 1]
            return _silu(y)

        meta_acc = _mm_nt(xn_ref[ms, :], w)
        acc_ref[0, 0:hist_rows, :] = jnp.zeros((hist_rows, tn), F32)
        acc_ref[0, hist_rows:hist_rows + N_META, :] = meta_acc
        proj_ref[ms, :] = conv_silu(0, N_META).astype(proj_ref.dtype)
        meta_tail = meta_acc[N_META - hist_rows:]
        for r in range(n_blk + 1):
            if r < n_blk:
                slot = r % 2
                rs = slice(r * row_blk, (r + 1) * row_blk)
                acc = _mm_nt(xn_ref[rs, :], w)
                batch_start = (r * row_blk) % seq_len == 0
                acc_ref[slot, 0:hist_rows, :] = (
                    meta_tail if batch_start else acc_ref[1 - slot, row_blk:row_blk + hist_rows, :])
                acc_ref[slot, hist_rows:hist_rows + row_blk, :] = acc
            if r > 0:
                ps = slice((r - 1) * row_blk, r * row_blk)
                proj_ref[ps, :] = conv_silu((r - 1) % 2, row_blk).astype(proj_ref.dtype)

    @pl.when(silu_tile)
    def _():
        plain(_silu)

    @pl.when(jnp.logical_not(conv_tile | silu_tile))
    def _():
        plain(lambda a: a)


def _in_proj(xn_all, w_t, conv_w, out_weights, *, seq_len):
    rows = xn_all.shape[0] - N_META
    cols = w_t.shape[0]
    tn = TN_PROJ
    n_conv_tiles = (COL_ZD - COL_Q) // tn
    n_steps = cols // tn
    assert W_CAST_SLABS <= n_steps
    slab = lambda wgt: pl.BlockSpec((wgt.shape[0] // W_CAST_SLABS, D_MODEL),
                                    lambda j: (jnp.minimum(j, W_CAST_SLABS - 1), 0))
    kern = functools.partial(_in_proj_kernel, rows=rows, row_blk=ROW_BLK_PROJ, seq_len=seq_len)
    return pl.pallas_call(
        kern,
        out_shape=(jax.ShapeDtypeStruct((rows + N_META, n_steps * tn), PROJ_DTYPE),
                   *[jax.ShapeDtypeStruct(wgt.shape, BF16) for wgt in out_weights]),
        grid=(n_steps,),
        in_specs=[
            pl.BlockSpec((rows + N_META, D_MODEL), lambda j: (0, 0), pipeline_mode=pl.Buffered(1)),
            pl.BlockSpec((tn, D_MODEL), lambda j: (j, 0)),
            pl.BlockSpec((CONV_WIDTH, tn), lambda j: (0, jnp.clip(j - COL_Q // tn, 0, n_conv_tiles - 1))),
            *[slab(wgt) for wgt in out_weights],
        ],
        out_specs=(pl.BlockSpec((rows + N_META, tn), lambda j: (0, j)),
                   *[slab(wgt) for wgt in out_weights]),
        scratch_shapes=[pltpu.VMEM((2, SUBLANES + ROW_BLK_PROJ, tn), F32)],
        compiler_params=pltpu.CompilerParams(
            dimension_semantics=("arbitrary",), vmem_limit_bytes=VMEM_LIMIT),
        name="in_proj",
    )(xn_all, w_t, conv_w, *out_weights)


PAIR = 2 * DN_HEAD_DIM


def _split3(x):
    hi = x.astype(BF16)
    r1 = x - hi.astype(F32)
    mid = r1.astype(BF16)
    lo = (r1 - mid.astype(F32)).astype(BF16)
    return hi, mid, lo


def _block_diag_rows(y, half):
    lane = lax.broadcasted_iota(jnp.int32, y.shape, 1)
    zero = jnp.zeros_like(y)
    return jnp.concatenate([jnp.where(lane < half, y, zero), jnp.where(lane >= half, y, zero)], axis=0)


def _inverse_masks(c):
    row = lax.broadcasted_iota(jnp.int32, (c, 2 * c), 0)
    col = lax.broadcasted_iota(jnp.int32, (c, 2 * c), 1) & (c - 1)
    blk = lambda x, m: x >> (m.bit_length() - 1)
    eye = (row == col).astype(F32)
    same_leaf = blk(row, LEAF) == blk(col, LEAF)
    offs = []
    m = 2 * LEAF
    while m <= c:
        offs.append((blk(row, m) == blk(col, m)) & (blk(row, m // 2) != blk(col, m // 2)))
        m *= 2
    return eye, same_leaf, offs


def _inv_unit_lower_packed(lps, c, masks, fill, n_fill=1):
    eye, same_leaf, offs = masks
    bd = lambda y: _block_diag_rows(y.astype(BF16), c)
    ps = [jnp.where(same_leaf, -l, 0.0) for l in lps]
    ts = [eye + p for p in ps]
    dep = ps[0]
    ps = [_mm(p, bd(p)) for p in ps]
    fill(dep, ps, n_fill + 1)
    span = 4
    while span <= LEAF:
        dep = ps[0]
        last = span == LEAF
        prod = [_mm(t if last else jnp.concatenate([t, p], axis=0), bd(p)) for t, p in zip(ts, ps)]
        ts = [t + x[:c] for t, x in zip(ts, prod)]
        if not last:
            ps = [x[c:] for x in prod]
        fill(dep, ts, n_fill)
        span *= 2
    for off in offs:
        dep = ts[0]
        inner = [_mm(jnp.where(off, l, 0.0), bd(t)) for l, t in zip(lps, ts)]
        fill(dep, inner, n_fill)
        dep = inner[0]
        ts = [t - _mm(t, bd(x)) for t, x in zip(ts, inner)]
        fill(dep, ts, n_fill)
    return ts


def _delta_rule_core(read_rows, read_gates, n_pos, state_ref, qn_s, kn_s, gsm_s, gl_s, gp_s, gr_s,
                     finish=None, o_s=None):
    c = CHUNK
    hd = DN_HEAD_DIM
    n_pairs = DN_HEADS // 2
    pairs = range(n_pairs)

    rowp = lax.broadcasted_iota(jnp.int32, (c, 2 * c), 0)
    colp = lax.broadcasted_iota(jnp.int32, (c, 2 * c), 1) & (c - 1)
    causal = rowp >= colp
    strict = rowp > colp
    lane_p = lax.broadcasted_iota(jnp.int32, (c, 2 * c), 1)
    inv_masks = _inverse_masks(c)
    tril = (lax.broadcasted_iota(jnp.int32, (c, c), 0) >= lax.broadcasted_iota(jnp.int32, (c, c), 1))
    tril3 = jnp.concatenate([tril.astype(BF16)] * 3, axis=1)
    zero_blk = jnp.zeros((hd, hd), BF16)
    head_cols = lambda h: slice(h * hd, (h + 1) * hd)
    pair_cols_of = lambda p: slice(p * PAIR, (p + 1) * PAIR)
    both = (0, 1)
    ha = [2 * p for p in pairs]
    hb = [2 * p + 1 for p in pairs]

    def l2n(xh, scale):
        return xh * (lax.rsqrt(jnp.sum(xh * xh, axis=-1, keepdims=True) + NORM_EPS) * scale)

    def prepare_pair(j, p, zero=None):
        cols = pair_cols_of(p)
        slot = j & 3
        for idx, (dst, scale) in enumerate(((qn_s, hd ** -0.5), (kn_s, 1.0))):
            x = read_rows(idx, j, cols).astype(F32)
            if zero is not None:
                x = x + jnp.concatenate([zero[0:1], zero[0:1]], axis=1)
            dst[slot, :, cols] = jnp.concatenate([l2n(x[:, :hd], scale), l2n(x[:, hd:], scale)], axis=1)
        r0 = pl.multiple_of((j >> 24) * SUBLANES, SUBLANES)
        back = qn_s[slot, pl.ds(r0, SUBLANES), cols] + kn_s[slot, pl.ds(r0, SUBLANES), cols]
        return back[:, :hd] + back[:, hd:]

    def out_rows(j):
        return pl.multiple_of(jnp.maximum(j, 0) * c, c)

    def exact_rows_dot(lhs3, x):
        hi, mid, lo = _split3(x)
        return jnp.dot(lhs3, jnp.concatenate([hi, mid, lo], axis=0), preferred_element_type=F32)

    def pair_cols(mat, la, lb):
        return jnp.where(lane_p < c, mat[:, la:la + 1], mat[:, lb:lb + 1])

    def pair_rows(mat_t, ra, rb):
        return jnp.concatenate([mat_t[ra:ra + 1, :], mat_t[rb:rb + 1, :]], axis=1)

    def zero_of(v):
        return jnp.where(v != v, v, 0.0)

    def prepare_gates(j):
        slot = j & 3
        gt = read_gates(j)
        gcum = exact_rows_dot(tril3, gt)
        g_last = gcum[c - 1:c, :]
        gsm_s[slot, 0] = jnp.exp(gcum)
        gsm_s[slot, 1] = jnp.exp(g_last - gcum)
        gsm_s[slot, 2] = jnp.broadcast_to(jnp.exp(g_last), (c, LANES))
        gt_t = gt.T
        gcum_t = gcum.T
        for p in pairs:
            g_col = pair_cols(gcum, DN_HEADS + ha[p], DN_HEADS + hb[p])
            g_row = pair_rows(gcum_t, DN_HEADS + ha[p], DN_HEADS + hb[p])
            b_col = pair_cols(gt, ha[p], hb[p])
            b_row = pair_rows(gt_t, ha[p], hb[p])
            dec = jnp.where(causal, jnp.exp(jnp.where(causal, g_col - g_row, 0.0)), 0.0)
            gp_s[slot, p] = dec
            gl_s[slot, p] = jnp.where(strict, b_col * dec, 0.0)
            gr_s[slot, p] = jnp.concatenate(
                [b_row, b_row * jnp.exp(g_row), jnp.zeros((SUBLANES - 2, 2 * c), F32)], axis=0)

    def pair_body(i, carry):
        pos = [2 * i, 2 * i + 1]
        slot = [pz & 3 for pz in pos]
        fillers = []
        for p in pairs:
            for x in both:
                if finish is not None:
                    fillers.append(functools.partial(finish, x, out_rows(pos[x] - 2), p))
                fillers.append(functools.partial(prepare_pair, pos[x] + 2, p))
        n_slices = len(fillers)
        fillers.reverse()

        def fill(dep, into, n=1):
            zero = None if dep is None else zero_of(dep[0:SUBLANES, 0:LANES].astype(F32))
            for _ in range(n):
                if fillers:
                    done = zero_of(fillers.pop()(zero=zero))[0:1]
                    if into[-1].shape[1] == PAIR:
                        done = jnp.concatenate([done, done], axis=1)
                    into[-1] = into[-1] + done

        prob = [(x, p) for x in both for p in pairs]
        eg = [gsm_s[slot[x], 0] for x in both]
        e_rest = [gsm_s[slot[x], 1] for x in both]
        e_last = [gsm_s[slot[x], 2, 0:1, :] for x in both]
        gl = [gl_s[slot[x], p] for x, p in prob]
        gp = [gp_s[slot[x], p] for x, p in prob]
        gr = [gr_s[slot[x], p] for x, p in prob]
        kpair = [kn_s[slot[x], :, pair_cols_of(p)].astype(BF16) for x, p in prob]
        kbd = [_block_diag_rows(kp, hd) for kp in kpair]
        qpair = []
        for n, (x, p) in enumerate(prob):
            staged = [qn_s[slot[x], :, pair_cols_of(p)]]
            if n % 2:
                fill(None, staged)
            qpair.append(staged[0].astype(BF16))
        aq = [_mm_nt(jnp.concatenate([kpair[n], qpair[n]], axis=0), kbd[n]) for n in range(len(prob))]
        for x in both:
            prepare_gates(pos[x] + 2)
        lps = [aq[n][:c] * gl[n] for n in range(len(prob))]
        pps = [aq[n][c:] * gp[n] for n in range(len(prob))]
        per_group = max(1, (n_slices - len(prob) // 2) // 12)
        tps = _inv_unit_lower_packed(lps, c, inv_masks, fill, per_group)
        vbd = [_block_diag_rows(read_rows(2, pos[x], pair_cols_of(p)), hd) for x, p in prob]
        u = [_mm(tps[n] * gr[n][0:1], vbd[n]) for n in range(len(prob))]
        w = [_mm(tps[n] * gr[n][1:2], kbd[n]) for n in range(len(prob))]
        fill(tps[0], w, per_group)
        state = [state_ref[h] for h in range(DN_HEADS)]
        outs = []
        for x in both:
            idx = [x * n_pairs + p for p in pairs]
            qd = [jnp.concatenate(
                [qn_s[slot[x], :, head_cols(h)] * eg[x][:, DN_HEADS + h:DN_HEADS + h + 1]
                 for h in (ha[p], hb[p])], axis=1) for p in pairs]
            sbd = [jnp.concatenate([jnp.concatenate([state[ha[p]].astype(BF16), zero_blk], axis=1),
                                    jnp.concatenate([zero_blk, state[hb[p]].astype(BF16)], axis=1)], axis=0)
                   for p in pairs]
            ws = [_mm(jnp.concatenate([w[idx[p]], qd[p]], axis=0), sbd[p]) for p in pairs]
            fill(w[idx[0]], ws, per_group)
            v_new = [u[idx[p]] - ws[p][:c] for p in pairs]
            o = [ws[p][c:] + _mm(pps[idx[p]], _block_diag_rows(v_new[p].astype(BF16), hd)) for p in pairs]
            new_state = []
            for h in range(DN_HEADS):
                p, half = divmod(h, 2)
                lane_h = DN_HEADS + h
                kd = kn_s[slot[x], :, head_cols(h)] * e_rest[x][:, lane_h:lane_h + 1]
                upd = _mm_tn(kd, v_new[p][:, half * hd:(half + 1) * hd])
                new_state.append(state[h] * e_last[x][:, lane_h:lane_h + 1] + upd)
            state = new_state
            if x == 1:
                fill(ws[0], o, len(fillers))
            outs.append(o)
        for h in range(DN_HEADS):
            state_ref[h] = state[h]
        if o_s is not None:
            for x in both:
                for p in pairs:
                    o_s[x, :, pair_cols_of(p)] = outs[x][p]
        return carry

    for x in both:
        for p in pairs:
            prepare_pair(jnp.int32(x), p)
        prepare_gates(jnp.int32(x))
    n_iter = n_pos // 2
    if n_iter == 1:
        pair_body(jnp.int32(0), 0)
    else:
        lax.fori_loop(0, n_iter, pair_body, 0)
    if finish is not None:
        for x in both:
            last_rows = out_rows(jnp.int32(n_pos - 2 + x))
            for p in pairs:
                finish(x, last_rows, p)


def _deltanet_kernel(q_ref, k_ref, v_ref, z_ref, gates_ref, s0_ref, dnw_ref,
                     y_ref, state_ref, qn_s, kn_s, o_s, gsm_s, gl_s, gp_s, gr_s, *, tt):
    c = CHUNK
    hd = DN_HEAD_DIM
    n_chunks = tt // c
    dnw = dnw_ref[...]

    @pl.when(pl.program_id(1) == 0)
    def _():
        state_ref[...] = s0_ref[...]
        o_s[...] = jnp.zeros(o_s.shape, F32)

    srcs = (q_ref, k_ref, v_ref)

    def rows_at(pos):
        return pl.ds(pl.multiple_of(jnp.minimum(pos, n_chunks - 1) * c, c), c)

    def read_rows(idx, pos, cols):
        return srcs[idx][rows_at(pos), cols]

    def read_gates(pos):
        return gates_ref[rows_at(pos), :]

    def finish(src, row0, p, zero=None):
        rows = pl.ds(row0, c)
        for half in range(2):
            cs = slice(p * PAIR + half * hd, p * PAIR + (half + 1) * hd)
            oh = o_s[src, :, cs]
            if zero is not None:
                oh = oh + zero[0:1]
            oh = oh * lax.rsqrt(jnp.mean(oh * oh, axis=-1, keepdims=True) + NORM_EPS) * dnw
            y_ref[rows, cs] = (oh * z_ref[rows, cs].astype(F32)).astype(y_ref.dtype)
        r0 = pl.multiple_of((row0 >> 24) * (2 * SUBLANES), 2 * SUBLANES)
        back = y_ref[pl.ds(r0, 2 * SUBLANES), p * PAIR:(p + 1) * PAIR].astype(F32)[0:SUBLANES]
        return back[:, :hd] + back[:, hd:]

    _delta_rule_core(read_rows, read_gates, n_chunks, state_ref, qn_s, kn_s, gsm_s, gl_s, gp_s, gr_s,
                     finish=finish, o_s=o_s)


def _deltanet_meta_kernel(mqkv_ref, mgates_ref, s_out_ref, state_ref, qn_s, kn_s, gsm_s, gl_s, gp_s, gr_s):
    c = CHUNK
    state_ref[...] = jnp.zeros(state_ref.shape, F32)

    def rows_at(pos):
        return pl.ds(pl.multiple_of(jnp.minimum(pos, 1) * c, c), c)

    def read_rows(idx, pos, cols):
        return mqkv_ref[rows_at(pos), slice(idx * DN_WIDTH + cols.start, idx * DN_WIDTH + cols.stop)]

    def read_gates(pos):
        return mgates_ref[rows_at(pos), :]

    _delta_rule_core(read_rows, read_gates, 2, state_ref, qn_s, kn_s, gsm_s, gl_s, gp_s, gr_s)
    s_out_ref[...] = state_ref[...]


def _lookahead_scratch():
    c = CHUNK
    return [
        pltpu.VMEM((4, c, DN_WIDTH), F32),
        pltpu.VMEM((4, c, DN_WIDTH), F32),
        pltpu.VMEM((4, 3, c, LANES), F32),
        pltpu.VMEM((4, DN_HEADS // 2, c, 2 * c), F32),
        pltpu.VMEM((4, DN_HEADS // 2, c, 2 * c), F32),
        pltpu.VMEM((4, DN_HEADS // 2, SUBLANES, 2 * c), F32),
    ]


def _deltanet_meta(meta_qkv, meta_gates):
    state_shape = (DN_HEADS, DN_HEAD_DIM, DN_HEAD_DIM)
    qn, kn, gsm, gl, gp, gr = _lookahead_scratch()
    return pl.pallas_call(
        _deltanet_meta_kernel,
        out_shape=jax.ShapeDtypeStruct(state_shape, F32),
        scratch_shapes=[pltpu.VMEM(state_shape, F32), qn, kn, gsm, gl, gp, gr],
        compiler_params=pltpu.CompilerParams(vmem_limit_bytes=VMEM_LIMIT),
        name="deltanet_meta",
    )(meta_qkv, meta_gates)


def _deltanet(proj, gates, state0, dn_norm_w, *, batch, seq_len):
    tt = TT_DN
    nt = seq_len // tt
    c = CHUNK
    kern = functools.partial(_deltanet_kernel, tt=tt)
    row = lambda b, t: b * nt + t
    const = lambda shape: pl.BlockSpec(shape, lambda b, t: (0,) * len(shape), pipeline_mode=pl.Buffered(1))
    state_shape = (DN_HEADS, DN_HEAD_DIM, DN_HEAD_DIM)
    qn, kn, gsm, gl, gp, gr = _lookahead_scratch()
    return pl.pallas_call(
        kern,
        out_shape=jax.ShapeDtypeStruct((batch * seq_len, DN_WIDTH), BF16),
        grid=(batch, nt),
        in_specs=[
            pl.BlockSpec((tt, DN_WIDTH), lambda b, t: (row(b, t), COL_Q // DN_WIDTH)),
            pl.BlockSpec((tt, DN_WIDTH), lambda b, t: (row(b, t), COL_K // DN_WIDTH)),
            pl.BlockSpec((tt, DN_WIDTH), lambda b, t: (row(b, t), COL_V // DN_WIDTH)),
            pl.BlockSpec((tt, DN_WIDTH), lambda b, t: (row(b, t), COL_ZD // DN_WIDTH)),
            pl.BlockSpec((tt, LANES), lambda b, t: (row(b, t), 0)),
            const(state_shape),
            const((1, DN_HEAD_DIM)),
        ],
        out_specs=pl.BlockSpec((tt, DN_WIDTH), lambda b, t: (row(b, t), 0)),
        scratch_shapes=[
            pltpu.VMEM(state_shape, F32),
            qn, kn,
            pltpu.VMEM((2, c, DN_WIDTH), F32),
            gsm, gl, gp, gr,
        ],
        compiler_params=pltpu.CompilerParams(
            dimension_semantics=("arbitrary", "arbitrary"), vmem_limit_bytes=VMEM_LIMIT),
        name="deltanet",
    )(proj, proj, proj, proj, gates, state0, dn_norm_w)


def _out_merge_kernel(u_ref, zp_ref, un_ref, zpn_ref, gpa_ref, gpb_ref, gda_ref, gdb_ref, ydn_ref, x_ref, mu_ref,
                      mix_ref, scale_ref, wpo_ref, wdo_ref, wo_ref, fnw_ref, out_ref, ubuf_ref, ypool_ref,
                      *, tm, tiles):
    g = pl.program_id(0) * tiles + pl.program_id(1)
    hist = N_META

    def pool_group(gi, zp_blk_ref, slot):
        w = POOL_WINDOWS[gi]
        cs = slice(gi * POOL_GROUP_DIM, (gi + 1) * POOL_GROUP_DIM)
        ext = ubuf_ref[:, cs]
        acc = ext
        span = 1
        while span < w:
            acc = acc + pltpu.roll(acc, span, axis=0)
            span *= 2
        d = acc[hist:] * (1.0 / w) - ext[hist:]
        mixed = _mm(d, mix_ref[gi])
        zp = zp_blk_ref[:, cs].astype(F32)
        ypool_ref[slot, :, cs] = (mixed * scale_ref[:, cs] * zp).astype(BF16)

    @pl.when(g == 0)
    def _():
        ubuf_ref[0:hist, :] = mu_ref[...].astype(F32)
        ubuf_ref[hist:hist + tm, :] = u_ref[...].astype(F32)
        for gi in range(POOL_GROUPS):
            pool_group(gi, zp_ref, 0)

    y_pool = ypool_ref[g % 2]

    frame = slice(GATE_SHIFT, GATE_SHIFT + D_MODEL)
    gp = jnp.concatenate([gpa_ref[...], gpb_ref[...]], axis=1).astype(F32)[:, frame]
    gd = jnp.concatenate([gda_ref[...], gdb_ref[...]], axis=1).astype(F32)[:, frame]
    t_pool = jnp.dot(y_pool, wpo_ref[...], preferred_element_type=F32)
    batch_start = (g + 1) % tiles == 0
    ubuf_ref[0:hist, :] = jnp.where(batch_start, mu_ref[...].astype(F32), ubuf_ref[tm:tm + hist, :])
    ubuf_ref[hist:hist + tm, :] = un_ref[...].astype(F32)
    ydn = ydn_ref[...]
    dn_blk = D_MODEL // POOL_GROUPS
    t_dn = []
    for gi in range(POOL_GROUPS):
        pool_group(gi, zpn_ref, (g + 1) % 2)
        t_dn.append(jnp.dot(ydn, wdo_ref[:, gi * dn_blk:(gi + 1) * dn_blk], preferred_element_type=F32))
    t_dn = jnp.concatenate(t_dn, axis=1)
    merged = _sigmoid(gp) * t_pool + _sigmoid(gd) * t_dn
    h = x_ref[...] + jnp.dot(merged.astype(BF16), wo_ref[...], preferred_element_type=F32)
    ms = jnp.mean(h * h, axis=-1, keepdims=True)
    out_ref[...] = h * lax.rsqrt(ms + NORM_EPS) * fnw_ref[...]


def _resident(shape, index_map):
    return pl.BlockSpec(shape, index_map, pipeline_mode=pl.Buffered(1))


def _out_merge(proj, proj_tail, y_dn, x2d, meta_u, mix, scale, wpo, wdo, wo, fnw, *, batch, seq_len):
    tm = TM_OUT
    tiles = seq_len // tm
    kern = functools.partial(_out_merge_kernel, tm=tm, tiles=tiles)
    row = lambda b, t: b * tiles + t
    nxt = lambda b, t: jnp.minimum(row(b, t) + 1, batch * tiles - 1)
    gp0, gd0 = COL_GP - GATE_SHIFT, COL_GD - GATE_SHIFT
    assert gp0 % D_MODEL == 0 and gd0 % D_MODEL == 0
    assert gd0 + D_MODEL == proj.shape[1]
    return pl.pallas_call(
        kern,
        out_shape=jax.ShapeDtypeStruct((batch * seq_len, D_MODEL), F32),
        grid=(batch, tiles),
        in_specs=[
            pl.BlockSpec((tm, POOL_WIDTH), lambda b, t: (row(b, t), COL_U // POOL_WIDTH)),
            pl.BlockSpec((tm, POOL_WIDTH), lambda b, t: (row(b, t), COL_ZP // POOL_WIDTH)),
            pl.BlockSpec((tm, POOL_WIDTH), lambda b, t: (nxt(b, t), COL_U // POOL_WIDTH)),
            pl.BlockSpec((tm, POOL_WIDTH), lambda b, t: (nxt(b, t), COL_ZP // POOL_WIDTH)),
            pl.BlockSpec((tm, D_MODEL), lambda b, t: (row(b, t), gp0 // D_MODEL)),
            pl.BlockSpec((tm, LANES), lambda b, t: (row(b, t), (gp0 + D_MODEL) // LANES)),
            pl.BlockSpec((tm, D_MODEL), lambda b, t: (row(b, t), gd0 // D_MODEL)),
            pl.BlockSpec((tm, LANES), lambda b, t: (row(b, t), 0)),
            pl.BlockSpec((tm, DN_WIDTH), lambda b, t: (row(b, t), 0)),
            pl.BlockSpec((tm, D_MODEL), lambda b, t: (row(b, t), 0)),
            _resident((N_META, POOL_WIDTH), lambda b, t: (0, 0)),
            _resident((POOL_GROUPS, POOL_GROUP_DIM, POOL_GROUP_DIM), lambda b, t: (0, 0, 0)),
            _resident((1, POOL_WIDTH), lambda b, t: (0, 0)),
            _resident((POOL_WIDTH, D_MODEL), lambda b, t: (0, 0)),
            _resident((DN_WIDTH, D_MODEL), lambda b, t: (0, 0)),
            _resident((D_MODEL, D_MODEL), lambda b, t: (0, 0)),
            _resident((1, D_MODEL), lambda b, t: (0, 0)),
        ],
        out_specs=pl.BlockSpec((tm, D_MODEL), lambda b, t: (row(b, t), 0)),
        scratch_shapes=[pltpu.VMEM((N_META + tm, POOL_WIDTH), F32),
                        pltpu.VMEM((2, tm, POOL_WIDTH), BF16)],
        compiler_params=pltpu.CompilerParams(
            dimension_semantics=("arbitrary", "arbitrary"), vmem_limit_bytes=VMEM_LIMIT),
        name="out_merge",
    )(proj, proj, proj, proj, proj, proj, proj, proj_tail, y_dn, x2d, meta_u, mix, scale, wpo, wdo, wo, fnw)


def kernel(x, meta_tokens, norm_w, w_in, conv_w, A_log, dt_bias, pool_mix, pool_scale, dn_norm_w,
           w_pool_out, w_dn_out, w_o, final_norm_w):
    batch, seq_len, _ = x.shape
    assert norm_w.shape[0] == 1, "single layer block"
    x2d = x.reshape(batch * seq_len, D_MODEL)

    rows = batch * seq_len
    w_t = jnp.swapaxes(w_in, 1, 2).reshape(w_in.shape[2], D_MODEL)
    gpar = jnp.zeros((2, LANES), F32)
    gpar = gpar.at[0, DN_HEADS:2 * DN_HEADS].set(A_log[0]).at[1, DN_HEADS:2 * DN_HEADS].set(dt_bias[0])
    nw = norm_w[0].reshape(1, D_MODEL)

    xn_all, gates, proj_tail = _norm_gates(x2d, meta_tokens.astype(F32), nw, w_t, gpar)
    proj, wpo, wdo, wo = _in_proj(xn_all, w_t, conv_w[0], (w_pool_out[0], w_dn_out[0], w_o[0]), seq_len=seq_len)

    front = 2 * CHUNK - N_META
    meta_proj = proj[rows:]
    meta_qkv = jnp.pad(meta_proj[:, COL_Q:COL_ZD], ((front, 0), (0, 0)))
    meta_gates = jnp.pad(gates[rows:], ((front, 0), (0, 0)))
    meta_u = meta_proj[:, COL_U:COL_U + POOL_WIDTH]

    state0 = _deltanet_meta(meta_qkv, meta_gates)
    y_dn = _deltanet(proj, gates, state0, dn_norm_w[0].reshape(1, DN_HEAD_DIM), batch=batch, seq_len=seq_len)

    out = _out_merge(proj, proj_tail, y_dn, x2d, meta_u, pool_mix[0].astype(BF16), pool_scale[0].reshape(1, POOL_WIDTH),
                     wpo, wdo, wo, final_norm_w.reshape(1, D_MODEL), batch=batch, seq_len=seq_len)
    return out.reshape(batch, seq_len, D_MODEL)
```

```python
import functools

import jax
import jax.numpy as jnp
from jax import lax
from jax.experimental import pallas as pl
from jax.experimental.pallas import tpu as pltpu

D_MODEL = 2048
N_META = 16
POOL_GROUPS = 4
POOL_WINDOWS = (2, 4, 8, 16)
POOL_WIDTH = D_MODEL // 2
POOL_GROUP_DIM = POOL_WIDTH // POOL_GROUPS
DN_HEADS = 16
DN_HEAD_DIM = 128
DN_WIDTH = DN_HEADS * DN_HEAD_DIM
CONV_WIDTH = 4
NORM_EPS = 1e-6

COL_U = 0
COL_ZP = COL_U + POOL_WIDTH
COL_Q = COL_ZP + POOL_WIDTH
COL_K = COL_Q + DN_WIDTH
COL_V = COL_K + DN_WIDTH
COL_ZD = COL_V + DN_WIDTH
BA_OFFSET = COL_ZD + DN_WIDTH
COL_GP = BA_OFFSET + 2 * DN_HEADS
COL_GD = COL_GP + D_MODEL
LANES = 128
SUBLANES = 8
GATE_SHIFT = COL_GP % LANES

PROJ_DTYPE = jnp.bfloat16
CHUNK = 64
LEAF = 16
TT_DN = 1024
TM_NORM = 1024
TN_PROJ = 256
ROW_BLK_PROJ = 1024
W_CAST_SLABS = 32
TM_OUT = 256
VMEM_LIMIT = 58 * 1024 * 1024

BF16 = jnp.bfloat16
F32 = jnp.float32


def _mm(a, b):
    return jnp.dot(a.astype(BF16), b.astype(BF16), preferred_element_type=F32)


def _mm_nt(a, b):
    return lax.dot_general(a.astype(BF16), b.astype(BF16), (((1,), (1,)), ((), ())),
                           preferred_element_type=F32)


def _mm_tn(a, b):
    return lax.dot_general(a.astype(BF16), b.astype(BF16), (((0,), (0,)), ((), ())),
                           preferred_element_type=F32)


def _sigmoid(x):
    return 0.5 * jnp.tanh(0.5 * x) + 0.5


def _silu(x):
    h = 0.5 * x
    return h * jnp.tanh(h) + h


def _softplus(x):
    return jnp.maximum(x, 0.0) + jnp.log1p(jnp.exp(-jnp.abs(x)))


def _norm_gates_kernel(x_ref, meta_ref, nw_ref, wba_ref, wtail_ref, gpar_ref, xn_ref, gates_ref, tail_ref,
                       *, x_tiles, sub):
    i = pl.program_id(0)
    nw = nw_ref[...]
    wtail = wtail_ref[...]
    wboth = jnp.concatenate(
        [wba_ref[...], wtail, jnp.zeros((LANES - wtail.shape[0], D_MODEL), F32)], axis=0).astype(BF16)
    a_rate = jnp.exp(gpar_ref[0:1, :])
    dt_bias = gpar_ref[1:2, :]

    def rows_block(x, rows):
        ms = jnp.mean(x * x, axis=-1, keepdims=True)
        xn = (x * lax.rsqrt(ms + NORM_EPS) * nw).astype(BF16)
        xn_ref[rows, :] = xn
        both = _mm_nt(xn, wboth)
        ba = both[:, :LANES]
        tail_ref[rows, :] = both[:, LANES:].astype(tail_ref.dtype)
        lane = lax.broadcasted_iota(jnp.int32, ba.shape, 1)
        g = -a_rate * _softplus(ba + dt_bias)
        gates_ref[rows, :] = jnp.where(lane < DN_HEADS, _sigmoid(ba), jnp.where(lane < 2 * DN_HEADS, g, 0.0))

    @pl.when(i < x_tiles)
    def _():
        def body(r, carry):
            rows = pl.ds(pl.multiple_of(r * sub, sub), sub)
            rows_block(x_ref[rows, :], rows)
            return carry

        lax.fori_loop(0, x_ref.shape[0] // sub, body, 0)

    @pl.when(i == x_tiles)
    def _():
        rows_block(meta_ref[...], pl.ds(0, N_META))


def _norm_gates(x2d, meta, norm_w, w_t, gpar):
    rows = x2d.shape[0]
    tm = TM_NORM
    tail_start = w_t.shape[0] // TN_PROJ * TN_PROJ
    tail_cols = w_t.shape[0] - tail_start
    assert tail_start % LANES == 0 and 0 < tail_cols <= LANES
    assert tail_cols % SUBLANES == 0 and tail_start % tail_cols == 0
    x_tiles = rows // tm
    kern = functools.partial(_norm_gates_kernel, x_tiles=x_tiles, sub=512)
    return pl.pallas_call(
        kern,
        out_shape=(jax.ShapeDtypeStruct((rows + N_META, D_MODEL), BF16),
                   jax.ShapeDtypeStruct((rows + N_META, LANES), F32),
                   jax.ShapeDtypeStruct((rows + N_META, LANES), PROJ_DTYPE)),
        grid=(x_tiles + 1,),
        in_specs=[
            pl.BlockSpec((tm, D_MODEL), lambda i: (jnp.minimum(i, x_tiles - 1), 0)),
            pl.BlockSpec((N_META, D_MODEL), lambda i: (0, 0)),
            pl.BlockSpec((1, D_MODEL), lambda i: (0, 0)),
            pl.BlockSpec((LANES, D_MODEL), lambda i: (BA_OFFSET // LANES, 0)),
            pl.BlockSpec((tail_cols, D_MODEL), lambda i: (tail_start // tail_cols, 0)),
            pl.BlockSpec((2, LANES), lambda i: (0, 0)),
        ],
        out_specs=(pl.BlockSpec((tm, D_MODEL), lambda i: (i, 0)),
                   pl.BlockSpec((tm, LANES), lambda i: (i, 0)),
                   pl.BlockSpec((tm, LANES), lambda i: (i, 0))),
        compiler_params=pltpu.CompilerParams(
            dimension_semantics=("arbitrary",), vmem_limit_bytes=VMEM_LIMIT),
        name="norm_gates",
    )(x2d, meta, norm_w, w_t, w_t, gpar)


def _in_proj_kernel(xn_ref, w_ref, cw_ref, wpo_ref, wdo_ref, wo_ref,
                    proj_ref, wpo_bf_ref, wdo_bf_ref, wo_bf_ref, acc_ref, *, rows, row_blk, seq_len):
    j = pl.program_id(0)
    tn = w_ref.shape[0]
    ms = slice(rows, rows + N_META)
    conv_tile = (j >= COL_Q // tn) & (j < COL_ZD // tn)
    silu_tile = ((j >= COL_ZP // tn) & (j < COL_Q // tn)) | ((j >= COL_ZD // tn) & (j < BA_OFFSET // tn))

    def cast_weights():
        for src, dst in ((wpo_ref, wpo_bf_ref), (wdo_ref, wdo_bf_ref), (wo_ref, wo_bf_ref)):
            dst[...] = src[...].astype(BF16)
        return w_ref[...].astype(BF16)

    def plain(fn):
        w = cast_weights()
        for r in range(rows // row_blk):
            rs = slice(r * row_blk, (r + 1) * row_blk)
            proj_ref[rs, :] = fn(_mm_nt(xn_ref[rs, :], w)).astype(proj_ref.dtype)
        proj_ref[ms, :] = fn(_mm_nt(xn_ref[ms, :], w)).astype(proj_ref.dtype)

    @pl.when(conv_tile)
    def _():
        w = cast_weights()
        cw = cw_ref[...]
        hist_rows = SUBLANES
        n_blk = rows // row_blk

        def conv_silu(slot, n):
            y = acc_ref[slot, hist_rows:hist_rows + n, :] * cw[CONV_WIDTH - 1:CONV_WIDTH]
            for kk in range(CONV_WIDTH - 1):
                off = hist_rows - (CONV_WIDTH - 1) + kk
                y = y + acc_ref[slot, off:off + n, :] * cw[kk:kk + 1]
            return _silu(y)

        meta_acc = _mm_nt(xn_ref[ms, :], w)
        acc_ref[0, 0:hist_rows, :] = jnp.zeros((hist_rows, tn), F32)
        acc_ref[0, hist_rows:hist_rows + N_META, :] = meta_acc
        proj_ref[ms, :] = conv_silu(0, N_META).astype(proj_ref.dtype)
        meta_tail = meta_acc[N_META - hist_rows:]
        for r in range(n_blk + 1):
            if r < n_blk:
                slot = r % 2
                rs = slice(r * row_blk, (r + 1) * row_blk)
                acc = _mm_nt(xn_ref[rs, :], w)
                batch_start = (r * row_blk) % seq_len == 0
                acc_ref[slot, 0:hist_rows, :] = (
                    meta_tail if batch_start else acc_ref[1 - slot, row_blk:row_blk + hist_rows, :])
                acc_ref[slot, hist_rows:hist_rows + row_blk, :] = acc
            if r > 0:
                ps = slice((r - 1) * row_blk, r * row_blk)
                proj_ref[ps, :] = conv_silu((r - 1) % 2, row_blk).astype(proj_ref.dtype)

    @pl.when(silu_tile)
    def _():
        plain(_silu)

    @pl.when(jnp.logical_not(conv_tile | silu_tile))
    def _():
        plain(lambda a: a)


def _in_proj(xn_all, w_t, conv_w, out_weights, *, seq_len):
    rows = xn_all.shape[0] - N_META
    cols = w_t.shape[0]
    tn = TN_PROJ
    n_conv_tiles = (COL_ZD - COL_Q) // tn
    n_steps = cols // tn
    assert W_CAST_SLABS <= n_steps
    slab = lambda wgt: pl.BlockSpec((wgt.shape[0] // W_CAST_SLABS, D_MODEL),
                                    lambda j: (jnp.minimum(j, W_CAST_SLABS - 1), 0))
    kern = functools.partial(_in_proj_kernel, rows=rows, row_blk=ROW_BLK_PROJ, seq_len=seq_len)
    return pl.pallas_call(
        kern,
        out_shape=(jax.ShapeDtypeStruct((rows + N_META, n_steps * tn), PROJ_DTYPE),
                   *[jax.ShapeDtypeStruct(wgt.shape, BF16) for wgt in out_weights]),
        grid=(n_steps,),
        in_specs=[
            pl.BlockSpec((rows + N_META, D_MODEL), lambda j: (0, 0), pipeline_mode=pl.Buffered(1)),
            pl.BlockSpec((tn, D_MODEL), lambda j: (j, 0)),
            pl.BlockSpec((CONV_WIDTH, tn), lambda j: (0, jnp.clip(j - COL_Q // tn, 0, n_conv_tiles - 1))),
            *[slab(wgt) for wgt in out_weights],
        ],
        out_specs=(pl.BlockSpec((rows + N_META, tn), lambda j: (0, j)),
                   *[slab(wgt) for wgt in out_weights]),
        scratch_shapes=[pltpu.VMEM((2, SUBLANES + ROW_BLK_PROJ, tn), F32)],
        compiler_params=pltpu.CompilerParams(
            dimension_semantics=("arbitrary",), vmem_limit_bytes=VMEM_LIMIT),
        name="in_proj",
    )(xn_all, w_t, conv_w, *out_weights)


PAIR = 2 * DN_HEAD_DIM


def _split3(x):
    hi = x.astype(BF16)
    r1 = x - hi.astype(F32)
    mid = r1.astype(BF16)
    lo = (r1 - mid.astype(F32)).astype(BF16)
    return hi, mid, lo


def _block_diag_rows(y, half):
    lane = lax.broadcasted_iota(jnp.int32, y.shape, 1)
    zero = jnp.zeros_like(y)
    return jnp.concatenate([jnp.where(lane < half, y, zero), jnp.where(lane >= half, y, zero)], axis=0)


def _inverse_masks(c):
    row = lax.broadcasted_iota(jnp.int32, (c, 2 * c), 0)
    col = lax.broadcasted_iota(jnp.int32, (c, 2 * c), 1) & (c - 1)
    blk = lambda x, m: x >> (m.bit_length() - 1)
    eye = (row == col).astype(F32)
    same_leaf = blk(row, LEAF) == blk(col, LEAF)
    offs = []
    m = 2 * LEAF
    while m <= c:
        offs.append((blk(row, m) == blk(col, m)) & (blk(row, m // 2) != blk(col, m // 2)))
        m *= 2
    return eye, same_leaf, offs


def _inv_unit_lower_packed(lps, c, masks, fill, n_fill=1):
    eye, same_leaf, offs = masks
    bd = lambda y: _block_diag_rows(y.astype(BF16), c)
    ps = [jnp.where(same_leaf, -l, 0.0) for l in lps]
    ts = [eye + p for p in ps]
    dep = ps[0]
    ps = [_mm(p, bd(p)) for p in ps]
    fill(dep, ps, n_fill + 1)
    span = 4
    while span <= LEAF:
        dep = ps[0]
        last = span == LEAF
        prod = [_mm(t if last else jnp.concatenate([t, p], axis=0), bd(p)) for t, p in zip(ts, ps)]
        ts = [t + x[:c] for t, x in zip(ts, prod)]
        if not last:
            ps = [x[c:] for x in prod]
        fill(dep, ts, n_fill)
        span *= 2
    for off in offs:
        dep = ts[0]
        inner = [_mm(jnp.where(off, l, 0.0), bd(t)) for l, t in zip(lps, ts)]
        fill(dep, inner, n_fill)
        dep = inner[0]
        ts = [t - _mm(t, bd(x)) for t, x in zip(ts, inner)]
        fill(dep, ts, n_fill)
    return ts


def _delta_rule_core(read_rows, read_gates, n_pos, state_ref, qn_s, kn_s, gsm_s, gl_s, gp_s, gr_s,
                     finish=None, o_s=None):
    c = CHUNK
    hd = DN_HEAD_DIM
    n_pairs = DN_HEADS // 2
    pairs = range(n_pairs)

    rowp = lax.broadcasted_iota(jnp.int32, (c, 2 * c), 0)
    colp = lax.broadcasted_iota(jnp.int32, (c, 2 * c), 1) & (c - 1)
    causal = rowp >= colp
    strict = rowp > colp
    lane_p = lax.broadcasted_iota(jnp.int32, (c, 2 * c), 1)
    inv_masks = _inverse_masks(c)
    tril = (lax.broadcasted_iota(jnp.int32, (c, c), 0) >= lax.broadcasted_iota(jnp.int32, (c, c), 1))
    tril3 = jnp.concatenate([tril.astype(BF16)] * 3, axis=1)
    zero_blk = jnp.zeros((hd, hd), BF16)
    head_cols = lambda h: slice(h * hd, (h + 1) * hd)
    pair_cols_of = lambda p: slice(p * PAIR, (p + 1) * PAIR)
    both = (0, 1)
    ha = [2 * p for p in pairs]
    hb = [2 * p + 1 for p in pairs]

    def l2n(xh, scale):
        return xh * (lax.rsqrt(jnp.sum(xh * xh, axis=-1, keepdims=True) + NORM_EPS) * scale)

    def prepare_pair(j, p, zero=None):
        cols = pair_cols_of(p)
        slot = j & 3
        for idx, (dst, scale) in enumerate(((qn_s, hd ** -0.5), (kn_s, 1.0))):
            x = read_rows(idx, j, cols).astype(F32)
            if zero is not None:
                x = x + jnp.concatenate([zero[0:1], zero[0:1]], axis=1)
            dst[slot, :, cols] = jnp.concatenate([l2n(x[:, :hd], scale), l2n(x[:, hd:], scale)], axis=1)
        r0 = pl.multiple_of((j >> 24) * SUBLANES, SUBLANES)
        back = qn_s[slot, pl.ds(r0, SUBLANES), cols] + kn_s[slot, pl.ds(r0, SUBLANES), cols]
        return back[:, :hd] + back[:, hd:]

    def out_rows(j):
        return pl.multiple_of(jnp.maximum(j, 0) * c, c)

    def exact_rows_dot(lhs3, x):
        hi, mid, lo = _split3(x)
        return jnp.dot(lhs3, jnp.concatenate([hi, mid, lo], axis=0), preferred_element_type=F32)

    def pair_cols(mat, la, lb):
        return jnp.where(lane_p < c, mat[:, la:la + 1], mat[:, lb:lb + 1])

    def pair_rows(mat_t, ra, rb):
        return jnp.concatenate([mat_t[ra:ra + 1, :], mat_t[rb:rb + 1, :]], axis=1)

    def zero_of(v):
        return jnp.where(v != v, v, 0.0)

    def prepare_gates(j):
        slot = j & 3
        gt = read_gates(j)
        gcum = exact_rows_dot(tril3, gt)
        g_last = gcum[c - 1:c, :]
        gsm_s[slot, 0] = jnp.exp(gcum)
        gsm_s[slot, 1] = jnp.exp(g_last - gcum)
        gsm_s[slot, 2] = jnp.broadcast_to(jnp.exp(g_last), (c, LANES))
        gt_t = gt.T
        gcum_t = gcum.T
        for p in pairs:
            g_col = pair_cols(gcum, DN_HEADS + ha[p], DN_HEADS + hb[p])
            g_row = pair_rows(gcum_t, DN_HEADS + ha[p], DN_HEADS + hb[p])
            b_col = pair_cols(gt, ha[p], hb[p])
            b_row = pair_rows(gt_t, ha[p], hb[p])
            dec = jnp.where(causal, jnp.exp(jnp.where(causal, g_col - g_row, 0.0)), 0.0)
            gp_s[slot, p] = dec
            gl_s[slot, p] = jnp.where(strict, b_col * dec, 0.0)
            gr_s[slot, p] = jnp.concatenate(
                [b_row, b_row * jnp.exp(g_row), jnp.zeros((SUBLANES - 2, 2 * c), F32)], axis=0)

    def pair_body(i, carry):
        pos = [2 * i, 2 * i + 1]
        slot = [pz & 3 for pz in pos]
        fillers = []
        for p in pairs:
            for x in both:
                if finish is not None:
                    fillers.append(functools.partial(finish, x, out_rows(pos[x] - 2), p))
                fillers.append(functools.partial(prepare_pair, pos[x] + 2, p))
        n_slices = len(fillers)
        fillers.reverse()

        def fill(dep, into, n=1):
            zero = None if dep is None else zero_of(dep[0:SUBLANES, 0:LANES].astype(F32))
            for _ in range(n):
                if fillers:
                    done = zero_of(fillers.pop()(zero=zero))[0:1]
                    if into[-1].shape[1] == PAIR:
                        done = jnp.concatenate([done, done], axis=1)
                    into[-1] = into[-1] + done

        prob = [(x, p) for x in both for p in pairs]
        eg = [gsm_s[slot[x], 0] for x in both]
        e_rest = [gsm_s[slot[x], 1] for x in both]
        e_last = [gsm_s[slot[x], 2, 0:1, :] for x in both]
        gl = [gl_s[slot[x], p] for x, p in prob]
        gp = [gp_s[slot[x], p] for x, p in prob]
        gr = [gr_s[slot[x], p] for x, p in prob]
        kpair = [kn_s[slot[x], :, pair_cols_of(p)].astype(BF16) for x, p in prob]
        kbd = [_block_diag_rows(kp, hd) for kp in kpair]
        qpair = []
        for n, (x, p) in enumerate(prob):
            staged = [qn_s[slot[x], :, pair_cols_of(p)]]
            if n % 2:
                fill(None, staged)
            qpair.append(staged[0].astype(BF16))
        aq = [_mm_nt(jnp.concatenate([kpair[n], qpair[n]], axis=0), kbd[n]) for n in range(len(prob))]
        for x in both:
            prepare_gates(pos[x] + 2)
        lps = [aq[n][:c] * gl[n] for n in range(len(prob))]
        pps = [aq[n][c:] * gp[n] for n in range(len(prob))]
        per_group = max(1, (n_slices - len(prob) // 2) // 12)
        tps = _inv_unit_lower_packed(lps, c, inv_masks, fill, per_group)
        vbd = [_block_diag_rows(read_rows(2, pos[x], pair_cols_of(p)), hd) for x, p in prob]
        u = [_mm(tps[n] * gr[n][0:1], vbd[n]) for n in range(len(prob))]
        w = [_mm(tps[n] * gr[n][1:2], kbd[n]) for n in range(len(prob))]
        fill(tps[0], w, per_group)
        state = [state_ref[h] for h in range(DN_HEADS)]
        outs = []
        for x in both:
            idx = [x * n_pairs + p for p in pairs]
            qd = [jnp.concatenate(
                [qn_s[slot[x], :, head_cols(h)] * eg[x][:, DN_HEADS + h:DN_HEADS + h + 1]
                 for h in (ha[p], hb[p])], axis=1) for p in pairs]
            sbd = [jnp.concatenate([jnp.concatenate([state[ha[p]].astype(BF16), zero_blk], axis=1),
                                    jnp.concatenate([zero_blk, state[hb[p]].astype(BF16)], axis=1)], axis=0)
                   for p in pairs]
            ws = [_mm(jnp.concatenate([w[idx[p]], qd[p]], axis=0), sbd[p]) for p in pairs]
            fill(w[idx[0]], ws, per_group)
            v_new = [u[idx[p]] - ws[p][:c] for p in pairs]
            o = [ws[p][c:] + _mm(pps[idx[p]], _block_diag_rows(v_new[p].astype(BF16), hd)) for p in pairs]
            new_state = []
            for h in range(DN_HEADS):
                p, half = divmod(h, 2)
                lane_h = DN_HEADS + h
                kd = kn_s[slot[x], :, head_cols(h)] * e_rest[x][:, lane_h:lane_h + 1]
                upd = _mm_tn(kd, v_new[p][:, half * hd:(half + 1) * hd])
                new_state.append(state[h] * e_last[x][:, lane_h:lane_h + 1] + upd)
            state = new_state
            if x == 1:
                fill(ws[0], o, len(fillers))
            outs.append(o)
        for h in range(DN_HEADS):
            state_ref[h] = state[h]
        if o_s is not None:
            for x in both:
                for p in pairs:
                    o_s[x, :, pair_cols_of(p)] = outs[x][p]
        return carry

    for x in both:
        for p in pairs:
            prepare_pair(jnp.int32(x), p)
        prepare_gates(jnp.int32(x))
    n_iter = n_pos // 2
    if n_iter == 1:
        pair_body(jnp.int32(0), 0)
    else:
        lax.fori_loop(0, n_iter, pair_body, 0)
    if finish is not None:
        for x in both:
            last_rows = out_rows(jnp.int32(n_pos - 2 + x))
            for p in pairs:
                finish(x, last_rows, p)


def _deltanet_kernel(q_ref, k_ref, v_ref, z_ref, gates_ref, s0_ref, dnw_ref,
                     y_ref, state_ref, qn_s, kn_s, o_s, gsm_s, gl_s, gp_s, gr_s, *, tt):
    c = CHUNK
    hd = DN_HEAD_DIM
    n_chunks = tt // c
    dnw = dnw_ref[...]

    @pl.when(pl.program_id(1) == 0)
    def _():
        state_ref[...] = s0_ref[...]
        o_s[...] = jnp.zeros(o_s.shape, F32)

    srcs = (q_ref, k_ref, v_ref)

    def rows_at(pos):
        return pl.ds(pl.multiple_of(jnp.minimum(pos, n_chunks - 1) * c, c), c)

    def read_rows(idx, pos, cols):
        return srcs[idx][rows_at(pos), cols]

    def read_gates(pos):
        return gates_ref[rows_at(pos), :]

    def finish(src, row0, p, zero=None):
        rows = pl.ds(row0, c)
        for half in range(2):
            cs = slice(p * PAIR + half * hd, p * PAIR + (half + 1) * hd)
            oh = o_s[src, :, cs]
            if zero is not None:
                oh = oh + zero[0:1]
            oh = oh * lax.rsqrt(jnp.mean(oh * oh, axis=-1, keepdims=True) + NORM_EPS) * dnw
            y_ref[rows, cs] = (oh * z_ref[rows, cs].astype(F32)).astype(y_ref.dtype)
        r0 = pl.multiple_of((row0 >> 24) * (2 * SUBLANES), 2 * SUBLANES)
        back = y_ref[pl.ds(r0, 2 * SUBLANES), p * PAIR:(p + 1) * PAIR].astype(F32)[0:SUBLANES]
        return back[:, :hd] + back[:, hd:]

    _delta_rule_core(read_rows, read_gates, n_chunks, state_ref, qn_s, kn_s, gsm_s, gl_s, gp_s, gr_s,
                     finish=finish, o_s=o_s)


def _deltanet_meta_kernel(mqkv_ref, mgates_ref, s_out_ref, state_ref, qn_s, kn_s, gsm_s, gl_s, gp_s, gr_s):
    c = CHUNK
    state_ref[...] = jnp.zeros(state_ref.shape, F32)

    def rows_at(pos):
        return pl.ds(pl.multiple_of(jnp.minimum(pos, 1) * c, c), c)

    def read_rows(idx, pos, cols):
        return mqkv_ref[rows_at(pos), slice(idx * DN_WIDTH + cols.start, idx * DN_WIDTH + cols.stop)]

    def read_gates(pos):
        return mgates_ref[rows_at(pos), :]

    _delta_rule_core(read_rows, read_gates, 2, state_ref, qn_s, kn_s, gsm_s, gl_s, gp_s, gr_s)
    s_out_ref[...] = state_ref[...]


def _lookahead_scratch():
    c = CHUNK
    return [
        pltpu.VMEM((4, c, DN_WIDTH), F32),
        pltpu.VMEM((4, c, DN_WIDTH), F32),
        pltpu.VMEM((4, 3, c, LANES), F32),
        pltpu.VMEM((4, DN_HEADS // 2, c, 2 * c), F32),
        pltpu.VMEM((4, DN_HEADS // 2, c, 2 * c), F32),
        pltpu.VMEM((4, DN_HEADS // 2, SUBLANES, 2 * c), F32),
    ]


def _deltanet_meta(meta_qkv, meta_gates):
    state_shape = (DN_HEADS, DN_HEAD_DIM, DN_HEAD_DIM)
    qn, kn, gsm, gl, gp, gr = _lookahead_scratch()
    return pl.pallas_call(
        _deltanet_meta_kernel,
        out_shape=jax.ShapeDtypeStruct(state_shape, F32),
        scratch_shapes=[pltpu.VMEM(state_shape, F32), qn, kn, gsm, gl, gp, gr],
        compiler_params=pltpu.CompilerParams(vmem_limit_bytes=VMEM_LIMIT),
        name="deltanet_meta",
    )(meta_qkv, meta_gates)


def _deltanet(proj, gates, state0, dn_norm_w, *, batch, seq_len):
    tt = TT_DN
    nt = seq_len // tt
    c = CHUNK
    kern = functools.partial(_deltanet_kernel, tt=tt)
    row = lambda b, t: b * nt + t
    const = lambda shape: pl.BlockSpec(shape, lambda b, t: (0,) * len(shape), pipeline_mode=pl.Buffered(1))
    state_shape = (DN_HEADS, DN_HEAD_DIM, DN_HEAD_DIM)
    qn, kn, gsm, gl, gp, gr = _lookahead_scratch()
    return pl.pallas_call(
        kern,
        out_shape=jax.ShapeDtypeStruct((batch * seq_len, DN_WIDTH), BF16),
        grid=(batch, nt),
        in_specs=[
            pl.BlockSpec((tt, DN_WIDTH), lambda b, t: (row(b, t), COL_Q // DN_WIDTH)),
            pl.BlockSpec((tt, DN_WIDTH), lambda b, t: (row(b, t), COL_K // DN_WIDTH)),
            pl.BlockSpec((tt, DN_WIDTH), lambda b, t: (row(b, t), COL_V // DN_WIDTH)),
            pl.BlockSpec((tt, DN_WIDTH), lambda b, t: (row(b, t), COL_ZD // DN_WIDTH)),
            pl.BlockSpec((tt, LANES), lambda b, t: (row(b, t), 0)),
            const(state_shape),
            const((1, DN_HEAD_DIM)),
        ],
        out_specs=pl.BlockSpec((tt, DN_WIDTH), lambda b, t: (row(b, t), 0)),
        scratch_shapes=[
            pltpu.VMEM(state_shape, F32),
            qn, kn,
            pltpu.VMEM((2, c, DN_WIDTH), F32),
            gsm, gl, gp, gr,
        ],
        compiler_params=pltpu.CompilerParams(
            dimension_semantics=("arbitrary", "arbitrary"), vmem_limit_bytes=VMEM_LIMIT),
        name="deltanet",
    )(proj, proj, proj, proj, gates, state0, dn_norm_w)


def _out_merge_kernel(u_ref, zp_ref, un_ref, zpn_ref, gpa_ref, gpb_ref, gda_ref, gdb_ref, ydn_ref, x_ref, mu_ref,
                      mix_ref, scale_ref, wpo_ref, wdo_ref, wo_ref, fnw_ref, out_ref, ubuf_ref, ypool_ref,
                      *, tm, tiles):
    g = pl.program_id(0) * tiles + pl.program_id(1)
    hist = N_META

    def pool_group(gi, zp_blk_ref, slot):
        w = POOL_WINDOWS[gi]
        cs = slice(gi * POOL_GROUP_DIM, (gi + 1) * POOL_GROUP_DIM)
        ext = ubuf_ref[:, cs]
        acc = ext
        span = 1
        while span < w:
            acc = acc + pltpu.roll(acc, span, axis=0)
            span *= 2
        d = acc[hist:] * (1.0 / w) - ext[hist:]
        mixed = _mm(d, mix_ref[gi])
        zp = zp_blk_ref[:, cs].astype(F32)
        ypool_ref[slot, :, cs] = (mixed * scale_ref[:, cs] * zp).astype(BF16)

    @pl.when(g == 0)
    def _():
        ubuf_ref[0:hist, :] = mu_ref[...].astype(F32)
        ubuf_ref[hist:hist + tm, :] = u_ref[...].astype(F32)
        for gi in range(POOL_GROUPS):
            pool_group(gi, zp_ref, 0)

    y_pool = ypool_ref[g % 2]

    frame = slice(GATE_SHIFT, GATE_SHIFT + D_MODEL)
    gp = jnp.concatenate([gpa_ref[...], gpb_ref[...]], axis=1).astype(F32)[:, frame]
    gd = jnp.concatenate([gda_ref[...], gdb_ref[...]], axis=1).astype(F32)[:, frame]
    t_pool = jnp.dot(y_pool, wpo_ref[...], preferred_element_type=F32)
    batch_start = (g + 1) % tiles == 0
    ubuf_ref[0:hist, :] = jnp.where(batch_start, mu_ref[...].astype(F32), ubuf_ref[tm:tm + hist, :])
    ubuf_ref[hist:hist + tm, :] = un_ref[...].astype(F32)
    ydn = ydn_ref[...]
    dn_blk = D_MODEL // POOL_GROUPS
    t_dn = []
    for gi in range(POOL_GROUPS):
        pool_group(gi, zpn_ref, (g + 1) % 2)
        t_dn.append(jnp.dot(ydn, wdo_ref[:, gi * dn_blk:(gi + 1) * dn_blk], preferred_element_type=F32))
    t_dn = jnp.concatenate(t_dn, axis=1)
    merged = _sigmoid(gp) * t_pool + _sigmoid(gd) * t_dn
    h = x_ref[...] + jnp.dot(merged.astype(BF16), wo_ref[...], preferred_element_type=F32)
    ms = jnp.mean(h * h, axis=-1, keepdims=True)
    out_ref[...] = h * lax.rsqrt(ms + NORM_EPS) * fnw_ref[...]


def _resident(shape, index_map):
    return pl.BlockSpec(shape, index_map, pipeline_mode=pl.Buffered(1))


def _out_merge(proj, proj_tail, y_dn, x2d, mix, scale, wpo, wdo, wo, fnw, *, batch, seq_len):
    tm = TM_OUT
    tiles = seq_len // tm
    meta_blk = batch * seq_len // N_META
    kern = functools.partial(_out_merge_kernel, tm=tm, tiles=tiles)
    row = lambda b, t: b * tiles + t
    nxt = lambda b, t: jnp.minimum(row(b, t) + 1, batch * tiles - 1)
    gp0, gd0 = COL_GP - GATE_SHIFT, COL_GD - GATE_SHIFT
    assert gp0 % D_MODEL == 0 and gd0 % D_MODEL == 0
    assert gd0 + D_MODEL == proj.shape[1]
    return pl.pallas_call(
        kern,
        out_shape=jax.ShapeDtypeStruct((batch * seq_len, D_MODEL), F32),
        grid=(batch, tiles),
        in_specs=[
            pl.BlockSpec((tm, POOL_WIDTH), lambda b, t: (row(b, t), COL_U // POOL_WIDTH)),
            pl.BlockSpec((tm, POOL_WIDTH), lambda b, t: (row(b, t), COL_ZP // POOL_WIDTH)),
            pl.BlockSpec((tm, POOL_WIDTH), lambda b, t: (nxt(b, t), COL_U // POOL_WIDTH)),
            pl.BlockSpec((tm, POOL_WIDTH), lambda b, t: (nxt(b, t), COL_ZP // POOL_WIDTH)),
            pl.BlockSpec((tm, D_MODEL), lambda b, t: (row(b, t), gp0 // D_MODEL)),
            pl.BlockSpec((tm, LANES), lambda b, t: (row(b, t), (gp0 + D_MODEL) // LANES)),
            pl.BlockSpec((tm, D_MODEL), lambda b, t: (row(b, t), gd0 // D_MODEL)),
            pl.BlockSpec((tm, LANES), lambda b, t: (row(b, t), 0)),
            pl.BlockSpec((tm, DN_WIDTH), lambda b, t: (row(b, t), 0)),
            pl.BlockSpec((tm, D_MODEL), lambda b, t: (row(b, t), 0)),
            _resident((N_META, POOL_WIDTH), lambda b, t: (meta_blk, COL_U // POOL_WIDTH)),
            _resident((POOL_GROUPS, POOL_GROUP_DIM, POOL_GROUP_DIM), lambda b, t: (0, 0, 0)),
            _resident((1, POOL_WIDTH), lambda b, t: (0, 0)),
            _resident((POOL_WIDTH, D_MODEL), lambda b, t: (0, 0)),
            _resident((DN_WIDTH, D_MODEL), lambda b, t: (0, 0)),
            _resident((D_MODEL, D_MODEL), lambda b, t: (0, 0)),
            _resident((1, D_MODEL), lambda b, t: (0, 0)),
        ],
        out_specs=pl.BlockSpec((tm, D_MODEL), lambda b, t: (row(b, t), 0)),
        scratch_shapes=[pltpu.VMEM((N_META + tm, POOL_WIDTH), F32),
                        pltpu.VMEM((2, tm, POOL_WIDTH), BF16)],
        compiler_params=pltpu.CompilerParams(
            dimension_semantics=("arbitrary", "arbitrary"), vmem_limit_bytes=VMEM_LIMIT),
        name="out_merge",
    )(proj, proj, proj, proj, proj, proj, proj, proj_tail, y_dn, x2d, proj, mix, scale, wpo, wdo, wo, fnw)


def kernel(x, meta_tokens, norm_w, w_in, conv_w, A_log, dt_bias, pool_mix, pool_scale, dn_norm_w,
           w_pool_out, w_dn_out, w_o, final_norm_w):
    batch, seq_len, _ = x.shape
    assert norm_w.shape[0] == 1, "single layer block"
    x2d = x.reshape(batch * seq_len, D_MODEL)

    rows = batch * seq_len
    w_t = jnp.swapaxes(w_in, 1, 2).reshape(w_in.shape[2], D_MODEL)
    gpar = jnp.pad(jnp.concatenate([A_log[0:1], dt_bias[0:1]], axis=0).astype(F32),
                   ((0, 0), (DN_HEADS, LANES - 2 * DN_HEADS)))
    nw = norm_w[0].reshape(1, D_MODEL)

    xn_all, gates, proj_tail = _norm_gates(x2d, meta_tokens.astype(F32), nw, w_t, gpar)
    proj, wpo, wdo, wo = _in_proj(xn_all, w_t, conv_w[0], (w_pool_out[0], w_dn_out[0], w_o[0]), seq_len=seq_len)

    front = 2 * CHUNK - N_META
    meta_proj = proj[rows:]
    meta_qkv = jnp.pad(meta_proj[:, COL_Q:COL_ZD], ((front, 0), (0, 0)))
    meta_gates = jnp.pad(gates[rows:], ((front, 0), (0, 0)))

    state0 = _deltanet_meta(meta_qkv, meta_gates)
    y_dn = _deltanet(proj, gates, state0, dn_norm_w[0].reshape(1, DN_HEAD_DIM), batch=batch, seq_len=seq_len)

    out = _out_merge(proj, proj_tail, y_dn, x2d, pool_mix[0], pool_scale[0].reshape(1, POOL_WIDTH),
                     wpo, wdo, wo, final_norm_w.reshape(1, D_MODEL), batch=batch, seq_len=seq_len)
    return out.reshape(batch, seq_len, D_MODEL)
```

```python
import functools

import jax
import jax.numpy as jnp
from jax import lax
from jax.experimental import pallas as pl
from jax.experimental.pallas import tpu as pltpu

D_MODEL = 2048
N_META = 16
POOL_GROUPS = 4
POOL_WINDOWS = (2, 4, 8, 16)
POOL_WIDTH = D_MODEL // 2
POOL_GROUP_DIM = POOL_WIDTH // POOL_GROUPS
DN_HEADS = 16
DN_HEAD_DIM = 128
DN_WIDTH = DN_HEADS * DN_HEAD_DIM
CONV_WIDTH = 4
NORM_EPS = 1e-6

COL_U = 0
COL_ZP = COL_U + POOL_WIDTH
COL_Q = COL_ZP + POOL_WIDTH
COL_K = COL_Q + DN_WIDTH
COL_V = COL_K + DN_WIDTH
COL_ZD = COL_V + DN_WIDTH
BA_OFFSET = COL_ZD + DN_WIDTH
COL_GP = BA_OFFSET + 2 * DN_HEADS
COL_GD = COL_GP + D_MODEL
LANES = 128
SUBLANES = 8
GATE_SHIFT = COL_GP % LANES

PROJ_DTYPE = jnp.bfloat16
CHUNK = 64
LEAF = 16
TT_DN = 1024
TM_NORM = 1024
TN_PROJ = 256
ROW_BLK_PROJ = 1024
W_CAST_SLABS = 32
TM_OUT = 256
VMEM_LIMIT = 58 * 1024 * 1024

BF16 = jnp.bfloat16
F32 = jnp.float32


def _mm(a, b):
    return jnp.dot(a.astype(BF16), b.astype(BF16), preferred_element_type=F32)


def _mm_nt(a, b):
    return lax.dot_general(a.astype(BF16), b.astype(BF16), (((1,), (1,)), ((), ())),
                           preferred_element_type=F32)


def _mm_tn(a, b):
    return lax.dot_general(a.astype(BF16), b.astype(BF16), (((0,), (0,)), ((), ())),
                           preferred_element_type=F32)


def _sigmoid(x):
    return 0.5 * jnp.tanh(0.5 * x) + 0.5


def _silu(x):
    h = 0.5 * x
    return h * jnp.tanh(h) + h


def _softplus(x):
    return jnp.maximum(x, 0.0) + jnp.log1p(jnp.exp(-jnp.abs(x)))


def _norm_gates_kernel(x_ref, meta_ref, nw_ref, wba_ref, wtail_ref, gpar_ref, xn_ref, gates_ref, tail_ref,
                       *, x_tiles, sub):
    i = pl.program_id(0)
    nw = nw_ref[...]
    wtail = wtail_ref[...]
    wboth = jnp.concatenate(
        [wba_ref[...], wtail, jnp.zeros((LANES - wtail.shape[0], D_MODEL), F32)], axis=0).astype(BF16)
    a_rate = jnp.exp(gpar_ref[0:1, :])
    dt_bias = gpar_ref[1:2, :]

    def rows_block(x, rows):
        ms = jnp.mean(x * x, axis=-1, keepdims=True)
        xn = (x * lax.rsqrt(ms + NORM_EPS) * nw).astype(BF16)
        xn_ref[rows, :] = xn
        both = _mm_nt(xn, wboth)
        ba = both[:, :LANES]
        tail_ref[rows, :] = both[:, LANES:].astype(tail_ref.dtype)
        lane = lax.broadcasted_iota(jnp.int32, ba.shape, 1)
        g = -a_rate * _softplus(ba + dt_bias)
        gates_ref[rows, :] = jnp.where(lane < DN_HEADS, _sigmoid(ba), jnp.where(lane < 2 * DN_HEADS, g, 0.0))

    @pl.when(i < x_tiles)
    def _():
        def body(r, carry):
            rows = pl.ds(pl.multiple_of(r * sub, sub), sub)
            rows_block(x_ref[rows, :], rows)
            return carry

        lax.fori_loop(0, x_ref.shape[0] // sub, body, 0)

    @pl.when(i == x_tiles)
    def _():
        rows_block(meta_ref[...], pl.ds(0, N_META))


def _norm_gates(x2d, meta, norm_w, w_t, gpar):
    rows = x2d.shape[0]
    tm = TM_NORM
    tail_start = w_t.shape[0] // TN_PROJ * TN_PROJ
    tail_cols = w_t.shape[0] - tail_start
    assert tail_start % LANES == 0 and 0 < tail_cols <= LANES
    assert tail_cols % SUBLANES == 0 and tail_start % tail_cols == 0
    x_tiles = rows // tm
    kern = functools.partial(_norm_gates_kernel, x_tiles=x_tiles, sub=512)
    return pl.pallas_call(
        kern,
        out_shape=(jax.ShapeDtypeStruct((rows + N_META, D_MODEL), BF16),
                   jax.ShapeDtypeStruct((rows + N_META, LANES), F32),
                   jax.ShapeDtypeStruct((rows + N_META, LANES), PROJ_DTYPE)),
        grid=(x_tiles + 1,),
        in_specs=[
            pl.BlockSpec((tm, D_MODEL), lambda i: (jnp.minimum(i, x_tiles - 1), 0)),
            pl.BlockSpec((N_META, D_MODEL), lambda i: (0, 0)),
            pl.BlockSpec((1, D_MODEL), lambda i: (0, 0)),
            pl.BlockSpec((LANES, D_MODEL), lambda i: (BA_OFFSET // LANES, 0)),
            pl.BlockSpec((tail_cols, D_MODEL), lambda i: (tail_start // tail_cols, 0)),
            pl.BlockSpec((2, LANES), lambda i: (0, 0)),
        ],
        out_specs=(pl.BlockSpec((tm, D_MODEL), lambda i: (i, 0)),
                   pl.BlockSpec((tm, LANES), lambda i: (i, 0)),
                   pl.BlockSpec((tm, LANES), lambda i: (i, 0))),
        compiler_params=pltpu.CompilerParams(
            dimension_semantics=("arbitrary",), vmem_limit_bytes=VMEM_LIMIT),
        name="norm_gates",
    )(x2d, meta, norm_w, w_t, w_t, gpar)


def _in_proj_kernel(xn_ref, w_ref, cw_ref, wpo_ref, wdo_ref, wo_ref,
                    proj_ref, wpo_bf_ref, wdo_bf_ref, wo_bf_ref, acc_ref, *, rows, row_blk, seq_len):
    j = pl.program_id(0)
    tn = w_ref.shape[0]
    ms = slice(rows, rows + N_META)
    conv_tile = (j >= COL_Q // tn) & (j < COL_ZD // tn)
    silu_tile = ((j >= COL_ZP // tn) & (j < COL_Q // tn)) | ((j >= COL_ZD // tn) & (j < BA_OFFSET // tn))

    def cast_weights():
        for src, dst in ((wpo_ref, wpo_bf_ref), (wdo_ref, wdo_bf_ref), (wo_ref, wo_bf_ref)):
            dst[...] = src[...].astype(BF16)
        return w_ref[...].astype(BF16)

    def plain(fn):
        w = cast_weights()
        for r in range(rows // row_blk):
            rs = slice(r * row_blk, (r + 1) * row_blk)
            proj_ref[rs, :] = fn(_mm_nt(xn_ref[rs, :], w)).astype(proj_ref.dtype)
        proj_ref[ms, :] = fn(_mm_nt(xn_ref[ms, :], w)).astype(proj_ref.dtype)

    @pl.when(conv_tile)
    def _():
        w = cast_weights()
        cw = cw_ref[...]
        hist_rows = SUBLANES
        n_blk = rows // row_blk

        def conv_silu(slot, n):
            y = acc_ref[slot, hist_rows:hist_rows + n, :] * cw[CONV_WIDTH - 1:CONV_WIDTH]
            for kk in range(CONV_WIDTH - 1):
                off = hist_rows - (CONV_WIDTH - 1) + kk
                y = y + acc_ref[slot, off:off + n, :] * cw[kk:kk + 1]
            return _silu(y)

        meta_acc = _mm_nt(xn_ref[ms, :], w)
        acc_ref[0, 0:hist_rows, :] = jnp.zeros((hist_rows, tn), F32)
        acc_ref[0, hist_rows:hist_rows + N_META, :] = meta_acc
        proj_ref[ms, :] = conv_silu(0, N_META).astype(proj_ref.dtype)
        meta_tail = meta_acc[N_META - hist_rows:]
        for r in range(n_blk + 1):
            if r < n_blk:
                slot = r % 2
                rs = slice(r * row_blk, (r + 1) * row_blk)
                acc = _mm_nt(xn_ref[rs, :], w)
                batch_start = (r * row_blk) % seq_len == 0
                acc_ref[slot, 0:hist_rows, :] = (
                    meta_tail if batch_start else acc_ref[1 - slot, row_blk:row_blk + hist_rows, :])
                acc_ref[slot, hist_rows:hist_rows + row_blk, :] = acc
            if r > 0:
                ps = slice((r - 1) * row_blk, r * row_blk)
                proj_ref[ps, :] = conv_silu((r - 1) % 2, row_blk).astype(proj_ref.dtype)

    @pl.when(silu_tile)
    def _():
        plain(_silu)

    @pl.when(jnp.logical_not(conv_tile | silu_tile))
    def _():
        plain(lambda a: a)


def _in_proj(xn_all, w_t, conv_w, out_weights, *, seq_len):
    rows = xn_all.shape[0] - N_META
    cols = w_t.shape[0]
    tn = TN_PROJ
    n_conv_tiles = (COL_ZD - COL_Q) // tn
    n_steps = cols // tn
    assert W_CAST_SLABS <= n_steps
    slab = lambda wgt: pl.BlockSpec((wgt.shape[0] // W_CAST_SLABS, D_MODEL),
                                    lambda j: (jnp.minimum(j, W_CAST_SLABS - 1), 0))
    kern = functools.partial(_in_proj_kernel, rows=rows, row_blk=ROW_BLK_PROJ, seq_len=seq_len)
    return pl.pallas_call(
        kern,
        out_shape=(jax.ShapeDtypeStruct((rows + N_META, n_steps * tn), PROJ_DTYPE),
                   *[jax.ShapeDtypeStruct(wgt.shape, BF16) for wgt in out_weights]),
        grid=(n_steps,),
        in_specs=[
            pl.BlockSpec((rows + N_META, D_MODEL), lambda j: (0, 0), pipeline_mode=pl.Buffered(1)),
            pl.BlockSpec((tn, D_MODEL), lambda j: (j, 0)),
            pl.BlockSpec((CONV_WIDTH, tn), lambda j: (0, jnp.clip(j - COL_Q // tn, 0, n_conv_tiles - 1))),
            *[slab(wgt) for wgt in out_weights],
        ],
        out_specs=(pl.BlockSpec((rows + N_META, tn), lambda j: (0, j)),
                   *[slab(wgt) for wgt in out_weights]),
        scratch_shapes=[pltpu.VMEM((2, SUBLANES + ROW_BLK_PROJ, tn), F32)],
        compiler_params=pltpu.CompilerParams(
            dimension_semantics=("arbitrary",), vmem_limit_bytes=VMEM_LIMIT),
        name="in_proj",
    )(xn_all, w_t, conv_w, *out_weights)


PAIR = 2 * DN_HEAD_DIM


def _split3(x):
    hi = x.astype(BF16)
    r1 = x - hi.astype(F32)
    mid = r1.astype(BF16)
    lo = (r1 - mid.astype(F32)).astype(BF16)
    return hi, mid, lo


def _block_diag_rows(y, half):
    lane = lax.broadcasted_iota(jnp.int32, y.shape, 1)
    zero = jnp.zeros_like(y)
    return jnp.concatenate([jnp.where(lane < half, y, zero), jnp.where(lane >= half, y, zero)], axis=0)


def _inverse_masks(c):
    row = lax.broadcasted_iota(jnp.int32, (c, 2 * c), 0)
    col = lax.broadcasted_iota(jnp.int32, (c, 2 * c), 1) & (c - 1)
    blk = lambda x, m: x >> (m.bit_length() - 1)
    eye = (row == col).astype(F32)
    same_leaf = blk(row, LEAF) == blk(col, LEAF)
    offs = []
    m = 2 * LEAF
    while m <= c:
        offs.append((blk(row, m) == blk(col, m)) & (blk(row, m // 2) != blk(col, m // 2)))
        m *= 2
    return eye, same_leaf, offs


def _inv_unit_lower_packed(lps, c, masks, fill, n_fill=1):
    eye, same_leaf, offs = masks
    bd = lambda y: _block_diag_rows(y.astype(BF16), c)
    ps = [jnp.where(same_leaf, -l, 0.0) for l in lps]
    ts = [eye + p for p in ps]
    dep = ps[0]
    ps = [_mm(p, bd(p)) for p in ps]
    fill(dep, ps, n_fill + 1)
    span = 4
    while span <= LEAF:
        dep = ps[0]
        last = span == LEAF
        prod = [_mm(t if last else jnp.concatenate([t, p], axis=0), bd(p)) for t, p in zip(ts, ps)]
        ts = [t + x[:c] for t, x in zip(ts, prod)]
        if not last:
            ps = [x[c:] for x in prod]
        fill(dep, ts, n_fill)
        span *= 2
    for off in offs:
        dep = ts[0]
        inner = [_mm(jnp.where(off, l, 0.0), bd(t)) for l, t in zip(lps, ts)]
        fill(dep, inner, n_fill)
        dep = inner[0]
        ts = [t - _mm(t, bd(x)) for t, x in zip(ts, inner)]
        fill(dep, ts, n_fill)
    return ts


def _delta_rule_core(read_rows, read_gates, n_pos, state_ref, qn_s, kn_s, gsm_s, gl_s, gp_s, gr_s,
                     finish=None, o_s=None):
    c = CHUNK
    hd = DN_HEAD_DIM
    n_pairs = DN_HEADS // 2
    pairs = range(n_pairs)

    rowp = lax.broadcasted_iota(jnp.int32, (c, 2 * c), 0)
    colp = lax.broadcasted_iota(jnp.int32, (c, 2 * c), 1) & (c - 1)
    causal = rowp >= colp
    strict = rowp > colp
    lane_p = lax.broadcasted_iota(jnp.int32, (c, 2 * c), 1)
    inv_masks = _inverse_masks(c)
    tril = (lax.broadcasted_iota(jnp.int32, (c, c), 0) >= lax.broadcasted_iota(jnp.int32, (c, c), 1))
    tril3 = jnp.concatenate([tril.astype(BF16)] * 3, axis=1)
    zero_blk = jnp.zeros((hd, hd), BF16)
    head_cols = lambda h: slice(h * hd, (h + 1) * hd)
    pair_cols_of = lambda p: slice(p * PAIR, (p + 1) * PAIR)
    both = (0, 1)
    ha = [2 * p for p in pairs]
    hb = [2 * p + 1 for p in pairs]

    def l2n(xh, scale):
        return xh * (lax.rsqrt(jnp.sum(xh * xh, axis=-1, keepdims=True) + NORM_EPS) * scale)

    def prepare_pair(j, p, zero=None):
        cols = pair_cols_of(p)
        slot = j & 3
        for idx, (dst, scale) in enumerate(((qn_s, hd ** -0.5), (kn_s, 1.0))):
            x = read_rows(idx, j, cols).astype(F32)
            if zero is not None:
                x = x + jnp.concatenate([zero[0:1], zero[0:1]], axis=1)
            dst[slot, :, cols] = jnp.concatenate([l2n(x[:, :hd], scale), l2n(x[:, hd:], scale)], axis=1)
        r0 = pl.multiple_of((j >> 24) * SUBLANES, SUBLANES)
        back = qn_s[slot, pl.ds(r0, SUBLANES), cols] + kn_s[slot, pl.ds(r0, SUBLANES), cols]
        return back[:, :hd] + back[:, hd:]

    def out_rows(j):
        return pl.multiple_of(jnp.maximum(j, 0) * c, c)

    def exact_rows_dot(lhs3, x):
        hi, mid, lo = _split3(x)
        return jnp.dot(lhs3, jnp.concatenate([hi, mid, lo], axis=0), preferred_element_type=F32)

    def pair_cols(mat, la, lb):
        return jnp.where(lane_p < c, mat[:, la:la + 1], mat[:, lb:lb + 1])

    def pair_rows(mat_t, ra, rb):
        return jnp.concatenate([mat_t[ra:ra + 1, :], mat_t[rb:rb + 1, :]], axis=1)

    def zero_of(v):
        return jnp.where(v != v, v, 0.0)

    def prepare_gates(j):
        slot = j & 3
        gt = read_gates(j)
        gcum = exact_rows_dot(tril3, gt)
        g_last = gcum[c - 1:c, :]
        gsm_s[slot, 0] = jnp.exp(gcum)
        gsm_s[slot, 1] = jnp.exp(g_last - gcum)
        gsm_s[slot, 2] = jnp.broadcast_to(jnp.exp(g_last), (c, LANES))
        gt_t = gt.T
        gcum_t = gcum.T
        for p in pairs:
            g_col = pair_cols(gcum, DN_HEADS + ha[p], DN_HEADS + hb[p])
            g_row = pair_rows(gcum_t, DN_HEADS + ha[p], DN_HEADS + hb[p])
            b_col = pair_cols(gt, ha[p], hb[p])
            b_row = pair_rows(gt_t, ha[p], hb[p])
            dec = jnp.where(causal, jnp.exp(jnp.where(causal, g_col - g_row, 0.0)), 0.0)
            gp_s[slot, p] = dec
            gl_s[slot, p] = jnp.where(strict, b_col * dec, 0.0)
            gr_s[slot, p] = jnp.concatenate(
                [b_row, b_row * jnp.exp(g_row), jnp.zeros((SUBLANES - 2, 2 * c), F32)], axis=0)

    def pair_body(i, carry):
        pos = [2 * i, 2 * i + 1]
        slot = [pz & 3 for pz in pos]
        fillers = []
        for p in pairs:
            for x in both:
                if finish is not None:
                    fillers.append(functools.partial(finish, x, out_rows(pos[x] - 2), p))
                fillers.append(functools.partial(prepare_pair, pos[x] + 2, p))
        n_slices = len(fillers)
        fillers.reverse()

        def fill(dep, into, n=1):
            zero = None if dep is None else zero_of(dep[0:SUBLANES, 0:LANES].astype(F32))
            for _ in range(n):
                if fillers:
                    done = zero_of(fillers.pop()(zero=zero))[0:1]
                    if into[-1].shape[1] == PAIR:
                        done = jnp.concatenate([done, done], axis=1)
                    into[-1] = into[-1] + done

        prob = [(x, p) for x in both for p in pairs]
        eg = [gsm_s[slot[x], 0] for x in both]
        e_rest = [gsm_s[slot[x], 1] for x in both]
        e_last = [gsm_s[slot[x], 2, 0:1, :] for x in both]
        gl = [gl_s[slot[x], p] for x, p in prob]
        gp = [gp_s[slot[x], p] for x, p in prob]
        gr = [gr_s[slot[x], p] for x, p in prob]
        kpair = [kn_s[slot[x], :, pair_cols_of(p)].astype(BF16) for x, p in prob]
        kbd = [_block_diag_rows(kp, hd) for kp in kpair]
        qpair = []
        for n, (x, p) in enumerate(prob):
            staged = [qn_s[slot[x], :, pair_cols_of(p)]]
            if n % 2:
                fill(None, staged)
            qpair.append(staged[0].astype(BF16))
        aq = [_mm_nt(jnp.concatenate([kpair[n], qpair[n]], axis=0), kbd[n]) for n in range(len(prob))]
        for x in both:
            prepare_gates(pos[x] + 2)
        lps = [aq[n][:c] * gl[n] for n in range(len(prob))]
        pps = [aq[n][c:] * gp[n] for n in range(len(prob))]
        per_group = max(1, (n_slices - len(prob) // 2) // 12)
        tps = _inv_unit_lower_packed(lps, c, inv_masks, fill, per_group)
        vbd = [_block_diag_rows(read_rows(2, pos[x], pair_cols_of(p)), hd) for x, p in prob]
        u = [_mm(tps[n] * gr[n][0:1], vbd[n]) for n in range(len(prob))]
        w = [_mm(tps[n] * gr[n][1:2], kbd[n]) for n in range(len(prob))]
        fill(tps[0], w, per_group)
        state = [state_ref[h] for h in range(DN_HEADS)]
        outs = []
        for x in both:
            idx = [x * n_pairs + p for p in pairs]
            qd = [jnp.concatenate(
                [qn_s[slot[x], :, head_cols(h)] * eg[x][:, DN_HEADS + h:DN_HEADS + h + 1]
                 for h in (ha[p], hb[p])], axis=1) for p in pairs]
            sbd = [jnp.concatenate([jnp.concatenate([state[ha[p]].astype(BF16), zero_blk], axis=1),
                                    jnp.concatenate([zero_blk, state[hb[p]].astype(BF16)], axis=1)], axis=0)
                   for p in pairs]
            ws = [_mm(jnp.concatenate([w[idx[p]], qd[p]], axis=0), sbd[p]) for p in pairs]
            fill(w[idx[0]], ws, per_group)
            v_new = [u[idx[p]] - ws[p][:c] for p in pairs]
            o = [ws[p][c:] + _mm(pps[idx[p]], _block_diag_rows(v_new[p].astype(BF16), hd)) for p in pairs]
            new_state = []
            for h in range(DN_HEADS):
                p, half = divmod(h, 2)
                lane_h = DN_HEADS + h
                kd = kn_s[slot[x], :, head_cols(h)] * e_rest[x][:, lane_h:lane_h + 1]
                upd = _mm_tn(kd, v_new[p][:, half * hd:(half + 1) * hd])
                new_state.append(state[h] * e_last[x][:, lane_h:lane_h + 1] + upd)
            state = new_state
            if x == 1:
                fill(ws[0], o, len(fillers))
            outs.append(o)
        for h in range(DN_HEADS):
            state_ref[h] = state[h]
        if o_s is not None:
            for x in both:
                for p in pairs:
                    o_s[x, :, pair_cols_of(p)] = outs[x][p]
        return carry

    for x in both:
        for p in pairs:
            prepare_pair(jnp.int32(x), p)
        prepare_gates(jnp.int32(x))
    n_iter = n_pos // 2
    if n_iter == 1:
        pair_body(jnp.int32(0), 0)
    else:
        lax.fori_loop(0, n_iter, pair_body, 0)
    if finish is not None:
        for x in both:
            last_rows = out_rows(jnp.int32(n_pos - 2 + x))
            for p in pairs:
                finish(x, last_rows, p)


def _deltanet_kernel(q_ref, k_ref, v_ref, z_ref, gates_ref, s0_ref, dnw_ref,
                     y_ref, state_ref, qn_s, kn_s, o_s, gsm_s, gl_s, gp_s, gr_s, *, tt):
    c = CHUNK
    hd = DN_HEAD_DIM
    n_chunks = tt // c
    dnw = dnw_ref[...]

    @pl.when(pl.program_id(1) == 0)
    def _():
        state_ref[...] = s0_ref[...]
        o_s[...] = jnp.zeros(o_s.shape, F32)

    srcs = (q_ref, k_ref, v_ref)

    def rows_at(pos):
        return pl.ds(pl.multiple_of(jnp.minimum(pos, n_chunks - 1) * c, c), c)

    def read_rows(idx, pos, cols):
        return srcs[idx][rows_at(pos), cols]

    def read_gates(pos):
        return gates_ref[rows_at(pos), :]

    def finish(src, row0, p, zero=None):
        rows = pl.ds(row0, c)
        for half in range(2):
            cs = slice(p * PAIR + half * hd, p * PAIR + (half + 1) * hd)
            oh = o_s[src, :, cs]
            if zero is not None:
                oh = oh + zero[0:1]
            oh = oh * lax.rsqrt(jnp.mean(oh * oh, axis=-1, keepdims=True) + NORM_EPS) * dnw
            y_ref[rows, cs] = (oh * z_ref[rows, cs].astype(F32)).astype(y_ref.dtype)
        r0 = pl.multiple_of((row0 >> 24) * (2 * SUBLANES), 2 * SUBLANES)
        back = y_ref[pl.ds(r0, 2 * SUBLANES), p * PAIR:(p + 1) * PAIR].astype(F32)[0:SUBLANES]
        return back[:, :hd] + back[:, hd:]

    _delta_rule_core(read_rows, read_gates, n_chunks, state_ref, qn_s, kn_s, gsm_s, gl_s, gp_s, gr_s,
                     finish=finish, o_s=o_s)


def _deltanet_meta_kernel(mq_ref, mk_ref, mv_ref, mg_ref, s_out_ref, state_ref, mqkv_ref, mgates_ref,
                          qn_s, kn_s, gsm_s, gl_s, gp_s, gr_s):
    c = CHUNK
    front = 2 * c - N_META
    state_ref[...] = jnp.zeros(state_ref.shape, F32)
    mqkv_ref[...] = jnp.zeros(mqkv_ref.shape, mqkv_ref.dtype)
    mgates_ref[...] = jnp.zeros(mgates_ref.shape, F32)
    for idx, src in enumerate((mq_ref, mk_ref, mv_ref)):
        mqkv_ref[front:, idx * DN_WIDTH:(idx + 1) * DN_WIDTH] = src[...]
    mgates_ref[front:, :] = mg_ref[...]

    def rows_at(pos):
        return pl.ds(pl.multiple_of(jnp.minimum(pos, 1) * c, c), c)

    def read_rows(idx, pos, cols):
        return mqkv_ref[rows_at(pos), slice(idx * DN_WIDTH + cols.start, idx * DN_WIDTH + cols.stop)]

    def read_gates(pos):
        return mgates_ref[rows_at(pos), :]

    _delta_rule_core(read_rows, read_gates, 2, state_ref, qn_s, kn_s, gsm_s, gl_s, gp_s, gr_s)
    s_out_ref[...] = state_ref[...]


def _lookahead_scratch():
    c = CHUNK
    return [
        pltpu.VMEM((4, c, DN_WIDTH), F32),
        pltpu.VMEM((4, c, DN_WIDTH), F32),
        pltpu.VMEM((4, 3, c, LANES), F32),
        pltpu.VMEM((4, DN_HEADS // 2, c, 2 * c), F32),
        pltpu.VMEM((4, DN_HEADS // 2, c, 2 * c), F32),
        pltpu.VMEM((4, DN_HEADS // 2, SUBLANES, 2 * c), F32),
    ]


def _deltanet_meta(proj, gates, *, rows):
    state_shape = (DN_HEADS, DN_HEAD_DIM, DN_HEAD_DIM)
    qn, kn, gsm, gl, gp, gr = _lookahead_scratch()
    meta_blk = rows // N_META
    qkv_spec = lambda col: pl.BlockSpec((N_META, DN_WIDTH), lambda i: (meta_blk, col // DN_WIDTH))
    return pl.pallas_call(
        _deltanet_meta_kernel,
        out_shape=jax.ShapeDtypeStruct(state_shape, F32),
        grid=(1,),
        in_specs=[qkv_spec(COL_Q), qkv_spec(COL_K), qkv_spec(COL_V),
                  pl.BlockSpec((N_META, LANES), lambda i: (meta_blk, 0))],
        out_specs=pl.BlockSpec(state_shape, lambda i: (0, 0, 0)),
        scratch_shapes=[pltpu.VMEM(state_shape, F32),
                        pltpu.VMEM((2 * CHUNK, 3 * DN_WIDTH), PROJ_DTYPE),
                        pltpu.VMEM((2 * CHUNK, LANES), F32),
                        qn, kn, gsm, gl, gp, gr],
        compiler_params=pltpu.CompilerParams(
            dimension_semantics=("arbitrary",), vmem_limit_bytes=VMEM_LIMIT),
        name="deltanet_meta",
    )(proj, proj, proj, gates)


def _deltanet(proj, gates, state0, dn_norm_w, *, batch, seq_len):
    tt = TT_DN
    nt = seq_len // tt
    c = CHUNK
    kern = functools.partial(_deltanet_kernel, tt=tt)
    row = lambda b, t: b * nt + t
    const = lambda shape: pl.BlockSpec(shape, lambda b, t: (0,) * len(shape), pipeline_mode=pl.Buffered(1))
    state_shape = (DN_HEADS, DN_HEAD_DIM, DN_HEAD_DIM)
    qn, kn, gsm, gl, gp, gr = _lookahead_scratch()
    return pl.pallas_call(
        kern,
        out_shape=jax.ShapeDtypeStruct((batch * seq_len, DN_WIDTH), BF16),
        grid=(batch, nt),
        in_specs=[
            pl.BlockSpec((tt, DN_WIDTH), lambda b, t: (row(b, t), COL_Q // DN_WIDTH)),
            pl.BlockSpec((tt, DN_WIDTH), lambda b, t: (row(b, t), COL_K // DN_WIDTH)),
            pl.BlockSpec((tt, DN_WIDTH), lambda b, t: (row(b, t), COL_V // DN_WIDTH)),
            pl.BlockSpec((tt, DN_WIDTH), lambda b, t: (row(b, t), COL_ZD // DN_WIDTH)),
            pl.BlockSpec((tt, LANES), lambda b, t: (row(b, t), 0)),
            const(state_shape),
            const((1, DN_HEAD_DIM)),
        ],
        out_specs=pl.BlockSpec((tt, DN_WIDTH), lambda b, t: (row(b, t), 0)),
        scratch_shapes=[
            pltpu.VMEM(state_shape, F32),
            qn, kn,
            pltpu.VMEM((2, c, DN_WIDTH), F32),
            gsm, gl, gp, gr,
        ],
        compiler_params=pltpu.CompilerParams(
            dimension_semantics=("arbitrary", "arbitrary"), vmem_limit_bytes=VMEM_LIMIT),
        name="deltanet",
    )(proj, proj, proj, proj, gates, state0, dn_norm_w)


def _out_merge_kernel(u_ref, zp_ref, un_ref, zpn_ref, gpa_ref, gpb_ref, gda_ref, gdb_ref, ydn_ref, x_ref, mu_ref,
                      mix_ref, scale_ref, wpo_ref, wdo_ref, wo_ref, fnw_ref, out_ref, ubuf_ref, ypool_ref,
                      *, tm, tiles):
    g = pl.program_id(0) * tiles + pl.program_id(1)
    hist = N_META

    def pool_group(gi, zp_blk_ref, slot):
        w = POOL_WINDOWS[gi]
        cs = slice(gi * POOL_GROUP_DIM, (gi + 1) * POOL_GROUP_DIM)
        ext = ubuf_ref[:, cs]
        acc = ext
        span = 1
        while span < w:
            acc = acc + pltpu.roll(acc, span, axis=0)
            span *= 2
        d = acc[hist:] * (1.0 / w) - ext[hist:]
        mixed = _mm(d, mix_ref[gi])
        zp = zp_blk_ref[:, cs].astype(F32)
        ypool_ref[slot, :, cs] = (mixed * scale_ref[:, cs] * zp).astype(BF16)

    @pl.when(g == 0)
    def _():
        ubuf_ref[0:hist, :] = mu_ref[...].astype(F32)
        ubuf_ref[hist:hist + tm, :] = u_ref[...].astype(F32)
        for gi in range(POOL_GROUPS):
            pool_group(gi, zp_ref, 0)

    y_pool = ypool_ref[g % 2]

    frame = slice(GATE_SHIFT, GATE_SHIFT + D_MODEL)
    gp = jnp.concatenate([gpa_ref[...], gpb_ref[...]], axis=1).astype(F32)[:, frame]
    gd = jnp.concatenate([gda_ref[...], gdb_ref[...]], axis=1).astype(F32)[:, frame]
    t_pool = jnp.dot(y_pool, wpo_ref[...], preferred_element_type=F32)
    batch_start = (g + 1) % tiles == 0
    ubuf_ref[0:hist, :] = jnp.where(batch_start, mu_ref[...].astype(F32), ubuf_ref[tm:tm + hist, :])
    ubuf_ref[hist:hist + tm, :] = un_ref[...].astype(F32)
    ydn = ydn_ref[...]
    dn_blk = D_MODEL // POOL_GROUPS
    t_dn = []
    for gi in range(POOL_GROUPS):
        pool_group(gi, zpn_ref, (g + 1) % 2)
        t_dn.append(jnp.dot(ydn, wdo_ref[:, gi * dn_blk:(gi + 1) * dn_blk], preferred_element_type=F32))
    t_dn = jnp.concatenate(t_dn, axis=1)
    merged = _sigmoid(gp) * t_pool + _sigmoid(gd) * t_dn
    h = x_ref[...] + jnp.dot(merged.astype(BF16), wo_ref[...], preferred_element_type=F32)
    ms = jnp.mean(h * h, axis=-1, keepdims=True)
    out_ref[...] = h * lax.rsqrt(ms + NORM_EPS) * fnw_ref[...]


def _resident(shape, index_map):
    return pl.BlockSpec(shape, index_map, pipeline_mode=pl.Buffered(1))


def _out_merge(proj, proj_tail, y_dn, x2d, mix, scale, wpo, wdo, wo, fnw, *, batch, seq_len):
    tm = TM_OUT
    tiles = seq_len // tm
    meta_blk = batch * seq_len // N_META
    kern = functools.partial(_out_merge_kernel, tm=tm, tiles=tiles)
    row = lambda b, t: b * tiles + t
    nxt = lambda b, t: jnp.minimum(row(b, t) + 1, batch * tiles - 1)
    gp0, gd0 = COL_GP - GATE_SHIFT, COL_GD - GATE_SHIFT
    assert gp0 % D_MODEL == 0 and gd0 % D_MODEL == 0
    assert gd0 + D_MODEL == proj.shape[1]
    return pl.pallas_call(
        kern,
        out_shape=jax.ShapeDtypeStruct((batch * seq_len, D_MODEL), F32),
        grid=(batch, tiles),
        in_specs=[
            pl.BlockSpec((tm, POOL_WIDTH), lambda b, t: (row(b, t), COL_U // POOL_WIDTH)),
            pl.BlockSpec((tm, POOL_WIDTH), lambda b, t: (row(b, t), COL_ZP // POOL_WIDTH)),
            pl.BlockSpec((tm, POOL_WIDTH), lambda b, t: (nxt(b, t), COL_U // POOL_WIDTH)),
            pl.BlockSpec((tm, POOL_WIDTH), lambda b, t: (nxt(b, t), COL_ZP // POOL_WIDTH)),
            pl.BlockSpec((tm, D_MODEL), lambda b, t: (row(b, t), gp0 // D_MODEL)),
            pl.BlockSpec((tm, LANES), lambda b, t: (row(b, t), (gp0 + D_MODEL) // LANES)),
            pl.BlockSpec((tm, D_MODEL), lambda b, t: (row(b, t), gd0 // D_MODEL)),
            pl.BlockSpec((tm, LANES), lambda b, t: (row(b, t), 0)),
            pl.BlockSpec((tm, DN_WIDTH), lambda b, t: (row(b, t), 0)),
            pl.BlockSpec((tm, D_MODEL), lambda b, t: (row(b, t), 0)),
            _resident((N_META, POOL_WIDTH), lambda b, t: (meta_blk, COL_U // POOL_WIDTH)),
            _resident((POOL_GROUPS, POOL_GROUP_DIM, POOL_GROUP_DIM), lambda b, t: (0, 0, 0)),
            _resident((1, POOL_WIDTH), lambda b, t: (0, 0)),
            _resident((POOL_WIDTH, D_MODEL), lambda b, t: (0, 0)),
            _resident((DN_WIDTH, D_MODEL), lambda b, t: (0, 0)),
            _resident((D_MODEL, D_MODEL), lambda b, t: (0, 0)),
            _resident((1, D_MODEL), lambda b, t: (0, 0)),
        ],
        out_specs=pl.BlockSpec((tm, D_MODEL), lambda b, t: (row(b, t), 0)),
        scratch_shapes=[pltpu.VMEM((N_META + tm, POOL_WIDTH), F32),
                        pltpu.VMEM((2, tm, POOL_WIDTH), BF16)],
        compiler_params=pltpu.CompilerParams(
            dimension_semantics=("arbitrary", "arbitrary"), vmem_limit_bytes=VMEM_LIMIT),
        name="out_merge",
    )(proj, proj, proj, proj, proj, proj, proj, proj_tail, y_dn, x2d, proj, mix, scale, wpo, wdo, wo, fnw)


def kernel(x, meta_tokens, norm_w, w_in, conv_w, A_log, dt_bias, pool_mix, pool_scale, dn_norm_w,
           w_pool_out, w_dn_out, w_o, final_norm_w):
    batch, seq_len, _ = x.shape
    assert norm_w.shape[0] == 1, "single layer block"
    x2d = x.reshape(batch * seq_len, D_MODEL)

    rows = batch * seq_len
    w_t = jnp.swapaxes(w_in, 1, 2).reshape(w_in.shape[2], D_MODEL)
    gpar = jnp.pad(jnp.concatenate([A_log[0:1], dt_bias[0:1]], axis=0).astype(F32),
                   ((0, 0), (DN_HEADS, LANES - 2 * DN_HEADS)))
    nw = norm_w[0].reshape(1, D_MODEL)

    xn_all, gates, proj_tail = _norm_gates(x2d, meta_tokens.astype(F32), nw, w_t, gpar)
    proj, wpo, wdo, wo = _in_proj(xn_all, w_t, conv_w[0], (w_pool_out[0], w_dn_out[0], w_o[0]), seq_len=seq_len)

    state0 = _deltanet_meta(proj, gates, rows=rows)
    y_dn = _deltanet(proj, gates, state0, dn_norm_w[0].reshape(1, DN_HEAD_DIM), batch=batch, seq_len=seq_len)

    out = _out_merge(proj, proj_tail, y_dn, x2d, pool_mix[0], pool_scale[0].reshape(1, POOL_WIDTH),
                     wpo, wdo, wo, final_norm_w.reshape(1, D_MODEL), batch=batch, seq_len=seq_len)
    return out.reshape(batch, seq_len, D_MODEL)
```

```python
import functools

import jax
import jax.numpy as jnp
from jax import lax
from jax.experimental import pallas as pl
from jax.experimental.pallas import tpu as pltpu

D_MODEL = 2048
N_META = 16
POOL_GROUPS = 4
POOL_WINDOWS = (2, 4, 8, 16)
POOL_WIDTH = D_MODEL // 2
POOL_GROUP_DIM = POOL_WIDTH // POOL_GROUPS
DN_HEADS = 16
DN_HEAD_DIM = 128
DN_WIDTH = DN_HEADS * DN_HEAD_DIM
CONV_WIDTH = 4
NORM_EPS = 1e-6

COL_U = 0
COL_ZP = COL_U + POOL_WIDTH
COL_Q = COL_ZP + POOL_WIDTH
COL_K = COL_Q + DN_WIDTH
COL_V = COL_K + DN_WIDTH
COL_ZD = COL_V + DN_WIDTH
BA_OFFSET = COL_ZD + DN_WIDTH
COL_GP = BA_OFFSET + 2 * DN_HEADS
COL_GD = COL_GP + D_MODEL
LANES = 128
SUBLANES = 8
GATE_SHIFT = COL_GP % LANES

PROJ_DTYPE = jnp.bfloat16
CHUNK = 64
LEAF = 16
TT_DN = 1024
TM_NORM = 1024
X_RING = 3
TN_PROJ = 256
ROW_BLK_PROJ = 1024
W_CAST_SLABS = 32
TM_OUT = 256
VMEM_LIMIT = 58 * 1024 * 1024

BF16 = jnp.bfloat16
F32 = jnp.float32


def _mm(a, b):
    return jnp.dot(a.astype(BF16), b.astype(BF16), preferred_element_type=F32)


def _mm_nt(a, b):
    return lax.dot_general(a.astype(BF16), b.astype(BF16), (((1,), (1,)), ((), ())),
                           preferred_element_type=F32)


def _mm_tn(a, b):
    return lax.dot_general(a.astype(BF16), b.astype(BF16), (((0,), (0,)), ((), ())),
                           preferred_element_type=F32)


def _sigmoid(x):
    return 0.5 * jnp.tanh(0.5 * x) + 0.5


def _silu(x):
    h = 0.5 * x
    return h * jnp.tanh(h) + h


def _softplus(x):
    return jnp.maximum(x, 0.0) + jnp.log1p(jnp.exp(-jnp.abs(x)))


def _norm_gates_kernel(x_hbm, meta_ref, nw_ref, wba_ref, wtail_ref, gpar_ref, xn_ref, gates_ref, tail_ref,
                       xbuf_ref, sem_ref, *, x_tiles, tm, sub):
    i = pl.program_id(0)

    def tile_copy(t):
        slot = t % X_RING
        return pltpu.make_async_copy(
            x_hbm.at[pl.ds(pl.multiple_of(t * tm, tm), tm)], xbuf_ref.at[slot], sem_ref.at[slot])

    @pl.when(i == 0)
    def _():
        for t in range(min(X_RING - 1, x_tiles)):
            tile_copy(jnp.int32(t)).start()

    @pl.when(i + X_RING - 1 < x_tiles)
    def _():
        tile_copy(i + X_RING - 1).start()

    nw = nw_ref[...]
    wtail = wtail_ref[...]
    wboth = jnp.concatenate(
        [wba_ref[...], wtail, jnp.zeros((LANES - wtail.shape[0], D_MODEL), F32)], axis=0).astype(BF16)
    a_rate = jnp.exp(gpar_ref[0:1, :])
    dt_bias = gpar_ref[1:2, :]

    def rows_block(x, rows):
        ms = jnp.mean(x * x, axis=-1, keepdims=True)
        xn = (x * lax.rsqrt(ms + NORM_EPS) * nw).astype(BF16)
        xn_ref[rows, :] = xn
        both = _mm_nt(xn, wboth)
        ba = both[:, :LANES]
        tail_ref[rows, :] = both[:, LANES:].astype(tail_ref.dtype)
        lane = lax.broadcasted_iota(jnp.int32, ba.shape, 1)
        g = -a_rate * _softplus(ba + dt_bias)
        gates_ref[rows, :] = jnp.where(lane < DN_HEADS, _sigmoid(ba), jnp.where(lane < 2 * DN_HEADS, g, 0.0))

    @pl.when(i < x_tiles)
    def _():
        tile_copy(i).wait()
        slot = i % X_RING

        def body(r, carry):
            rows = pl.ds(pl.multiple_of(r * sub, sub), sub)
            rows_block(xbuf_ref[slot, rows, :], rows)
            return carry

        lax.fori_loop(0, tm // sub, body, 0)

    @pl.when(i == x_tiles)
    def _():
        rows_block(meta_ref[...], pl.ds(0, N_META))


def _norm_gates(x2d, meta, norm_w, w_t, gpar):
    rows = x2d.shape[0]
    tm = TM_NORM
    tail_start = w_t.shape[0] // TN_PROJ * TN_PROJ
    tail_cols = w_t.shape[0] - tail_start
    assert tail_start % LANES == 0 and 0 < tail_cols <= LANES
    assert tail_cols % SUBLANES == 0 and tail_start % tail_cols == 0
    x_tiles = rows // tm
    assert rows % tm == 0
    kern = functools.partial(_norm_gates_kernel, x_tiles=x_tiles, tm=tm, sub=512)
    return pl.pallas_call(
        kern,
        out_shape=(jax.ShapeDtypeStruct((rows + N_META, D_MODEL), BF16),
                   jax.ShapeDtypeStruct((rows + N_META, LANES), F32),
                   jax.ShapeDtypeStruct((rows + N_META, LANES), PROJ_DTYPE)),
        grid=(x_tiles + 1,),
        in_specs=[
            pl.BlockSpec(memory_space=pl.ANY),
            pl.BlockSpec((N_META, D_MODEL), lambda i: (0, 0)),
            pl.BlockSpec((1, D_MODEL), lambda i: (0, 0)),
            pl.BlockSpec((LANES, D_MODEL), lambda i: (BA_OFFSET // LANES, 0)),
            pl.BlockSpec((tail_cols, D_MODEL), lambda i: (tail_start // tail_cols, 0)),
            pl.BlockSpec((2, LANES), lambda i: (0, 0)),
        ],
        out_specs=(pl.BlockSpec((tm, D_MODEL), lambda i: (i, 0)),
                   pl.BlockSpec((tm, LANES), lambda i: (i, 0)),
                   pl.BlockSpec((tm, LANES), lambda i: (i, 0))),
        scratch_shapes=[pltpu.VMEM((X_RING, tm, D_MODEL), F32), pltpu.SemaphoreType.DMA((X_RING,))],
        compiler_params=pltpu.CompilerParams(
            dimension_semantics=("arbitrary",), vmem_limit_bytes=VMEM_LIMIT),
        name="norm_gates",
    )(x2d, meta, norm_w, w_t, w_t, gpar)


def _in_proj_kernel(xn_ref, w_ref, cw_ref, wpo_ref, wdo_ref, wo_ref,
                    proj_ref, wpo_bf_ref, wdo_bf_ref, wo_bf_ref, acc_ref, *, rows, row_blk, seq_len):
    j = pl.program_id(0)
    tn = w_ref.shape[0]
    ms = slice(rows, rows + N_META)
    conv_tile = (j >= COL_Q // tn) & (j < COL_ZD // tn)
    silu_tile = ((j >= COL_ZP // tn) & (j < COL_Q // tn)) | ((j >= COL_ZD // tn) & (j < BA_OFFSET // tn))

    def cast_weights():
        for src, dst in ((wpo_ref, wpo_bf_ref), (wdo_ref, wdo_bf_ref), (wo_ref, wo_bf_ref)):
            dst[...] = src[...].astype(BF16)
        return w_ref[...].astype(BF16)

    def plain(fn):
        w = cast_weights()
        for r in range(rows // row_blk):
            rs = slice(r * row_blk, (r + 1) * row_blk)
            proj_ref[rs, :] = fn(_mm_nt(xn_ref[rs, :], w)).astype(proj_ref.dtype)
        proj_ref[ms, :] = fn(_mm_nt(xn_ref[ms, :], w)).astype(proj_ref.dtype)

    @pl.when(conv_tile)
    def _():
        w = cast_weights()
        cw = cw_ref[...]
        hist_rows = SUBLANES
        n_blk = rows // row_blk

        def conv_silu(slot, n):
            y = acc_ref[slot, hist_rows:hist_rows + n, :] * cw[CONV_WIDTH - 1:CONV_WIDTH]
            for kk in range(CONV_WIDTH - 1):
                off = hist_rows - (CONV_WIDTH - 1) + kk
                y = y + acc_ref[slot, off:off + n, :] * cw[kk:kk + 1]
            return _silu(y)

        meta_acc = _mm_nt(xn_ref[ms, :], w)
        acc_ref[0, 0:hist_rows, :] = jnp.zeros((hist_rows, tn), F32)
        acc_ref[0, hist_rows:hist_rows + N_META, :] = meta_acc
        proj_ref[ms, :] = conv_silu(0, N_META).astype(proj_ref.dtype)
        meta_tail = meta_acc[N_META - hist_rows:]
        for r in range(n_blk + 1):
            if r < n_blk:
                slot = r % 2
                rs = slice(r * row_blk, (r + 1) * row_blk)
                acc = _mm_nt(xn_ref[rs, :], w)
                batch_start = (r * row_blk) % seq_len == 0
                acc_ref[slot, 0:hist_rows, :] = (
                    meta_tail if batch_start else acc_ref[1 - slot, row_blk:row_blk + hist_rows, :])
                acc_ref[slot, hist_rows:hist_rows + row_blk, :] = acc
            if r > 0:
                ps = slice((r - 1) * row_blk, r * row_blk)
                proj_ref[ps, :] = conv_silu((r - 1) % 2, row_blk).astype(proj_ref.dtype)

    @pl.when(silu_tile)
    def _():
        plain(_silu)

    @pl.when(jnp.logical_not(conv_tile | silu_tile))
    def _():
        plain(lambda a: a)


def _in_proj(xn_all, w_t, conv_w, out_weights, *, seq_len):
    rows = xn_all.shape[0] - N_META
    cols = w_t.shape[0]
    tn = TN_PROJ
    n_conv_tiles = (COL_ZD - COL_Q) // tn
    n_steps = cols // tn
    assert W_CAST_SLABS <= n_steps
    slab = lambda wgt: pl.BlockSpec((wgt.shape[0] // W_CAST_SLABS, D_MODEL),
                                    lambda j: (jnp.minimum(j, W_CAST_SLABS - 1), 0))
    kern = functools.partial(_in_proj_kernel, rows=rows, row_blk=ROW_BLK_PROJ, seq_len=seq_len)
    return pl.pallas_call(
        kern,
        out_shape=(jax.ShapeDtypeStruct((rows + N_META, n_steps * tn), PROJ_DTYPE),
                   *[jax.ShapeDtypeStruct(wgt.shape, BF16) for wgt in out_weights]),
        grid=(n_steps,),
        in_specs=[
            pl.BlockSpec((rows + N_META, D_MODEL), lambda j: (0, 0), pipeline_mode=pl.Buffered(1)),
            pl.BlockSpec((tn, D_MODEL), lambda j: (j, 0)),
            pl.BlockSpec((CONV_WIDTH, tn), lambda j: (0, jnp.clip(j - COL_Q // tn, 0, n_conv_tiles - 1))),
            *[slab(wgt) for wgt in out_weights],
        ],
        out_specs=(pl.BlockSpec((rows + N_META, tn), lambda j: (0, j)),
                   *[slab(wgt) for wgt in out_weights]),
        scratch_shapes=[pltpu.VMEM((2, SUBLANES + ROW_BLK_PROJ, tn), F32)],
        compiler_params=pltpu.CompilerParams(
            dimension_semantics=("arbitrary",), vmem_limit_bytes=VMEM_LIMIT),
        name="in_proj",
    )(xn_all, w_t, conv_w, *out_weights)


PAIR = 2 * DN_HEAD_DIM


def _split3(x):
    hi = x.astype(BF16)
    r1 = x - hi.astype(F32)
    mid = r1.astype(BF16)
    lo = (r1 - mid.astype(F32)).astype(BF16)
    return hi, mid, lo


def _block_diag_rows(y, half):
    lane = lax.broadcasted_iota(jnp.int32, y.shape, 1)
    zero = jnp.zeros_like(y)
    return jnp.concatenate([jnp.where(lane < half, y, zero), jnp.where(lane >= half, y, zero)], axis=0)


def _inverse_masks(c):
    row = lax.broadcasted_iota(jnp.int32, (c, 2 * c), 0)
    col = lax.broadcasted_iota(jnp.int32, (c, 2 * c), 1) & (c - 1)
    blk = lambda x, m: x >> (m.bit_length() - 1)
    eye = (row == col).astype(F32)
    same_leaf = blk(row, LEAF) == blk(col, LEAF)
    offs = []
    m = 2 * LEAF
    while m <= c:
        offs.append((blk(row, m) == blk(col, m)) & (blk(row, m // 2) != blk(col, m // 2)))
        m *= 2
    return eye, same_leaf, offs


def _inv_unit_lower_packed(lps, c, masks, fill, n_fill=1):
    eye, same_leaf, offs = masks
    bd = lambda y: _block_diag_rows(y.astype(BF16), c)
    ps = [jnp.where(same_leaf, -l, 0.0) for l in lps]
    ts = [eye + p for p in ps]
    dep = ps[0]
    ps = [_mm(p, bd(p)) for p in ps]
    fill(dep, ps, n_fill + 1)
    span = 4
    while span <= LEAF:
        dep = ps[0]
        last = span == LEAF
        prod = [_mm(t if last else jnp.concatenate([t, p], axis=0), bd(p)) for t, p in zip(ts, ps)]
        ts = [t + x[:c] for t, x in zip(ts, prod)]
        if not last:
            ps = [x[c:] for x in prod]
        fill(dep, ts, n_fill)
        span *= 2
    for off in offs:
        dep = ts[0]
        inner = [_mm(jnp.where(off, l, 0.0), bd(t)) for l, t in zip(lps, ts)]
        fill(dep, inner, n_fill)
        dep = inner[0]
        ts = [t - _mm(t, bd(x)) for t, x in zip(ts, inner)]
        fill(dep, ts, n_fill)
    return ts


def _delta_rule_core(read_rows, read_gates, n_pos, state_ref, qn_s, kn_s, gsm_s, gl_s, gp_s, gr_s,
                     finish=None, o_s=None):
    c = CHUNK
    hd = DN_HEAD_DIM
    n_pairs = DN_HEADS // 2
    pairs = range(n_pairs)

    rowp = lax.broadcasted_iota(jnp.int32, (c, 2 * c), 0)
    colp = lax.broadcasted_iota(jnp.int32, (c, 2 * c), 1) & (c - 1)
    causal = rowp >= colp
    strict = rowp > colp
    lane_p = lax.broadcasted_iota(jnp.int32, (c, 2 * c), 1)
    inv_masks = _inverse_masks(c)
    tril = (lax.broadcasted_iota(jnp.int32, (c, c), 0) >= lax.broadcasted_iota(jnp.int32, (c, c), 1))
    tril3 = jnp.concatenate([tril.astype(BF16)] * 3, axis=1)
    zero_blk = jnp.zeros((hd, hd), BF16)
    head_cols = lambda h: slice(h * hd, (h + 1) * hd)
    pair_cols_of = lambda p: slice(p * PAIR, (p + 1) * PAIR)
    both = (0, 1)
    ha = [2 * p for p in pairs]
    hb = [2 * p + 1 for p in pairs]

    def l2n(xh, scale):
        return xh * (lax.rsqrt(jnp.sum(xh * xh, axis=-1, keepdims=True) + NORM_EPS) * scale)

    def prepare_pair(j, p, zero=None):
        cols = pair_cols_of(p)
        slot = j & 3
        for idx, (dst, scale) in enumerate(((qn_s, hd ** -0.5), (kn_s, 1.0))):
            x = read_rows(idx, j, cols).astype(F32)
            if zero is not None:
                x = x + jnp.concatenate([zero[0:1], zero[0:1]], axis=1)
            dst[slot, :, cols] = jnp.concatenate([l2n(x[:, :hd], scale), l2n(x[:, hd:], scale)], axis=1)
        r0 = pl.multiple_of((j >> 24) * SUBLANES, SUBLANES)
        back = qn_s[slot, pl.ds(r0, SUBLANES), cols] + kn_s[slot, pl.ds(r0, SUBLANES), cols]
        return back[:, :hd] + back[:, hd:]

    def out_rows(j):
        return pl.multiple_of(jnp.maximum(j, 0) * c, c)

    def exact_rows_dot(lhs3, x):
        hi, mid, lo = _split3(x)
        return jnp.dot(lhs3, jnp.concatenate([hi, mid, lo], axis=0), preferred_element_type=F32)

    def pair_cols(mat, la, lb):
        return jnp.where(lane_p < c, mat[:, la:la + 1], mat[:, lb:lb + 1])

    def pair_rows(mat_t, ra, rb):
        return jnp.concatenate([mat_t[ra:ra + 1, :], mat_t[rb:rb + 1, :]], axis=1)

    def zero_of(v):
        return jnp.where(v != v, v, 0.0)

    def prepare_gates(j):
        slot = j & 3
        gt = read_gates(j)
        gcum = exact_rows_dot(tril3, gt)
        g_last = gcum[c - 1:c, :]
        gsm_s[slot, 0] = jnp.exp(gcum)
        gsm_s[slot, 1] = jnp.exp(g_last - gcum)
        gsm_s[slot, 2] = jnp.broadcast_to(jnp.exp(g_last), (c, LANES))
        gt_t = gt.T
        gcum_t = gcum.T
        for p in pairs:
            g_col = pair_cols(gcum, DN_HEADS + ha[p], DN_HEADS + hb[p])
            g_row = pair_rows(gcum_t, DN_HEADS + ha[p], DN_HEADS + hb[p])
            b_col = pair_cols(gt, ha[p], hb[p])
            b_row = pair_rows(gt_t, ha[p], hb[p])
            dec = jnp.where(causal, jnp.exp(jnp.where(causal, g_col - g_row, 0.0)), 0.0)
            gp_s[slot, p] = dec
            gl_s[slot, p] = jnp.where(strict, b_col * dec, 0.0)
            gr_s[slot, p] = jnp.concatenate(
                [b_row, b_row * jnp.exp(g_row), jnp.zeros((SUBLANES - 2, 2 * c), F32)], axis=0)

    def pair_body(i, carry):
        pos = [2 * i, 2 * i + 1]
        slot = [pz & 3 for pz in pos]
        fillers = []
        for p in pairs:
            for x in both:
                if finish is not None:
                    fillers.append(functools.partial(finish, x, out_rows(pos[x] - 2), p))
                fillers.append(functools.partial(prepare_pair, pos[x] + 2, p))
        n_slices = len(fillers)
        fillers.reverse()

        def fill(dep, into, n=1):
            zero = None if dep is None else zero_of(dep[0:SUBLANES, 0:LANES].astype(F32))
            for _ in range(n):
                if fillers:
                    done = zero_of(fillers.pop()(zero=zero))[0:1]
                    if into[-1].shape[1] == PAIR:
                        done = jnp.concatenate([done, done], axis=1)
                    into[-1] = into[-1] + done

        prob = [(x, p) for x in both for p in pairs]
        eg = [gsm_s[slot[x], 0] for x in both]
        e_rest = [gsm_s[slot[x], 1] for x in both]
        e_last = [gsm_s[slot[x], 2, 0:1, :] for x in both]
        gl = [gl_s[slot[x], p] for x, p in prob]
        gp = [gp_s[slot[x], p] for x, p in prob]
        gr = [gr_s[slot[x], p] for x, p in prob]
        kpair = [kn_s[slot[x], :, pair_cols_of(p)].astype(BF16) for x, p in prob]
        kbd = [_block_diag_rows(kp, hd) for kp in kpair]
        qpair = []
        for n, (x, p) in enumerate(prob):
            staged = [qn_s[slot[x], :, pair_cols_of(p)]]
            if n % 2:
                fill(None, staged)
            qpair.append(staged[0].astype(BF16))
        aq = [_mm_nt(jnp.concatenate([kpair[n], qpair[n]], axis=0), kbd[n]) for n in range(len(prob))]
        for x in both:
            prepare_gates(pos[x] + 2)
        lps = [aq[n][:c] * gl[n] for n in range(len(prob))]
        pps = [aq[n][c:] * gp[n] for n in range(len(prob))]
        per_group = max(1, (n_slices - len(prob) // 2) // 12)
        tps = _inv_unit_lower_packed(lps, c, inv_masks, fill, per_group)
        vbd = [_block_diag_rows(read_rows(2, pos[x], pair_cols_of(p)), hd) for x, p in prob]
        u = [_mm(tps[n] * gr[n][0:1], vbd[n]) for n in range(len(prob))]
        w = [_mm(tps[n] * gr[n][1:2], kbd[n]) for n in range(len(prob))]
        fill(tps[0], w, per_group)
        state = [state_ref[h] for h in range(DN_HEADS)]
        outs = []
        for x in both:
            idx = [x * n_pairs + p for p in pairs]
            qd = [jnp.concatenate(
                [qn_s[slot[x], :, head_cols(h)] * eg[x][:, DN_HEADS + h:DN_HEADS + h + 1]
                 for h in (ha[p], hb[p])], axis=1) for p in pairs]
            sbd = [jnp.concatenate([jnp.concatenate([state[ha[p]].astype(BF16), zero_blk], axis=1),
                                    jnp.concatenate([zero_blk, state[hb[p]].astype(BF16)], axis=1)], axis=0)
                   for p in pairs]
            ws = [_mm(jnp.concatenate([w[idx[p]], qd[p]], axis=0), sbd[p]) for p in pairs]
            fill(w[idx[0]], ws, per_group)
            v_new = [u[idx[p]] - ws[p][:c] for p in pairs]
            o = [ws[p][c:] + _mm(pps[idx[p]], _block_diag_rows(v_new[p].astype(BF16), hd)) for p in pairs]
            new_state = []
            for h in range(DN_HEADS):
                p, half = divmod(h, 2)
                lane_h = DN_HEADS + h
                kd = kn_s[slot[x], :, head_cols(h)] * e_rest[x][:, lane_h:lane_h + 1]
                upd = _mm_tn(kd, v_new[p][:, half * hd:(half + 1) * hd])
                new_state.append(state[h] * e_last[x][:, lane_h:lane_h + 1] + upd)
            state = new_state
            if x == 1:
                fill(ws[0], o, len(fillers))
            outs.append(o)
        for h in range(DN_HEADS):
            state_ref[h] = state[h]
        if o_s is not None:
            for x in both:
                for p in pairs:
                    o_s[x, :, pair_cols_of(p)] = outs[x][p]
        return carry

    for x in both:
        for p in pairs:
            prepare_pair(jnp.int32(x), p)
        prepare_gates(jnp.int32(x))
    n_iter = n_pos // 2
    if n_iter == 1:
        pair_body(jnp.int32(0), 0)
    else:
        lax.fori_loop(0, n_iter, pair_body, 0)
    if finish is not None:
        for x in both:
            last_rows = out_rows(jnp.int32(n_pos - 2 + x))
            for p in pairs:
                finish(x, last_rows, p)


def _deltanet_kernel(q_ref, k_ref, v_ref, z_ref, gates_ref, s0_ref, dnw_ref,
                     y_ref, state_ref, qn_s, kn_s, o_s, gsm_s, gl_s, gp_s, gr_s, *, tt):
    c = CHUNK
    hd = DN_HEAD_DIM
    n_chunks = tt // c
    dnw = dnw_ref[...]

    @pl.when(pl.program_id(1) == 0)
    def _():
        state_ref[...] = s0_ref[...]
        o_s[...] = jnp.zeros(o_s.shape, F32)

    srcs = (q_ref, k_ref, v_ref)

    def rows_at(pos):
        return pl.ds(pl.multiple_of(jnp.minimum(pos, n_chunks - 1) * c, c), c)

    def read_rows(idx, pos, cols):
        return srcs[idx][rows_at(pos), cols]

    def read_gates(pos):
        return gates_ref[rows_at(pos), :]

    def finish(src, row0, p, zero=None):
        rows = pl.ds(row0, c)
        for half in range(2):
            cs = slice(p * PAIR + half * hd, p * PAIR + (half + 1) * hd)
            oh = o_s[src, :, cs]
            if zero is not None:
                oh = oh + zero[0:1]
            oh = oh * lax.rsqrt(jnp.mean(oh * oh, axis=-1, keepdims=True) + NORM_EPS) * dnw
            y_ref[rows, cs] = (oh * z_ref[rows, cs].astype(F32)).astype(y_ref.dtype)
        r0 = pl.multiple_of((row0 >> 24) * (2 * SUBLANES), 2 * SUBLANES)
        back = y_ref[pl.ds(r0, 2 * SUBLANES), p * PAIR:(p + 1) * PAIR].astype(F32)[0:SUBLANES]
        return back[:, :hd] + back[:, hd:]

    _delta_rule_core(read_rows, read_gates, n_chunks, state_ref, qn_s, kn_s, gsm_s, gl_s, gp_s, gr_s,
                     finish=finish, o_s=o_s)


def _deltanet_meta_kernel(mq_ref, mk_ref, mv_ref, mg_ref, s_out_ref, state_ref, mqkv_ref, mgates_ref,
                          qn_s, kn_s, gsm_s, gl_s, gp_s, gr_s):
    c = CHUNK
    front = 2 * c - N_META
    state_ref[...] = jnp.zeros(state_ref.shape, F32)
    mqkv_ref[...] = jnp.zeros(mqkv_ref.shape, mqkv_ref.dtype)
    mgates_ref[...] = jnp.zeros(mgates_ref.shape, F32)
    for idx, src in enumerate((mq_ref, mk_ref, mv_ref)):
        mqkv_ref[front:, idx * DN_WIDTH:(idx + 1) * DN_WIDTH] = src[...]
    mgates_ref[front:, :] = mg_ref[...]

    def rows_at(pos):
        return pl.ds(pl.multiple_of(jnp.minimum(pos, 1) * c, c), c)

    def read_rows(idx, pos, cols):
        return mqkv_ref[rows_at(pos), slice(idx * DN_WIDTH + cols.start, idx * DN_WIDTH + cols.stop)]

    def read_gates(pos):
        return mgates_ref[rows_at(pos), :]

    _delta_rule_core(read_rows, read_gates, 2, state_ref, qn_s, kn_s, gsm_s, gl_s, gp_s, gr_s)
    s_out_ref[...] = state_ref[...]


def _lookahead_scratch():
    c = CHUNK
    return [
        pltpu.VMEM((4, c, DN_WIDTH), F32),
        pltpu.VMEM((4, c, DN_WIDTH), F32),
        pltpu.VMEM((4, 3, c, LANES), F32),
        pltpu.VMEM((4, DN_HEADS // 2, c, 2 * c), F32),
        pltpu.VMEM((4, DN_HEADS // 2, c, 2 * c), F32),
        pltpu.VMEM((4, DN_HEADS // 2, SUBLANES, 2 * c), F32),
    ]


def _deltanet_meta(proj, gates, *, rows):
    state_shape = (DN_HEADS, DN_HEAD_DIM, DN_HEAD_DIM)
    qn, kn, gsm, gl, gp, gr = _lookahead_scratch()
    meta_blk = rows // N_META
    qkv_spec = lambda col: pl.BlockSpec((N_META, DN_WIDTH), lambda i: (meta_blk, col // DN_WIDTH))
    return pl.pallas_call(
        _deltanet_meta_kernel,
        out_shape=jax.ShapeDtypeStruct(state_shape, F32),
        grid=(1,),
        in_specs=[qkv_spec(COL_Q), qkv_spec(COL_K), qkv_spec(COL_V),
                  pl.BlockSpec((N_META, LANES), lambda i: (meta_blk, 0))],
        out_specs=pl.BlockSpec(state_shape, lambda i: (0, 0, 0)),
        scratch_shapes=[pltpu.VMEM(state_shape, F32),
                        pltpu.VMEM((2 * CHUNK, 3 * DN_WIDTH), PROJ_DTYPE),
                        pltpu.VMEM((2 * CHUNK, LANES), F32),
                        qn, kn, gsm, gl, gp, gr],
        compiler_params=pltpu.CompilerParams(
            dimension_semantics=("arbitrary",), vmem_limit_bytes=VMEM_LIMIT),
        name="deltanet_meta",
    )(proj, proj, proj, gates)


def _deltanet(proj, gates, state0, dn_norm_w, *, batch, seq_len):
    tt = TT_DN
    nt = seq_len // tt
    c = CHUNK
    kern = functools.partial(_deltanet_kernel, tt=tt)
    row = lambda b, t: b * nt + t
    const = lambda shape: pl.BlockSpec(shape, lambda b, t: (0,) * len(shape), pipeline_mode=pl.Buffered(1))
    state_shape = (DN_HEADS, DN_HEAD_DIM, DN_HEAD_DIM)
    qn, kn, gsm, gl, gp, gr = _lookahead_scratch()
    return pl.pallas_call(
        kern,
        out_shape=jax.ShapeDtypeStruct((batch * seq_len, DN_WIDTH), BF16),
        grid=(batch, nt),
        in_specs=[
            pl.BlockSpec((tt, DN_WIDTH), lambda b, t: (row(b, t), COL_Q // DN_WIDTH)),
            pl.BlockSpec((tt, DN_WIDTH), lambda b, t: (row(b, t), COL_K // DN_WIDTH)),
            pl.BlockSpec((tt, DN_WIDTH), lambda b, t: (row(b, t), COL_V // DN_WIDTH)),
            pl.BlockSpec((tt, DN_WIDTH), lambda b, t: (row(b, t), COL_ZD // DN_WIDTH)),
            pl.BlockSpec((tt, LANES), lambda b, t: (row(b, t), 0)),
            const(state_shape),
            const((1, DN_HEAD_DIM)),
        ],
        out_specs=pl.BlockSpec((tt, DN_WIDTH), lambda b, t: (row(b, t), 0)),
        scratch_shapes=[
            pltpu.VMEM(state_shape, F32),
            qn, kn,
            pltpu.VMEM((2, c, DN_WIDTH), F32),
            gsm, gl, gp, gr,
        ],
        compiler_params=pltpu.CompilerParams(
            dimension_semantics=("arbitrary", "arbitrary"), vmem_limit_bytes=VMEM_LIMIT),
        name="deltanet",
    )(proj, proj, proj, proj, gates, state0, dn_norm_w)


def _out_merge_kernel(u_ref, zp_ref, un_ref, zpn_ref, gpa_ref, gpb_ref, gda_ref, gdb_ref, ydn_ref, x_ref, mu_ref,
                      mix_ref, scale_ref, wpo_ref, wdo_ref, wo_ref, fnw_ref, out_ref, ubuf_ref, ypool_ref,
                      *, tm, tiles):
    g = pl.program_id(0) * tiles + pl.program_id(1)
    hist = N_META

    def pool_group(gi, zp_blk_ref, slot):
        w = POOL_WINDOWS[gi]
        cs = slice(gi * POOL_GROUP_DIM, (gi + 1) * POOL_GROUP_DIM)
        ext = ubuf_ref[:, cs]
        acc = ext
        span = 1
        while span < w:
            acc = acc + pltpu.roll(acc, span, axis=0)
            span *= 2
        d = acc[hist:] * (1.0 / w) - ext[hist:]
        mixed = _mm(d, mix_ref[gi])
        zp = zp_blk_ref[:, cs].astype(F32)
        ypool_ref[slot, :, cs] = (mixed * scale_ref[:, cs] * zp).astype(BF16)

    @pl.when(g == 0)
    def _():
        ubuf_ref[0:hist, :] = mu_ref[...].astype(F32)
        ubuf_ref[hist:hist + tm, :] = u_ref[...].astype(F32)
        for gi in range(POOL_GROUPS):
            pool_group(gi, zp_ref, 0)

    y_pool = ypool_ref[g % 2]

    frame = slice(GATE_SHIFT, GATE_SHIFT + D_MODEL)
    gp = jnp.concatenate([gpa_ref[...], gpb_ref[...]], axis=1).astype(F32)[:, frame]
    gd = jnp.concatenate([gda_ref[...], gdb_ref[...]], axis=1).astype(F32)[:, frame]
    t_pool = jnp.dot(y_pool, wpo_ref[...], preferred_element_type=F32)
    batch_start = (g + 1) % tiles == 0
    ubuf_ref[0:hist, :] = jnp.where(batch_start, mu_ref[...].astype(F32), ubuf_ref[tm:tm + hist, :])
    ubuf_ref[hist:hist + tm, :] = un_ref[...].astype(F32)
    ydn = ydn_ref[...]
    dn_blk = D_MODEL // POOL_GROUPS
    t_dn = []
    for gi in range(POOL_GROUPS):
        pool_group(gi, zpn_ref, (g + 1) % 2)
        t_dn.append(jnp.dot(ydn, wdo_ref[:, gi * dn_blk:(gi + 1) * dn_blk], preferred_element_type=F32))
    t_dn = jnp.concatenate(t_dn, axis=1)
    merged = _sigmoid(gp) * t_pool + _sigmoid(gd) * t_dn
    h = x_ref[...] + jnp.dot(merged.astype(BF16), wo_ref[...], preferred_element_type=F32)
    ms = jnp.mean(h * h, axis=-1, keepdims=True)
    out_ref[...] = h * lax.rsqrt(ms + NORM_EPS) * fnw_ref[...]


def _resident(shape, index_map):
    return pl.BlockSpec(shape, index_map, pipeline_mode=pl.Buffered(1))


def _out_merge(proj, proj_tail, y_dn, x2d, mix, scale, wpo, wdo, wo, fnw, *, batch, seq_len):
    tm = TM_OUT
    tiles = seq_len // tm
    meta_blk = batch * seq_len // N_META
    kern = functools.partial(_out_merge_kernel, tm=tm, tiles=tiles)
    row = lambda b, t: b * tiles + t
    nxt = lambda b, t: jnp.minimum(row(b, t) + 1, batch * tiles - 1)
    gp0, gd0 = COL_GP - GATE_SHIFT, COL_GD - GATE_SHIFT
    assert gp0 % D_MODEL == 0 and gd0 % D_MODEL == 0
    assert gd0 + D_MODEL == proj.shape[1]
    return pl.pallas_call(
        kern,
        out_shape=jax.ShapeDtypeStruct((batch * seq_len, D_MODEL), F32),
        grid=(batch, tiles),
        in_specs=[
            pl.BlockSpec((tm, POOL_WIDTH), lambda b, t: (row(b, t), COL_U // POOL_WIDTH)),
            pl.BlockSpec((tm, POOL_WIDTH), lambda b, t: (row(b, t), COL_ZP // POOL_WIDTH)),
            pl.BlockSpec((tm, POOL_WIDTH), lambda b, t: (nxt(b, t), COL_U // POOL_WIDTH)),
            pl.BlockSpec((tm, POOL_WIDTH), lambda b, t: (nxt(b, t), COL_ZP // POOL_WIDTH)),
            pl.BlockSpec((tm, D_MODEL), lambda b, t: (row(b, t), gp0 // D_MODEL)),
            pl.BlockSpec((tm, LANES), lambda b, t: (row(b, t), (gp0 + D_MODEL) // LANES)),
            pl.BlockSpec((tm, D_MODEL), lambda b, t: (row(b, t), gd0 // D_MODEL)),
            pl.BlockSpec((tm, LANES), lambda b, t: (row(b, t), 0)),
            pl.BlockSpec((tm, DN_WIDTH), lambda b, t: (row(b, t), 0)),
            pl.BlockSpec((tm, D_MODEL), lambda b, t: (row(b, t), 0)),
            _resident((N_META, POOL_WIDTH), lambda b, t: (meta_blk, COL_U // POOL_WIDTH)),
            _resident((POOL_GROUPS, POOL_GROUP_DIM, POOL_GROUP_DIM), lambda b, t: (0, 0, 0)),
            _resident((1, POOL_WIDTH), lambda b, t: (0, 0)),
            _resident((POOL_WIDTH, D_MODEL), lambda b, t: (0, 0)),
            _resident((DN_WIDTH, D_MODEL), lambda b, t: (0, 0)),
            _resident((D_MODEL, D_MODEL), lambda b, t: (0, 0)),
            _resident((1, D_MODEL), lambda b, t: (0, 0)),
        ],
        out_specs=pl.BlockSpec((tm, D_MODEL), lambda b, t: (row(b, t), 0)),
        scratch_shapes=[pltpu.VMEM((N_META + tm, POOL_WIDTH), F32),
                        pltpu.VMEM((2, tm, POOL_WIDTH), BF16)],
        compiler_params=pltpu.CompilerParams(
            dimension_semantics=("arbitrary", "arbitrary"), vmem_limit_bytes=VMEM_LIMIT),
        name="out_merge",
    )(proj, proj, proj, proj, proj, proj, proj, proj_tail, y_dn, x2d, proj, mix, scale, wpo, wdo, wo, fnw)


def kernel(x, meta_tokens, norm_w, w_in, conv_w, A_log, dt_bias, pool_mix, pool_scale, dn_norm_w,
           w_pool_out, w_dn_out, w_o, final_norm_w):
    batch, seq_len, _ = x.shape
    assert norm_w.shape[0] == 1, "single layer block"
    x2d = x.reshape(batch * seq_len, D_MODEL)

    rows = batch * seq_len
    w_t = jnp.swapaxes(w_in, 1, 2).reshape(w_in.shape[2], D_MODEL)
    gpar = jnp.pad(jnp.concatenate([A_log[0:1], dt_bias[0:1]], axis=0).astype(F32),
                   ((0, 0), (DN_HEADS, LANES - 2 * DN_HEADS)))
    nw = norm_w[0].reshape(1, D_MODEL)

    xn_all, gates, proj_tail = _norm_gates(x2d, meta_tokens.astype(F32), nw, w_t, gpar)
    proj, wpo, wdo, wo = _in_proj(xn_all, w_t, conv_w[0], (w_pool_out[0], w_dn_out[0], w_o[0]), seq_len=seq_len)

    state0 = _deltanet_meta(proj, gates, rows=rows)
    y_dn = _deltanet(proj, gates, state0, dn_norm_w[0].reshape(1, DN_HEAD_DIM), batch=batch, seq_len=seq_len)

    out = _out_merge(proj, proj_tail, y_dn, x2d, pool_mix[0], pool_scale[0].reshape(1, POOL_WIDTH),
                     wpo, wdo, wo, final_norm_w.reshape(1, D_MODEL), batch=batch, seq_len=seq_len)
    return out.reshape(batch, seq_len, D_MODEL)
```
